```python
import math
import jax, jax.numpy as jnp
from jax import lax
import numpy as np

D_MODEL = 1024
BATCH = 2
SEQ = 16384
DEPTH = 4

N_EVEN = (DEPTH + 1) // 2
N_ODD = DEPTH // 2
D_FF = 4 * D_MODEL
EPS = 1e-6
NEG = -1e30
FORCE = 1e9
HEAD_DIM = 64

SSD_HEADS = 8
SSD_INNER = SSD_HEADS * HEAD_DIM
SSD_GROUPS = 2
SSD_RPG = SSD_HEADS // SSD_GROUPS
SSD_STATE = 128
SSD_CONV = 4
SSD_CHUNK = 128
SSD_CONV_DIM = SSD_INNER + 2 * SSD_GROUPS * SSD_STATE
SSD_IN = SSD_INNER + SSD_CONV_DIM + SSD_HEADS
DT_MIN = 0.001
DT_MAX = 0.1

NSA_HEADS = 8
NSA_KV = 2
NSA_RPG = NSA_HEADS // NSA_KV
NSA_CMP_LEN = 32
NSA_CMP_STRIDE = 16
NSA_SLC_LEN = 64
NSA_TOPK = 16
NSA_WIN = 512
NSA_CMP_HIDDEN = 256
NSA_QBLK = 128
NSA_Q = NSA_HEADS * HEAD_DIM
NSA_KVW = NSA_KV * HEAD_DIM
NSA_IN = NSA_Q + 6 * NSA_KVW + 3 * NSA_HEADS

EVEN_IN = SSD_IN + NSA_IN
EVEN_MIX = SSD_INNER + NSA_Q

SWA_HEADS = 8
SWA_KV = 2
SWA_RPG = SWA_HEADS // SWA_KV
SWA_WIN = 128
SWA_QBLK = 128
SWA_Q = SWA_HEADS * HEAD_DIM
SWA_KVW = SWA_KV * HEAD_DIM
SWA_IN = SWA_Q + 2 * SWA_KVW

S5_CH = 512
S5_GROUP_CH = 16
S5_GROUPS = S5_CH // S5_GROUP_CH
S5_STATE = 64

ODD_IN = SWA_IN + S5_CH
ODD_MIX = SWA_Q + S5_CH

kernel_name = 'hybrid_ssd_nsa_swa_s5_trunk'


def _rmsnorm(x, g):
    xf = x.astype(jnp.float32)
    y = xf * lax.rsqrt(jnp.mean(xf * xf, axis=-1, keepdims=True) + EPS)
    return (y * g.astype(jnp.float32)).astype(x.dtype)


def _masked_softmax(s, mask):
    s = jnp.where(mask, s.astype(jnp.float32), NEG)
    p = jax.nn.softmax(s, axis=-1)
    return jnp.where(mask, p, 0.0)


def _causal_depthwise_conv(x, w, b):
    k, c = w.shape
    y = lax.conv_general_dilated(x, w.astype(x.dtype)[:, None, :], window_strides=(1,),
                                 padding=[(k - 1, 0)], dimension_numbers=('NWC', 'WIO', 'NWC'),
                                 feature_group_count=c)
    return y + b.astype(y.dtype)


def _ssd_chunked_scan(xs, dt, a, bm, cm):
    bsz, seq, g, r, p = xs.shape
    n = bm.shape[-1]
    t = SSD_CHUNK
    nc = seq // t
    dt = dt.reshape(bsz, nc, t, g, r)
    a_cum = jnp.cumsum(dt * a, axis=2)
    xd = xs.astype(jnp.float32).reshape(bsz, nc, t, g, r, p) * dt[..., None]
    bc = bm.astype(jnp.float32).reshape(bsz, nc, t, g, n)
    cc = cm.astype(jnp.float32).reshape(bsz, nc, t, g, n)
    causal = jnp.tril(jnp.ones((t, t), dtype=bool))[:, :, None, None]
    seg = a_cum[:, :, :, None] - a_cum[:, :, None, :]
    decay = jnp.exp(jnp.where(causal, seg, NEG))
    cb = jnp.einsum('bctgn,bcsgn->bctsg', cc, bc)
    y_diag = jnp.einsum('bctsg,bctsgr,bcsgrp->bctgrp', cb, decay, xd)
    decay_to_end = jnp.exp(a_cum[:, :, -1:] - a_cum)
    states = jnp.einsum('bctgn,bctgr,bctgrp->bcgrpn', bc, decay_to_end, xd)
    chunk_decay = jnp.exp(a_cum[:, :, -1])

    def carry_state(h, inp):
        s_c, d_c = inp
        return h * d_c[..., None, None] + s_c, h

    h0 = jnp.zeros((bsz, g, r, p, n), jnp.float32)
    _, h_in = lax.scan(carry_state, h0, (jnp.moveaxis(states, 1, 0), jnp.moveaxis(chunk_decay, 1, 0)))
    h_in = jnp.moveaxis(h_in, 0, 1)
    y_off = jnp.einsum('bctgn,bcgrpn,bctgr->bctgrp', cc, h_in, jnp.exp(a_cum))
    return (y_diag + y_off).reshape(bsz, seq, g, r, p)


def _ssd_mixer(u, conv_w, conv_b, dt_bias, a_log, d_skip, norm_g):
    bsz, seq, _ = u.shape
    gn = SSD_GROUPS * SSD_STATE
    z = u[..., :SSD_INNER]
    xbc = u[..., SSD_INNER:SSD_INNER + SSD_CONV_DIM]
    dt_raw = u[..., SSD_INNER + SSD_CONV_DIM:]
    xbc = jax.nn.silu(_causal_depthwise_conv(xbc, conv_w, conv_b))
    xs = xbc[..., :SSD_INNER].reshape(bsz, seq, SSD_GROUPS, SSD_RPG, HEAD_DIM)
    bm = xbc[..., SSD_INNER:SSD_INNER + gn].reshape(bsz, seq, SSD_GROUPS, SSD_STATE)
    cm = xbc[..., SSD_INNER + gn:].reshape(bsz, seq, SSD_GROUPS, SSD_STATE)
    dt = jax.nn.softplus(dt_raw.astype(jnp.float32) + dt_bias.astype(jnp.float32))
    dt = dt.reshape(bsz, seq, SSD_GROUPS, SSD_RPG)
    a = -jnp.exp(a_log.astype(jnp.float32)).reshape(SSD_GROUPS, SSD_RPG)
    y = _ssd_chunked_scan(xs, dt, a, bm, cm)
    y = y + xs.astype(jnp.float32) * d_skip.astype(jnp.float32).reshape(SSD_GROUPS, SSD_RPG, 1)
    y = y.reshape(bsz, seq, SSD_INNER) * jax.nn.silu(z.astype(jnp.float32))
    y = _rmsnorm(y.reshape(bsz, seq, SSD_GROUPS, SSD_INNER // SSD_GROUPS),
                 norm_g.reshape(SSD_GROUPS, SSD_INNER // SSD_GROUPS))
    return y.reshape(bsz, seq, SSD_INNER).astype(u.dtype)


def _nsa_compress(kv, pe, w1, b1, w2, b2):
    bsz, seq, g, d = kv.shape
    ratio = NSA_CMP_LEN // NSA_CMP_STRIDE
    nc = seq // NSA_CMP_STRIDE - ratio + 1
    pieces = kv.reshape(bsz, seq // NSA_CMP_STRIDE, NSA_CMP_STRIDE, g, d)
    blocks = jnp.concatenate([pieces[:, j:j + nc] for j in range(ratio)], axis=2)
    blocks = blocks + pe[:, None, :]
    flat = jnp.moveaxis(blocks, 3, 2).reshape(bsz, nc, g, NSA_CMP_LEN * d)
    hid = jax.nn.gelu(flat @ w1 + b1)
    return hid @ w2 + b2


def _nsa_mixer(u, pe, w1, b1, w2, b2):
    bsz, seq, _ = u.shape
    g_, r_, d_ = NSA_KV, NSA_RPG, HEAD_DIM
    q = u[..., :NSA_Q].reshape(bsz, seq, g_, r_, d_)
    parts = [u[..., NSA_Q + i * NSA_KVW:NSA_Q + (i + 1) * NSA_KVW].reshape(bsz, seq, g_, d_) for i in range(6)]
    k_cmp, v_cmp, k_slc, v_slc, k_win, v_win = parts
    gates = jax.nn.sigmoid(u[..., NSA_Q + 6 * NSA_KVW:].astype(jnp.float32)).reshape(bsz, seq, g_, r_, 3)
    kc = _nsa_compress(k_cmp, pe[0], w1[0], b1[0], w2[0], b2[0])
    vc = _nsa_compress(v_cmp, pe[1], w1[1], b1[1], w2[1], b2[1])
    nc = kc.shape[1]
    ns = seq // NSA_SLC_LEN
    topk = min(NSA_TOPK, ns)
    cmp_end = jnp.arange(nc) * NSA_CMP_STRIDE + NSA_CMP_LEN - 1
    c_start = jnp.arange(nc)[:, None] * NSA_CMP_STRIDE
    s_start = jnp.arange(ns)[None, :] * NSA_SLC_LEN
    overlap = ((c_start < s_start + NSA_SLC_LEN) & (c_start + NSA_CMP_LEN > s_start)).astype(jnp.float32)
    ks_blocks = jnp.moveaxis(k_slc.reshape(bsz, ns, NSA_SLC_LEN, g_, d_), 3, 1)
    vs_blocks = jnp.moveaxis(v_slc.reshape(bsz, ns, NSA_SLC_LEN, g_, d_), 3, 1)
    pad = ((0, 0), (NSA_WIN, 0), (0, 0), (0, 0))
    kw_pad = jnp.pad(k_win, pad)
    vw_pad = jnp.pad(v_win, pad)
    gather = jax.vmap(jax.vmap(lambda blocks, ix: blocks[ix]))
    scale = HEAD_DIM ** -0.5
    span = NSA_WIN + NSA_QBLK
    x_sel = topk * NSA_SLC_LEN

    def query_block(qb):
        s0 = qb * NSA_QBLK
        t = s0 + jnp.arange(NSA_QBLK)
        qblk = lax.dynamic_slice_in_dim(q, s0, NSA_QBLK, axis=1)
        gblk = lax.dynamic_slice_in_dim(gates, s0, NSA_QBLK, axis=1)
        cmask = cmp_end[None, :] <= t[:, None]
        p_cmp = _masked_softmax(jnp.einsum('bqgrd,bngd->bgrqn', qblk, kc) * scale, cmask)
        o_cmp = jnp.einsum('bgrqn,bngd->bqgrd', p_cmp, vc)
        imp = jnp.einsum('bgrqn,nj->bgqj', p_cmp, overlap)
        cur = (t // NSA_SLC_LEN)[:, None]
        j = jnp.arange(ns)[None, :]
        forced = (j == 0) | (j == cur) | (j == cur - 1)
        imp = jnp.where(forced, FORCE, jnp.where(j <= cur, imp, -FORCE))
        _, idx = lax.top_k(imp, topk)
        ks = gather(ks_blocks, idx).reshape(bsz, g_, NSA_QBLK, x_sel, d_)
        vs = gather(vs_blocks, idx).reshape(bsz, g_, NSA_QBLK, x_sel, d_)
        kpos = (idx[..., None] * NSA_SLC_LEN + jnp.arange(NSA_SLC_LEN)).reshape(bsz, g_, NSA_QBLK, x_sel)
        smask = (kpos <= t[None, None, :, None])[:, :, None]
        p_slc = _masked_softmax(jnp.einsum('bqgrd,bgqxd->bgrqx', qblk, ks) * scale, smask)
        o_slc = jnp.einsum('bgrqx,bgqxd->bqgrd', p_slc, vs)
        kw = lax.dynamic_slice_in_dim(kw_pad, s0, span, axis=1)
        vw = lax.dynamic_slice_in_dim(vw_pad, s0, span, axis=1)
        kp = (s0 - NSA_WIN + jnp.arange(span))[None, :]
        wmask = (kp <= t[:, None]) & (kp > t[:, None] - NSA_WIN) & (kp >= 0)
        p_win = _masked_softmax(jnp.einsum('bqgrd,bkgd->bgrqk', qblk, kw) * scale, wmask)
        o_win = jnp.einsum('bgrqk,bkgd->bqgrd', p_win, vw)
        return gblk[..., 0:1] * o_cmp + gblk[..., 1:2] * o_slc + gblk[..., 2:3] * o_win

    out = lax.map(query_block, jnp.arange(seq // NSA_QBLK))
    return jnp.moveaxis(out, 0, 1).reshape(bsz, seq, NSA_Q).astype(u.dtype)


def _swa_sinks_mixer(u, sinks):
    bsz, seq, _ = u.shape
    g_, r_, d_, t_ = SWA_KV, SWA_RPG, HEAD_DIM, SWA_QBLK
    nb = seq // t_
    q = u[..., :SWA_Q].reshape(bsz, nb, t_, g_, r_, d_)
    k = u[..., SWA_Q:SWA_Q + SWA_KVW].reshape(bsz, nb, t_, g_, d_)
    v = u[..., SWA_Q + SWA_KVW:].reshape(bsz, nb, t_, g_, d_)

    def band(a):
        prev = jnp.concatenate([jnp.zeros_like(a[:, :1]), a[:, :-1]], axis=1)
        return jnp.concatenate([prev, a], axis=2)

    kb, vb = band(k), band(v)
    s = jnp.einsum('bnqgrd,bnkgd->bngrqk', q, kb).astype(jnp.float32) * (HEAD_DIM ** -0.5)
    qpos = jnp.arange(t_)[:, None]
    kpos = jnp.arange(2 * t_)[None, :] - t_
    in_win = (kpos <= qpos) & (kpos > qpos - SWA_WIN)
    first = jnp.arange(nb)[:, None, None] == 0
    mask = (in_win[None] & ~(first & (kpos < 0)[None]))[None, :, None, None]
    sink = sinks.astype(jnp.float32).reshape(1, 1, g_, r_, 1, 1)
    s = jnp.where(mask, s, NEG)
    m = jnp.maximum(jnp.max(s, axis=-1, keepdims=True), sink)
    e = jnp.where(mask, jnp.exp(s - m), 0.0)
    p = e / (jnp.sum(e, axis=-1, keepdims=True) + jnp.exp(sink - m))
    o = jnp.einsum('bngrqk,bnkgd->bnqgrd', p, vb)
    return o.reshape(bsz, seq, SWA_Q).astype(u.dtype)


def _s5_mixer(u, a_re, a_im, log_dt, b_re, b_im, c_re, c_im, d_skip, glu_w, glu_b):
    bsz, seq, _ = u.shape
    f32 = jnp.float32
    uf = u.astype(f32).reshape(bsz, seq, S5_GROUPS, S5_GROUP_CH)
    lam = lax.complex(a_re.astype(f32), a_im.astype(f32))
    step = jnp.exp(log_dt.astype(f32))[:, None]
    lam_bar = jnp.exp(lam * step)
    b_bar = ((lam_bar - 1.0) / lam)[..., None] * lax.complex(b_re.astype(f32), b_im.astype(f32))
    bu = jnp.einsum('gph,blgh->blgp', b_bar, uf.astype(jnp.complex64))
    a = jnp.broadcast_to(lam_bar, bu.shape)

    def combine(e1, e2):
        a1, x1 = e1
        a2, x2 = e2
        return a1 * a2, a2 * x1 + x2

    _, states = lax.associative_scan(combine, (a, bu), axis=1)
    c = lax.complex(c_re.astype(f32), c_im.astype(f32))
    y = jnp.einsum('ghp,blgp->blgh', c, states).real + d_skip.astype(f32).reshape(S5_GROUPS, S5_GROUP_CH) * uf
    y = jax.nn.gelu(y.reshape(bsz, seq, S5_CH))
    y = y * jax.nn.sigmoid(y @ glu_w.astype(f32) + glu_b.astype(f32))
    return y.astype(u.dtype)


def _sq_relu_mlp(x, w_up, w_down):
    return jnp.square(jax.nn.relu(x @ w_up)) @ w_down


def setup_inputs(seed: int = 0) -> dict:
    key = jax.random.key(seed)
    k = jax.random.split(key, 32)
    f32 = jnp.float32

    def nrm(i, shape, scale):
        return scale * jax.random.normal(k[i], shape, f32)

    def uni(i, shape, lo, hi):
        return jax.random.uniform(k[i], shape, f32, lo, hi)

    ssd_dt = jnp.exp(uni(10, (N_EVEN, SSD_HEADS), math.log(DT_MIN), math.log(DT_MAX)))
    return {
        'x': nrm(0, (BATCH, SEQ, D_MODEL), 1.0),
        'norm_mix': 1.0 + nrm(1, (DEPTH, D_MODEL), 0.02),
        'norm_mlp': 1.0 + nrm(2, (DEPTH, D_MODEL), 0.02),
        'norm_final': 1.0 + nrm(3, (D_MODEL,), 0.02),
        'mlp_w_up': nrm(4, (DEPTH, D_MODEL, D_FF), D_MODEL ** -0.5),
        'mlp_w_down': nrm(5, (DEPTH, D_FF, D_MODEL), D_FF ** -0.5),
        'ev_w_in': nrm(6, (N_EVEN, D_MODEL, EVEN_IN), D_MODEL ** -0.5),
        'ev_w_out': nrm(7, (N_EVEN, EVEN_MIX, D_MODEL), EVEN_MIX ** -0.5),
        'ssd_conv_w': nrm(8, (N_EVEN, SSD_CONV, SSD_CONV_DIM), SSD_CONV ** -0.5),
        'ssd_conv_b': nrm(9, (N_EVEN, SSD_CONV_DIM), 0.01),
        'ssd_dt_bias': ssd_dt + jnp.log(-jnp.expm1(-ssd_dt)),
        'ssd_a_log': jnp.log(uni(11, (N_EVEN, SSD_HEADS), 1.0, 16.0)),
        'ssd_d': 1.0 + nrm(12, (N_EVEN, SSD_HEADS), 0.1),
        'ssd_norm': 1.0 + nrm(13, (N_EVEN, SSD_INNER), 0.02),
        'nsa_pe': nrm(14, (N_EVEN, 2, NSA_CMP_LEN, HEAD_DIM), 0.02),
        'nsa_cmp_w1': nrm(15, (N_EVEN, 2, NSA_CMP_LEN * HEAD_DIM, NSA_CMP_HIDDEN), (NSA_CMP_LEN * HEAD_DIM) ** -0.5),
        'nsa_cmp_b1': nrm(16, (N_EVEN, 2, NSA_CMP_HIDDEN), 0.01),
        'nsa_cmp_w2': nrm(17, (N_EVEN, 2, NSA_CMP_HIDDEN, HEAD_DIM), NSA_CMP_HIDDEN ** -0.5),
        'nsa_cmp_b2': nrm(18, (N_EVEN, 2, HEAD_DIM), 0.01),
        'od_w_in': nrm(19, (N_ODD, D_MODEL, ODD_IN), D_MODEL ** -0.5),
        'od_w_out': nrm(20, (N_ODD, ODD_MIX, D_MODEL), ODD_MIX ** -0.5),
        'swa_sinks': nrm(21, (N_ODD, SWA_HEADS), 0.5),
        's5_a_re': -0.5 + nrm(22, (N_ODD, S5_GROUPS, S5_STATE), 0.005),
        's5_a_im': jnp.broadcast_to(math.pi * jnp.arange(S5_STATE, dtype=f32), (N_ODD, S5_GROUPS, S5_STATE)),
        's5_log_dt': uni(23, (N_ODD, S5_GROUPS), math.log(DT_MIN), math.log(DT_MAX)),
        's5_b_re': nrm(24, (N_ODD, S5_GROUPS, S5_STATE, S5_GROUP_CH), (2 * S5_GROUP_CH) ** -0.5),
        's5_b_im': nrm(25, (N_ODD, S5_GROUPS, S5_STATE, S5_GROUP_CH), (2 * S5_GROUP_CH) ** -0.5),
        's5_c_re': nrm(26, (N_ODD, S5_GROUPS, S5_GROUP_CH, S5_STATE), (2 * S5_STATE) ** -0.5),
        's5_c_im': nrm(27, (N_ODD, S5_GROUPS, S5_GROUP_CH, S5_STATE), (2 * S5_STATE) ** -0.5),
        's5_d': nrm(28, (N_ODD, S5_CH), 1.0),
        's5_glu_w': nrm(29, (N_ODD, S5_CH, S5_CH), S5_CH ** -0.5),
        's5_glu_b': nrm(30, (N_ODD, S5_CH), 0.01),
    }


def reference(x, norm_mix, norm_mlp, norm_final, mlp_w_up, mlp_w_down,
              ev_w_in, ev_w_out, ssd_conv_w, ssd_conv_b, ssd_dt_bias, ssd_a_log, ssd_d, ssd_norm,
              nsa_pe, nsa_cmp_w1, nsa_cmp_b1, nsa_cmp_w2, nsa_cmp_b2,
              od_w_in, od_w_out, swa_sinks, s5_a_re, s5_a_im, s5_log_dt, s5_b_re, s5_b_im,
              s5_c_re, s5_c_im, s5_d, s5_glu_w, s5_glu_b):
    h = x
    for layer in range(DEPTH):
        hn = _rmsnorm(h, norm_mix[layer])
        i = layer // 2
        if layer % 2 == 0:
            u = hn @ ev_w_in[i]
            y_a = _ssd_mixer(u[..., :SSD_IN], ssd_conv_w[i], ssd_conv_b[i], ssd_dt_bias[i],
                             ssd_a_log[i], ssd_d[i], ssd_norm[i])
            y_b = _nsa_mixer(u[..., SSD_IN:], nsa_pe[i], nsa_cmp_w1[i], nsa_cmp_b1[i],
                             nsa_cmp_w2[i], nsa_cmp_b2[i])
            y = jnp.concatenate([y_a, y_b], axis=-1) @ ev_w_out[i]
        else:
            u = hn @ od_w_in[i]
            y_c = _swa_sinks_mixer(u[..., :SWA_IN], swa_sinks[i])
            y_d = _s5_mixer(u[..., SWA_IN:], s5_a_re[i], s5_a_im[i], s5_log_dt[i], s5_b_re[i], s5_b_im[i],
                            s5_c_re[i], s5_c_im[i], s5_d[i], s5_glu_w[i], s5_glu_b[i])
            y = jnp.concatenate([y_c, y_d], axis=-1) @ od_w_out[i]
        h = h + y.astype(h.dtype)
        h = h + _sq_relu_mlp(_rmsnorm(h, norm_mlp[layer]), mlp_w_up[layer], mlp_w_down[layer]).astype(h.dtype)
    return _rmsnorm(h, norm_final)
```

```python
import functools
import math

import jax
import jax.numpy as jnp
from jax import lax
from jax.experimental import pallas as pl
from jax.experimental.pallas import tpu as pltpu

F32 = jnp.float32
BF16 = jnp.bfloat16

EPS = 1e-6
NEG = -1e30
FORCE = 1e9
HEAD_DIM = 64
LANES = 128
VMEM_LIMIT_BYTES = 56 * 1024 * 1024

SSD_HEADS = 8
SSD_INNER = 512
SSD_GROUPS = 2
SSD_STATE = 128
SSD_CONV = 4
SSD_CHUNK = 128
SSD_CONV_DIM = 1024
SSD_IN = SSD_INNER + SSD_CONV_DIM + SSD_HEADS

NSA_HEADS = 8
NSA_KV = 2
NSA_RPG = 4
NSA_CMP_LEN = 32
NSA_CMP_STRIDE = 16
NSA_SLC_LEN = 64
NSA_TOPK = 16
NSA_WIN = 512
NSA_CMP_HIDDEN = 256
NSA_QBLK = 128
NSA_Q = 512
NSA_KVW = 128
NSA_KTILE = 512

SWA_RPG = 4
SWA_WIN = 128
SWA_Q = 512
SWA_KVW = 128

S5_CH = 512
S5_GROUP_CH = 16
S5_GROUPS = 32
S5_STATE = 64
S5_CHUNK = 32


def _cparams(*sem):
    return pltpu.CompilerParams(dimension_semantics=sem, vmem_limit_bytes=VMEM_LIMIT_BYTES)


def _full(shape):
    n = len(shape)
    return pl.BlockSpec(shape, lambda *_: (0,) * n)


def _dot(a, b):
    return jnp.dot(a, b, preferred_element_type=F32)


def _dot_nt(a, b):
    return lax.dot_general(a, b, (((1,), (1,)), ((), ())), preferred_element_type=F32)


def _split3(a):
    hi = a.astype(BF16)
    r1 = a - hi.astype(F32)
    mid = r1.astype(BF16)
    lo = (r1 - mid.astype(F32)).astype(BF16)
    return hi, mid, lo


def _dot_exact_rhs(a, b_exact):
    hi, mid, lo = _split3(a)
    return _dot(hi, b_exact) + _dot(mid, b_exact) + _dot(lo, b_exact)


def _dot_exact_lhs(a_exact, b):
    hi, mid, lo = _split3(b)
    return _dot(a_exact, hi) + _dot(a_exact, mid) + _dot(a_exact, lo)


def _rms(x, g):
    return x * lax.rsqrt(jnp.mean(x * x, axis=-1, keepdims=True) + EPS) * g


def _gelu_tanh(x):
    c = math.sqrt(2.0 / math.pi)
    return 0.5 * x * (1.0 + jnp.tanh(c * (x + 0.044715 * (x * x * x))))


def _sigmoid(x):
    return 1.0 / (1.0 + jnp.exp(-x))


def _proj_kernel(kinds, tm, h_ref, g_ref, *refs):
    n = len(kinds)
    w_refs, o_refs = refs[:n], refs[n:]
    yb = _rms(h_ref[...], g_ref[...]).astype(BF16)
    for kind, w_ref, o_ref in zip(kinds, w_refs, o_refs):
        if kind == "nat":
            o_ref[...] = _dot(yb, w_ref[...]).astype(o_ref.dtype)
        else:
            ot = _dot_nt(w_ref[...], yb)
            for j in range(tm // LANES):
                o_ref[j] = ot[:, j * LANES:(j + 1) * LANES].astype(o_ref.dtype)


def _norm_proj(h, g, segs, tm=512):
    m, d = h.shape
    kinds = tuple(s[0] for s in segs)
    ws, w_specs, out_shapes, out_specs = [], [], [], []
    for kind, w, dt in segs:
        n_out = w.shape[1]
        if kind == "nat":
            ws.append(w.astype(BF16))
            w_specs.append(_full((d, n_out)))
            out_shapes.append(jax.ShapeDtypeStruct((m, n_out), dt))
            out_specs.append(pl.BlockSpec((tm, n_out), lambda i: (i, 0)))
        else:
            ws.append(w.T.astype(BF16))
            w_specs.append(_full((n_out, d)))
            out_shapes.append(jax.ShapeDtypeStruct((m // LANES, n_out, LANES), dt))
            out_specs.append(pl.BlockSpec((tm // LANES, n_out, LANES), lambda i: (i, 0, 0)))
    return pl.pallas_call(
        functools.partial(_proj_kernel, kinds, tm),
        grid=(m // tm,),
        in_specs=[pl.BlockSpec((tm, d), lambda i: (i, 0)), _full((1, d))] + w_specs,
        out_specs=out_specs,
        out_shape=out_shapes,
        compiler_params=_cparams("parallel"),
        name="norm_proj",
    )(h, g.reshape(1, d), *ws)


def _mlp_kernel(final, h_ref, ya_ref, yb_ref, woa_ref, wob_ref, gm_ref, wup_ref, wdn_ref, gf_ref,
                o_ref, h2_s, xn_s, acc_s):
    j = pl.program_id(1)

    @pl.when(j == 0)
    def _():
        h2 = h_ref[...] + _dot(ya_ref[...], woa_ref[...]) + _dot(yb_ref[...], wob_ref[...])
        h2_s[...] = h2
        xn_s[...] = _rms(h2, gm_ref[...]).astype(BF16)
        acc_s[...] = jnp.zeros_like(acc_s)

    hid = jnp.square(jnp.maximum(_dot(xn_s[...], wup_ref[...]), 0.0))
    acc_s[...] += _dot(hid.astype(BF16), wdn_ref[...])

    @pl.when(j == pl.num_programs(1) - 1)
    def _():
        out = h2_s[...] + acc_s[...]
        if final:
            out = _rms(out, gf_ref[...])
        o_ref[...] = out


def _mix_out_mlp(h, ya, yb, w_out, g_mlp, w_up, w_down, g_final, final, tm=1024, tf=1024):
    m, d = h.shape
    dff = w_up.shape[1]
    na = ya.shape[1]
    nb = yb.shape[1]
    return pl.pallas_call(
        functools.partial(_mlp_kernel, final),
        grid=(m // tm, dff // tf),
        in_specs=[
            pl.BlockSpec((tm, d), lambda i, j: (i, 0)),
            pl.BlockSpec((tm, na), lambda i, j: (i, 0)),
            pl.BlockSpec((tm, nb), lambda i, j: (i, 0)),
            _full((na, d)), _full((nb, d)), _full((1, d)),
            pl.BlockSpec((d, tf), lambda i, j: (0, j)),
            pl.BlockSpec((tf, d), lambda i, j: (j, 0)),
            _full((1, d)),
        ],
        out_specs=pl.BlockSpec((tm, d), lambda i, j: (i, 0)),
        out_shape=jax.ShapeDtypeStruct((m, d), F32),
        scratch_shapes=[pltpu.VMEM((tm, d), F32), pltpu.VMEM((tm, d), BF16), pltpu.VMEM((tm, d), F32)],
        compiler_params=_cparams("parallel", "arbitrary"),
        name="out_proj_mlp",
    )(h, ya, yb, w_out[:na].astype(BF16), w_out[na:].astype(BF16), g_mlp.reshape(1, d),
      w_up.astype(BF16), w_down.astype(BF16), g_final.reshape(1, d))


def _ssd_kernel(u_ref, cw_ref, cb_ref, dtb_ref, alog_ref, dsk_ref, ng_ref, o_ref, xext_s, st_s):
    t = SSD_CHUNK
    c = pl.program_id(1)

    @pl.when(c == 0)
    def _():
        xext_s[0:8, :] = jnp.zeros((8, SSD_CONV_DIM), F32)
        st_s[...] = jnp.zeros_like(st_s)

    z = u_ref[:, 0:SSD_INNER]
    dt_raw = u_ref[:, SSD_INNER + SSD_CONV_DIM:]
    xext_s[8:8 + t, :] = u_ref[:, SSD_INNER:SSD_INNER + SSD_CONV_DIM]
    conv = cb_ref[...] + cw_ref[0:1, :] * xext_s[5:5 + t, :]
    for k in range(1, SSD_CONV):
        conv = conv + cw_ref[k:k + 1, :] * xext_s[5 + k:5 + k + t, :]
    xext_s[0:8, :] = xext_s[t:t + 8, :]
    xc = conv * _sigmoid(conv)
    xs = xc[:, 0:SSD_INNER]
    gn = SSD_GROUPS * SSD_STATE

    dtp = dt_raw + dtb_ref[...]
    dt = jnp.maximum(dtp, 0.0) + jnp.log1p(jnp.exp(-jnp.abs(dtp)))
    a = -jnp.exp(alog_ref[...])
    da = dt * a

    row = lax.broadcasted_iota(jnp.int32, (t, t), 0)
    col = lax.broadcasted_iota(jnp.int32, (t, t), 1)
    causal = col <= row
    tril = jnp.where(causal, 1.0, 0.0).astype(BF16)
    a_cum = _dot_exact_lhs(tril, da)
    er = lax.broadcasted_iota(jnp.int32, (LANES, SSD_INNER), 0)
    ec = lax.broadcasted_iota(jnp.int32, (LANES, SSD_INNER), 1)
    expand = jnp.where((ec >> 6) == er, 1.0, 0.0).astype(BF16)
    a_cum_x = _dot_exact_rhs(a_cum, expand)
    dt_x = _dot_exact_rhs(dt, expand)
    a_cum_t = a_cum.T
    a_last_x = a_cum_x[t - 1:t, :]
    decay_end_x = jnp.exp(a_last_x - a_cum_x)
    decay_in_x = jnp.exp(a_cum_x)
    chunk_decay_x = jnp.exp(a_last_x)

    xd = xs * dt_x
    xd_end = (xd * decay_end_x).astype(BF16)
    xd_b = xd.astype(BF16)
    lane = lax.broadcasted_iota(jnp.int32, (t, LANES), 1)
    first_half = lane < HEAD_DIM

    pieces = []
    for g in range(SSD_GROUPS):
        bm = xc[:, SSD_INNER + g * SSD_STATE:SSD_INNER + (g + 1) * SSD_STATE]
        cm = xc[:, SSD_INNER + gn + g * SSD_STATE:SSD_INNER + gn + (g + 1) * SSD_STATE].astype(BF16)
        bm_t = bm.T.astype(BF16)
        cb = _dot_nt(cm, bm.astype(BF16))
        for pr in range(2):
            i = g * 2 + pr
            sl = slice(i * LANES, (i + 1) * LANES)
            ms = []
            for hh in range(2):
                h = 2 * i + hh
                seg = a_cum[:, h:h + 1] - a_cum_t[h:h + 1, :]
                dec = jnp.exp(jnp.where(causal, seg, NEG))
                ms.append((cb * dec).astype(BF16))
            y_diag = jnp.where(first_half, _dot(ms[0], xd_b[:, sl]), _dot(ms[1], xd_b[:, sl]))
            st = st_s[i]
            y_off = _dot(cm, st.astype(BF16)) * decay_in_x[:, sl]
            st_s[i] = st * chunk_decay_x[:, sl] + _dot(bm_t, xd_end[:, sl])
            pieces.append(y_diag + y_off)
    y = jnp.concatenate(pieces, axis=1) + xs * dsk_ref[...]
    y = y * (z * _sigmoid(z))
    half = SSD_INNER // SSD_GROUPS
    outs = [_rms(y[:, g * half:(g + 1) * half], ng_ref[:, g * half:(g + 1) * half]) for g in range(SSD_GROUPS)]
    o_ref[...] = jnp.concatenate(outs, axis=1).astype(o_ref.dtype)


def _ssd_mixer(u_ssd, bsz, seq, conv_w, conv_b, dt_bias, a_log, d_skip, norm_g):
    m, width = u_ssd.shape
    nch = seq // SSD_CHUNK
    pad = LANES - SSD_HEADS

    def padded(v):
        return jnp.concatenate([v.astype(F32), jnp.zeros((pad,), F32)]).reshape(1, LANES)

    return pl.pallas_call(
        _ssd_kernel,
        grid=(bsz, nch),
        in_specs=[
            pl.BlockSpec((SSD_CHUNK, width), lambda b, c: (b * nch + c, 0)),
            _full((SSD_CONV, SSD_CONV_DIM)), _full((1, SSD_CONV_DIM)),
            _full((1, LANES)), _full((1, LANES)), _full((1, SSD_INNER)), _full((1, SSD_INNER)),
        ],
        out_specs=pl.BlockSpec((SSD_CHUNK, SSD_INNER), lambda b, c: (b * nch + c, 0)),
        out_shape=jax.ShapeDtypeStruct((m, SSD_INNER), BF16),
        scratch_shapes=[pltpu.VMEM((SSD_CHUNK + 8, SSD_CONV_DIM), F32),
                        pltpu.VMEM((SSD_HEADS // 2, SSD_STATE, LANES), F32)],
        compiler_params=_cparams("arbitrary", "arbitrary"),
        name="ssd_mixer",
    )(u_ssd, conv_w.astype(F32), conv_b.reshape(1, -1).astype(F32), padded(dt_bias), padded(a_log),
      jnp.repeat(d_skip.astype(F32), HEAD_DIM).reshape(1, SSD_INNER), norm_g.reshape(1, SSD_INNER).astype(F32))


def _nsa_compress_kernel(x_ref, pe_ref, w1_ref, b1_ref, w2_ref, b2_ref, o_ref):
    x = x_ref[0, 0, 0]
    npc = x.shape[0]
    half = NSA_CMP_STRIDE * HEAD_DIM
    top = _dot((x + pe_ref[0, :, 0:half]).astype(BF16), w1_ref[0, 0:half, :])
    bot = _dot((x + pe_ref[0, :, half:]).astype(BF16), w1_ref[0, half:, :])
    pre = top + pltpu.roll(bot, npc - 1, 0) + b1_ref[0]
    hid = _gelu_tanh(pre).astype(BF16)
    out = _dot(hid, w2_ref[0]) + b2_ref[0]
    rowi = lax.broadcasted_iota(jnp.int32, out.shape, 0)
    o_ref[0, 0, 0] = jnp.where(rowi < npc - 1, out, 0.0)


def _nsa_compress(kvc, bsz, seq, pe, w1, b1, w2, b2):
    npc = seq // NSA_CMP_STRIDE
    flat = NSA_CMP_STRIDE * HEAD_DIM
    x = kvc.reshape(bsz, npc, NSA_CMP_STRIDE, 2, NSA_KV, HEAD_DIM)
    x = x.transpose(0, 3, 4, 1, 2, 5).reshape(bsz, 2, NSA_KV, npc, flat)
    return pl.pallas_call(
        _nsa_compress_kernel,
        grid=(bsz, 2, NSA_KV),
        in_specs=[
            pl.BlockSpec((1, 1, 1, npc, flat), lambda b, s, g: (b, s, g, 0, 0)),
            pl.BlockSpec((1, 1, 2 * flat), lambda b, s, g: (s, 0, 0)),
            pl.BlockSpec((1, 2 * flat, NSA_CMP_HIDDEN), lambda b, s, g: (s, 0, 0)),
            pl.BlockSpec((1, 1, NSA_CMP_HIDDEN), lambda b, s, g: (s, 0, 0)),
            pl.BlockSpec((1, NSA_CMP_HIDDEN, HEAD_DIM), lambda b, s, g: (s, 0, 0)),
            pl.BlockSpec((1, 1, HEAD_DIM), lambda b, s, g: (s, 0, 0)),
        ],
        out_specs=pl.BlockSpec((1, 1, 1, npc, HEAD_DIM), lambda b, s, g: (b, s, g, 0, 0)),
        out_shape=jax.ShapeDtypeStruct((bsz, 2, NSA_KV, npc, HEAD_DIM), F32),
        compiler_params=_cparams("parallel", "parallel", "parallel"),
        name="nsa_compress",
    )(x, pe.reshape(2, 1, 2 * flat).astype(F32), w1.astype(BF16), b1.reshape(2, 1, -1).astype(F32),
      w2.astype(BF16), b2.reshape(2, 1, -1).astype(F32))


def _softmax_cols(s, mask):
    s = jnp.where(mask, s, NEG)
    mx = jnp.max(s, axis=0, keepdims=True)
    e = jnp.where(mask, jnp.exp(s - mx), 0.0)
    den = jnp.sum(e, axis=0, keepdims=True)
    return e * jnp.where(den > 0.0, 1.0 / den, 0.0)


def _nsa_kernel(q_ref, kc_ref, vc_ref, ks_ref, kw_ref, vs_ref, vw_ref, g_ref, o_ref, bias_s):
    g = pl.program_id(1)
    qb = pl.program_id(2)
    nqt = NSA_RPG * NSA_QBLK
    s0 = qb * NSA_QBLK
    ncp = kc_ref.shape[2]
    ns = ncp // 4

    qcat = jnp.concatenate([q_ref[0, r * HEAD_DIM:(r + 1) * HEAD_DIM, :] for r in range(NSA_RPG)], axis=1)
    rowq = lax.broadcasted_iota(jnp.int32, (2 * HEAD_DIM, 1), 0)
    qext = jnp.where((rowq // HEAD_DIM) == g, jnp.concatenate([qcat, qcat], axis=0), jnp.zeros((), BF16))
    qpos = s0 + lax.broadcasted_iota(jnp.int32, (1, NSA_QBLK), 1)

    sc = _dot(kc_ref[0, 0], qcat)
    rc = lax.broadcasted_iota(jnp.int32, (ncp, 1), 0)
    ncmp = 4 * (rc % ns) + rc // ns
    cmask = (ncmp * NSA_CMP_STRIDE + (NSA_CMP_LEN - 1)) <= qpos
    psum = jnp.zeros((ncp, NSA_QBLK), F32)
    p_all = []
    for r in range(NSA_RPG):
        p = _softmax_cols(sc[:, r * NSA_QBLK:(r + 1) * NSA_QBLK], cmask)
        psum = psum + p
        p_all.append(p.astype(BF16))
    o_cmp = _dot(vc_ref[0, 0], jnp.concatenate(p_all, axis=1))

    p3 = psum[3 * ns:4 * ns]
    rj = lax.broadcasted_iota(jnp.int32, (ns, NSA_QBLK), 0)
    p3_prev = jnp.where(rj >= 1, pltpu.roll(p3, 1, 0), 0.0)
    imp = psum[0:ns] + psum[ns:2 * ns] + psum[2 * ns:3 * ns] + p3 + p3_prev
    cur = qpos // NSA_SLC_LEN
    forced = (rj == 0) | (rj == cur) | (rj == cur - 1)
    imp = jnp.where(forced, FORCE, jnp.where(rj <= cur, imp, -FORCE))
    rjf = rj.astype(F32)
    sel = jnp.zeros((ns, NSA_QBLK), F32)
    for _ in range(min(NSA_TOPK, ns)):
        mx = jnp.max(imp, axis=0, keepdims=True)
        first = jnp.min(jnp.where(imp == mx, rjf, float(ns)), axis=0, keepdims=True)
        pick = rjf == first
        sel = jnp.where(pick, 1.0, sel)
        imp = jnp.where(pick, -3.0e38, imp)
    bias_s[...] = jnp.where(sel > 0.0, 0.0, NEG)

    kt_diag = s0 // NSA_KTILE
    blocks_per_tile = NSA_KTILE // NSA_SLC_LEN
    vtiles = NSA_KTILE // LANES

    def tile_scores(kt):
        k0 = pl.multiple_of(kt * NSA_KTILE, NSA_KTILE)
        s = _dot(ks_ref[pl.ds(k0, NSA_KTILE), :], qext)
        b8 = bias_s[pl.ds(pl.multiple_of(kt * blocks_per_tile, blocks_per_tile), blocks_per_tile), :]
        bias = jnp.concatenate(
            [jnp.broadcast_to(b8[jj:jj + 1, :], (NSA_SLC_LEN, NSA_QBLK)) for jj in range(blocks_per_tile)], axis=0)
        return s + jnp.concatenate([bias] * NSA_RPG, axis=1)

    def tile_update(kt, s, carry):
        m_run, l_run, acc = carry
        m_new = jnp.maximum(m_run, jnp.max(s, axis=0, keepdims=True))
        alpha = jnp.exp(m_run - m_new)
        p = jnp.exp(s - m_new)
        l_new = alpha * l_run + jnp.sum(p, axis=0, keepdims=True)
        vt = jnp.concatenate([vs_ref[kt * vtiles + i] for i in range(vtiles)], axis=1)
        acc_new = alpha * acc + _dot(vt, p.astype(BF16))
        return m_new, l_new, acc_new

    def full_tile(kt, carry):
        return tile_update(kt, tile_scores(kt), carry)

    init = (jnp.full((1, nqt), NEG, F32), jnp.zeros((1, nqt), F32), jnp.zeros((HEAD_DIM, nqt), F32))
    carry = lax.fori_loop(0, kt_diag, full_tile, init)
    kpos = kt_diag * NSA_KTILE + lax.broadcasted_iota(jnp.int32, (NSA_KTILE, 1), 0)
    causal = jnp.concatenate([kpos <= qpos] * NSA_RPG, axis=1)
    s_diag = jnp.where(causal, tile_scores(kt_diag), NEG)
    _, l_fin, acc_fin = tile_update(kt_diag, s_diag, carry)
    o_slc = acc_fin * (1.0 / l_fin)

    span = NSA_WIN + NSA_QBLK
    start = pl.multiple_of(jnp.maximum(s0 - NSA_WIN, 0), NSA_QBLK)
    sw = _dot(kw_ref[pl.ds(start, span), :], qext)
    kp = start + lax.broadcasted_iota(jnp.int32, (span, 1), 0)
    wmask = (kp <= qpos) & (kp > qpos - NSA_WIN)
    pw = jnp.concatenate(
        [_softmax_cols(sw[:, r * NSA_QBLK:(r + 1) * NSA_QBLK], wmask).astype(BF16) for r in range(NSA_RPG)], axis=1)
    sblk = start // LANES
    vwt = jnp.concatenate([vw_ref[sblk + i] for i in range(span // LANES)], axis=1)
    o_win = _dot(vwt, pw)

    gates = _sigmoid(g_ref[0])
    for r in range(NSA_RPG):
        sl = slice(r * NSA_QBLK, (r + 1) * NSA_QBLK)
        out = (gates[r:r + 1, :] * o_cmp[:, sl] + gates[NSA_RPG + r:NSA_RPG + r + 1, :] * o_slc[:, sl]
               + gates[2 * NSA_RPG + r:2 * NSA_RPG + r + 1, :] * o_win[:, sl])
        o_ref[0, r * HEAD_DIM:(r + 1) * HEAD_DIM, :] = out.astype(o_ref.dtype)


def _nsa_mixer(q_tt, kvc, k_slc, k_win, v_tt, g_tt, bsz, seq, pe, w1, b1, w2, b2):
    nqb = seq // NSA_QBLK
    npc = seq // NSA_CMP_STRIDE
    ns = seq // NSA_SLC_LEN
    cmp_out = _nsa_compress(kvc, bsz, seq, pe, w1, b1, w2, b2)
    perm = cmp_out.reshape(bsz, 2, NSA_KV, ns, 4, HEAD_DIM).transpose(0, 1, 2, 4, 3, 5).reshape(
        bsz, 2, NSA_KV, npc, HEAD_DIM)
    kc = perm[:, 0].astype(BF16)
    vc_t = perm[:, 1].transpose(0, 1, 3, 2).astype(BF16)
    gq = NSA_RPG * HEAD_DIM
    return pl.pallas_call(
        _nsa_kernel,
        grid=(bsz, NSA_KV, nqb),
        in_specs=[
            pl.BlockSpec((1, gq, LANES), lambda b, g, q: (b * nqb + q, g, 0)),
            pl.BlockSpec((1, 1, npc, HEAD_DIM), lambda b, g, q: (b, g, 0, 0)),
            pl.BlockSpec((1, 1, HEAD_DIM, npc), lambda b, g, q: (b, g, 0, 0)),
            pl.BlockSpec((seq, NSA_KVW), lambda b, g, q: (b, 0)),
            pl.BlockSpec((seq, NSA_KVW), lambda b, g, q: (b, 0)),
            pl.BlockSpec((nqb, HEAD_DIM, LANES), lambda b, g, q: (b, g, 0)),
            pl.BlockSpec((nqb, HEAD_DIM, LANES), lambda b, g, q: (b, NSA_KV + g, 0)),
            pl.BlockSpec((1, 16, LANES), lambda b, g, q: (b * nqb + q, g, 0)),
        ],
        out_specs=pl.BlockSpec((1, gq, LANES), lambda b, g, q: (b * nqb + q, g, 0)),
        out_shape=jax.ShapeDtypeStruct((bsz * nqb, NSA_Q, LANES), BF16),
        scratch_shapes=[pltpu.VMEM((ns, NSA_QBLK), F32)],
        compiler_params=_cparams("arbitrary", "arbitrary", "arbitrary"),
        name="nsa_attention",
    )(q_tt, kc, vc_t, k_slc, k_win, v_tt, v_tt, g_tt)


def _swa_kernel(q_ref, kp_ref, kc_ref, vp_ref, vc_ref, sink_ref, o_ref):
    qb = pl.program_id(1)
    t = SWA_WIN
    kband = jnp.concatenate([kp_ref[...], kc_ref[...]], axis=0)
    krel = lax.broadcasted_iota(jnp.int32, (2 * t, 1), 0) - t
    qrel = lax.broadcasted_iota(jnp.int32, (1, t), 1)
    lowest = jnp.where(qb > 0, -t, 0)
    mask = (krel <= qrel) & (krel > qrel - SWA_WIN) & (krel >= lowest)
    for g in range(2):
        rows = slice(g * SWA_RPG * HEAD_DIM, (g + 1) * SWA_RPG * HEAD_DIM)
        qg = q_ref[0, rows, :]
        qcat = jnp.concatenate([qg[r * HEAD_DIM:(r + 1) * HEAD_DIM, :] for r in range(SWA_RPG)], axis=1)
        zq = jnp.zeros_like(qcat)
        qext = jnp.concatenate([qcat, zq] if g == 0 else [zq, qcat], axis=0)
        s = _dot(kband, qext)
        ps = []
        for r in range(SWA_RPG):
            h = g * SWA_RPG + r
            sink = sink_ref[h:h + 1, :]
            sr = jnp.where(mask, s[:, r * t:(r + 1) * t], NEG)
            mx = jnp.maximum(jnp.max(sr, axis=0, keepdims=True), sink)
            e = jnp.where(mask, jnp.exp(sr - mx), 0.0)
            den = jnp.sum(e, axis=0, keepdims=True) + jnp.exp(sink - mx)
            ps.append((e / den).astype(BF16))
        vband = jnp.concatenate([vp_ref[0, g * HEAD_DIM:(g + 1) * HEAD_DIM, :],
                                 vc_ref[0, g * HEAD_DIM:(g + 1) * HEAD_DIM, :]], axis=1)
        og = _dot(vband, jnp.concatenate(ps, axis=1))
        for r in range(SWA_RPG):
            h = g * SWA_RPG + r
            o_ref[0, h * HEAD_DIM:(h + 1) * HEAD_DIM, :] = og[:, r * t:(r + 1) * t].astype(o_ref.dtype)


def _swa_mixer(q_tt, k_nat, v_tt, sinks, bsz, seq):
    nqb = seq // SWA_WIN
    sink_rows = jnp.broadcast_to(sinks.astype(F32)[:, None], (sinks.shape[0], LANES))
    prev = lambda b, q: b * nqb + jnp.maximum(q - 1, 0)
    return pl.pallas_call(
        _swa_kernel,
        grid=(bsz, nqb),
        in_specs=[
            pl.BlockSpec((1, SWA_Q, LANES), lambda b, q: (b * nqb + q, 0, 0)),
            pl.BlockSpec((SWA_WIN, SWA_KVW), lambda b, q: (prev(b, q), 0)),
            pl.BlockSpec((SWA_WIN, SWA_KVW), lambda b, q: (b * nqb + q, 0)),
            pl.BlockSpec((1, SWA_KVW, LANES), lambda b, q: (prev(b, q), 0, 0)),
            pl.BlockSpec((1, SWA_KVW, LANES), lambda b, q: (b * nqb + q, 0, 0)),
            _full(sink_rows.shape),
        ],
        out_specs=pl.BlockSpec((1, SWA_Q, LANES), lambda b, q: (b * nqb + q, 0, 0)),
        out_shape=jax.ShapeDtypeStruct((bsz * nqb, SWA_Q, LANES), BF16),
        compiler_params=_cparams("parallel", "parallel"),
        name="swa_attention",
    )(q_tt, k_nat, k_nat, v_tt, v_tt, sink_rows)


def _s5_params(a_re, a_im, log_dt, b_re, b_im, c_re, c_im, n_chunks):
    f = F32
    t = S5_CHUNK
    step = jnp.exp(log_dt.astype(f))[:, None]
    lr, li = a_re.astype(f), a_im.astype(f)

    def lam_pow(tau):
        tau = tau.astype(f)[:, None, None]
        mag = jnp.exp(lr * step * tau)
        ang = li * step * tau
        return mag * jnp.cos(ang), mag * jnp.sin(ang)

    lb_r, lb_i = (v[0] for v in lam_pow(jnp.ones((1,))))
    nr, ni = lb_r - 1.0, lb_i
    den = lr * lr + li * li
    fr, fi = (nr * lr + ni * li) / den, (ni * lr - nr * li) / den
    br, bi = b_re.astype(f), b_im.astype(f)
    bb_r = fr[..., None] * br - fi[..., None] * bi
    bb_i = fr[..., None] * bi + fi[..., None] * br
    cr, ci = c_re.astype(f), c_im.astype(f)

    pr, pi = lam_pow(jnp.arange(t + 1))
    cl_r = cr[None] * pr[:, :, None, :] - ci[None] * pi[:, :, None, :]
    cl_i = cr[None] * pi[:, :, None, :] + ci[None] * pr[:, :, None, :]
    kern = jnp.einsum("tghp,gpk->gthk", cl_r[:t], bb_r, precision="highest") - jnp.einsum(
        "tghp,gpk->gthk", cl_i[:t], bb_i, precision="highest")
    lag = jnp.arange(t)[None, :] - jnp.arange(t)[:, None]
    toep = jnp.where((lag >= 0)[None, :, :, None, None], kern[:, jnp.clip(lag, 0, t - 1)], 0.0)
    toep = toep.transpose(0, 1, 4, 2, 3).reshape(S5_GROUPS, t * S5_GROUP_CH, t * S5_GROUP_CH)
    rr, ri = pr[t - 1 - jnp.arange(t)], pi[t - 1 - jnp.arange(t)]
    bs_r = rr[..., None] * bb_r[None] - ri[..., None] * bb_i[None]
    bs_i = rr[..., None] * bb_i[None] + ri[..., None] * bb_r[None]
    bs = jnp.concatenate([bs_r, bs_i], axis=2)
    bs = bs.transpose(1, 0, 3, 2).reshape(S5_GROUPS, t * S5_GROUP_CH, 2 * S5_STATE)
    cs = jnp.concatenate([cl_r[1:], -cl_i[1:]], axis=3)
    cs = cs.transpose(1, 3, 0, 2).reshape(S5_GROUPS, 2 * S5_STATE, t * S5_GROUP_CH)
    ar, ai = pr[t], pi[t]
    a1, a2 = [], []
    k = 1
    while k < n_chunks:
        a1.append(jnp.concatenate([ar, ar], axis=1))
        a2.append(jnp.concatenate([-ai, ai], axis=1))
        ar, ai = ar * ar - ai * ai, 2.0 * ar * ai
        k *= 2
    a1 = jnp.stack(a1, axis=1)
    a2 = jnp.stack(a2, axis=1)
    return toep.astype(BF16), bs.astype(BF16), cs.astype(BF16), a1, a2


def _s5_kernel(bsz, u_ref, toep_ref, bs_ref, cs_ref, a1_ref, a2_ref, o_ref):
    u = u_ref[0]
    n_all = u.shape[0]
    n = n_all // bsz
    sc = _dot(u, bs_ref[0])
    rowi = lax.broadcasted_iota(jnp.int32, (n, 2 * S5_STATE), 0)
    h_in = []
    for b in range(bsz):
        x = sc[b * n:(b + 1) * n]
        k, step = 1, 0
        while k < n:
            xs = jnp.where(rowi >= k, pltpu.roll(x, k, 0), 0.0)
            xs_sw = pltpu.roll(xs, S5_STATE, 1)
            x = x + a1_ref[0, step:step + 1, :] * xs + a2_ref[0, step:step + 1, :] * xs_sw
            k *= 2
            step += 1
        h_in.append(jnp.where(rowi >= 1, pltpu.roll(x, 1, 0), 0.0))
    h_in = jnp.concatenate(h_in, axis=0)
    hi = h_in.astype(BF16)
    lo = (h_in - hi.astype(F32)).astype(BF16)
    o_ref[0] = _dot(u, toep_ref[0]) + _dot(hi, cs_ref[0]) + _dot(lo, cs_ref[0])


def _s5_glu_kernel(y_ref, u_ref, d_ref, w_ref, b_ref, o_ref):
    y = _gelu_tanh(y_ref[...] + d_ref[...] * u_ref[...])
    gate = _sigmoid(_dot(y.astype(BF16), w_ref[...]) + b_ref[...])
    o_ref[...] = (y * gate).astype(o_ref.dtype)


def _s5_mixer(u5, bsz, seq, a_re, a_im, log_dt, b_re, b_im, c_re, c_im, d_skip, glu_w, glu_b, tm=1024):
    m = u5.shape[0]
    t = S5_CHUNK
    nch = m // t
    width = t * S5_GROUP_CH
    toep, bs, cs, a1, a2 = _s5_params(a_re, a_im, log_dt, b_re, b_im, c_re, c_im, nch // bsz)
    nsteps = a1.shape[1]
    ug = u5.reshape(nch, t, S5_GROUPS, S5_GROUP_CH).transpose(2, 0, 1, 3).reshape(S5_GROUPS, nch, width).astype(BF16)
    yg = pl.pallas_call(
        functools.partial(_s5_kernel, bsz),
        grid=(S5_GROUPS,),
        in_specs=[
            pl.BlockSpec((1, nch, width), lambda g: (g, 0, 0)),
            pl.BlockSpec((1, width, width), lambda g: (g, 0, 0)),
            pl.BlockSpec((1, width, 2 * S5_STATE), lambda g: (g, 0, 0)),
            pl.BlockSpec((1, 2 * S5_STATE, width), lambda g: (g, 0, 0)),
            pl.BlockSpec((1, nsteps, 2 * S5_STATE), lambda g: (g, 0, 0)),
            pl.BlockSpec((1, nsteps, 2 * S5_STATE), lambda g: (g, 0, 0)),
        ],
        out_specs=pl.BlockSpec((1, nch, width), lambda g: (g, 0, 0)),
        out_shape=jax.ShapeDtypeStruct((S5_GROUPS, nch, width), F32),
        compiler_params=_cparams("parallel"),
        name="s5_scan",
    )(ug, toep, bs, cs, a1, a2)
    y = yg.reshape(S5_GROUPS, nch, t, S5_GROUP_CH).transpose(1, 2, 0, 3).reshape(m, S5_CH)
    return pl.pallas_call(
        _s5_glu_kernel,
        grid=(m // tm,),
        in_specs=[
            pl.BlockSpec((tm, S5_CH), lambda i: (i, 0)),
            pl.BlockSpec((tm, S5_CH), lambda i: (i, 0)),
            _full((1, S5_CH)), _full((S5_CH, S5_CH)), _full((1, S5_CH)),
        ],
        out_specs=pl.BlockSpec((tm, S5_CH), lambda i: (i, 0)),
        out_shape=jax.ShapeDtypeStruct((m, S5_CH), BF16),
        compiler_params=_cparams("parallel"),
        name="s5_glu",
    )(y, u5, d_skip.reshape(1, S5_CH).astype(F32), glu_w.astype(BF16), glu_b.reshape(1, S5_CH).astype(F32))


def _tt_to_nat(x_tt):
    nb, feat, lanes = x_tt.shape
    return x_tt.transpose(0, 2, 1).reshape(nb * lanes, feat)


def _even_mixers(h, bsz, seq, g_mix, w_in, conv_w, conv_b, dt_bias, a_log, d_skip, norm_g, pe, w1, b1, w2, b2):
    d = h.shape[1]
    scale = HEAD_DIM ** -0.5
    o = SSD_IN
    w_ssd = jnp.concatenate([w_in[:, :SSD_IN], jnp.zeros((d, LANES - SSD_HEADS), w_in.dtype)], axis=1)
    w_q = w_in[:, o:o + NSA_Q] * scale
    kv = [w_in[:, o + NSA_Q + i * NSA_KVW:o + NSA_Q + (i + 1) * NSA_KVW] for i in range(6)]
    w_gate = w_in[:, o + NSA_Q + 6 * NSA_KVW:].reshape(d, NSA_KV, NSA_RPG, 3).transpose(0, 1, 3, 2)
    w_gate = jnp.concatenate([w_gate.reshape(d, NSA_KV, 12), jnp.zeros((d, NSA_KV, 4), w_in.dtype)],
                             axis=2).reshape(d, NSA_KV * 16)
    segs = [
        ("nat", w_ssd, F32),
        ("tt", w_q, BF16),
        ("nat", jnp.concatenate([kv[0], kv[1]], axis=1), F32),
        ("nat", kv[2], BF16),
        ("nat", kv[4], BF16),
        ("tt", jnp.concatenate([kv[3], kv[5]], axis=1), BF16),
        ("tt", w_gate, F32),
    ]
    u_ssd, q_tt, kvc, k_slc, k_win, v_tt, g_tt = _norm_proj(h, g_mix, segs)
    ya = _ssd_mixer(u_ssd, bsz, seq, conv_w, conv_b, dt_bias, a_log, d_skip, norm_g)
    yb_tt = _nsa_mixer(q_tt, kvc, k_slc, k_win, v_tt, g_tt, bsz, seq, pe, w1, b1, w2, b2)
    return ya, _tt_to_nat(yb_tt)


def _odd_mixers(h, bsz, seq, g_mix, w_in, sinks, a_re, a_im, log_dt, b_re, b_im, c_re, c_im, d_skip, glu_w, glu_b):
    scale = HEAD_DIM ** -0.5
    segs = [
        ("tt", w_in[:, :SWA_Q] * scale, BF16),
        ("nat", w_in[:, SWA_Q:SWA_Q + SWA_KVW], BF16),
        ("tt", w_in[:, SWA_Q + SWA_KVW:SWA_Q + 2 * SWA_KVW], BF16),
        ("nat", w_in[:, SWA_Q + 2 * SWA_KVW:], F32),
    ]
    q_tt, k_nat, v_tt, u5 = _norm_proj(h, g_mix, segs)
    yc_tt = _swa_mixer(q_tt, k_nat, v_tt, sinks, bsz, seq)
    yd = _s5_mixer(u5, bsz, seq, a_re, a_im, log_dt, b_re, b_im, c_re, c_im, d_skip, glu_w, glu_b)
    return _tt_to_nat(yc_tt), yd


def kernel(x, norm_mix, norm_mlp, norm_final, mlp_w_up, mlp_w_down, ev_w_in, ev_w_out, ssd_conv_w, ssd_conv_b,
           ssd_dt_bias, ssd_a_log, ssd_d, ssd_norm, nsa_pe, nsa_cmp_w1, nsa_cmp_b1, nsa_cmp_w2, nsa_cmp_b2,
           od_w_in, od_w_out, swa_sinks, s5_a_re, s5_a_im, s5_log_dt, s5_b_re, s5_b_im, s5_c_re, s5_c_im,
           s5_d, s5_glu_w, s5_glu_b):
    bsz, seq, d = x.shape
    depth = norm_mix.shape[0]
    assert seq % NSA_KTILE == 0 and seq >= NSA_WIN + NSA_QBLK
    h = x.reshape(bsz * seq, d)
    for layer in range(depth):
        i = layer // 2
        if layer % 2 == 0:
            ya, yb = _even_mixers(h, bsz, seq, norm_mix[layer], ev_w_in[i], ssd_conv_w[i], ssd_conv_b[i],
                                  ssd_dt_bias[i], ssd_a_log[i], ssd_d[i], ssd_norm[i], nsa_pe[i],
                                  nsa_cmp_w1[i], nsa_cmp_b1[i], nsa_cmp_w2[i], nsa_cmp_b2[i])
            w_out = ev_w_out[i]
        else:
            ya, yb = _odd_mixers(h, bsz, seq, norm_mix[layer], od_w_in[i], swa_sinks[i], s5_a_re[i], s5_a_im[i],
                                 s5_log_dt[i], s5_b_re[i], s5_b_im[i], s5_c_re[i], s5_c_im[i], s5_d[i],
                                 s5_glu_w[i], s5_glu_b[i])
            w_out = od_w_out[i]
        h = _mix_out_mlp(h, ya, yb, w_out, norm_mlp[layer], mlp_w_up[layer], mlp_w_down[layer], norm_final,
                         final=(layer == depth - 1))
    return h.reshape(bsz, seq, d)
```

```python
import functools
import math

import jax
import jax.numpy as jnp
from jax import lax
from jax.experimental import pallas as pl
from jax.experimental.pallas import tpu as pltpu

F32 = jnp.float32
BF16 = jnp.bfloat16

EPS = 1e-6
NEG = -1e30
FORCE = 1e9
HEAD_DIM = 64
LANES = 128
VMEM_LIMIT_BYTES = 56 * 1024 * 1024

SSD_HEADS = 8
SSD_INNER = 512
SSD_GROUPS = 2
SSD_STATE = 128
SSD_CONV = 4
SSD_CHUNK = 128
SSD_CONV_DIM = 1024
SSD_IN = SSD_INNER + SSD_CONV_DIM + SSD_HEADS

NSA_HEADS = 8
NSA_KV = 2
NSA_RPG = 4
NSA_CMP_LEN = 32
NSA_CMP_STRIDE = 16
NSA_SLC_LEN = 64
NSA_TOPK = 16
NSA_WIN = 512
NSA_CMP_HIDDEN = 256
NSA_QBLK = 128
NSA_Q = 512
NSA_KVW = 128
NSA_KTILE = 512

SWA_RPG = 4
SWA_WIN = 128
SWA_Q = 512
SWA_KVW = 128

S5_CH = 512
S5_GROUP_CH = 16
S5_GROUPS = 32
S5_STATE = 64
S5_CHUNK = 32


def _cparams(*sem):
    return pltpu.CompilerParams(dimension_semantics=sem, vmem_limit_bytes=VMEM_LIMIT_BYTES)


def _full(shape):
    n = len(shape)
    return pl.BlockSpec(shape, lambda *_: (0,) * n)


def _dot(a, b):
    return jnp.dot(a, b, preferred_element_type=F32)


def _dot_nt(a, b):
    return lax.dot_general(a, b, (((1,), (1,)), ((), ())), preferred_element_type=F32)


def _split3(a):
    hi = a.astype(BF16)
    r1 = a - hi.astype(F32)
    mid = r1.astype(BF16)
    lo = (r1 - mid.astype(F32)).astype(BF16)
    return hi, mid, lo


def _dot_exact_rhs(a, b_exact):
    hi, mid, lo = _split3(a)
    return _dot(hi, b_exact) + _dot(mid, b_exact) + _dot(lo, b_exact)


def _dot_exact_lhs(a_exact, b):
    hi, mid, lo = _split3(b)
    return _dot(a_exact, hi) + _dot(a_exact, mid) + _dot(a_exact, lo)


def _rms(x, g):
    return x * lax.rsqrt(jnp.mean(x * x, axis=-1, keepdims=True) + EPS) * g


def _gelu_tanh(x):
    c = math.sqrt(2.0 / math.pi)
    return 0.5 * x * (1.0 + jnp.tanh(c * (x + 0.044715 * (x * x * x))))


def _sigmoid(x):
    return 1.0 / (1.0 + jnp.exp(-x))


def _proj_kernel(kinds, tm, h_ref, g_ref, *refs):
    n = len(kinds)
    n_add = sum(k == "nat+" for k in kinds)
    w_refs, add_refs, o_refs = refs[:n], list(refs[n:n + n_add]), refs[n + n_add:]
    yb = _rms(h_ref[...], g_ref[...]).astype(BF16)
    for kind, w_ref, o_ref in zip(kinds, w_refs, o_refs):
        if kind == "nat":
            o_ref[...] = _dot(yb, w_ref[...]).astype(o_ref.dtype)
        elif kind == "nat+":
            o_ref[...] = _dot(yb, w_ref[...]).astype(o_ref.dtype) + add_refs.pop(0)[...]
        else:
            ot = _dot_nt(w_ref[...], yb)
            for j in range(tm // LANES):
                o_ref[j] = ot[:, j * LANES:(j + 1) * LANES].astype(o_ref.dtype)


def _norm_proj(h, g, segs, tm=512):
    m, d = h.shape
    kinds = tuple(s[0] for s in segs)
    adds = [s[3] for s in segs if s[0] == "nat+"]
    ws, w_specs, out_shapes, out_specs = [], [], [], []
    for kind, w, dt in (s[:3] for s in segs):
        n_out = w.shape[1]
        if kind in ("nat", "nat+"):
            ws.append(w.astype(BF16))
            w_specs.append(_full((d, n_out)))
            out_shapes.append(jax.ShapeDtypeStruct((m, n_out), dt))
            out_specs.append(pl.BlockSpec((tm, n_out), lambda i: (i, 0)))
        else:
            ws.append(w.T.astype(BF16))
            w_specs.append(_full((n_out, d)))
            out_shapes.append(jax.ShapeDtypeStruct((m // LANES, n_out, LANES), dt))
            out_specs.append(pl.BlockSpec((tm // LANES, n_out, LANES), lambda i: (i, 0, 0)))
    return pl.pallas_call(
        functools.partial(_proj_kernel, kinds, tm),
        grid=(m // tm,),
        in_specs=[pl.BlockSpec((tm, d), lambda i: (i, 0)), _full((1, d))] + w_specs + [
            pl.BlockSpec((tm, a.shape[1]), functools.partial(lambda i, nb: (i % nb, 0), nb=a.shape[0] // tm))
            for a in adds],
        out_specs=out_specs,
        out_shape=out_shapes,
        compiler_params=_cparams("parallel"),
        name="norm_proj",
    )(h, g.reshape(1, d), *ws, *adds)


def _mlp_kernel(final, h_ref, ya_ref, yb_ref, woa_ref, wob_ref, gm_ref, wup_ref, wdn_ref, gf_ref,
                o_ref, h2_s, xn_s, acc_s):
    j = pl.program_id(1)

    @pl.when(j == 0)
    def _():
        h2 = h_ref[...] + _dot(ya_ref[...], woa_ref[...]) + _dot(yb_ref[...], wob_ref[...])
        h2_s[...] = h2
        xn_s[...] = _rms(h2, gm_ref[...]).astype(BF16)
        acc_s[...] = jnp.zeros_like(acc_s)

    hid = jnp.square(jnp.maximum(_dot(xn_s[...], wup_ref[...]), 0.0))
    acc_s[...] += _dot(hid.astype(BF16), wdn_ref[...])

    @pl.when(j == pl.num_programs(1) - 1)
    def _():
        out = h2_s[...] + acc_s[...]
        if final:
            out = _rms(out, gf_ref[...])
        o_ref[...] = out


def _mix_out_mlp(h, ya, yb, w_out, g_mlp, w_up, w_down, g_final, final, tm=1024, tf=1024):
    m, d = h.shape
    dff = w_up.shape[1]
    na = ya.shape[1]
    nb = yb.shape[1]
    return pl.pallas_call(
        functools.partial(_mlp_kernel, final),
        grid=(m // tm, dff // tf),
        in_specs=[
            pl.BlockSpec((tm, d), lambda i, j: (i, 0)),
            pl.BlockSpec((tm, na), lambda i, j: (i, 0)),
            pl.BlockSpec((tm, nb), lambda i, j: (i, 0)),
            _full((na, d)), _full((nb, d)), _full((1, d)),
            pl.BlockSpec((d, tf), lambda i, j: (0, j)),
            pl.BlockSpec((tf, d), lambda i, j: (j, 0)),
            _full((1, d)),
        ],
        out_specs=pl.BlockSpec((tm, d), lambda i, j: (i, 0)),
        out_shape=jax.ShapeDtypeStruct((m, d), F32),
        scratch_shapes=[pltpu.VMEM((tm, d), F32), pltpu.VMEM((tm, d), BF16), pltpu.VMEM((tm, d), F32)],
        compiler_params=_cparams("parallel", "arbitrary"),
        name="out_proj_mlp",
    )(h, ya, yb, w_out[:na].astype(BF16), w_out[na:].astype(BF16), g_mlp.reshape(1, d),
      w_up.astype(BF16), w_down.astype(BF16), g_final.reshape(1, d))


def _ssd_kernel(u_ref, cw_ref, cb_ref, dtb_ref, alog_ref, dsk_ref, ng_ref, o_ref, xext_s, st_s):
    t = SSD_CHUNK
    c = pl.program_id(1)

    @pl.when(c == 0)
    def _():
        xext_s[0:8, :] = jnp.zeros((8, SSD_CONV_DIM), F32)
        st_s[...] = jnp.zeros_like(st_s)

    z = u_ref[:, 0:SSD_INNER]
    dt_raw = u_ref[:, SSD_INNER + SSD_CONV_DIM:]
    xext_s[8:8 + t, :] = u_ref[:, SSD_INNER:SSD_INNER + SSD_CONV_DIM]
    conv = cb_ref[...] + cw_ref[0:1, :] * xext_s[5:5 + t, :]
    for k in range(1, SSD_CONV):
        conv = conv + cw_ref[k:k + 1, :] * xext_s[5 + k:5 + k + t, :]
    xext_s[0:8, :] = xext_s[t:t + 8, :]
    xc = conv * _sigmoid(conv)
    xs = xc[:, 0:SSD_INNER]
    gn = SSD_GROUPS * SSD_STATE

    dtp = dt_raw + dtb_ref[...]
    dt = jnp.maximum(dtp, 0.0) + jnp.log1p(jnp.exp(-jnp.abs(dtp)))
    a = -jnp.exp(alog_ref[...])
    da = dt * a

    row = lax.broadcasted_iota(jnp.int32, (t, t), 0)
    col = lax.broadcasted_iota(jnp.int32, (t, t), 1)
    causal = col <= row
    tril = jnp.where(causal, 1.0, 0.0).astype(BF16)
    a_cum = _dot_exact_lhs(tril, da)
    er = lax.broadcasted_iota(jnp.int32, (LANES, SSD_INNER), 0)
    ec = lax.broadcasted_iota(jnp.int32, (LANES, SSD_INNER), 1)
    expand = jnp.where((ec >> 6) == er, 1.0, 0.0).astype(BF16)
    a_cum_x = _dot_exact_rhs(a_cum, expand)
    dt_x = _dot_exact_rhs(dt, expand)
    a_cum_t = a_cum.T
    a_last_x = a_cum_x[t - 1:t, :]
    decay_end_x = jnp.exp(a_last_x - a_cum_x)
    decay_in_x = jnp.exp(a_cum_x)
    chunk_decay_x = jnp.exp(a_last_x)

    xd = xs * dt_x
    xd_end = (xd * decay_end_x).astype(BF16)
    xd_b = xd.astype(BF16)
    lane = lax.broadcasted_iota(jnp.int32, (t, LANES), 1)
    first_half = lane < HEAD_DIM

    pieces = []
    for g in range(SSD_GROUPS):
        bm = xc[:, SSD_INNER + g * SSD_STATE:SSD_INNER + (g + 1) * SSD_STATE]
        cm = xc[:, SSD_INNER + gn + g * SSD_STATE:SSD_INNER + gn + (g + 1) * SSD_STATE].astype(BF16)
        bm_t = bm.T.astype(BF16)
        cb = _dot_nt(cm, bm.astype(BF16))
        for pr in range(2):
            i = g * 2 + pr
            sl = slice(i * LANES, (i + 1) * LANES)
            ms = []
            for hh in range(2):
                h = 2 * i + hh
                seg = a_cum[:, h:h + 1] - a_cum_t[h:h + 1, :]
                dec = jnp.exp(jnp.where(causal, seg, NEG))
                ms.append((cb * dec).astype(BF16))
            y_diag = jnp.where(first_half, _dot(ms[0], xd_b[:, sl]), _dot(ms[1], xd_b[:, sl]))
            st = st_s[i]
            y_off = _dot(cm, st.astype(BF16)) * decay_in_x[:, sl]
            st_s[i] = st * chunk_decay_x[:, sl] + _dot(bm_t, xd_end[:, sl])
            pieces.append(y_diag + y_off)
    y = jnp.concatenate(pieces, axis=1) + xs * dsk_ref[...]
    y = y * (z * _sigmoid(z))
    half = SSD_INNER // SSD_GROUPS
    outs = [_rms(y[:, g * half:(g + 1) * half], ng_ref[:, g * half:(g + 1) * half]) for g in range(SSD_GROUPS)]
    o_ref[...] = jnp.concatenate(outs, axis=1).astype(o_ref.dtype)


def _ssd_mixer(u_ssd, bsz, seq, conv_w, conv_b, dt_bias, a_log, d_skip, norm_g):
    m, width = u_ssd.shape
    nch = seq // SSD_CHUNK
    pad = LANES - SSD_HEADS

    def padded(v):
        return jnp.concatenate([v.astype(F32), jnp.zeros((pad,), F32)]).reshape(1, LANES)

    return pl.pallas_call(
        _ssd_kernel,
        grid=(bsz, nch),
        in_specs=[
            pl.BlockSpec((SSD_CHUNK, width), lambda b, c: (b * nch + c, 0)),
            _full((SSD_CONV, SSD_CONV_DIM)), _full((1, SSD_CONV_DIM)),
            _full((1, LANES)), _full((1, LANES)), _full((1, SSD_INNER)), _full((1, SSD_INNER)),
        ],
        out_specs=pl.BlockSpec((SSD_CHUNK, SSD_INNER), lambda b, c: (b * nch + c, 0)),
        out_shape=jax.ShapeDtypeStruct((m, SSD_INNER), BF16),
        scratch_shapes=[pltpu.VMEM((SSD_CHUNK + 8, SSD_CONV_DIM), F32),
                        pltpu.VMEM((SSD_HEADS // 2, SSD_STATE, LANES), F32)],
        compiler_params=_cparams("arbitrary", "arbitrary"),
        name="ssd_mixer",
    )(u_ssd, conv_w.astype(F32), conv_b.reshape(1, -1).astype(F32), padded(dt_bias), padded(a_log),
      jnp.repeat(d_skip.astype(F32), HEAD_DIM).reshape(1, SSD_INNER), norm_g.reshape(1, SSD_INNER).astype(F32))


def _nsa_compress_kernel(x_ref, pe_ref, w1_ref, b1_ref, w2_ref, b2_ref, o_ref):
    x = x_ref[0, 0, 0]
    npc = x.shape[0]
    half = NSA_CMP_STRIDE * HEAD_DIM
    top = _dot((x + pe_ref[0, :, 0:half]).astype(BF16), w1_ref[0, 0:half, :])
    bot = _dot((x + pe_ref[0, :, half:]).astype(BF16), w1_ref[0, half:, :])
    pre = top + pltpu.roll(bot, npc - 1, 0) + b1_ref[0]
    hid = _gelu_tanh(pre).astype(BF16)
    out = _dot(hid, w2_ref[0]) + b2_ref[0]
    rowi = lax.broadcasted_iota(jnp.int32, out.shape, 0)
    o_ref[0, 0, 0] = jnp.where(rowi < npc - 1, out, 0.0)


def _nsa_compress(kvc, bsz, seq, pe, w1, b1, w2, b2):
    npc = seq // NSA_CMP_STRIDE
    flat = NSA_CMP_STRIDE * HEAD_DIM
    x = kvc.reshape(bsz, npc, NSA_CMP_STRIDE, 2, NSA_KV, HEAD_DIM)
    x = x.transpose(0, 3, 4, 1, 2, 5).reshape(bsz, 2, NSA_KV, npc, flat)
    return pl.pallas_call(
        _nsa_compress_kernel,
        grid=(bsz, 2, NSA_KV),
        in_specs=[
            pl.BlockSpec((1, 1, 1, npc, flat), lambda b, s, g: (b, s, g, 0, 0)),
            pl.BlockSpec((1, 1, 2 * flat), lambda b, s, g: (s, 0, 0)),
            pl.BlockSpec((1, 2 * flat, NSA_CMP_HIDDEN), lambda b, s, g: (s, 0, 0)),
            pl.BlockSpec((1, 1, NSA_CMP_HIDDEN), lambda b, s, g: (s, 0, 0)),
            pl.BlockSpec((1, NSA_CMP_HIDDEN, HEAD_DIM), lambda b, s, g: (s, 0, 0)),
            pl.BlockSpec((1, 1, HEAD_DIM), lambda b, s, g: (s, 0, 0)),
        ],
        out_specs=pl.BlockSpec((1, 1, 1, npc, HEAD_DIM), lambda b, s, g: (b, s, g, 0, 0)),
        out_shape=jax.ShapeDtypeStruct((bsz, 2, NSA_KV, npc, HEAD_DIM), F32),
        compiler_params=_cparams("parallel", "parallel", "parallel"),
        name="nsa_compress",
    )(x, pe.reshape(2, 1, 2 * flat).astype(F32), w1.astype(BF16), b1.reshape(2, 1, -1).astype(F32),
      w2.astype(BF16), b2.reshape(2, 1, -1).astype(F32))


def _nsa_block_onehot(rows):
    r = lax.broadcasted_iota(jnp.int32, (rows, 2 * NSA_KVW), 0)
    c = lax.broadcasted_iota(jnp.int32, (rows, 2 * NSA_KVW), 1)
    blk = (r % NSA_KTILE) // NSA_SLC_LEN
    return jnp.where((c % NSA_KVW) == HEAD_DIM + blk, 1.0, 0.0).astype(BF16)


def _nsa_kernel(q_ref, kc_ref, vc_ref, ks_ref, kw_ref, vs_ref, vw_ref, g_ref, o_ref,
                bias_s, qaug_s, qaug2_s, sa_s, sb_s, pa_s, pb_s):
    qb = pl.program_id(2)
    nqt = NSA_RPG * NSA_QBLK
    s0 = qb * NSA_QBLK
    ncp = kc_ref.shape[2]
    ns = ncp // 4
    heads = [slice(r * NSA_QBLK, (r + 1) * NSA_QBLK) for r in range(NSA_RPG)]

    qcat = jnp.concatenate([q_ref[0, r * HEAD_DIM:(r + 1) * HEAD_DIM, :] for r in range(NSA_RPG)], axis=1)
    qpos = s0 + lax.broadcasted_iota(jnp.int32, (1, NSA_QBLK), 1)
    qaug_s[0:HEAD_DIM, :] = qcat
    qaug_s[HEAD_DIM:, :] = jnp.zeros((HEAD_DIM, nqt), BF16)

    kc = kc_ref[0, 0]
    rc = lax.broadcasted_iota(jnp.int32, (ncp, 1), 0)
    ncmp = 4 * (rc % ns) + rc // ns
    cbias = jnp.where((ncmp * NSA_CMP_STRIDE + (NSA_CMP_LEN - 1)) <= qpos, 0.0, NEG)
    cvalid = jnp.where(qpos >= NSA_CMP_LEN - 1, 1.0, 0.0)
    psum = jnp.zeros((ncp, NSA_QBLK), F32)
    p_all = []
    for sl in heads:
        s = _dot(kc, qcat[:, sl]) + cbias
        e = jnp.exp2(s - jnp.max(s, axis=0, keepdims=True))
        p = e * (cvalid / jnp.sum(e, axis=0, keepdims=True))
        psum = psum + p
        p_all.append(p.astype(BF16))
    o_cmp = _dot(vc_ref[0, 0], jnp.concatenate(p_all, axis=1))

    p3 = psum[3 * ns:4 * ns]
    rj = lax.broadcasted_iota(jnp.int32, (ns, NSA_QBLK), 0)
    p3_prev = jnp.where(rj >= 1, pltpu.roll(p3, 1, 0), 0.0)
    imp = psum[0:ns] + psum[ns:2 * ns] + psum[2 * ns:3 * ns] + p3 + p3_prev
    cur = qpos // NSA_SLC_LEN
    forced = (rj == 0) | (rj == cur) | (rj == cur - 1)
    imp = jnp.where(forced, FORCE, jnp.where(rj <= cur, imp, -FORCE))
    rjf = rj.astype(F32)
    taken = -3.0e38
    for _ in range(min(NSA_TOPK, ns)):
        mx = jnp.max(imp, axis=0, keepdims=True)
        first = jnp.min(jnp.where(imp == mx, rjf, float(ns)), axis=0, keepdims=True)
        imp = jnp.where(rjf == first, taken, imp)
    bias_s[...] = jnp.where(imp == taken, 0.0, NEG)

    kt_diag = s0 // NSA_KTILE
    blocks_per_tile = NSA_KTILE // NSA_SLC_LEN
    vtiles = NSA_KTILE // LANES

    def qk_tile(kt, qaug_ref):
        k0 = pl.multiple_of(kt * NSA_KTILE, NSA_KTILE)
        b8 = bias_s[pl.ds(pl.multiple_of(kt * blocks_per_tile, blocks_per_tile), blocks_per_tile), :]
        b16 = jnp.concatenate([b8, jnp.zeros_like(b8)], axis=0).astype(BF16)
        qaug_ref[HEAD_DIM:HEAD_DIM + 16, :] = jnp.concatenate([b16] * NSA_RPG, axis=1)
        return _dot(ks_ref[pl.ds(k0, NSA_KTILE), :], qaug_ref[...])

    def pv_tile(kt, p):
        vt = jnp.concatenate([vs_ref[kt * vtiles + i] for i in range(vtiles)], axis=1)
        return _dot(vt, p)

    def softmax_tile(s, m_old, l_old):
        m_new = jnp.maximum(m_old, jnp.max(s, axis=0, keepdims=True))
        alpha = jnp.exp2(m_old - m_new)
        p = jnp.exp2(s - m_new)
        return p.astype(BF16), m_new, alpha * l_old + jnp.sum(p, axis=0, keepdims=True), alpha

    def visible(kt):
        kpos = kt * NSA_KTILE + lax.broadcasted_iota(jnp.int32, (NSA_KTILE, 1), 0)
        return jnp.concatenate([kpos <= qpos] * NSA_RPG, axis=1)

    def tile_pair(i, carry, last):
        m_run, l_run, acc, alpha_b = carry
        ta, tb = 2 * i, 2 * i + 1
        acc = alpha_b * acc + pv_tile(jnp.maximum(ta - 1, 0), pb_s[...])
        sa = jnp.where(visible(ta), sa_s[...], NEG) if last else sa_s[...]
        pa, m_run, l_run, alpha_a = softmax_tile(sa, m_run, l_run)
        pa_s[...] = pa
        sb_s[...] = qk_tile(tb, qaug_s)
        acc = alpha_a * acc + pv_tile(ta, pa_s[...])
        sb = jnp.where(visible(tb), sb_s[...], NEG) if last else sb_s[...]
        pb, m_run, l_run, alpha_b = softmax_tile(sb, m_run, l_run)
        if last:
            acc = alpha_b * acc + pv_tile(tb, pb)
        else:
            pb_s[...] = pb
            sa_s[...] = qk_tile(tb + 1, qaug2_s)
        return m_run, l_run, acc, alpha_b

    qaug2_s[...] = qaug_s[...]
    pb_s[...] = jnp.zeros((NSA_KTILE, nqt), BF16)
    sa_s[...] = qk_tile(0, qaug2_s)
    init = (jnp.full((1, nqt), NEG, F32), jnp.zeros((1, nqt), F32), jnp.zeros((HEAD_DIM, nqt), F32),
            jnp.ones((1, nqt), F32))
    pair_diag = kt_diag // 2
    carry = lax.fori_loop(0, pair_diag, lambda i, c: tile_pair(i, c, False), init)
    _, l_run, acc, _ = tile_pair(pair_diag, carry, True)
    o_slc = acc * (1.0 / l_run)

    span = NSA_WIN + NSA_QBLK
    start = pl.multiple_of(jnp.maximum(s0 - NSA_WIN, 0), NSA_QBLK)
    kwin = kw_ref[pl.ds(start, span), :]
    kp = start + lax.broadcasted_iota(jnp.int32, (span, 1), 0)
    wbias = jnp.where((kp <= qpos) & (kp > qpos - NSA_WIN), 0.0, NEG)
    pw, dens = [], []
    for sl in heads:
        s = _dot(kwin, qaug_s[:, sl]) + wbias
        e = jnp.exp2(s - jnp.max(s, axis=0, keepdims=True))
        dens.append(jnp.sum(e, axis=0, keepdims=True))
        pw.append(e.astype(BF16))
    sblk = start // LANES
    vwt = jnp.concatenate([vw_ref[sblk + i] for i in range(span // LANES)], axis=1)
    o_win = _dot(vwt, jnp.concatenate(pw, axis=1)) * (1.0 / jnp.concatenate(dens, axis=1))

    gates = _sigmoid(g_ref[0])
    for r, sl in enumerate(heads):
        out = (gates[r:r + 1, :] * o_cmp[:, sl] + gates[NSA_RPG + r:NSA_RPG + r + 1, :] * o_slc[:, sl]
               + gates[2 * NSA_RPG + r:2 * NSA_RPG + r + 1, :] * o_win[:, sl])
        o_ref[0, r * HEAD_DIM:(r + 1) * HEAD_DIM, :] = out.astype(o_ref.dtype)


def _nsa_mixer(q_tt, kvc, k_slc, k_win, v_tt, g_tt, bsz, seq, pe, w1, b1, w2, b2):
    nqb = seq // NSA_QBLK
    npc = seq // NSA_CMP_STRIDE
    ns = seq // NSA_SLC_LEN
    cmp_out = _nsa_compress(kvc, bsz, seq, pe, w1, b1, w2, b2)
    perm = cmp_out.reshape(bsz, 2, NSA_KV, ns, 4, HEAD_DIM).transpose(0, 1, 2, 4, 3, 5).reshape(
        bsz, 2, NSA_KV, npc, HEAD_DIM)
    kc = perm[:, 0].astype(BF16)
    vc_t = perm[:, 1].transpose(0, 1, 3, 2).astype(BF16)
    gq = NSA_RPG * HEAD_DIM
    return pl.pallas_call(
        _nsa_kernel,
        grid=(bsz, NSA_KV, nqb),
        in_specs=[
            pl.BlockSpec((1, gq, LANES), lambda b, g, q: (b * nqb + q, g, 0)),
            pl.BlockSpec((1, 1, npc, HEAD_DIM), lambda b, g, q: (b, g, 0, 0)),
            pl.BlockSpec((1, 1, HEAD_DIM, npc), lambda b, g, q: (b, g, 0, 0)),
            pl.BlockSpec((seq, NSA_KVW), lambda b, g, q: (b, g)),
            pl.BlockSpec((seq, NSA_KVW), lambda b, g, q: (b, g)),
            pl.BlockSpec((nqb, HEAD_DIM, LANES), lambda b, g, q: (b, g, 0)),
            pl.BlockSpec((nqb, HEAD_DIM, LANES), lambda b, g, q: (b, NSA_KV + g, 0)),
            pl.BlockSpec((1, 16, LANES), lambda b, g, q: (b * nqb + q, g, 0)),
        ],
        out_specs=pl.BlockSpec((1, gq, LANES), lambda b, g, q: (b * nqb + q, g, 0)),
        out_shape=jax.ShapeDtypeStruct((bsz * nqb, NSA_Q, LANES), BF16),
        scratch_shapes=[pltpu.VMEM((ns, NSA_QBLK), F32),
                        pltpu.VMEM((2 * HEAD_DIM, NSA_RPG * NSA_QBLK), BF16),
                        pltpu.VMEM((2 * HEAD_DIM, NSA_RPG * NSA_QBLK), BF16),
                        pltpu.VMEM((NSA_KTILE, NSA_RPG * NSA_QBLK), F32),
                        pltpu.VMEM((NSA_KTILE, NSA_RPG * NSA_QBLK), F32),
                        pltpu.VMEM((NSA_KTILE, NSA_RPG * NSA_QBLK), BF16),
                        pltpu.VMEM((NSA_KTILE, NSA_RPG * NSA_QBLK), BF16)],
        compiler_params=_cparams("arbitrary", "arbitrary", "arbitrary"),
        name="nsa_attention",
    )(q_tt, kc, vc_t, k_slc, k_win, v_tt, v_tt, g_tt)


def _swa_kernel(q_ref, kp_ref, kc_ref, vp_ref, vc_ref, sink_ref, o_ref):
    qb = pl.program_id(1)
    t = SWA_WIN
    kband = jnp.concatenate([kp_ref[...], kc_ref[...]], axis=0)
    krel = lax.broadcasted_iota(jnp.int32, (2 * t, 1), 0) - t
    qrel = lax.broadcasted_iota(jnp.int32, (1, t), 1)
    lowest = jnp.where(qb > 0, -t, 0)
    mask = (krel <= qrel) & (krel > qrel - SWA_WIN) & (krel >= lowest)
    for g in range(2):
        rows = slice(g * SWA_RPG * HEAD_DIM, (g + 1) * SWA_RPG * HEAD_DIM)
        qg = q_ref[0, rows, :]
        qcat = jnp.concatenate([qg[r * HEAD_DIM:(r + 1) * HEAD_DIM, :] for r in range(SWA_RPG)], axis=1)
        zq = jnp.zeros_like(qcat)
        qext = jnp.concatenate([qcat, zq] if g == 0 else [zq, qcat], axis=0)
        s = _dot(kband, qext)
        ps = []
        for r in range(SWA_RPG):
            h = g * SWA_RPG + r
            sink = sink_ref[h:h + 1, :]
            sr = jnp.where(mask, s[:, r * t:(r + 1) * t], NEG)
            mx = jnp.maximum(jnp.max(sr, axis=0, keepdims=True), sink)
            e = jnp.where(mask, jnp.exp(sr - mx), 0.0)
            den = jnp.sum(e, axis=0, keepdims=True) + jnp.exp(sink - mx)
            ps.append((e / den).astype(BF16))
        vband = jnp.concatenate([vp_ref[0, g * HEAD_DIM:(g + 1) * HEAD_DIM, :],
                                 vc_ref[0, g * HEAD_DIM:(g + 1) * HEAD_DIM, :]], axis=1)
        og = _dot(vband, jnp.concatenate(ps, axis=1))
        for r in range(SWA_RPG):
            h = g * SWA_RPG + r
            o_ref[0, h * HEAD_DIM:(h + 1) * HEAD_DIM, :] = og[:, r * t:(r + 1) * t].astype(o_ref.dtype)


def _swa_mixer(q_tt, k_nat, v_tt, sinks, bsz, seq):
    nqb = seq // SWA_WIN
    sink_rows = jnp.broadcast_to(sinks.astype(F32)[:, None], (sinks.shape[0], LANES))
    prev = lambda b, q: b * nqb + jnp.maximum(q - 1, 0)
    return pl.pallas_call(
        _swa_kernel,
        grid=(bsz, nqb),
        in_specs=[
            pl.BlockSpec((1, SWA_Q, LANES), lambda b, q: (b * nqb + q, 0, 0)),
            pl.BlockSpec((SWA_WIN, SWA_KVW), lambda b, q: (prev(b, q), 0)),
            pl.BlockSpec((SWA_WIN, SWA_KVW), lambda b, q: (b * nqb + q, 0)),
            pl.BlockSpec((1, SWA_KVW, LANES), lambda b, q: (prev(b, q), 0, 0)),
            pl.BlockSpec((1, SWA_KVW, LANES), lambda b, q: (b * nqb + q, 0, 0)),
            _full(sink_rows.shape),
        ],
        out_specs=pl.BlockSpec((1, SWA_Q, LANES), lambda b, q: (b * nqb + q, 0, 0)),
        out_shape=jax.ShapeDtypeStruct((bsz * nqb, SWA_Q, LANES), BF16),
        compiler_params=_cparams("parallel", "parallel"),
        name="swa_attention",
    )(q_tt, k_nat, k_nat, v_tt, v_tt, sink_rows)


def _s5_params(a_re, a_im, log_dt, b_re, b_im, c_re, c_im, n_chunks):
    f = F32
    t = S5_CHUNK
    step = jnp.exp(log_dt.astype(f))[:, None]
    lr, li = a_re.astype(f), a_im.astype(f)

    def lam_pow(tau):
        tau = tau.astype(f)[:, None, None]
        mag = jnp.exp(lr * step * tau)
        ang = li * step * tau
        return mag * jnp.cos(ang), mag * jnp.sin(ang)

    lb_r, lb_i = (v[0] for v in lam_pow(jnp.ones((1,))))
    nr, ni = lb_r - 1.0, lb_i
    den = lr * lr + li * li
    fr, fi = (nr * lr + ni * li) / den, (ni * lr - nr * li) / den
    br, bi = b_re.astype(f), b_im.astype(f)
    bb_r = fr[..., None] * br - fi[..., None] * bi
    bb_i = fr[..., None] * bi + fi[..., None] * br
    cr, ci = c_re.astype(f), c_im.astype(f)

    pr, pi = lam_pow(jnp.arange(t + 1))
    cl_r = cr[None] * pr[:, :, None, :] - ci[None] * pi[:, :, None, :]
    cl_i = cr[None] * pi[:, :, None, :] + ci[None] * pr[:, :, None, :]
    kern = jnp.einsum("tghp,gpk->gthk", cl_r[:t], bb_r, precision="highest") - jnp.einsum(
        "tghp,gpk->gthk", cl_i[:t], bb_i, precision="highest")
    lag = jnp.arange(t)[None, :] - jnp.arange(t)[:, None]
    toep = jnp.where((lag >= 0)[None, :, :, None, None], kern[:, jnp.clip(lag, 0, t - 1)], 0.0)
    toep = toep.transpose(0, 1, 4, 2, 3).reshape(S5_GROUPS, t * S5_GROUP_CH, t * S5_GROUP_CH)
    rr, ri = pr[t - 1 - jnp.arange(t)], pi[t - 1 - jnp.arange(t)]
    bs_r = rr[..., None] * bb_r[None] - ri[..., None] * bb_i[None]
    bs_i = rr[..., None] * bb_i[None] + ri[..., None] * bb_r[None]
    bs = jnp.concatenate([bs_r, bs_i], axis=2)
    bs = bs.transpose(1, 0, 3, 2).reshape(S5_GROUPS, t * S5_GROUP_CH, 2 * S5_STATE)
    cs = jnp.concatenate([cl_r[1:], -cl_i[1:]], axis=3)
    cs = cs.transpose(1, 3, 0, 2).reshape(S5_GROUPS, 2 * S5_STATE, t * S5_GROUP_CH)
    ar, ai = pr[t], pi[t]
    a1, a2 = [], []
    k = 1
    while k < n_chunks:
        a1.append(jnp.concatenate([ar, ar], axis=1))
        a2.append(jnp.concatenate([-ai, ai], axis=1))
        ar, ai = ar * ar - ai * ai, 2.0 * ar * ai
        k *= 2
    a1 = jnp.stack(a1, axis=1)
    a2 = jnp.stack(a2, axis=1)
    return toep.astype(BF16), bs.astype(BF16), cs.astype(BF16), a1, a2


def _s5_kernel(bsz, u_ref, toep_ref, bs_ref, cs_ref, a1_ref, a2_ref, o_ref):
    u = u_ref[0]
    n_all = u.shape[0]
    n = n_all // bsz
    sc = _dot(u, bs_ref[0])
    rowi = lax.broadcasted_iota(jnp.int32, (n, 2 * S5_STATE), 0)
    h_in = []
    for b in range(bsz):
        x = sc[b * n:(b + 1) * n]
        k, step = 1, 0
        while k < n:
            xs = jnp.where(rowi >= k, pltpu.roll(x, k, 0), 0.0)
            xs_sw = pltpu.roll(xs, S5_STATE, 1)
            x = x + a1_ref[0, step:step + 1, :] * xs + a2_ref[0, step:step + 1, :] * xs_sw
            k *= 2
            step += 1
        h_in.append(jnp.where(rowi >= 1, pltpu.roll(x, 1, 0), 0.0))
    h_in = jnp.concatenate(h_in, axis=0)
    hi = h_in.astype(BF16)
    lo = (h_in - hi.astype(F32)).astype(BF16)
    o_ref[0] = _dot(u, toep_ref[0]) + _dot(hi, cs_ref[0]) + _dot(lo, cs_ref[0])


def _s5_glu_kernel(y_ref, u_ref, d_ref, w_ref, b_ref, o_ref):
    y = _gelu_tanh(y_ref[...] + d_ref[...] * u_ref[...])
    gate = _sigmoid(_dot(y.astype(BF16), w_ref[...]) + b_ref[...])
    o_ref[...] = (y * gate).astype(o_ref.dtype)


def _s5_mixer(u5, bsz, seq, a_re, a_im, log_dt, b_re, b_im, c_re, c_im, d_skip, glu_w, glu_b, tm=1024):
    m = u5.shape[0]
    t = S5_CHUNK
    nch = m // t
    width = t * S5_GROUP_CH
    toep, bs, cs, a1, a2 = _s5_params(a_re, a_im, log_dt, b_re, b_im, c_re, c_im, nch // bsz)
    nsteps = a1.shape[1]
    ug = u5.reshape(nch, t, S5_GROUPS, S5_GROUP_CH).transpose(2, 0, 1, 3).reshape(S5_GROUPS, nch, width).astype(BF16)
    yg = pl.pallas_call(
        functools.partial(_s5_kernel, bsz),
        grid=(S5_GROUPS,),
        in_specs=[
            pl.BlockSpec((1, nch, width), lambda g: (g, 0, 0)),
            pl.BlockSpec((1, width, width), lambda g: (g, 0, 0)),
            pl.BlockSpec((1, width, 2 * S5_STATE), lambda g: (g, 0, 0)),
            pl.BlockSpec((1, 2 * S5_STATE, width), lambda g: (g, 0, 0)),
            pl.BlockSpec((1, nsteps, 2 * S5_STATE), lambda g: (g, 0, 0)),
            pl.BlockSpec((1, nsteps, 2 * S5_STATE), lambda g: (g, 0, 0)),
        ],
        out_specs=pl.BlockSpec((1, nch, width), lambda g: (g, 0, 0)),
        out_shape=jax.ShapeDtypeStruct((S5_GROUPS, nch, width), F32),
        compiler_params=_cparams("parallel"),
        name="s5_scan",
    )(ug, toep, bs, cs, a1, a2)
    y = yg.reshape(S5_GROUPS, nch, t, S5_GROUP_CH).transpose(1, 2, 0, 3).reshape(m, S5_CH)
    return pl.pallas_call(
        _s5_glu_kernel,
        grid=(m // tm,),
        in_specs=[
            pl.BlockSpec((tm, S5_CH), lambda i: (i, 0)),
            pl.BlockSpec((tm, S5_CH), lambda i: (i, 0)),
            _full((1, S5_CH)), _full((S5_CH, S5_CH)), _full((1, S5_CH)),
        ],
        out_specs=pl.BlockSpec((tm, S5_CH), lambda i: (i, 0)),
        out_shape=jax.ShapeDtypeStruct((m, S5_CH), BF16),
        compiler_params=_cparams("parallel"),
        name="s5_glu",
    )(y, u5, d_skip.reshape(1, S5_CH).astype(F32), glu_w.astype(BF16), glu_b.reshape(1, S5_CH).astype(F32))


def _tt_to_nat(x_tt):
    nb, feat, lanes = x_tt.shape
    return x_tt.transpose(0, 2, 1).reshape(nb * lanes, feat)


def _even_mixers(h, bsz, seq, g_mix, w_in, conv_w, conv_b, dt_bias, a_log, d_skip, norm_g, pe, w1, b1, w2, b2):
    d = h.shape[1]
    scale = HEAD_DIM ** -0.5 * math.log2(math.e)
    o = SSD_IN
    w_ssd = jnp.concatenate([w_in[:, :SSD_IN], jnp.zeros((d, LANES - SSD_HEADS), w_in.dtype)], axis=1)
    w_q = w_in[:, o:o + NSA_Q] * scale
    kv = [w_in[:, o + NSA_Q + i * NSA_KVW:o + NSA_Q + (i + 1) * NSA_KVW] for i in range(6)]
    w_gate = w_in[:, o + NSA_Q + 6 * NSA_KVW:].reshape(d, NSA_KV, NSA_RPG, 3).transpose(0, 1, 3, 2)
    w_gate = jnp.concatenate([w_gate.reshape(d, NSA_KV, 12), jnp.zeros((d, NSA_KV, 4), w_in.dtype)],
                             axis=2).reshape(d, NSA_KV * 16)

    def per_group_halves(w):
        wg = w.reshape(d, NSA_KV, HEAD_DIM)
        return jnp.concatenate([wg, jnp.zeros_like(wg)], axis=2).reshape(d, NSA_KV * NSA_KVW)
    segs = [
        ("nat", w_ssd, F32),
        ("tt", w_q, BF16),
        ("nat", jnp.concatenate([kv[0], kv[1]], axis=1), F32),
        ("nat+", per_group_halves(kv[2]), BF16, _nsa_block_onehot(NSA_KTILE)),
        ("nat", per_group_halves(kv[4]), BF16),
        ("tt", jnp.concatenate([kv[3], kv[5]], axis=1), BF16),
        ("tt", w_gate, F32),
    ]
    u_ssd, q_tt, kvc, k_slc, k_win, v_tt, g_tt = _norm_proj(h, g_mix, segs)
    ya = _ssd_mixer(u_ssd, bsz, seq, conv_w, conv_b, dt_bias, a_log, d_skip, norm_g)
    yb_tt = _nsa_mixer(q_tt, kvc, k_slc, k_win, v_tt, g_tt, bsz, seq, pe, w1, b1, w2, b2)
    return ya, _tt_to_nat(yb_tt)


def _odd_mixers(h, bsz, seq, g_mix, w_in, sinks, a_re, a_im, log_dt, b_re, b_im, c_re, c_im, d_skip, glu_w, glu_b):
    scale = HEAD_DIM ** -0.5
    segs = [
        ("tt", w_in[:, :SWA_Q] * scale, BF16),
        ("nat", w_in[:, SWA_Q:SWA_Q + SWA_KVW], BF16),
        ("tt", w_in[:, SWA_Q + SWA_KVW:SWA_Q + 2 * SWA_KVW], BF16),
        ("nat", w_in[:, SWA_Q + 2 * SWA_KVW:], F32),
    ]
    q_tt, k_nat, v_tt, u5 = _norm_proj(h, g_mix, segs)
    yc_tt = _swa_mixer(q_tt, k_nat, v_tt, sinks, bsz, seq)
    yd = _s5_mixer(u5, bsz, seq, a_re, a_im, log_dt, b_re, b_im, c_re, c_im, d_skip, glu_w, glu_b)
    return _tt_to_nat(yc_tt), yd


def kernel(x, norm_mix, norm_mlp, norm_final, mlp_w_up, mlp_w_down, ev_w_in, ev_w_out, ssd_conv_w, ssd_conv_b,
           ssd_dt_bias, ssd_a_log, ssd_d, ssd_norm, nsa_pe, nsa_cmp_w1, nsa_cmp_b1, nsa_cmp_w2, nsa_cmp_b2,
           od_w_in, od_w_out, swa_sinks, s5_a_re, s5_a_im, s5_log_dt, s5_b_re, s5_b_im, s5_c_re, s5_c_im,
           s5_d, s5_glu_w, s5_glu_b):
    bsz, seq, d = x.shape
    depth = norm_mix.shape[0]
    assert seq % (2 * NSA_KTILE) == 0 and seq >= NSA_WIN + NSA_QBLK
    h = x.reshape(bsz * seq, d)
    for layer in range(depth):
        i = layer // 2
        if layer % 2 == 0:
            ya, yb = _even_mixers(h, bsz, seq, norm_mix[layer], ev_w_in[i], ssd_conv_w[i], ssd_conv_b[i],
                                  ssd_dt_bias[i], ssd_a_log[i], ssd_d[i], ssd_norm[i], nsa_pe[i],
                                  nsa_cmp_w1[i], nsa_cmp_b1[i], nsa_cmp_w2[i], nsa_cmp_b2[i])
            w_out = ev_w_out[i]
        else:
            ya, yb = _odd_mixers(h, bsz, seq, norm_mix[layer], od_w_in[i], swa_sinks[i], s5_a_re[i], s5_a_im[i],
                                 s5_log_dt[i], s5_b_re[i], s5_b_im[i], s5_c_re[i], s5_c_im[i], s5_d[i],
                                 s5_glu_w[i], s5_glu_b[i])
            w_out = od_w_out[i]
        h = _mix_out_mlp(h, ya, yb, w_out, norm_mlp[layer], mlp_w_up[layer], mlp_w_down[layer], norm_final,
                         final=(layer == depth - 1))
    return h.reshape(bsz, seq, d)
```

```python
import functools
import math

import jax
import jax.numpy as jnp
from jax import lax
from jax.experimental import pallas as pl
from jax.experimental.pallas import tpu as pltpu

F32 = jnp.float32
BF16 = jnp.bfloat16

EPS = 1e-6
NEG = -1e30
FORCE = 1e9
HEAD_DIM = 64
LANES = 128
VMEM_LIMIT_BYTES = 56 * 1024 * 1024

SSD_HEADS = 8
SSD_INNER = 512
SSD_GROUPS = 2
SSD_STATE = 128
SSD_CONV = 4
SSD_CHUNK = 128
SSD_CONV_DIM = 1024
SSD_IN = SSD_INNER + SSD_CONV_DIM + SSD_HEADS

NSA_HEADS = 8
NSA_KV = 2
NSA_RPG = 4
NSA_CMP_LEN = 32
NSA_CMP_STRIDE = 16
NSA_SLC_LEN = 64
NSA_TOPK = 16
NSA_WIN = 512
NSA_CMP_HIDDEN = 256
NSA_QBLK = 128
NSA_Q = 512
NSA_KVW = 128
NSA_KTILE = 512
NSA_UNROLL = 2

SWA_RPG = 4
SWA_WIN = 128
SWA_Q = 512
SWA_KVW = 128

S5_CH = 512
S5_GROUP_CH = 16
S5_GROUPS = 32
S5_STATE = 64
S5_CHUNK = 32


def _cparams(*sem):
    return pltpu.CompilerParams(dimension_semantics=sem, vmem_limit_bytes=VMEM_LIMIT_BYTES)


def _full(shape):
    n = len(shape)
    return pl.BlockSpec(shape, lambda *_: (0,) * n)


def _dot(a, b):
    return jnp.dot(a, b, preferred_element_type=F32)


def _dot_nt(a, b):
    return lax.dot_general(a, b, (((1,), (1,)), ((), ())), preferred_element_type=F32)


def _split3(a):
    hi = a.astype(BF16)
    r1 = a - hi.astype(F32)
    mid = r1.astype(BF16)
    lo = (r1 - mid.astype(F32)).astype(BF16)
    return hi, mid, lo


def _dot_exact_rhs(a, b_exact):
    hi, mid, lo = _split3(a)
    return _dot(hi, b_exact) + _dot(mid, b_exact) + _dot(lo, b_exact)


def _dot_exact_lhs(a_exact, b):
    hi, mid, lo = _split3(b)
    return _dot(a_exact, hi) + _dot(a_exact, mid) + _dot(a_exact, lo)


def _rms(x, g):
    return x * lax.rsqrt(jnp.mean(x * x, axis=-1, keepdims=True) + EPS) * g


def _gelu_tanh(x):
    c = math.sqrt(2.0 / math.pi)
    return 0.5 * x * (1.0 + jnp.tanh(c * (x + 0.044715 * (x * x * x))))


def _sigmoid(x):
    return 1.0 / (1.0 + jnp.exp(-x))


def _proj_kernel(kinds, tm, h_ref, g_ref, *refs):
    n = len(kinds)
    n_add = sum(k == "nat+" for k in kinds)
    w_refs, add_refs, o_refs = refs[:n], list(refs[n:n + n_add]), refs[n + n_add:]
    yb = _rms(h_ref[...], g_ref[...]).astype(BF16)
    for kind, w_ref, o_ref in zip(kinds, w_refs, o_refs):
        if kind == "nat":
            o_ref[...] = _dot(yb, w_ref[...]).astype(o_ref.dtype)
        elif kind == "nat+":
            o_ref[...] = _dot(yb, w_ref[...]).astype(o_ref.dtype) + add_refs.pop(0)[...]
        else:
            ot = _dot_nt(w_ref[...], yb)
            for j in range(tm // LANES):
                o_ref[j] = ot[:, j * LANES:(j + 1) * LANES].astype(o_ref.dtype)


def _norm_proj(h, g, segs, tm=512):
    m, d = h.shape
    kinds = tuple(s[0] for s in segs)
    adds = [s[3] for s in segs if s[0] == "nat+"]
    ws, w_specs, out_shapes, out_specs = [], [], [], []
    for kind, w, dt in (s[:3] for s in segs):
        n_out = w.shape[1]
        if kind in ("nat", "nat+"):
            ws.append(w.astype(BF16))
            w_specs.append(_full((d, n_out)))
            out_shapes.append(jax.ShapeDtypeStruct((m, n_out), dt))
            out_specs.append(pl.BlockSpec((tm, n_out), lambda i: (i, 0)))
        else:
            ws.append(w.T.astype(BF16))
            w_specs.append(_full((n_out, d)))
            out_shapes.append(jax.ShapeDtypeStruct((m // LANES, n_out, LANES), dt))
            out_specs.append(pl.BlockSpec((tm // LANES, n_out, LANES), lambda i: (i, 0, 0)))
    return pl.pallas_call(
        functools.partial(_proj_kernel, kinds, tm),
        grid=(m // tm,),
        in_specs=[pl.BlockSpec((tm, d), lambda i: (i, 0)), _full((1, d))] + w_specs + [
            pl.BlockSpec((tm, a.shape[1]), functools.partial(lambda i, nb: (i % nb, 0), nb=a.shape[0] // tm))
            for a in adds],
        out_specs=out_specs,
        out_shape=out_shapes,
        compiler_params=_cparams("parallel"),
        name="norm_proj",
    )(h, g.reshape(1, d), *ws, *adds)


def _mlp_kernel(final, h_ref, ya_ref, yb_ref, woa_ref, wob_ref, gm_ref, wup_ref, wdn_ref, gf_ref,
                o_ref, h2_s, xn_s, acc_s):
    j = pl.program_id(1)

    @pl.when(j == 0)
    def _():
        h2 = h_ref[...] + _dot(ya_ref[...], woa_ref[...]) + _dot(yb_ref[...], wob_ref[...])
        h2_s[...] = h2
        xn_s[...] = _rms(h2, gm_ref[...]).astype(BF16)
        acc_s[...] = jnp.zeros_like(acc_s)

    hid = jnp.square(jnp.maximum(_dot(xn_s[...], wup_ref[...]), 0.0))
    acc_s[...] += _dot(hid.astype(BF16), wdn_ref[...])

    @pl.when(j == pl.num_programs(1) - 1)
    def _():
        out = h2_s[...] + acc_s[...]
        if final:
            out = _rms(out, gf_ref[...])
        o_ref[...] = out


def _mix_out_mlp(h, ya, yb, w_out, g_mlp, w_up, w_down, g_final, final, tm=1024, tf=1024):
    m, d = h.shape
    dff = w_up.shape[1]
    na = ya.shape[1]
    nb = yb.shape[1]
    return pl.pallas_call(
        functools.partial(_mlp_kernel, final),
        grid=(m // tm, dff // tf),
        in_specs=[
            pl.BlockSpec((tm, d), lambda i, j: (i, 0)),
            pl.BlockSpec((tm, na), lambda i, j: (i, 0)),
            pl.BlockSpec((tm, nb), lambda i, j: (i, 0)),
            _full((na, d)), _full((nb, d)), _full((1, d)),
            pl.BlockSpec((d, tf), lambda i, j: (0, j)),
            pl.BlockSpec((tf, d), lambda i, j: (j, 0)),
            _full((1, d)),
        ],
        out_specs=pl.BlockSpec((tm, d), lambda i, j: (i, 0)),
        out_shape=jax.ShapeDtypeStruct((m, d), F32),
        scratch_shapes=[pltpu.VMEM((tm, d), F32), pltpu.VMEM((tm, d), BF16), pltpu.VMEM((tm, d), F32)],
        compiler_params=_cparams("parallel", "arbitrary"),
        name="out_proj_mlp",
    )(h, ya, yb, w_out[:na].astype(BF16), w_out[na:].astype(BF16), g_mlp.reshape(1, d),
      w_up.astype(BF16), w_down.astype(BF16), g_final.reshape(1, d))


def _ssd_kernel(u_ref, cw_ref, cb_ref, dtb_ref, alog_ref, dsk_ref, ng_ref, o_ref, xext_s, st_s):
    t = SSD_CHUNK
    c = pl.program_id(1)

    @pl.when(c == 0)
    def _():
        xext_s[0:8, :] = jnp.zeros((8, SSD_CONV_DIM), F32)
        st_s[...] = jnp.zeros_like(st_s)

    z = u_ref[:, 0:SSD_INNER]
    dt_raw = u_ref[:, SSD_INNER + SSD_CONV_DIM:]
    xext_s[8:8 + t, :] = u_ref[:, SSD_INNER:SSD_INNER + SSD_CONV_DIM]
    conv = cb_ref[...] + cw_ref[0:1, :] * xext_s[5:5 + t, :]
    for k in range(1, SSD_CONV):
        conv = conv + cw_ref[k:k + 1, :] * xext_s[5 + k:5 + k + t, :]
    xext_s[0:8, :] = xext_s[t:t + 8, :]
    xc = conv * _sigmoid(conv)
    xs = xc[:, 0:SSD_INNER]
    gn = SSD_GROUPS * SSD_STATE

    dtp = dt_raw + dtb_ref[...]
    dt = jnp.maximum(dtp, 0.0) + jnp.log1p(jnp.exp(-jnp.abs(dtp)))
    a = -jnp.exp(alog_ref[...])
    da = dt * a

    row = lax.broadcasted_iota(jnp.int32, (t, t), 0)
    col = lax.broadcasted_iota(jnp.int32, (t, t), 1)
    causal = col <= row
    tril = jnp.where(causal, 1.0, 0.0).astype(BF16)
    a_cum = _dot_exact_lhs(tril, da)
    er = lax.broadcasted_iota(jnp.int32, (LANES, SSD_INNER), 0)
    ec = lax.broadcasted_iota(jnp.int32, (LANES, SSD_INNER), 1)
    expand = jnp.where((ec >> 6) == er, 1.0, 0.0).astype(BF16)
    a_cum_x = _dot_exact_rhs(a_cum, expand)
    dt_x = _dot_exact_rhs(dt, expand)
    a_cum_t = a_cum.T
    a_last_x = a_cum_x[t - 1:t, :]
    decay_end_x = jnp.exp(a_last_x - a_cum_x)
    decay_in_x = jnp.exp(a_cum_x)
    chunk_decay_x = jnp.exp(a_last_x)

    xd = xs * dt_x
    xd_end = (xd * decay_end_x).astype(BF16)
    xd_b = xd.astype(BF16)
    lane = lax.broadcasted_iota(jnp.int32, (t, LANES), 1)
    first_half = lane < HEAD_DIM

    pieces = []
    for g in range(SSD_GROUPS):
        bm = xc[:, SSD_INNER + g * SSD_STATE:SSD_INNER + (g + 1) * SSD_STATE]
        cm = xc[:, SSD_INNER + gn + g * SSD_STATE:SSD_INNER + gn + (g + 1) * SSD_STATE].astype(BF16)
        bm_t = bm.T.astype(BF16)
        cb = _dot_nt(cm, bm.astype(BF16))
        for pr in range(2):
            i = g * 2 + pr
            sl = slice(i * LANES, (i + 1) * LANES)
            ms = []
            for hh in range(2):
                h = 2 * i + hh
                seg = a_cum[:, h:h + 1] - a_cum_t[h:h + 1, :]
                dec = jnp.exp(jnp.where(causal, seg, NEG))
                ms.append((cb * dec).astype(BF16))
            y_diag = jnp.where(first_half, _dot(ms[0], xd_b[:, sl]), _dot(ms[1], xd_b[:, sl]))
            st = st_s[i]
            y_off = _dot(cm, st.astype(BF16)) * decay_in_x[:, sl]
            st_s[i] = st * chunk_decay_x[:, sl] + _dot(bm_t, xd_end[:, sl])
            pieces.append(y_diag + y_off)
    y = jnp.concatenate(pieces, axis=1) + xs * dsk_ref[...]
    y = y * (z * _sigmoid(z))
    half = SSD_INNER // SSD_GROUPS
    outs = [_rms(y[:, g * half:(g + 1) * half], ng_ref[:, g * half:(g + 1) * half]) for g in range(SSD_GROUPS)]
    o_ref[...] = jnp.concatenate(outs, axis=1).astype(o_ref.dtype)


def _ssd_mixer(u_ssd, bsz, seq, conv_w, conv_b, dt_bias, a_log, d_skip, norm_g):
    m, width = u_ssd.shape
    nch = seq // SSD_CHUNK
    pad = LANES - SSD_HEADS

    def padded(v):
        return jnp.concatenate([v.astype(F32), jnp.zeros((pad,), F32)]).reshape(1, LANES)

    return pl.pallas_call(
        _ssd_kernel,
        grid=(bsz, nch),
        in_specs=[
            pl.BlockSpec((SSD_CHUNK, width), lambda b, c: (b * nch + c, 0)),
            _full((SSD_CONV, SSD_CONV_DIM)), _full((1, SSD_CONV_DIM)),
            _full((1, LANES)), _full((1, LANES)), _full((1, SSD_INNER)), _full((1, SSD_INNER)),
        ],
        out_specs=pl.BlockSpec((SSD_CHUNK, SSD_INNER), lambda b, c: (b * nch + c, 0)),
        out_shape=jax.ShapeDtypeStruct((m, SSD_INNER), BF16),
        scratch_shapes=[pltpu.VMEM((SSD_CHUNK + 8, SSD_CONV_DIM), F32),
                        pltpu.VMEM((SSD_HEADS // 2, SSD_STATE, LANES), F32)],
        compiler_params=_cparams("arbitrary", "arbitrary"),
        name="ssd_mixer",
    )(u_ssd, conv_w.astype(F32), conv_b.reshape(1, -1).astype(F32), padded(dt_bias), padded(a_log),
      jnp.repeat(d_skip.astype(F32), HEAD_DIM).reshape(1, SSD_INNER), norm_g.reshape(1, SSD_INNER).astype(F32))


def _nsa_compress_kernel(x_ref, pe_ref, w1_ref, b1_ref, w2_ref, b2_ref, o_ref):
    x = x_ref[0, 0, 0]
    npc = x.shape[0]
    half = NSA_CMP_STRIDE * HEAD_DIM
    top = _dot((x + pe_ref[0, :, 0:half]).astype(BF16), w1_ref[0, 0:half, :])
    bot = _dot((x + pe_ref[0, :, half:]).astype(BF16), w1_ref[0, half:, :])
    pre = top + pltpu.roll(bot, npc - 1, 0) + b1_ref[0]
    hid = _gelu_tanh(pre).astype(BF16)
    out = _dot(hid, w2_ref[0]) + b2_ref[0]
    rowi = lax.broadcasted_iota(jnp.int32, out.shape, 0)
    o_ref[0, 0, 0] = jnp.where(rowi < npc - 1, out, 0.0)


def _nsa_compress(kvc, bsz, seq, pe, w1, b1, w2, b2):
    npc = seq // NSA_CMP_STRIDE
    flat = NSA_CMP_STRIDE * HEAD_DIM
    x = kvc.reshape(bsz, npc, NSA_CMP_STRIDE, 2, NSA_KV, HEAD_DIM)
    x = x.transpose(0, 3, 4, 1, 2, 5).reshape(bsz, 2, NSA_KV, npc, flat)
    return pl.pallas_call(
        _nsa_compress_kernel,
        grid=(bsz, 2, NSA_KV),
        in_specs=[
            pl.BlockSpec((1, 1, 1, npc, flat), lambda b, s, g: (b, s, g, 0, 0)),
            pl.BlockSpec((1, 1, 2 * flat), lambda b, s, g: (s, 0, 0)),
            pl.BlockSpec((1, 2 * flat, NSA_CMP_HIDDEN), lambda b, s, g: (s, 0, 0)),
            pl.BlockSpec((1, 1, NSA_CMP_HIDDEN), lambda b, s, g: (s, 0, 0)),
            pl.BlockSpec((1, NSA_CMP_HIDDEN, HEAD_DIM), lambda b, s, g: (s, 0, 0)),
            pl.BlockSpec((1, 1, HEAD_DIM), lambda b, s, g: (s, 0, 0)),
        ],
        out_specs=pl.BlockSpec((1, 1, 1, npc, HEAD_DIM), lambda b, s, g: (b, s, g, 0, 0)),
        out_shape=jax.ShapeDtypeStruct((bsz, 2, NSA_KV, npc, HEAD_DIM), F32),
        compiler_params=_cparams("parallel", "parallel", "parallel"),
        name="nsa_compress",
    )(x, pe.reshape(2, 1, 2 * flat).astype(F32), w1.astype(BF16), b1.reshape(2, 1, -1).astype(F32),
      w2.astype(BF16), b2.reshape(2, 1, -1).astype(F32))


def _nsa_block_onehot(rows):
    r = lax.broadcasted_iota(jnp.int32, (rows, 2 * NSA_KVW), 0)
    c = lax.broadcasted_iota(jnp.int32, (rows, 2 * NSA_KVW), 1)
    blk = (r % NSA_KTILE) // NSA_SLC_LEN
    return jnp.where((c % NSA_KVW) == HEAD_DIM + blk, 1.0, 0.0).astype(BF16)


def _nsa_kernel(q_ref, kc_ref, vc_ref, ks_ref, kw_ref, vs_ref, vw_ref, g_ref, o_ref,
                bias_s, qaug_s, qaug2_s, sa_s, sb_s, pa_s, pb_s):
    qb = pl.program_id(2)
    nqt = NSA_RPG * NSA_QBLK
    s0 = qb * NSA_QBLK
    ncp = kc_ref.shape[2]
    ns = ncp // 4
    heads = [slice(r * NSA_QBLK, (r + 1) * NSA_QBLK) for r in range(NSA_RPG)]

    qcat = jnp.concatenate([q_ref[0, r * HEAD_DIM:(r + 1) * HEAD_DIM, :] for r in range(NSA_RPG)], axis=1)
    qpos = s0 + lax.broadcasted_iota(jnp.int32, (1, NSA_QBLK), 1)
    qaug_s[0:HEAD_DIM, :] = qcat
    qaug_s[HEAD_DIM:, :] = jnp.zeros((HEAD_DIM, nqt), BF16)

    kc = kc_ref[0, 0]
    rc = lax.broadcasted_iota(jnp.int32, (ncp, 1), 0)
    ncmp = 4 * (rc % ns) + rc // ns
    cbias = jnp.where((ncmp * NSA_CMP_STRIDE + (NSA_CMP_LEN - 1)) <= qpos, 0.0, NEG)
    cvalid = jnp.where(qpos >= NSA_CMP_LEN - 1, 1.0, 0.0)
    psum = jnp.zeros((ncp, NSA_QBLK), F32)
    p_all = []
    for sl in heads:
        s = _dot(kc, qcat[:, sl]) + cbias
        e = jnp.exp2(s - jnp.max(s, axis=0, keepdims=True))
        p = e * (cvalid / jnp.sum(e, axis=0, keepdims=True))
        psum = psum + p
        p_all.append(p.astype(BF16))
    o_cmp = _dot(vc_ref[0, 0], jnp.concatenate(p_all, axis=1))

    p3 = psum[3 * ns:4 * ns]
    rj = lax.broadcasted_iota(jnp.int32, (ns, NSA_QBLK), 0)
    p3_prev = jnp.where(rj >= 1, pltpu.roll(p3, 1, 0), 0.0)
    imp = psum[0:ns] + psum[ns:2 * ns] + psum[2 * ns:3 * ns] + p3 + p3_prev
    cur = qpos // NSA_SLC_LEN
    forced = (rj == 0) | (rj == cur) | (rj == cur - 1)
    rjf = rj.astype(F32)
    taken = -3.0e38
    imp = jnp.where(forced, taken, jnp.where(rj <= cur, imp, -FORCE))
    for _ in range(min(NSA_TOPK, ns) - 3):
        mx = jnp.max(imp, axis=0, keepdims=True)
        first = jnp.min(jnp.where(imp == mx, rjf, float(ns)), axis=0, keepdims=True)
        imp = jnp.where(rjf == first, taken, imp)
    bias_s[...] = jnp.where(imp == taken, 0.0, NEG)

    kt_diag = s0 // NSA_KTILE
    blocks_per_tile = NSA_KTILE // NSA_SLC_LEN
    vtiles = NSA_KTILE // LANES

    def qk_tile(kt, qaug_ref):
        k0 = pl.multiple_of(kt * NSA_KTILE, NSA_KTILE)
        b8 = bias_s[pl.ds(pl.multiple_of(kt * blocks_per_tile, blocks_per_tile), blocks_per_tile), :]
        b16 = jnp.concatenate([b8, jnp.zeros_like(b8)], axis=0).astype(BF16)
        qaug_ref[HEAD_DIM:HEAD_DIM + 16, :] = jnp.concatenate([b16] * NSA_RPG, axis=1)
        return _dot(ks_ref[pl.ds(k0, NSA_KTILE), :], qaug_ref[...])

    ones_rows = jnp.ones((16, NSA_KTILE), BF16)

    def pv_tile(kt, p):
        vt = jnp.concatenate([vs_ref[kt * vtiles + i] for i in range(vtiles)], axis=1)
        return _dot(jnp.concatenate([vt, ones_rows], axis=0), p)

    def softmax_tile(s, m_old):
        m_new = jnp.maximum(m_old, jnp.max(s, axis=0, keepdims=True))
        p = jnp.exp2((s - m_new).astype(BF16))
        return p, m_new, jnp.exp2(m_old - m_new)

    def visible(kt):
        kpos = kt * NSA_KTILE + lax.broadcasted_iota(jnp.int32, (NSA_KTILE, 1), 0)
        return jnp.concatenate([kpos <= qpos] * NSA_RPG, axis=1)

    def tile_group(i, carry, last):
        m_run, acc, alpha_prev = carry
        for k in range(NSA_UNROLL):
            t = NSA_UNROLL * i + k
            s_cur, p_cur, s_nxt, p_prv, qa = ((sa_s, pa_s, sb_s, pb_s, qaug_s) if k % 2 == 0 else
                                              (sb_s, pb_s, sa_s, pa_s, qaug2_s))
            acc = alpha_prev * acc + pv_tile(jnp.maximum(t - 1, 0), p_prv[...])
            s = jnp.where(visible(t), s_cur[...], NEG) if last else s_cur[...]
            p, m_run, alpha_prev = softmax_tile(s, m_run)
            p_cur[...] = p
            if last and k == NSA_UNROLL - 1:
                acc = alpha_prev * acc + pv_tile(t, p_cur[...])
            else:
                s_nxt[...] = qk_tile(t + 1, qa)
        return m_run, acc, alpha_prev

    qaug2_s[...] = qaug_s[...]
    pb_s[...] = jnp.zeros((NSA_KTILE, nqt), BF16)
    sa_s[...] = qk_tile(0, qaug2_s)

    span = NSA_WIN + NSA_QBLK
    start = pl.multiple_of(jnp.maximum(s0 - NSA_WIN, 0), NSA_QBLK)
    kwin = kw_ref[pl.ds(start, span), :]
    kp = start + lax.broadcasted_iota(jnp.int32, (span, 1), 0)
    wbias = jnp.where((kp <= qpos) & (kp > qpos - NSA_WIN), 0.0, NEG)
    pw, dens = [], []
    for sl in heads:
        s = _dot(kwin, qaug_s[:, sl]) + wbias
        e = jnp.exp2(s - jnp.max(s, axis=0, keepdims=True))
        dens.append(jnp.sum(e, axis=0, keepdims=True))
        pw.append(e.astype(BF16))
    sblk = start // LANES
    vwt = jnp.concatenate([vw_ref[sblk + i] for i in range(span // LANES)], axis=1)
    o_win = _dot(vwt, jnp.concatenate(pw, axis=1)) * (1.0 / jnp.concatenate(dens, axis=1))

    init = (jnp.full((1, nqt), NEG, F32), jnp.zeros((HEAD_DIM + 16, nqt), F32), jnp.ones((1, nqt), F32))
    group_diag = kt_diag // NSA_UNROLL
    carry = lax.fori_loop(0, group_diag, lambda i, c: tile_group(i, c, False), init)
    _, acc, _ = tile_group(group_diag, carry, True)
    o_slc = acc[0:HEAD_DIM] * (1.0 / acc[HEAD_DIM:HEAD_DIM + 1])

    gates = _sigmoid(g_ref[0])
    outs = [gates[r:r + 1, :] * o_cmp[:, sl] + gates[NSA_RPG + r:NSA_RPG + r + 1, :] * o_slc[:, sl]
            + gates[2 * NSA_RPG + r:2 * NSA_RPG + r + 1, :] * o_win[:, sl] for r, sl in enumerate(heads)]
    o_ref[...] = jnp.concatenate(outs, axis=0).T.astype(o_ref.dtype)


def _nsa_mixer(q_tt, kvc, k_slc, k_win, v_tt, g_tt, bsz, seq, pe, w1, b1, w2, b2):
    nqb = seq // NSA_QBLK
    npc = seq // NSA_CMP_STRIDE
    ns = seq // NSA_SLC_LEN
    cmp_out = _nsa_compress(kvc, bsz, seq, pe, w1, b1, w2, b2)
    perm = cmp_out.reshape(bsz, 2, NSA_KV, ns, 4, HEAD_DIM).transpose(0, 1, 2, 4, 3, 5).reshape(
        bsz, 2, NSA_KV, npc, HEAD_DIM)
    kc = perm[:, 0].astype(BF16)
    vc_t = perm[:, 1].transpose(0, 1, 3, 2).astype(BF16)
    gq = NSA_RPG * HEAD_DIM
    return pl.pallas_call(
        _nsa_kernel,
        grid=(bsz, NSA_KV, nqb),
        in_specs=[
            pl.BlockSpec((1, gq, LANES), lambda b, g, q: (b * nqb + q, g, 0)),
            pl.BlockSpec((1, 1, npc, HEAD_DIM), lambda b, g, q: (b, g, 0, 0)),
            pl.BlockSpec((1, 1, HEAD_DIM, npc), lambda b, g, q: (b, g, 0, 0)),
            pl.BlockSpec((seq, NSA_KVW), lambda b, g, q: (b, g)),
            pl.BlockSpec((seq, NSA_KVW), lambda b, g, q: (b, g)),
            pl.BlockSpec((nqb, HEAD_DIM, LANES), lambda b, g, q: (b, g, 0)),
            pl.BlockSpec((nqb, HEAD_DIM, LANES), lambda b, g, q: (b, NSA_KV + g, 0)),
            pl.BlockSpec((1, 16, LANES), lambda b, g, q: (b * nqb + q, g, 0)),
        ],
        out_specs=pl.BlockSpec((NSA_QBLK, gq), lambda b, g, q: (b * nqb + q, g)),
        out_shape=jax.ShapeDtypeStruct((bsz * seq, NSA_Q), BF16),
        scratch_shapes=[pltpu.VMEM((ns, NSA_QBLK), F32),
                        pltpu.VMEM((2 * HEAD_DIM, NSA_RPG * NSA_QBLK), BF16),
                        pltpu.VMEM((2 * HEAD_DIM, NSA_RPG * NSA_QBLK), BF16),
                        pltpu.VMEM((NSA_KTILE, NSA_RPG * NSA_QBLK), F32),
                        pltpu.VMEM((NSA_KTILE, NSA_RPG * NSA_QBLK), F32),
                        pltpu.VMEM((NSA_KTILE, NSA_RPG * NSA_QBLK), BF16),
                        pltpu.VMEM((NSA_KTILE, NSA_RPG * NSA_QBLK), BF16)],
        compiler_params=_cparams("arbitrary", "arbitrary", "arbitrary"),
        name="nsa_attention",
    )(q_tt, kc, vc_t, k_slc, k_win, v_tt, v_tt, g_tt)


def _swa_kernel(q_ref, kp_ref, kc_ref, vp_ref, vc_ref, sink_ref, o_ref):
    qb = pl.program_id(1)
    t = SWA_WIN
    kband = jnp.concatenate([kp_ref[...], kc_ref[...]], axis=0)
    krel = lax.broadcasted_iota(jnp.int32, (2 * t, 1), 0) - t
    qrel = lax.broadcasted_iota(jnp.int32, (1, t), 1)
    lowest = jnp.where(qb > 0, -t, 0)
    mbias = jnp.where((krel <= qrel) & (krel > qrel - SWA_WIN) & (krel >= lowest), 0.0, NEG)
    outs = []
    for g in range(2):
        rows = slice(g * SWA_RPG * HEAD_DIM, (g + 1) * SWA_RPG * HEAD_DIM)
        qg = q_ref[0, rows, :]
        qcat = jnp.concatenate([qg[r * HEAD_DIM:(r + 1) * HEAD_DIM, :] for r in range(SWA_RPG)], axis=1)
        zq = jnp.zeros_like(qcat)
        qext = jnp.concatenate([qcat, zq] if g == 0 else [zq, qcat], axis=0)
        s = _dot(kband, qext)
        ps, dens = [], []
        for r in range(SWA_RPG):
            h = g * SWA_RPG + r
            sink = sink_ref[h:h + 1, :]
            sr = s[:, r * t:(r + 1) * t] + mbias
            mx = jnp.maximum(jnp.max(sr, axis=0, keepdims=True), sink)
            e = jnp.exp2(sr - mx)
            dens.append(jnp.sum(e, axis=0, keepdims=True) + jnp.exp2(sink - mx))
            ps.append(e.astype(BF16))
        vband = jnp.concatenate([vp_ref[0, g * HEAD_DIM:(g + 1) * HEAD_DIM, :],
                                 vc_ref[0, g * HEAD_DIM:(g + 1) * HEAD_DIM, :]], axis=1)
        og = _dot(vband, jnp.concatenate(ps, axis=1)) * (1.0 / jnp.concatenate(dens, axis=1))
        outs.append(jnp.concatenate([og[:, r * t:(r + 1) * t] for r in range(SWA_RPG)], axis=0).T)
    o_ref[...] = jnp.concatenate(outs, axis=1).astype(o_ref.dtype)


def _swa_mixer(q_tt, k_nat, v_tt, sinks, bsz, seq):
    nqb = seq // SWA_WIN
    sink_rows = jnp.broadcast_to((sinks.astype(F32) * math.log2(math.e))[:, None], (sinks.shape[0], LANES))
    prev = lambda b, q: b * nqb + jnp.maximum(q - 1, 0)
    return pl.pallas_call(
        _swa_kernel,
        grid=(bsz, nqb),
        in_specs=[
            pl.BlockSpec((1, SWA_Q, LANES), lambda b, q: (b * nqb + q, 0, 0)),
            pl.BlockSpec((SWA_WIN, SWA_KVW), lambda b, q: (prev(b, q), 0)),
            pl.BlockSpec((SWA_WIN, SWA_KVW), lambda b, q: (b * nqb + q, 0)),
            pl.BlockSpec((1, SWA_KVW, LANES), lambda b, q: (prev(b, q), 0, 0)),
            pl.BlockSpec((1, SWA_KVW, LANES), lambda b, q: (b * nqb + q, 0, 0)),
            _full(sink_rows.shape),
        ],
        out_specs=pl.BlockSpec((SWA_WIN, SWA_Q), lambda b, q: (b * nqb + q, 0)),
        out_shape=jax.ShapeDtypeStruct((bsz * seq, SWA_Q), BF16),
        compiler_params=_cparams("parallel", "parallel"),
        name="swa_attention",
    )(q_tt, k_nat, k_nat, v_tt, v_tt, sink_rows)


def _s5_params(a_re, a_im, log_dt, b_re, b_im, c_re, c_im, n_chunks):
    f = F32
    t = S5_CHUNK
    step = jnp.exp(log_dt.astype(f))[:, None]
    lr, li = a_re.astype(f), a_im.astype(f)

    def lam_pow(tau):
        tau = tau.astype(f)[:, None, None]
        mag = jnp.exp(lr * step * tau)
        ang = li * step * tau
        return mag * jnp.cos(ang), mag * jnp.sin(ang)

    lb_r, lb_i = (v[0] for v in lam_pow(jnp.ones((1,))))
    nr, ni = lb_r - 1.0, lb_i
    den = lr * lr + li * li
    fr, fi = (nr * lr + ni * li) / den, (ni * lr - nr * li) / den
    br, bi = b_re.astype(f), b_im.astype(f)
    bb_r = fr[..., None] * br - fi[..., None] * bi
    bb_i = fr[..., None] * bi + fi[..., None] * br
    cr, ci = c_re.astype(f), c_im.astype(f)

    pr, pi = lam_pow(jnp.arange(t + 1))
    cl_r = cr[None] * pr[:, :, None, :] - ci[None] * pi[:, :, None, :]
    cl_i = cr[None] * pi[:, :, None, :] + ci[None] * pr[:, :, None, :]
    kern = jnp.einsum("tghp,gpk->gthk", cl_r[:t], bb_r, precision="highest") - jnp.einsum(
        "tghp,gpk->gthk", cl_i[:t], bb_i, precision="highest")
    lag = jnp.arange(t)[None, :] - jnp.arange(t)[:, None]
    toep = jnp.where((lag >= 0)[None, :, :, None, None], kern[:, jnp.clip(lag, 0, t - 1)], 0.0)
    toep = toep.transpose(0, 1, 4, 2, 3).reshape(S5_GROUPS, t * S5_GROUP_CH, t * S5_GROUP_CH)
    rr, ri = pr[t - 1 - jnp.arange(t)], pi[t - 1 - jnp.arange(t)]
    bs_r = rr[..., None] * bb_r[None] - ri[..., None] * bb_i[None]
    bs_i = rr[..., None] * bb_i[None] + ri[..., None] * bb_r[None]
    bs = jnp.concatenate([bs_r, bs_i], axis=2)
    bs = bs.transpose(1, 0, 3, 2).reshape(S5_GROUPS, t * S5_GROUP_CH, 2 * S5_STATE)
    cs = jnp.concatenate([cl_r[1:], -cl_i[1:]], axis=3)
    cs = cs.transpose(1, 3, 0, 2).reshape(S5_GROUPS, 2 * S5_STATE, t * S5_GROUP_CH)
    ar, ai = pr[t], pi[t]
    a1, a2 = [], []
    k = 1
    while k < n_chunks:
        a1.append(jnp.concatenate([ar, ar], axis=1))
        a2.append(jnp.concatenate([-ai, ai], axis=1))
        ar, ai = ar * ar - ai * ai, 2.0 * ar * ai
        k *= 2
    a1 = jnp.stack(a1, axis=1)
    a2 = jnp.stack(a2, axis=1)
    return toep.astype(BF16), bs.astype(BF16), cs.astype(BF16), a1, a2


def _s5_kernel(bsz, u_ref, toep_ref, bs_ref, cs_ref, a1_ref, a2_ref, o_ref):
    u = u_ref[0]
    n_all = u.shape[0]
    n = n_all // bsz
    sc = _dot(u, bs_ref[0])
    rowi = lax.broadcasted_iota(jnp.int32, (n, 2 * S5_STATE), 0)
    h_in = []
    for b in range(bsz):
        x = sc[b * n:(b + 1) * n]
        k, step = 1, 0
        while k < n:
            xs = jnp.where(rowi >= k, pltpu.roll(x, k, 0), 0.0)
            xs_sw = pltpu.roll(xs, S5_STATE, 1)
            x = x + a1_ref[0, step:step + 1, :] * xs + a2_ref[0, step:step + 1, :] * xs_sw
            k *= 2
            step += 1
        h_in.append(jnp.where(rowi >= 1, pltpu.roll(x, 1, 0), 0.0))
    h_in = jnp.concatenate(h_in, axis=0)
    hi = h_in.astype(BF16)
    lo = (h_in - hi.astype(F32)).astype(BF16)
    o_ref[0] = (_dot(u, toep_ref[0]) + _dot(hi, cs_ref[0]) + _dot(lo, cs_ref[0])).astype(o_ref.dtype)


def _s5_glu_kernel(y_ref, u_ref, d_ref, w_ref, b_ref, o_ref):
    y = _gelu_tanh(y_ref[...] + d_ref[...] * u_ref[...])
    gate = _sigmoid(_dot(y.astype(BF16), w_ref[...]) + b_ref[...])
    o_ref[...] = (y * gate).astype(o_ref.dtype)


def _s5_mixer(u5, bsz, seq, a_re, a_im, log_dt, b_re, b_im, c_re, c_im, d_skip, glu_w, glu_b, tm=1024):
    m = u5.shape[0]
    t = S5_CHUNK
    nch = m // t
    width = t * S5_GROUP_CH
    toep, bs, cs, a1, a2 = _s5_params(a_re, a_im, log_dt, b_re, b_im, c_re, c_im, nch // bsz)
    nsteps = a1.shape[1]
    ug = u5.reshape(nch, t, S5_GROUPS, S5_GROUP_CH).transpose(2, 0, 1, 3).reshape(S5_GROUPS, nch, width).astype(BF16)
    yg = pl.pallas_call(
        functools.partial(_s5_kernel, bsz),
        grid=(S5_GROUPS,),
        in_specs=[
            pl.BlockSpec((1, nch, width), lambda g: (g, 0, 0)),
            pl.BlockSpec((1, width, width), lambda g: (g, 0, 0)),
            pl.BlockSpec((1, width, 2 * S5_STATE), lambda g: (g, 0, 0)),
            pl.BlockSpec((1, 2 * S5_STATE, width), lambda g: (g, 0, 0)),
            pl.BlockSpec((1, nsteps, 2 * S5_STATE), lambda g: (g, 0, 0)),
            pl.BlockSpec((1, nsteps, 2 * S5_STATE), lambda g: (g, 0, 0)),
        ],
        out_specs=pl.BlockSpec((1, nch, width), lambda g: (g, 0, 0)),
        out_shape=jax.ShapeDtypeStruct((S5_GROUPS, nch, width), BF16),
        compiler_params=_cparams("parallel"),
        name="s5_scan",
    )(ug, toep, bs, cs, a1, a2)
    y = yg.reshape(S5_GROUPS, nch, t, S5_GROUP_CH).transpose(1, 2, 0, 3).reshape(m, S5_CH)
    return pl.pallas_call(
        _s5_glu_kernel,
        grid=(m // tm,),
        in_specs=[
            pl.BlockSpec((tm, S5_CH), lambda i: (i, 0)),
            pl.BlockSpec((tm, S5_CH), lambda i: (i, 0)),
            _full((1, S5_CH)), _full((S5_CH, S5_CH)), _full((1, S5_CH)),
        ],
        out_specs=pl.BlockSpec((tm, S5_CH), lambda i: (i, 0)),
        out_shape=jax.ShapeDtypeStruct((m, S5_CH), BF16),
        compiler_params=_cparams("parallel"),
        name="s5_glu",
    )(y, u5, d_skip.reshape(1, S5_CH).astype(F32), glu_w.astype(BF16), glu_b.reshape(1, S5_CH).astype(F32))


def _even_mixers(h, bsz, seq, g_mix, w_in, conv_w, conv_b, dt_bias, a_log, d_skip, norm_g, pe, w1, b1, w2, b2):
    d = h.shape[1]
    scale = HEAD_DIM ** -0.5 * math.log2(math.e)
    o = SSD_IN
    w_ssd = jnp.concatenate([w_in[:, :SSD_IN], jnp.zeros((d, LANES - SSD_HEADS), w_in.dtype)], axis=1)
    w_q = w_in[:, o:o + NSA_Q] * scale
    kv = [w_in[:, o + NSA_Q + i * NSA_KVW:o + NSA_Q + (i + 1) * NSA_KVW] for i in range(6)]
    w_gate = w_in[:, o + NSA_Q + 6 * NSA_KVW:].reshape(d, NSA_KV, NSA_RPG, 3).transpose(0, 1, 3, 2)
    w_gate = jnp.concatenate([w_gate.reshape(d, NSA_KV, 12), jnp.zeros((d, NSA_KV, 4), w_in.dtype)],
                             axis=2).reshape(d, NSA_KV * 16)

    def per_group_halves(w):
        wg = w.reshape(d, NSA_KV, HEAD_DIM)
        return jnp.concatenate([wg, jnp.zeros_like(wg)], axis=2).reshape(d, NSA_KV * NSA_KVW)
    segs = [
        ("nat", w_ssd, F32),
        ("tt", w_q, BF16),
        ("nat", jnp.concatenate([kv[0], kv[1]], axis=1), F32),
        ("nat+", per_group_halves(kv[2]), BF16, _nsa_block_onehot(NSA_KTILE)),
        ("nat", per_group_halves(kv[4]), BF16),
        ("tt", jnp.concatenate([kv[3], kv[5]], axis=1), BF16),
        ("tt", w_gate, F32),
    ]
    u_ssd, q_tt, kvc, k_slc, k_win, v_tt, g_tt = _norm_proj(h, g_mix, segs)
    ya = _ssd_mixer(u_ssd, bsz, seq, conv_w, conv_b, dt_bias, a_log, d_skip, norm_g)
    yb = _nsa_mixer(q_tt, kvc, k_slc, k_win, v_tt, g_tt, bsz, seq, pe, w1, b1, w2, b2)
    return ya, yb


def _odd_mixers(h, bsz, seq, g_mix, w_in, sinks, a_re, a_im, log_dt, b_re, b_im, c_re, c_im, d_skip, glu_w, glu_b):
    scale = HEAD_DIM ** -0.5 * math.log2(math.e)
    segs = [
        ("tt", w_in[:, :SWA_Q] * scale, BF16),
        ("nat", w_in[:, SWA_Q:SWA_Q + SWA_KVW], BF16),
        ("tt", w_in[:, SWA_Q + SWA_KVW:SWA_Q + 2 * SWA_KVW], BF16),
        ("nat", w_in[:, SWA_Q + 2 * SWA_KVW:], F32),
    ]
    q_tt, k_nat, v_tt, u5 = _norm_proj(h, g_mix, segs)
    yc = _swa_mixer(q_tt, k_nat, v_tt, sinks, bsz, seq)
    yd = _s5_mixer(u5, bsz, seq, a_re, a_im, log_dt, b_re, b_im, c_re, c_im, d_skip, glu_w, glu_b)
    return yc, yd


def kernel(x, norm_mix, norm_mlp, norm_final, mlp_w_up, mlp_w_down, ev_w_in, ev_w_out, ssd_conv_w, ssd_conv_b,
           ssd_dt_bias, ssd_a_log, ssd_d, ssd_norm, nsa_pe, nsa_cmp_w1, nsa_cmp_b1, nsa_cmp_w2, nsa_cmp_b2,
           od_w_in, od_w_out, swa_sinks, s5_a_re, s5_a_im, s5_log_dt, s5_b_re, s5_b_im, s5_c_re, s5_c_im,
           s5_d, s5_glu_w, s5_glu_b):
    bsz, seq, d = x.shape
    depth = norm_mix.shape[0]
    assert seq % (NSA_UNROLL * NSA_KTILE) == 0 and seq >= NSA_WIN + NSA_QBLK
    h = x.reshape(bsz * seq, d)
    for layer in range(depth):
        i = layer // 2
        if layer % 2 == 0:
            ya, yb = _even_mixers(h, bsz, seq, norm_mix[layer], ev_w_in[i], ssd_conv_w[i], ssd_conv_b[i],
                                  ssd_dt_bias[i], ssd_a_log[i], ssd_d[i], ssd_norm[i], nsa_pe[i],
                                  nsa_cmp_w1[i], nsa_cmp_b1[i], nsa_cmp_w2[i], nsa_cmp_b2[i])
            w_out = ev_w_out[i]
        else:
            ya, yb = _odd_mixers(h, bsz, seq, norm_mix[layer], od_w_in[i], swa_sinks[i], s5_a_re[i], s5_a_im[i],
                                 s5_log_dt[i], s5_b_re[i], s5_b_im[i], s5_c_re[i], s5_c_im[i], s5_d[i],
                                 s5_glu_w[i], s5_glu_b[i])
            w_out = od_w_out[i]
        h = _mix_out_mlp(h, ya, yb, w_out, norm_mlp[layer], mlp_w_up[layer], mlp_w_down[layer], norm_final,
                         final=(layer == depth - 1))
    return h.reshape(bsz, seq, d)
```

```python
import functools
import math

import jax
import jax.numpy as jnp
from jax import lax
from jax.experimental import pallas as pl
from jax.experimental.pallas import tpu as pltpu

F32 = jnp.float32
BF16 = jnp.bfloat16

EPS = 1e-6
NEG = -1e30
FORCE = 1e9
HEAD_DIM = 64
LANES = 128
VMEM_LIMIT_BYTES = 56 * 1024 * 1024

SSD_HEADS = 8
SSD_INNER = 512
SSD_GROUPS = 2
SSD_STATE = 128
SSD_CONV = 4
SSD_CHUNK = 128
SSD_CONV_DIM = 1024
SSD_IN = SSD_INNER + SSD_CONV_DIM + SSD_HEADS

NSA_HEADS = 8
NSA_KV = 2
NSA_RPG = 4
NSA_CMP_LEN = 32
NSA_CMP_STRIDE = 16
NSA_SLC_LEN = 64
NSA_TOPK = 16
NSA_WIN = 512
NSA_CMP_HIDDEN = 256
NSA_QBLK = 128
NSA_Q = 512
NSA_KVW = 128
NSA_KTILE = 512
NSA_UNROLL = 2

SWA_RPG = 4
SWA_WIN = 128
SWA_Q = 512
SWA_KVW = 128

S5_CH = 512
S5_GROUP_CH = 16
S5_GROUPS = 32
S5_STATE = 64
S5_CHUNK = 32


def _cparams(*sem):
    return pltpu.CompilerParams(dimension_semantics=sem, vmem_limit_bytes=VMEM_LIMIT_BYTES)


def _full(shape):
    n = len(shape)
    return pl.BlockSpec(shape, lambda *_: (0,) * n)


def _dot(a, b):
    return jnp.dot(a, b, preferred_element_type=F32)


def _dot_nt(a, b):
    return lax.dot_general(a, b, (((1,), (1,)), ((), ())), preferred_element_type=F32)


def _split3(a):
    hi = a.astype(BF16)
    r1 = a - hi.astype(F32)
    mid = r1.astype(BF16)
    lo = (r1 - mid.astype(F32)).astype(BF16)
    return hi, mid, lo


def _dot_exact_rhs(a, b_exact):
    hi, mid, lo = _split3(a)
    return _dot(hi, b_exact) + _dot(mid, b_exact) + _dot(lo, b_exact)


def _dot_exact_lhs(a_exact, b):
    hi, mid, lo = _split3(b)
    return _dot(a_exact, hi) + _dot(a_exact, mid) + _dot(a_exact, lo)


def _rms(x, g):
    return x * lax.rsqrt(jnp.mean(x * x, axis=-1, keepdims=True) + EPS) * g


def _gelu_tanh(x):
    c = math.sqrt(2.0 / math.pi)
    return 0.5 * x * (1.0 + jnp.tanh(c * (x + 0.044715 * (x * x * x))))


def _sigmoid(x):
    return 1.0 / (1.0 + jnp.exp(-x))


def _proj_kernel(kinds, tm, h_ref, g_ref, *refs):
    n = len(kinds)
    n_add = sum(k == "nat+" for k in kinds)
    w_refs, add_refs, o_refs = refs[:n], list(refs[n:n + n_add]), refs[n + n_add:2 * n + n_add]
    scratch = list(refs[2 * n + n_add:])
    yb = _rms(h_ref[...], g_ref[...]).astype(BF16)
    for kind, w_ref, o_ref in zip(kinds, w_refs, o_refs):
        if kind == "nat":
            o_ref[...] = _dot(yb, w_ref[...]).astype(o_ref.dtype)
        elif kind == "nat+":
            o_ref[...] = _dot(yb, w_ref[...]).astype(o_ref.dtype) + add_refs.pop(0)[...]
        elif kind == "pc":
            pc_s = scratch.pop(0)
            res = _dot(yb, w_ref[...])
            flat = NSA_CMP_STRIDE * HEAD_DIM
            per_tile = LANES // HEAD_DIM
            for j in range(pc_s.shape[0]):
                pc_s[j] = res[:, j * LANES:(j + 1) * LANES]
            for p in range(NSA_CMP_STRIDE):
                for j in range(pc_s.shape[0]):
                    tok = pc_s[j, pl.ds(p, tm // NSA_CMP_STRIDE, stride=NSA_CMP_STRIDE), :]
                    for c in range(per_tile):
                        col = (j * per_tile + c) * flat + p * HEAD_DIM
                        o_ref[:, col:col + HEAD_DIM] = tok[:, c * HEAD_DIM:(c + 1) * HEAD_DIM].astype(o_ref.dtype)
        else:
            ot = _dot_nt(w_ref[...], yb)
            for j in range(tm // LANES):
                o_ref[j] = ot[:, j * LANES:(j + 1) * LANES].astype(o_ref.dtype)


def _norm_proj(h, g, segs, tm=512):
    m, d = h.shape
    kinds = tuple(s[0] for s in segs)
    adds = [s[3] for s in segs if s[0] == "nat+"]
    ws, w_specs, out_shapes, out_specs, scratch = [], [], [], [], []
    for kind, w, dt in (s[:3] for s in segs):
        n_out = w.shape[1]
        if kind == "pc":
            ws.append(w.astype(BF16))
            w_specs.append(_full((d, n_out)))
            out_shapes.append(jax.ShapeDtypeStruct((m // NSA_CMP_STRIDE, NSA_CMP_STRIDE * n_out), dt))
            out_specs.append(pl.BlockSpec((tm // NSA_CMP_STRIDE, NSA_CMP_STRIDE * n_out), lambda i: (i, 0)))
            scratch.append(pltpu.VMEM((n_out // LANES, tm, LANES), F32))
        elif kind in ("nat", "nat+"):
            ws.append(w.astype(BF16))
            w_specs.append(_full((d, n_out)))
            out_shapes.append(jax.ShapeDtypeStruct((m, n_out), dt))
            out_specs.append(pl.BlockSpec((tm, n_out), lambda i: (i, 0)))
        else:
            ws.append(w.T.astype(BF16))
            w_specs.append(_full((n_out, d)))
            out_shapes.append(jax.ShapeDtypeStruct((m // LANES, n_out, LANES), dt))
            out_specs.append(pl.BlockSpec((tm // LANES, n_out, LANES), lambda i: (i, 0, 0)))
    return pl.pallas_call(
        functools.partial(_proj_kernel, kinds, tm),
        grid=(m // tm,),
        in_specs=[pl.BlockSpec((tm, d), lambda i: (i, 0)), _full((1, d))] + w_specs + [
            pl.BlockSpec((tm, a.shape[1]), functools.partial(lambda i, nb: (i % nb, 0), nb=a.shape[0] // tm))
            for a in adds],
        out_specs=out_specs,
        out_shape=out_shapes,
        scratch_shapes=scratch,
        compiler_params=_cparams("parallel"),
        name="norm_proj",
    )(h, g.reshape(1, d), *ws, *adds)


def _mlp_kernel(final, h_ref, ya_ref, yb_ref, woa_ref, wob_ref, gm_ref, wup_ref, wdn_ref, gf_ref,
                o_ref, h2_s, xn_s, acc_s):
    j = pl.program_id(1)

    @pl.when(j == 0)
    def _():
        h2 = h_ref[...] + _dot(ya_ref[...], woa_ref[...]) + _dot(yb_ref[...], wob_ref[...])
        h2_s[...] = h2
        xn_s[...] = _rms(h2, gm_ref[...]).astype(BF16)
        acc_s[...] = jnp.zeros_like(acc_s)

    hid = jnp.square(jnp.maximum(_dot(xn_s[...], wup_ref[...]), 0.0))
    acc_s[...] += _dot(hid.astype(BF16), wdn_ref[...])

    @pl.when(j == pl.num_programs(1) - 1)
    def _():
        out = h2_s[...] + acc_s[...]
        if final:
            out = _rms(out, gf_ref[...])
        o_ref[...] = out


def _mix_out_mlp(h, ya, yb, w_out, g_mlp, w_up, w_down, g_final, final, tm=1024, tf=1024):
    m, d = h.shape
    dff = w_up.shape[1]
    na = ya.shape[1]
    nb = yb.shape[1]
    return pl.pallas_call(
        functools.partial(_mlp_kernel, final),
        grid=(m // tm, dff // tf),
        in_specs=[
            pl.BlockSpec((tm, d), lambda i, j: (i, 0)),
            pl.BlockSpec((tm, na), lambda i, j: (i, 0)),
            pl.BlockSpec((tm, nb), lambda i, j: (i, 0)),
            _full((na, d)), _full((nb, d)), _full((1, d)),
            pl.BlockSpec((d, tf), lambda i, j: (0, j)),
            pl.BlockSpec((tf, d), lambda i, j: (j, 0)),
            _full((1, d)),
        ],
        out_specs=pl.BlockSpec((tm, d), lambda i, j: (i, 0)),
        out_shape=jax.ShapeDtypeStruct((m, d), F32),
        scratch_shapes=[pltpu.VMEM((tm, d), F32), pltpu.VMEM((tm, d), BF16), pltpu.VMEM((tm, d), F32)],
        compiler_params=_cparams("parallel", "arbitrary"),
        name="out_proj_mlp",
    )(h, ya, yb, w_out[:na].astype(BF16), w_out[na:].astype(BF16), g_mlp.reshape(1, d),
      w_up.astype(BF16), w_down.astype(BF16), g_final.reshape(1, d))


def _ssd_kernel(u_ref, cw_ref, cb_ref, dtb_ref, alog_ref, dsk_ref, ng_ref, o_ref, xext_s, st_s):
    t = SSD_CHUNK
    c = pl.program_id(1)

    @pl.when(c == 0)
    def _():
        xext_s[0:8, :] = jnp.zeros((8, SSD_CONV_DIM), F32)
        st_s[...] = jnp.zeros_like(st_s)

    z = u_ref[:, 0:SSD_INNER]
    dt_raw = u_ref[:, SSD_INNER + SSD_CONV_DIM:]
    xext_s[8:8 + t, :] = u_ref[:, SSD_INNER:SSD_INNER + SSD_CONV_DIM]
    conv = cb_ref[...] + cw_ref[0:1, :] * xext_s[5:5 + t, :]
    for k in range(1, SSD_CONV):
        conv = conv + cw_ref[k:k + 1, :] * xext_s[5 + k:5 + k + t, :]
    xext_s[0:8, :] = xext_s[t:t + 8, :]
    xc = conv * _sigmoid(conv)
    xs = xc[:, 0:SSD_INNER]
    gn = SSD_GROUPS * SSD_STATE

    dtp = dt_raw + dtb_ref[...]
    dt = jnp.maximum(dtp, 0.0) + jnp.log1p(jnp.exp(-jnp.abs(dtp)))
    a = -jnp.exp(alog_ref[...])
    da = dt * a

    row = lax.broadcasted_iota(jnp.int32, (t, t), 0)
    col = lax.broadcasted_iota(jnp.int32, (t, t), 1)
    causal = col <= row
    tril = jnp.where(causal, 1.0, 0.0).astype(BF16)
    a_cum = _dot_exact_lhs(tril, da)
    er = lax.broadcasted_iota(jnp.int32, (LANES, SSD_INNER), 0)
    ec = lax.broadcasted_iota(jnp.int32, (LANES, SSD_INNER), 1)
    expand = jnp.where((ec >> 6) == er, 1.0, 0.0).astype(BF16)
    a_cum_x = _dot_exact_rhs(a_cum, expand)
    dt_x = _dot_exact_rhs(dt, expand)
    a_cum_t = a_cum.T
    a_last_x = a_cum_x[t - 1:t, :]
    decay_end_x = jnp.exp(a_last_x - a_cum_x)
    decay_in_x = jnp.exp(a_cum_x)
    chunk_decay_x = jnp.exp(a_last_x)

    xd = xs * dt_x
    xd_end = (xd * decay_end_x).astype(BF16)
    xd_b = xd.astype(BF16)
    lane = lax.broadcasted_iota(jnp.int32, (t, LANES), 1)
    first_half = lane < HEAD_DIM

    pieces = []
    for g in range(SSD_GROUPS):
        bm = xc[:, SSD_INNER + g * SSD_STATE:SSD_INNER + (g + 1) * SSD_STATE]
        cm = xc[:, SSD_INNER + gn + g * SSD_STATE:SSD_INNER + gn + (g + 1) * SSD_STATE].astype(BF16)
        bm_t = bm.T.astype(BF16)
        cb = _dot_nt(cm, bm.astype(BF16))
        for pr in range(2):
            i = g * 2 + pr
            sl = slice(i * LANES, (i + 1) * LANES)
            ms = []
            for hh in range(2):
                h = 2 * i + hh
                seg = a_cum[:, h:h + 1] - a_cum_t[h:h + 1, :]
                dec = jnp.exp(jnp.where(causal, seg, NEG))
                ms.append((cb * dec).astype(BF16))
            y_diag = jnp.where(first_half, _dot(ms[0], xd_b[:, sl]), _dot(ms[1], xd_b[:, sl]))
            st = st_s[i]
            y_off = _dot(cm, st.astype(BF16)) * decay_in_x[:, sl]
            st_s[i] = st * chunk_decay_x[:, sl] + _dot(bm_t, xd_end[:, sl])
            pieces.append(y_diag + y_off)
    y = jnp.concatenate(pieces, axis=1) + xs * dsk_ref[...]
    y = y * (z * _sigmoid(z))
    half = SSD_INNER // SSD_GROUPS
    outs = [_rms(y[:, g * half:(g + 1) * half], ng_ref[:, g * half:(g + 1) * half]) for g in range(SSD_GROUPS)]
    o_ref[...] = jnp.concatenate(outs, axis=1).astype(o_ref.dtype)


def _ssd_mixer(u_ssd, bsz, seq, conv_w, conv_b, dt_bias, a_log, d_skip, norm_g):
    m, width = u_ssd.shape
    nch = seq // SSD_CHUNK
    pad = LANES - SSD_HEADS

    def padded(v):
        return jnp.concatenate([v.astype(F32), jnp.zeros((pad,), F32)]).reshape(1, LANES)

    return pl.pallas_call(
        _ssd_kernel,
        grid=(bsz, nch),
        in_specs=[
            pl.BlockSpec((SSD_CHUNK, width), lambda b, c: (b * nch + c, 0)),
            _full((SSD_CONV, SSD_CONV_DIM)), _full((1, SSD_CONV_DIM)),
            _full((1, LANES)), _full((1, LANES)), _full((1, SSD_INNER)), _full((1, SSD_INNER)),
        ],
        out_specs=pl.BlockSpec((SSD_CHUNK, SSD_INNER), lambda b, c: (b * nch + c, 0)),
        out_shape=jax.ShapeDtypeStruct((m, SSD_INNER), BF16),
        scratch_shapes=[pltpu.VMEM((SSD_CHUNK + 8, SSD_CONV_DIM), F32),
                        pltpu.VMEM((SSD_HEADS // 2, SSD_STATE, LANES), F32)],
        compiler_params=_cparams("arbitrary", "arbitrary"),
        name="ssd_mixer",
    )(u_ssd, conv_w.astype(F32), conv_b.reshape(1, -1).astype(F32), padded(dt_bias), padded(a_log),
      jnp.repeat(d_skip.astype(F32), HEAD_DIM).reshape(1, SSD_INNER), norm_g.reshape(1, SSD_INNER).astype(F32))


def _nsa_compress_kernel(x_ref, pe_ref, w1_ref, b1_ref, w2_ref, b2_ref, o_ref):
    x = x_ref[...]
    npc = x.shape[0]
    half = NSA_CMP_STRIDE * HEAD_DIM
    top = _dot((x + pe_ref[0, :, 0:half]).astype(BF16), w1_ref[0, 0:half, :])
    bot = _dot((x + pe_ref[0, :, half:]).astype(BF16), w1_ref[0, half:, :])
    pre = top + pltpu.roll(bot, npc - 1, 0) + b1_ref[0]
    hid = _gelu_tanh(pre).astype(BF16)
    out = _dot(hid, w2_ref[0]) + b2_ref[0]
    rowi = lax.broadcasted_iota(jnp.int32, out.shape, 0)
    o_ref[0, 0, 0] = jnp.where(rowi < npc - 1, out, 0.0)


def _nsa_compress(kvc_pieces, bsz, seq, pe, w1, b1, w2, b2):
    npc = seq // NSA_CMP_STRIDE
    flat = NSA_CMP_STRIDE * HEAD_DIM
    x = kvc_pieces
    return pl.pallas_call(
        _nsa_compress_kernel,
        grid=(bsz, 2, NSA_KV),
        in_specs=[
            pl.BlockSpec((npc, flat), lambda b, s, g: (b, s * NSA_KV + g)),
            pl.BlockSpec((1, 1, 2 * flat), lambda b, s, g: (s, 0, 0)),
            pl.BlockSpec((1, 2 * flat, NSA_CMP_HIDDEN), lambda b, s, g: (s, 0, 0)),
            pl.BlockSpec((1, 1, NSA_CMP_HIDDEN), lambda b, s, g: (s, 0, 0)),
            pl.BlockSpec((1, NSA_CMP_HIDDEN, HEAD_DIM), lambda b, s, g: (s, 0, 0)),
            pl.BlockSpec((1, 1, HEAD_DIM), lambda b, s, g: (s, 0, 0)),
        ],
        out_specs=pl.BlockSpec((1, 1, 1, npc, HEAD_DIM), lambda b, s, g: (b, s, g, 0, 0)),
        out_shape=jax.ShapeDtypeStruct((bsz, 2, NSA_KV, npc, HEAD_DIM), F32),
        compiler_params=_cparams("parallel", "parallel", "parallel"),
        name="nsa_compress",
    )(x, pe.reshape(2, 1, 2 * flat).astype(F32), w1.astype(BF16), b1.reshape(2, 1, -1).astype(F32),
      w2.astype(BF16), b2.reshape(2, 1, -1).astype(F32))


def _nsa_cmp_split(ns):
    return max(1, min(4, (4 * ns) // LANES))


def _nsa_block_onehot(rows):
    r = lax.broadcasted_iota(jnp.int32, (rows, 2 * NSA_KVW), 0)
    c = lax.broadcasted_iota(jnp.int32, (rows, 2 * NSA_KVW), 1)
    blk = (r % NSA_KTILE) // NSA_SLC_LEN
    return jnp.where((c % NSA_KVW) == HEAD_DIM + blk, 1.0, 0.0).astype(BF16)


def _nsa_kernel(q_ref, kc_ref, vc_ref, ks_ref, kw_ref, vs_ref, vw_ref, g_ref, o_ref,
                bias_s, qaug_s, qaug2_s, sa_s, sb_s, pa_s, pb_s):
    qb = pl.program_id(2)
    nqt = NSA_RPG * NSA_QBLK
    s0 = qb * NSA_QBLK
    ncp = kc_ref.shape[2]
    ns = ncp // 4
    heads = [slice(r * NSA_QBLK, (r + 1) * NSA_QBLK) for r in range(NSA_RPG)]

    qcat = jnp.concatenate([q_ref[0, r * HEAD_DIM:(r + 1) * HEAD_DIM, :] for r in range(NSA_RPG)], axis=1)
    qpos = s0 + lax.broadcasted_iota(jnp.int32, (1, NSA_QBLK), 1)
    qaug_s[0:HEAD_DIM, :] = qcat
    qaug_s[HEAD_DIM:, :] = jnp.zeros((HEAD_DIM, nqt), BF16)

    split = _nsa_cmp_split(ns)
    chunk = ns // split
    cvalid = jnp.where(qpos >= NSA_CMP_LEN - 1, 1.0, 0.0)
    cur = qpos // NSA_SLC_LEN
    taken = -3.0e38

    def cmp_and_select(nchunks):
        rows, jmax = nchunks * 4 * chunk, nchunks * chunk
        kc = kc_ref[0, 0, 0:rows, :]
        rc = lax.broadcasted_iota(jnp.int32, (rows, 1), 0)
        ncmp = 4 * ((rc // (4 * chunk)) * chunk + rc % chunk) + (rc % (4 * chunk)) // chunk
        cbias = jnp.where((ncmp * NSA_CMP_STRIDE + (NSA_CMP_LEN - 1)) <= qpos, 0.0, NEG)
        psum = jnp.zeros((rows, NSA_QBLK), F32)
        p_all = []
        for sl in heads:
            s = _dot(kc, qcat[:, sl]) + cbias
            e = jnp.exp2(s - jnp.max(s, axis=0, keepdims=True))
            p = e * (cvalid / jnp.sum(e, axis=0, keepdims=True))
            psum = psum + p
            p_all.append(p.astype(BF16))
        o_cmp = _dot(vc_ref[0, 0, :, 0:rows], jnp.concatenate(p_all, axis=1))

        tot, p3 = [], []
        for c in range(nchunks):
            part = [psum[(4 * c + i) * chunk:(4 * c + i + 1) * chunk] for i in range(4)]
            tot.append(part[0] + part[1] + part[2] + part[3])
            p3.append(part[3])
        tot, p3 = jnp.concatenate(tot, axis=0), jnp.concatenate(p3, axis=0)
        rj = lax.broadcasted_iota(jnp.int32, (jmax, NSA_QBLK), 0)
        imp = tot + jnp.where(rj >= 1, pltpu.roll(p3, 1, 0), 0.0)
        forced = (rj == 0) | (rj == cur) | (rj == cur - 1)
        rjf = rj.astype(F32)
        imp = jnp.where(forced, taken, jnp.where(rj <= cur, imp, -FORCE))
        for _ in range(min(NSA_TOPK, ns) - 3):
            mx = jnp.max(imp, axis=0, keepdims=True)
            first = jnp.min(jnp.where(imp == mx, rjf, float(ns)), axis=0, keepdims=True)
            imp = jnp.where(rjf == first, taken, imp)
        bias = jnp.where(imp == taken, 0.0, NEG)
        if jmax < ns:
            bias = jnp.concatenate([bias, jnp.full((ns - jmax, NSA_QBLK), NEG, F32)], axis=0)
        return o_cmp, bias

    last_block = (s0 + NSA_QBLK - 1) // NSA_SLC_LEN
    o_cmp, bias = lax.switch(last_block // chunk,
                             [functools.partial(cmp_and_select, n + 1) for n in range(split)])
    bias_s[...] = bias

    kt_diag = s0 // NSA_KTILE
    blocks_per_tile = NSA_KTILE // NSA_SLC_LEN
    vtiles = NSA_KTILE // LANES

    def qk_tile(kt, qaug_ref):
        k0 = pl.multiple_of(kt * NSA_KTILE, NSA_KTILE)
        b8 = bias_s[pl.ds(pl.multiple_of(kt * blocks_per_tile, blocks_per_tile), blocks_per_tile), :]
        b16 = jnp.concatenate([b8, jnp.zeros_like(b8)], axis=0).astype(BF16)
        qaug_ref[HEAD_DIM:HEAD_DIM + 16, :] = jnp.concatenate([b16] * NSA_RPG, axis=1)
        return _dot(ks_ref[pl.ds(k0, NSA_KTILE), :], qaug_ref[...])

    ones_rows = jnp.ones((16, NSA_KTILE), BF16)

    def pv_tile(kt, p):
        vt = jnp.concatenate([vs_ref[kt * vtiles + i] for i in range(vtiles)], axis=1)
        return _dot(jnp.concatenate([vt, ones_rows], axis=0), p)

    def softmax_tile(s, m_old):
        m_new = jnp.maximum(m_old, jnp.max(s, axis=0, keepdims=True))
        p = jnp.exp2((s - m_new).astype(BF16))
        return p, m_new, jnp.exp2(m_old - m_new)

    def visible(kt):
        kpos = kt * NSA_KTILE + lax.broadcasted_iota(jnp.int32, (NSA_KTILE, 1), 0)
        return jnp.concatenate([kpos <= qpos] * NSA_RPG, axis=1)

    def tile_group(i, carry, last):
        m_run, acc, alpha_prev = carry
        for k in range(NSA_UNROLL):
            t = NSA_UNROLL * i + k
            s_cur, p_cur, s_nxt, p_prv, qa = ((sa_s, pa_s, sb_s, pb_s, qaug_s) if k % 2 == 0 else
                                              (sb_s, pb_s, sa_s, pa_s, qaug2_s))
            acc = alpha_prev * acc + pv_tile(jnp.maximum(t - 1, 0), p_prv[...])
            s = jnp.where(visible(t), s_cur[...], NEG) if last else s_cur[...]
            p, m_run, alpha_prev = softmax_tile(s, m_run)
            p_cur[...] = p
            if last and k == NSA_UNROLL - 1:
                acc = alpha_prev * acc + pv_tile(t, p_cur[...])
            else:
                s_nxt[...] = qk_tile(t + 1, qa)
        return m_run, acc, alpha_prev

    qaug2_s[...] = qaug_s[...]
    pb_s[...] = jnp.zeros((NSA_KTILE, nqt), BF16)
    sa_s[...] = qk_tile(0, qaug2_s)

    span = NSA_WIN + NSA_QBLK
    start = pl.multiple_of(jnp.maximum(s0 - NSA_WIN, 0), NSA_QBLK)
    kwin = kw_ref[pl.ds(start, span), :]
    kp = start + lax.broadcasted_iota(jnp.int32, (span, 1), 0)
    wbias = jnp.where((kp <= qpos) & (kp > qpos - NSA_WIN), 0.0, NEG)
    pw, dens = [], []
    for sl in heads:
        s = _dot(kwin, qaug_s[:, sl]) + wbias
        e = jnp.exp2(s - jnp.max(s, axis=0, keepdims=True))
        dens.append(jnp.sum(e, axis=0, keepdims=True))
        pw.append(e.astype(BF16))
    sblk = start // LANES
    vwt = jnp.concatenate([vw_ref[sblk + i] for i in range(span // LANES)], axis=1)
    o_win = _dot(vwt, jnp.concatenate(pw, axis=1)) * (1.0 / jnp.concatenate(dens, axis=1))

    init = (jnp.full((1, nqt), NEG, F32), jnp.zeros((HEAD_DIM + 16, nqt), F32), jnp.ones((1, nqt), F32))
    group_diag = kt_diag // NSA_UNROLL
    carry = lax.fori_loop(0, group_diag, lambda i, c: tile_group(i, c, False), init)
    _, acc, _ = tile_group(group_diag, carry, True)
    o_slc = acc[0:HEAD_DIM] * (1.0 / acc[HEAD_DIM:HEAD_DIM + 1])

    gates = _sigmoid(g_ref[0])
    outs = [gates[r:r + 1, :] * o_cmp[:, sl] + gates[NSA_RPG + r:NSA_RPG + r + 1, :] * o_slc[:, sl]
            + gates[2 * NSA_RPG + r:2 * NSA_RPG + r + 1, :] * o_win[:, sl] for r, sl in enumerate(heads)]
    o_ref[...] = jnp.concatenate(outs, axis=0).T.astype(o_ref.dtype)


def _nsa_mixer(q_tt, kvc, k_slc, k_win, v_tt, g_tt, bsz, seq, pe, w1, b1, w2, b2):
    nqb = seq // NSA_QBLK
    npc = seq // NSA_CMP_STRIDE
    ns = seq // NSA_SLC_LEN
    cmp_out = _nsa_compress(kvc, bsz, seq, pe, w1, b1, w2, b2)
    split = _nsa_cmp_split(ns)
    perm = cmp_out.reshape(bsz, 2, NSA_KV, split, ns // split, 4, HEAD_DIM).transpose(0, 1, 2, 3, 5, 4, 6).reshape(
        bsz, 2, NSA_KV, npc, HEAD_DIM)
    kc = perm[:, 0].astype(BF16)
    vc_t = perm[:, 1].transpose(0, 1, 3, 2).astype(BF16)
    gq = NSA_RPG * HEAD_DIM
    return pl.pallas_call(
        _nsa_kernel,
        grid=(bsz, NSA_KV, nqb),
        in_specs=[
            pl.BlockSpec((1, gq, LANES), lambda b, g, q: (b * nqb + q, g, 0)),
            pl.BlockSpec((1, 1, npc, HEAD_DIM), lambda b, g, q: (b, g, 0, 0)),
            pl.BlockSpec((1, 1, HEAD_DIM, npc), lambda b, g, q: (b, g, 0, 0)),
            pl.BlockSpec((seq, NSA_KVW), lambda b, g, q: (b, g)),
            pl.BlockSpec((seq, NSA_KVW), lambda b, g, q: (b, g)),
            pl.BlockSpec((nqb, HEAD_DIM, LANES), lambda b, g, q: (b, g, 0)),
            pl.BlockSpec((nqb, HEAD_DIM, LANES), lambda b, g, q: (b, NSA_KV + g, 0)),
            pl.BlockSpec((1, 16, LANES), lambda b, g, q: (b * nqb + q, g, 0)),
        ],
        out_specs=pl.BlockSpec((NSA_QBLK, gq), lambda b, g, q: (b * nqb + q, g)),
        out_shape=jax.ShapeDtypeStruct((bsz * seq, NSA_Q), BF16),
        scratch_shapes=[pltpu.VMEM((ns, NSA_QBLK), F32),
                        pltpu.VMEM((2 * HEAD_DIM, NSA_RPG * NSA_QBLK), BF16),
                        pltpu.VMEM((2 * HEAD_DIM, NSA_RPG * NSA_QBLK), BF16),
                        pltpu.VMEM((NSA_KTILE, NSA_RPG * NSA_QBLK), F32),
                        pltpu.VMEM((NSA_KTILE, NSA_RPG * NSA_QBLK), F32),
                        pltpu.VMEM((NSA_KTILE, NSA_RPG * NSA_QBLK), BF16),
                        pltpu.VMEM((NSA_KTILE, NSA_RPG * NSA_QBLK), BF16)],
        compiler_params=_cparams("arbitrary", "arbitrary", "arbitrary"),
        name="nsa_attention",
    )(q_tt, kc, vc_t, k_slc, k_win, v_tt, v_tt, g_tt)


def _swa_kernel(q_ref, kp_ref, kc_ref, vp_ref, vc_ref, sink_ref, o_ref):
    qb = pl.program_id(1)
    t = SWA_WIN
    kband = jnp.concatenate([kp_ref[...], kc_ref[...]], axis=0)
    krel = lax.broadcasted_iota(jnp.int32, (2 * t, 1), 0) - t
    qrel = lax.broadcasted_iota(jnp.int32, (1, t), 1)
    lowest = jnp.where(qb > 0, -t, 0)
    mbias = jnp.where((krel <= qrel) & (krel > qrel - SWA_WIN) & (krel >= lowest), 0.0, NEG)
    outs = []
    for g in range(2):
        rows = slice(g * SWA_RPG * HEAD_DIM, (g + 1) * SWA_RPG * HEAD_DIM)
        qg = q_ref[0, rows, :]
        qcat = jnp.concatenate([qg[r * HEAD_DIM:(r + 1) * HEAD_DIM, :] for r in range(SWA_RPG)], axis=1)
        zq = jnp.zeros_like(qcat)
        qext = jnp.concatenate([qcat, zq] if g == 0 else [zq, qcat], axis=0)
        s = _dot(kband, qext)
        ps, dens = [], []
        for r in range(SWA_RPG):
            h = g * SWA_RPG + r
            sink = sink_ref[h:h + 1, :]
            sr = s[:, r * t:(r + 1) * t] + mbias
            mx = jnp.maximum(jnp.max(sr, axis=0, keepdims=True), sink)
            e = jnp.exp2(sr - mx)
            dens.append(jnp.sum(e, axis=0, keepdims=True) + jnp.exp2(sink - mx))
            ps.append(e.astype(BF16))
        vband = jnp.concatenate([vp_ref[0, g * HEAD_DIM:(g + 1) * HEAD_DIM, :],
                                 vc_ref[0, g * HEAD_DIM:(g + 1) * HEAD_DIM, :]], axis=1)
        og = _dot(vband, jnp.concatenate(ps, axis=1)) * (1.0 / jnp.concatenate(dens, axis=1))
        outs.append(jnp.concatenate([og[:, r * t:(r + 1) * t] for r in range(SWA_RPG)], axis=0).T)
    o_ref[...] = jnp.concatenate(outs, axis=1).astype(o_ref.dtype)


def _swa_mixer(q_tt, k_nat, v_tt, sinks, bsz, seq):
    nqb = seq // SWA_WIN
    sink_rows = jnp.broadcast_to((sinks.astype(F32) * math.log2(math.e))[:, None], (sinks.shape[0], LANES))
    prev = lambda b, q: b * nqb + jnp.maximum(q - 1, 0)
    return pl.pallas_call(
        _swa_kernel,
        grid=(bsz, nqb),
        in_specs=[
            pl.BlockSpec((1, SWA_Q, LANES), lambda b, q: (b * nqb + q, 0, 0)),
            pl.BlockSpec((SWA_WIN, SWA_KVW), lambda b, q: (prev(b, q), 0)),
            pl.BlockSpec((SWA_WIN, SWA_KVW), lambda b, q: (b * nqb + q, 0)),
            pl.BlockSpec((1, SWA_KVW, LANES), lambda b, q: (prev(b, q), 0, 0)),
            pl.BlockSpec((1, SWA_KVW, LANES), lambda b, q: (b * nqb + q, 0, 0)),
            _full(sink_rows.shape),
        ],
        out_specs=pl.BlockSpec((SWA_WIN, SWA_Q), lambda b, q: (b * nqb + q, 0)),
        out_shape=jax.ShapeDtypeStruct((bsz * seq, SWA_Q), BF16),
        compiler_params=_cparams("parallel", "parallel"),
        name="swa_attention",
    )(q_tt, k_nat, k_nat, v_tt, v_tt, sink_rows)


def _s5_params(a_re, a_im, log_dt, b_re, b_im, c_re, c_im, n_chunks):
    f = F32
    t = S5_CHUNK
    step = jnp.exp(log_dt.astype(f))[:, None]
    lr, li = a_re.astype(f), a_im.astype(f)

    def lam_pow(tau):
        tau = tau.astype(f)[:, None, None]
        mag = jnp.exp(lr * step * tau)
        ang = li * step * tau
        return mag * jnp.cos(ang), mag * jnp.sin(ang)

    lb_r, lb_i = (v[0] for v in lam_pow(jnp.ones((1,))))
    nr, ni = lb_r - 1.0, lb_i
    den = lr * lr + li * li
    fr, fi = (nr * lr + ni * li) / den, (ni * lr - nr * li) / den
    br, bi = b_re.astype(f), b_im.astype(f)
    bb_r = fr[..., None] * br - fi[..., None] * bi
    bb_i = fr[..., None] * bi + fi[..., None] * br
    cr, ci = c_re.astype(f), c_im.astype(f)

    pr, pi = lam_pow(jnp.arange(t + 1))
    cl_r = cr[None] * pr[:, :, None, :] - ci[None] * pi[:, :, None, :]
    cl_i = cr[None] * pi[:, :, None, :] + ci[None] * pr[:, :, None, :]
    kern = jnp.einsum("tghp,gpk->gthk", cl_r[:t], bb_r, precision="highest") - jnp.einsum(
        "tghp,gpk->gthk", cl_i[:t], bb_i, precision="highest")
    lag = jnp.arange(t)[None, :] - jnp.arange(t)[:, None]
    toep = jnp.where((lag >= 0)[None, :, :, None, None], kern[:, jnp.clip(lag, 0, t - 1)], 0.0)
    toep = toep.transpose(0, 1, 4, 2, 3).reshape(S5_GROUPS, t * S5_GROUP_CH, t * S5_GROUP_CH)
    rr, ri = pr[t - 1 - jnp.arange(t)], pi[t - 1 - jnp.arange(t)]
    bs_r = rr[..., None] * bb_r[None] - ri[..., None] * bb_i[None]
    bs_i = rr[..., None] * bb_i[None] + ri[..., None] * bb_r[None]
    bs = jnp.concatenate([bs_r, bs_i], axis=2)
    bs = bs.transpose(1, 0, 3, 2).reshape(S5_GROUPS, t * S5_GROUP_CH, 2 * S5_STATE)
    cs = jnp.concatenate([cl_r[1:], -cl_i[1:]], axis=3)
    cs = cs.transpose(1, 3, 0, 2).reshape(S5_GROUPS, 2 * S5_STATE, t * S5_GROUP_CH)
    ar, ai = pr[t], pi[t]
    a1, a2 = [], []
    k = 1
    while k < n_chunks:
        a1.append(jnp.concatenate([ar, ar], axis=1))
        a2.append(jnp.concatenate([-ai, ai], axis=1))
        ar, ai = ar * ar - ai * ai, 2.0 * ar * ai
        k *= 2
    a1 = jnp.stack(a1, axis=1)
    a2 = jnp.stack(a2, axis=1)
    return toep.astype(BF16), bs.astype(BF16), cs.astype(BF16), a1, a2


def _s5_kernel(bsz, u_ref, toep_ref, bs_ref, cs_ref, a1_ref, a2_ref, o_ref):
    u = u_ref[0]
    n_all = u.shape[0]
    n = n_all // bsz
    sc = _dot(u, bs_ref[0])
    rowi = lax.broadcasted_iota(jnp.int32, (n, 2 * S5_STATE), 0)
    h_in = []
    for b in range(bsz):
        x = sc[b * n:(b + 1) * n]
        k, step = 1, 0
        while k < n:
            xs = jnp.where(rowi >= k, pltpu.roll(x, k, 0), 0.0)
            xs_sw = pltpu.roll(xs, S5_STATE, 1)
            x = x + a1_ref[0, step:step + 1, :] * xs + a2_ref[0, step:step + 1, :] * xs_sw
            k *= 2
            step += 1
        h_in.append(jnp.where(rowi >= 1, pltpu.roll(x, 1, 0), 0.0))
    h_in = jnp.concatenate(h_in, axis=0)
    hi = h_in.astype(BF16)
    lo = (h_in - hi.astype(F32)).astype(BF16)
    o_ref[0] = (_dot(u, toep_ref[0]) + _dot(hi, cs_ref[0]) + _dot(lo, cs_ref[0])).astype(o_ref.dtype)


def _s5_glu_kernel(y_ref, u_ref, d_ref, w_ref, b_ref, o_ref):
    y = _gelu_tanh(y_ref[...] + d_ref[...] * u_ref[...])
    gate = _sigmoid(_dot(y.astype(BF16), w_ref[...]) + b_ref[...])
    o_ref[...] = (y * gate).astype(o_ref.dtype)


def _s5_mixer(u5, bsz, seq, a_re, a_im, log_dt, b_re, b_im, c_re, c_im, d_skip, glu_w, glu_b, tm=1024):
    m = u5.shape[0]
    t = S5_CHUNK
    nch = m // t
    width = t * S5_GROUP_CH
    toep, bs, cs, a1, a2 = _s5_params(a_re, a_im, log_dt, b_re, b_im, c_re, c_im, nch // bsz)
    nsteps = a1.shape[1]
    ug = u5.reshape(nch, t, S5_GROUPS, S5_GROUP_CH).transpose(2, 0, 1, 3).reshape(S5_GROUPS, nch, width).astype(BF16)
    yg = pl.pallas_call(
        functools.partial(_s5_kernel, bsz),
        grid=(S5_GROUPS,),
        in_specs=[
            pl.BlockSpec((1, nch, width), lambda g: (g, 0, 0)),
            pl.BlockSpec((1, width, width), lambda g: (g, 0, 0)),
            pl.BlockSpec((1, width, 2 * S5_STATE), lambda g: (g, 0, 0)),
            pl.BlockSpec((1, 2 * S5_STATE, width), lambda g: (g, 0, 0)),
            pl.BlockSpec((1, nsteps, 2 * S5_STATE), lambda g: (g, 0, 0)),
            pl.BlockSpec((1, nsteps, 2 * S5_STATE), lambda g: (g, 0, 0)),
        ],
        out_specs=pl.BlockSpec((1, nch, width), lambda g: (g, 0, 0)),
        out_shape=jax.ShapeDtypeStruct((S5_GROUPS, nch, width), BF16),
        compiler_params=_cparams("parallel"),
        name="s5_scan",
    )(ug, toep, bs, cs, a1, a2)
    y = yg.reshape(S5_GROUPS, nch, t, S5_GROUP_CH).transpose(1, 2, 0, 3).reshape(m, S5_CH)
    return pl.pallas_call(
        _s5_glu_kernel,
        grid=(m // tm,),
        in_specs=[
            pl.BlockSpec((tm, S5_CH), lambda i: (i, 0)),
            pl.BlockSpec((tm, S5_CH), lambda i: (i, 0)),
            _full((1, S5_CH)), _full((S5_CH, S5_CH)), _full((1, S5_CH)),
        ],
        out_specs=pl.BlockSpec((tm, S5_CH), lambda i: (i, 0)),
        out_shape=jax.ShapeDtypeStruct((m, S5_CH), BF16),
        compiler_params=_cparams("parallel"),
        name="s5_glu",
    )(y, u5, d_skip.reshape(1, S5_CH).astype(F32), glu_w.astype(BF16), glu_b.reshape(1, S5_CH).astype(F32))


def _even_mixers(h, bsz, seq, g_mix, w_in, conv_w, conv_b, dt_bias, a_log, d_skip, norm_g, pe, w1, b1, w2, b2):
    d = h.shape[1]
    scale = HEAD_DIM ** -0.5 * math.log2(math.e)
    o = SSD_IN
    w_ssd = jnp.concatenate([w_in[:, :SSD_IN], jnp.zeros((d, LANES - SSD_HEADS), w_in.dtype)], axis=1)
    w_q = w_in[:, o:o + NSA_Q] * scale
    kv = [w_in[:, o + NSA_Q + i * NSA_KVW:o + NSA_Q + (i + 1) * NSA_KVW] for i in range(6)]
    w_gate = w_in[:, o + NSA_Q + 6 * NSA_KVW:].reshape(d, NSA_KV, NSA_RPG, 3).transpose(0, 1, 3, 2)
    w_gate = jnp.concatenate([w_gate.reshape(d, NSA_KV, 12), jnp.zeros((d, NSA_KV, 4), w_in.dtype)],
                             axis=2).reshape(d, NSA_KV * 16)

    def per_group_halves(w):
        wg = w.reshape(d, NSA_KV, HEAD_DIM)
        return jnp.concatenate([wg, jnp.zeros_like(wg)], axis=2).reshape(d, NSA_KV * NSA_KVW)
    segs = [
        ("nat", w_ssd, F32),
        ("tt", w_q, BF16),
        ("pc", jnp.concatenate([kv[0], kv[1]], axis=1), F32),
        ("nat+", per_group_halves(kv[2]), BF16, _nsa_block_onehot(NSA_KTILE)),
        ("nat", per_group_halves(kv[4]), BF16),
        ("tt", jnp.concatenate([kv[3], kv[5]], axis=1), BF16),
        ("tt", w_gate, F32),
    ]
    u_ssd, q_tt, kvc, k_slc, k_win, v_tt, g_tt = _norm_proj(h, g_mix, segs)
    ya = _ssd_mixer(u_ssd, bsz, seq, conv_w, conv_b, dt_bias, a_log, d_skip, norm_g)
    yb = _nsa_mixer(q_tt, kvc, k_slc, k_win, v_tt, g_tt, bsz, seq, pe, w1, b1, w2, b2)
    return ya, yb


def _odd_mixers(h, bsz, seq, g_mix, w_in, sinks, a_re, a_im, log_dt, b_re, b_im, c_re, c_im, d_skip, glu_w, glu_b):
    scale = HEAD_DIM ** -0.5 * math.log2(math.e)
    segs = [
        ("tt", w_in[:, :SWA_Q] * scale, BF16),
        ("nat", w_in[:, SWA_Q:SWA_Q + SWA_KVW], BF16),
        ("tt", w_in[:, SWA_Q + SWA_KVW:SWA_Q + 2 * SWA_KVW], BF16),
        ("nat", w_in[:, SWA_Q + 2 * SWA_KVW:], F32),
    ]
    q_tt, k_nat, v_tt, u5 = _norm_proj(h, g_mix, segs)
    yc = _swa_mixer(q_tt, k_nat, v_tt, sinks, bsz, seq)
    yd = _s5_mixer(u5, bsz, seq, a_re, a_im, log_dt, b_re, b_im, c_re, c_im, d_skip, glu_w, glu_b)
    return yc, yd


def kernel(x, norm_mix, norm_mlp, norm_final, mlp_w_up, mlp_w_down, ev_w_in, ev_w_out, ssd_conv_w, ssd_conv_b,
           ssd_dt_bias, ssd_a_log, ssd_d, ssd_norm, nsa_pe, nsa_cmp_w1, nsa_cmp_b1, nsa_cmp_w2, nsa_cmp_b2,
           od_w_in, od_w_out, swa_sinks, s5_a_re, s5_a_im, s5_log_dt, s5_b_re, s5_b_im, s5_c_re, s5_c_im,
           s5_d, s5_glu_w, s5_glu_b):
    bsz, seq, d = x.shape
    depth = norm_mix.shape[0]
    assert seq % (NSA_UNROLL * NSA_KTILE) == 0 and seq >= NSA_WIN + NSA_QBLK
    h = x.reshape(bsz * seq, d)
    for layer in range(depth):
        i = layer // 2
        if layer % 2 == 0:
            ya, yb = _even_mixers(h, bsz, seq, norm_mix[layer], ev_w_in[i], ssd_conv_w[i], ssd_conv_b[i],
                                  ssd_dt_bias[i], ssd_a_log[i], ssd_d[i], ssd_norm[i], nsa_pe[i],
                                  nsa_cmp_w1[i], nsa_cmp_b1[i], nsa_cmp_w2[i], nsa_cmp_b2[i])
            w_out = ev_w_out[i]
        else:
            ya, yb = _odd_mixers(h, bsz, seq, norm_mix[layer], od_w_in[i], swa_sinks[i], s5_a_re[i], s5_a_im[i],
                                 s5_log_dt[i], s5_b_re[i], s5_b_im[i], s5_c_re[i], s5_c_im[i], s5_d[i],
                                 s5_glu_w[i], s5_glu_b[i])
            w_out = od_w_out[i]
        h = _mix_out_mlp(h, ya, yb, w_out, norm_mlp[layer], mlp_w_up[layer], mlp_w_down[layer], norm_final,
                         final=(layer == depth - 1))
    return h.reshape(bsz, seq, d)
```

```python
import functools
import math

import jax
import jax.numpy as jnp
from jax import lax
from jax.experimental import pallas as pl
from jax.experimental.pallas import tpu as pltpu

F32 = jnp.float32
BF16 = jnp.bfloat16

EPS = 1e-6
NEG = -1e30
FORCE = 1e9
HEAD_DIM = 64
LANES = 128
VMEM_LIMIT_BYTES = 56 * 1024 * 1024

SSD_HEADS = 8
SSD_INNER = 512
SSD_GROUPS = 2
SSD_STATE = 128
SSD_CONV = 4
SSD_CHUNK = 128
SSD_CONV_DIM = 1024
SSD_IN = SSD_INNER + SSD_CONV_DIM + SSD_HEADS

NSA_HEADS = 8
NSA_KV = 2
NSA_RPG = 4
NSA_CMP_LEN = 32
NSA_CMP_STRIDE = 16
NSA_SLC_LEN = 64
NSA_TOPK = 16
NSA_WIN = 512
NSA_CMP_HIDDEN = 256
NSA_QBLK = 128
NSA_Q = 512
NSA_KVW = 128
NSA_KTILE = 512
NSA_UNROLL = 2

SWA_RPG = 4
SWA_WIN = 128
SWA_Q = 512
SWA_KVW = 128

S5_CH = 512
S5_GROUP_CH = 16
S5_GROUPS = 32
S5_STATE = 64
S5_CHUNK = 32
S5_OCT = LANES // S5_GROUP_CH


def _cparams(*sem):
    return pltpu.CompilerParams(dimension_semantics=sem, vmem_limit_bytes=VMEM_LIMIT_BYTES)


def _full(shape):
    n = len(shape)
    return pl.BlockSpec(shape, lambda *_: (0,) * n)


def _dot(a, b):
    return jnp.dot(a, b, preferred_element_type=F32)


def _dot_nt(a, b):
    return lax.dot_general(a, b, (((1,), (1,)), ((), ())), preferred_element_type=F32)


def _split3(a):
    hi = a.astype(BF16)
    r1 = a - hi.astype(F32)
    mid = r1.astype(BF16)
    lo = (r1 - mid.astype(F32)).astype(BF16)
    return hi, mid, lo


def _dot_exact_rhs(a, b_exact):
    hi, mid, lo = _split3(a)
    return _dot(hi, b_exact) + _dot(mid, b_exact) + _dot(lo, b_exact)


def _dot_exact_lhs(a_exact, b):
    hi, mid, lo = _split3(b)
    return _dot(a_exact, hi) + _dot(a_exact, mid) + _dot(a_exact, lo)


def _rms(x, g):
    return x * lax.rsqrt(jnp.mean(x * x, axis=-1, keepdims=True) + EPS) * g


def _gelu_tanh(x):
    c = math.sqrt(2.0 / math.pi)
    return 0.5 * x * (1.0 + jnp.tanh(c * (x + 0.044715 * (x * x * x))))


def _sigmoid(x):
    return 1.0 / (1.0 + jnp.exp(-x))


def _proj_kernel(kinds, tm, h_ref, g_ref, *refs):
    n = len(kinds)
    n_add = sum(k == "nat+" for k in kinds)
    n_out = n + sum(k == "nat+ch" for k in kinds)
    w_refs, add_refs = refs[:n], list(refs[n:n + n_add])
    o_refs, scratch = list(refs[n + n_add:n + n_add + n_out]), list(refs[n + n_add + n_out:])
    yb = _rms(h_ref[...], g_ref[...]).astype(BF16)
    for kind, w_ref in zip(kinds, w_refs):
        o_ref = o_refs.pop(0)
        if kind == "nat":
            o_ref[...] = _dot(yb, w_ref[...]).astype(o_ref.dtype)
        elif kind == "nat+ch":
            ch_ref, ch_s = o_refs.pop(0), scratch.pop(0)
            res = _dot(yb, w_ref[...])
            o_ref[...] = res.astype(o_ref.dtype)
            for j in range(ch_s.shape[0]):
                ch_s[j] = res[:, j * LANES:(j + 1) * LANES]
            for t in range(S5_CHUNK):
                for j in range(ch_s.shape[0]):
                    col = (j * S5_CHUNK + t) * LANES
                    ch_ref[:, col:col + LANES] = ch_s[j, pl.ds(t, tm // S5_CHUNK, stride=S5_CHUNK), :].astype(ch_ref.dtype)
        elif kind == "nat+":
            o_ref[...] = _dot(yb, w_ref[...]).astype(o_ref.dtype) + add_refs.pop(0)[...]
        elif kind == "pc":
            pc_s = scratch.pop(0)
            res = _dot(yb, w_ref[...])
            flat = NSA_CMP_STRIDE * HEAD_DIM
            per_tile = LANES // HEAD_DIM
            for j in range(pc_s.shape[0]):
                pc_s[j] = res[:, j * LANES:(j + 1) * LANES]
            for p in range(NSA_CMP_STRIDE):
                for j in range(pc_s.shape[0]):
                    tok = pc_s[j, pl.ds(p, tm // NSA_CMP_STRIDE, stride=NSA_CMP_STRIDE), :]
                    for c in range(per_tile):
                        col = (j * per_tile + c) * flat + p * HEAD_DIM
                        o_ref[:, col:col + HEAD_DIM] = tok[:, c * HEAD_DIM:(c + 1) * HEAD_DIM].astype(o_ref.dtype)
        else:
            ot = _dot_nt(w_ref[...], yb)
            for j in range(tm // LANES):
                o_ref[j] = ot[:, j * LANES:(j + 1) * LANES].astype(o_ref.dtype)


def _norm_proj(h, g, segs, tm=512):
    m, d = h.shape
    kinds = tuple(s[0] for s in segs)
    adds = [s[3] for s in segs if s[0] == "nat+"]
    ws, w_specs, out_shapes, out_specs, scratch = [], [], [], [], []
    for kind, w, dt in (s[:3] for s in segs):
        n_out = w.shape[1]
        if kind == "pc":
            ws.append(w.astype(BF16))
            w_specs.append(_full((d, n_out)))
            out_shapes.append(jax.ShapeDtypeStruct((m // NSA_CMP_STRIDE, NSA_CMP_STRIDE * n_out), dt))
            out_specs.append(pl.BlockSpec((tm // NSA_CMP_STRIDE, NSA_CMP_STRIDE * n_out), lambda i: (i, 0)))
            scratch.append(pltpu.VMEM((n_out // LANES, tm, LANES), F32))
        elif kind in ("nat", "nat+", "nat+ch"):
            ws.append(w.astype(BF16))
            w_specs.append(_full((d, n_out)))
            out_shapes.append(jax.ShapeDtypeStruct((m, n_out), dt))
            out_specs.append(pl.BlockSpec((tm, n_out), lambda i: (i, 0)))
            if kind == "nat+ch":
                out_shapes.append(jax.ShapeDtypeStruct((m // S5_CHUNK, S5_CHUNK * n_out), BF16))
                out_specs.append(pl.BlockSpec((tm // S5_CHUNK, S5_CHUNK * n_out), lambda i: (i, 0)))
                scratch.append(pltpu.VMEM((n_out // LANES, tm, LANES), F32))
        else:
            ws.append(w.T.astype(BF16))
            w_specs.append(_full((n_out, d)))
            out_shapes.append(jax.ShapeDtypeStruct((m // LANES, n_out, LANES), dt))
            out_specs.append(pl.BlockSpec((tm // LANES, n_out, LANES), lambda i: (i, 0, 0)))
    return pl.pallas_call(
        functools.partial(_proj_kernel, kinds, tm),
        grid=(m // tm,),
        in_specs=[pl.BlockSpec((tm, d), lambda i: (i, 0)), _full((1, d))] + w_specs + [
            pl.BlockSpec((tm, a.shape[1]), functools.partial(lambda i, nb: (i % nb, 0), nb=a.shape[0] // tm))
            for a in adds],
        out_specs=out_specs,
        out_shape=out_shapes,
        scratch_shapes=scratch,
        compiler_params=_cparams("parallel"),
        name="norm_proj",
    )(h, g.reshape(1, d), *ws, *adds)


def _mlp_kernel(final, h_ref, ya_ref, yb_ref, woa_ref, wob_ref, gm_ref, wup_ref, wdn_ref, gf_ref,
                o_ref, h2_s, xn_s, acc_s):
    j = pl.program_id(1)

    @pl.when(j == 0)
    def _():
        h2 = h_ref[...] + _dot(ya_ref[...], woa_ref[...]) + _dot(yb_ref[...], wob_ref[...])
        h2_s[...] = h2
        xn_s[...] = _rms(h2, gm_ref[...]).astype(BF16)
        acc_s[...] = jnp.zeros_like(acc_s)

    hid = jnp.square(jnp.maximum(_dot(xn_s[...], wup_ref[...]), 0.0))
    acc_s[...] += _dot(hid.astype(BF16), wdn_ref[...])

    @pl.when(j == pl.num_programs(1) - 1)
    def _():
        out = h2_s[...] + acc_s[...]
        if final:
            out = _rms(out, gf_ref[...])
        o_ref[...] = out


def _mix_out_mlp(h, ya, yb, w_out, g_mlp, w_up, w_down, g_final, final, tm=1024, tf=1024):
    m, d = h.shape
    dff = w_up.shape[1]
    na = ya.shape[1]
    nb = yb.shape[1]
    return pl.pallas_call(
        functools.partial(_mlp_kernel, final),
        grid=(m // tm, dff // tf),
        in_specs=[
            pl.BlockSpec((tm, d), lambda i, j: (i, 0)),
            pl.BlockSpec((tm, na), lambda i, j: (i, 0)),
            pl.BlockSpec((tm, nb), lambda i, j: (i, 0)),
            _full((na, d)), _full((nb, d)), _full((1, d)),
            pl.BlockSpec((d, tf), lambda i, j: (0, j)),
            pl.BlockSpec((tf, d), lambda i, j: (j, 0)),
            _full((1, d)),
        ],
        out_specs=pl.BlockSpec((tm, d), lambda i, j: (i, 0)),
        out_shape=jax.ShapeDtypeStruct((m, d), F32),
        scratch_shapes=[pltpu.VMEM((tm, d), F32), pltpu.VMEM((tm, d), BF16), pltpu.VMEM((tm, d), F32)],
        compiler_params=_cparams("parallel", "arbitrary"),
        name="out_proj_mlp",
    )(h, ya, yb, w_out[:na].astype(BF16), w_out[na:].astype(BF16), g_mlp.reshape(1, d),
      w_up.astype(BF16), w_down.astype(BF16), g_final.reshape(1, d))


def _ssd_kernel(u_ref, cw_ref, cb_ref, dtb_ref, alog_ref, dsk_ref, ng_ref, o_ref, xext_s, st_s):
    t = SSD_CHUNK
    c = pl.program_id(1)

    @pl.when(c == 0)
    def _():
        xext_s[0:8, :] = jnp.zeros((8, SSD_CONV_DIM), F32)
        st_s[...] = jnp.zeros_like(st_s)

    z = u_ref[:, 0:SSD_INNER]
    dt_raw = u_ref[:, SSD_INNER + SSD_CONV_DIM:]
    xext_s[8:8 + t, :] = u_ref[:, SSD_INNER:SSD_INNER + SSD_CONV_DIM]
    conv = cb_ref[...] + cw_ref[0:1, :] * xext_s[5:5 + t, :]
    for k in range(1, SSD_CONV):
        conv = conv + cw_ref[k:k + 1, :] * xext_s[5 + k:5 + k + t, :]
    xext_s[0:8, :] = xext_s[t:t + 8, :]
    xc = conv * _sigmoid(conv)
    xs = xc[:, 0:SSD_INNER]
    gn = SSD_GROUPS * SSD_STATE

    dtp = dt_raw + dtb_ref[...]
    dt = jnp.maximum(dtp, 0.0) + jnp.log1p(jnp.exp(-jnp.abs(dtp)))
    a = -jnp.exp(alog_ref[...])
    da = dt * a

    row = lax.broadcasted_iota(jnp.int32, (t, t), 0)
    col = lax.broadcasted_iota(jnp.int32, (t, t), 1)
    causal = col <= row
    tril = jnp.where(causal, 1.0, 0.0).astype(BF16)
    a_cum = _dot_exact_lhs(tril, da)
    er = lax.broadcasted_iota(jnp.int32, (LANES, SSD_INNER), 0)
    ec = lax.broadcasted_iota(jnp.int32, (LANES, SSD_INNER), 1)
    expand = jnp.where((ec >> 6) == er, 1.0, 0.0).astype(BF16)
    a_cum_x = _dot_exact_rhs(a_cum, expand)
    dt_x = _dot_exact_rhs(dt, expand)
    a_cum_t = a_cum.T
    a_last_x = a_cum_x[t - 1:t, :]
    decay_end_x = jnp.exp(a_last_x - a_cum_x)
    decay_in_x = jnp.exp(a_cum_x)
    chunk_decay_x = jnp.exp(a_last_x)

    xd = xs * dt_x
    xd_end = (xd * decay_end_x).astype(BF16)
    xd_b = xd.astype(BF16)
    lane = lax.broadcasted_iota(jnp.int32, (t, LANES), 1)
    first_half = lane < HEAD_DIM

    pieces = []
    for g in range(SSD_GROUPS):
        bm = xc[:, SSD_INNER + g * SSD_STATE:SSD_INNER + (g + 1) * SSD_STATE]
        cm = xc[:, SSD_INNER + gn + g * SSD_STATE:SSD_INNER + gn + (g + 1) * SSD_STATE].astype(BF16)
        bm_t = bm.T.astype(BF16)
        cb = _dot_nt(cm, bm.astype(BF16))
        for pr in range(2):
            i = g * 2 + pr
            sl = slice(i * LANES, (i + 1) * LANES)
            ms = []
            for hh in range(2):
                h = 2 * i + hh
                seg = a_cum[:, h:h + 1] - a_cum_t[h:h + 1, :]
                dec = jnp.exp(jnp.where(causal, seg, NEG))
                ms.append((cb * dec).astype(BF16))
            y_diag = jnp.where(first_half, _dot(ms[0], xd_b[:, sl]), _dot(ms[1], xd_b[:, sl]))
            st = st_s[i]
            y_off = _dot(cm, st.astype(BF16)) * decay_in_x[:, sl]
            st_s[i] = st * chunk_decay_x[:, sl] + _dot(bm_t, xd_end[:, sl])
            pieces.append(y_diag + y_off)
    y = jnp.concatenate(pieces, axis=1) + xs * dsk_ref[...]
    y = y * (z * _sigmoid(z))
    half = SSD_INNER // SSD_GROUPS
    outs = [_rms(y[:, g * half:(g + 1) * half], ng_ref[:, g * half:(g + 1) * half]) for g in range(SSD_GROUPS)]
    o_ref[...] = jnp.concatenate(outs, axis=1).astype(o_ref.dtype)


def _ssd_mixer(u_ssd, bsz, seq, conv_w, conv_b, dt_bias, a_log, d_skip, norm_g):
    m, width = u_ssd.shape
    nch = seq // SSD_CHUNK
    pad = LANES - SSD_HEADS

    def padded(v):
        return jnp.concatenate([v.astype(F32), jnp.zeros((pad,), F32)]).reshape(1, LANES)

    return pl.pallas_call(
        _ssd_kernel,
        grid=(bsz, nch),
        in_specs=[
            pl.BlockSpec((SSD_CHUNK, width), lambda b, c: (b * nch + c, 0)),
            _full((SSD_CONV, SSD_CONV_DIM)), _full((1, SSD_CONV_DIM)),
            _full((1, LANES)), _full((1, LANES)), _full((1, SSD_INNER)), _full((1, SSD_INNER)),
        ],
        out_specs=pl.BlockSpec((SSD_CHUNK, SSD_INNER), lambda b, c: (b * nch + c, 0)),
        out_shape=jax.ShapeDtypeStruct((m, SSD_INNER), BF16),
        scratch_shapes=[pltpu.VMEM((SSD_CHUNK + 8, SSD_CONV_DIM), F32),
                        pltpu.VMEM((SSD_HEADS // 2, SSD_STATE, LANES), F32)],
        compiler_params=_cparams("arbitrary", "arbitrary"),
        name="ssd_mixer",
    )(u_ssd, conv_w.astype(F32), conv_b.reshape(1, -1).astype(F32), padded(dt_bias), padded(a_log),
      jnp.repeat(d_skip.astype(F32), HEAD_DIM).reshape(1, SSD_INNER), norm_g.reshape(1, SSD_INNER).astype(F32))


def _nsa_compress_kernel(x_ref, pe_ref, w1_ref, b1_ref, w2_ref, b2_ref, o_ref):
    x = x_ref[...]
    npc = x.shape[0]
    half = NSA_CMP_STRIDE * HEAD_DIM
    top = _dot((x + pe_ref[0, :, 0:half]).astype(BF16), w1_ref[0, 0:half, :])
    bot = _dot((x + pe_ref[0, :, half:]).astype(BF16), w1_ref[0, half:, :])
    pre = top + pltpu.roll(bot, npc - 1, 0) + b1_ref[0]
    hid = _gelu_tanh(pre).astype(BF16)
    out = _dot(hid, w2_ref[0]) + b2_ref[0]
    rowi = lax.broadcasted_iota(jnp.int32, out.shape, 0)
    o_ref[0, 0, 0] = jnp.where(rowi < npc - 1, out, 0.0)


def _nsa_compress(kvc_pieces, bsz, seq, pe, w1, b1, w2, b2):
    npc = seq // NSA_CMP_STRIDE
    flat = NSA_CMP_STRIDE * HEAD_DIM
    x = kvc_pieces
    return pl.pallas_call(
        _nsa_compress_kernel,
        grid=(bsz, 2, NSA_KV),
        in_specs=[
            pl.BlockSpec((npc, flat), lambda b, s, g: (b, s * NSA_KV + g)),
            pl.BlockSpec((1, 1, 2 * flat), lambda b, s, g: (s, 0, 0)),
            pl.BlockSpec((1, 2 * flat, NSA_CMP_HIDDEN), lambda b, s, g: (s, 0, 0)),
            pl.BlockSpec((1, 1, NSA_CMP_HIDDEN), lambda b, s, g: (s, 0, 0)),
            pl.BlockSpec((1, NSA_CMP_HIDDEN, HEAD_DIM), lambda b, s, g: (s, 0, 0)),
            pl.BlockSpec((1, 1, HEAD_DIM), lambda b, s, g: (s, 0, 0)),
        ],
        out_specs=pl.BlockSpec((1, 1, 1, npc, HEAD_DIM), lambda b, s, g: (b, s, g, 0, 0)),
        out_shape=jax.ShapeDtypeStruct((bsz, 2, NSA_KV, npc, HEAD_DIM), F32),
        compiler_params=_cparams("parallel", "parallel", "parallel"),
        name="nsa_compress",
    )(x, pe.reshape(2, 1, 2 * flat).astype(F32), w1.astype(BF16), b1.reshape(2, 1, -1).astype(F32),
      w2.astype(BF16), b2.reshape(2, 1, -1).astype(F32))


def _nsa_cmp_split(ns):
    return max(1, min(4, (4 * ns) // LANES))


def _nsa_block_onehot(rows):
    r = lax.broadcasted_iota(jnp.int32, (rows, 2 * NSA_KVW), 0)
    c = lax.broadcasted_iota(jnp.int32, (rows, 2 * NSA_KVW), 1)
    blk = (r % NSA_KTILE) // NSA_SLC_LEN
    return jnp.where((c % NSA_KVW) == HEAD_DIM + blk, 1.0, 0.0).astype(BF16)


def _nsa_kernel(q_ref, kc_ref, vc_ref, ks_ref, kw_ref, vs_ref, vw_ref, g_ref, o_ref,
                bias_s, qaug_s, qaug2_s, sa_s, sb_s, pa_s, pb_s):
    qb = pl.program_id(2)
    nqt = NSA_RPG * NSA_QBLK
    s0 = qb * NSA_QBLK
    ncp = kc_ref.shape[2]
    ns = ncp // 4
    heads = [slice(r * NSA_QBLK, (r + 1) * NSA_QBLK) for r in range(NSA_RPG)]

    qcat = jnp.concatenate([q_ref[0, r * HEAD_DIM:(r + 1) * HEAD_DIM, :] for r in range(NSA_RPG)], axis=1)
    qpos = s0 + lax.broadcasted_iota(jnp.int32, (1, NSA_QBLK), 1)
    qaug_s[0:HEAD_DIM, :] = qcat
    qaug_s[HEAD_DIM:, :] = jnp.zeros((HEAD_DIM, nqt), BF16)

    split = _nsa_cmp_split(ns)
    chunk = ns // split
    cvalid = jnp.where(qpos >= NSA_CMP_LEN - 1, 1.0, 0.0)
    cur = qpos // NSA_SLC_LEN
    taken = -3.0e38

    def cmp_and_select(nchunks):
        rows, jmax = nchunks * 4 * chunk, nchunks * chunk
        kc = kc_ref[0, 0, 0:rows, :]
        rc = lax.broadcasted_iota(jnp.int32, (rows, 1), 0)
        ncmp = 4 * ((rc // (4 * chunk)) * chunk + rc % chunk) + (rc % (4 * chunk)) // chunk
        cbias = jnp.where((ncmp * NSA_CMP_STRIDE + (NSA_CMP_LEN - 1)) <= qpos, 0.0, NEG)
        psum = jnp.zeros((rows, NSA_QBLK), F32)
        p_all = []
        for sl in heads:
            s = _dot(kc, qcat[:, sl]) + cbias
            e = jnp.exp2(s - jnp.max(s, axis=0, keepdims=True))
            p = e * (cvalid / jnp.sum(e, axis=0, keepdims=True))
            psum = psum + p
            p_all.append(p.astype(BF16))
        o_cmp = _dot(vc_ref[0, 0, :, 0:rows], jnp.concatenate(p_all, axis=1))

        tot, p3 = [], []
        for c in range(nchunks):
            part = [psum[(4 * c + i) * chunk:(4 * c + i + 1) * chunk] for i in range(4)]
            tot.append(part[0] + part[1] + part[2] + part[3])
            p3.append(part[3])
        tot, p3 = jnp.concatenate(tot, axis=0), jnp.concatenate(p3, axis=0)
        rj = lax.broadcasted_iota(jnp.int32, (jmax, NSA_QBLK), 0)
        imp = tot + jnp.where(rj >= 1, pltpu.roll(p3, 1, 0), 0.0)
        forced = (rj == 0) | (rj == cur) | (rj == cur - 1)
        rjf = rj.astype(F32)
        imp = jnp.where(forced, taken, jnp.where(rj <= cur, imp, -FORCE))
        for _ in range(min(NSA_TOPK, ns) - 3):
            mx = jnp.max(imp, axis=0, keepdims=True)
            first = jnp.min(jnp.where(imp == mx, rjf, float(ns)), axis=0, keepdims=True)
            imp = jnp.where(rjf == first, taken, imp)
        bias = jnp.where(imp == taken, 0.0, NEG)
        if jmax < ns:
            bias = jnp.concatenate([bias, jnp.full((ns - jmax, NSA_QBLK), NEG, F32)], axis=0)
        return o_cmp, bias

    last_block = (s0 + NSA_QBLK - 1) // NSA_SLC_LEN
    o_cmp, bias = lax.switch(last_block // chunk,
                             [functools.partial(cmp_and_select, n + 1) for n in range(split)])
    bias_s[...] = bias

    kt_diag = s0 // NSA_KTILE
    blocks_per_tile = NSA_KTILE // NSA_SLC_LEN
    vtiles = NSA_KTILE // LANES

    def qk_tile(kt, qaug_ref):
        k0 = pl.multiple_of(kt * NSA_KTILE, NSA_KTILE)
        b8 = bias_s[pl.ds(pl.multiple_of(kt * blocks_per_tile, blocks_per_tile), blocks_per_tile), :]
        b16 = jnp.concatenate([b8, jnp.zeros_like(b8)], axis=0).astype(BF16)
        qaug_ref[HEAD_DIM:HEAD_DIM + 16, :] = jnp.concatenate([b16] * NSA_RPG, axis=1)
        return _dot(ks_ref[pl.ds(k0, NSA_KTILE), :], qaug_ref[...])

    ones_rows = jnp.ones((16, NSA_KTILE), BF16)

    def pv_tile(kt, p):
        vt = jnp.concatenate([vs_ref[kt * vtiles + i] for i in range(vtiles)], axis=1)
        return _dot(jnp.concatenate([vt, ones_rows], axis=0), p)

    def softmax_tile(s, m_old):
        m_new = jnp.maximum(m_old, jnp.max(s, axis=0, keepdims=True))
        p = jnp.exp2((s - m_new).astype(BF16))
        return p, m_new, jnp.exp2(m_old - m_new)

    def visible(kt):
        kpos = kt * NSA_KTILE + lax.broadcasted_iota(jnp.int32, (NSA_KTILE, 1), 0)
        return jnp.concatenate([kpos <= qpos] * NSA_RPG, axis=1)

    def tile_group(i, carry, last):
        m_run, acc, alpha_prev = carry
        for k in range(NSA_UNROLL):
            t = NSA_UNROLL * i + k
            s_cur, p_cur, s_nxt, p_prv, qa = ((sa_s, pa_s, sb_s, pb_s, qaug_s) if k % 2 == 0 else
                                              (sb_s, pb_s, sa_s, pa_s, qaug2_s))
            acc = alpha_prev * acc + pv_tile(jnp.maximum(t - 1, 0), p_prv[...])
            s = jnp.where(visible(t), s_cur[...], NEG) if last else s_cur[...]
            p, m_run, alpha_prev = softmax_tile(s, m_run)
            p_cur[...] = p
            if last and k == NSA_UNROLL - 1:
                acc = alpha_prev * acc + pv_tile(t, p_cur[...])
            else:
                s_nxt[...] = qk_tile(t + 1, qa)
        return m_run, acc, alpha_prev

    qaug2_s[...] = qaug_s[...]
    pb_s[...] = jnp.zeros((NSA_KTILE, nqt), BF16)
    sa_s[...] = qk_tile(0, qaug2_s)

    span = NSA_WIN + NSA_QBLK
    start = pl.multiple_of(jnp.maximum(s0 - NSA_WIN, 0), NSA_QBLK)
    kwin = kw_ref[pl.ds(start, span), :]
    kp = start + lax.broadcasted_iota(jnp.int32, (span, 1), 0)
    wbias = jnp.where((kp <= qpos) & (kp > qpos - NSA_WIN), 0.0, NEG)
    pw, dens = [], []
    for sl in heads:
        s = _dot(kwin, qaug_s[:, sl]) + wbias
        e = jnp.exp2(s - jnp.max(s, axis=0, keepdims=True))
        dens.append(jnp.sum(e, axis=0, keepdims=True))
        pw.append(e.astype(BF16))
    sblk = start // LANES
    vwt = jnp.concatenate([vw_ref[sblk + i] for i in range(span // LANES)], axis=1)
    o_win = _dot(vwt, jnp.concatenate(pw, axis=1)) * (1.0 / jnp.concatenate(dens, axis=1))

    init = (jnp.full((1, nqt), NEG, F32), jnp.zeros((HEAD_DIM + 16, nqt), F32), jnp.ones((1, nqt), F32))
    group_diag = kt_diag // NSA_UNROLL
    carry = lax.fori_loop(0, group_diag, lambda i, c: tile_group(i, c, False), init)
    _, acc, _ = tile_group(group_diag, carry, True)
    o_slc = acc[0:HEAD_DIM] * (1.0 / acc[HEAD_DIM:HEAD_DIM + 1])

    gates = _sigmoid(g_ref[0])
    outs = [gates[r:r + 1, :] * o_cmp[:, sl] + gates[NSA_RPG + r:NSA_RPG + r + 1, :] * o_slc[:, sl]
            + gates[2 * NSA_RPG + r:2 * NSA_RPG + r + 1, :] * o_win[:, sl] for r, sl in enumerate(heads)]
    o_ref[...] = jnp.concatenate(outs, axis=0).T.astype(o_ref.dtype)


def _nsa_mixer(q_tt, kvc, k_slc, k_win, v_tt, g_tt, bsz, seq, pe, w1, b1, w2, b2):
    nqb = seq // NSA_QBLK
    npc = seq // NSA_CMP_STRIDE
    ns = seq // NSA_SLC_LEN
    cmp_out = _nsa_compress(kvc, bsz, seq, pe, w1, b1, w2, b2)
    split = _nsa_cmp_split(ns)
    perm = cmp_out.reshape(bsz, 2, NSA_KV, split, ns // split, 4, HEAD_DIM).transpose(0, 1, 2, 3, 5, 4, 6).reshape(
        bsz, 2, NSA_KV, npc, HEAD_DIM)
    kc = perm[:, 0].astype(BF16)
    vc_t = perm[:, 1].transpose(0, 1, 3, 2).astype(BF16)
    gq = NSA_RPG * HEAD_DIM
    return pl.pallas_call(
        _nsa_kernel,
        grid=(bsz, NSA_KV, nqb),
        in_specs=[
            pl.BlockSpec((1, gq, LANES), lambda b, g, q: (b * nqb + q, g, 0)),
            pl.BlockSpec((1, 1, npc, HEAD_DIM), lambda b, g, q: (b, g, 0, 0)),
            pl.BlockSpec((1, 1, HEAD_DIM, npc), lambda b, g, q: (b, g, 0, 0)),
            pl.BlockSpec((seq, NSA_KVW), lambda b, g, q: (b, g)),
            pl.BlockSpec((seq, NSA_KVW), lambda b, g, q: (b, g)),
            pl.BlockSpec((nqb, HEAD_DIM, LANES), lambda b, g, q: (b, g, 0)),
            pl.BlockSpec((nqb, HEAD_DIM, LANES), lambda b, g, q: (b, NSA_KV + g, 0)),
            pl.BlockSpec((1, 16, LANES), lambda b, g, q: (b * nqb + q, g, 0)),
        ],
        out_specs=pl.BlockSpec((NSA_QBLK, gq), lambda b, g, q: (b * nqb + q, g)),
        out_shape=jax.ShapeDtypeStruct((bsz * seq, NSA_Q), BF16),
        scratch_shapes=[pltpu.VMEM((ns, NSA_QBLK), F32),
                        pltpu.VMEM((2 * HEAD_DIM, NSA_RPG * NSA_QBLK), BF16),
                        pltpu.VMEM((2 * HEAD_DIM, NSA_RPG * NSA_QBLK), BF16),
                        pltpu.VMEM((NSA_KTILE, NSA_RPG * NSA_QBLK), F32),
                        pltpu.VMEM((NSA_KTILE, NSA_RPG * NSA_QBLK), F32),
                        pltpu.VMEM((NSA_KTILE, NSA_RPG * NSA_QBLK), BF16),
                        pltpu.VMEM((NSA_KTILE, NSA_RPG * NSA_QBLK), BF16)],
        compiler_params=_cparams("arbitrary", "arbitrary", "arbitrary"),
        name="nsa_attention",
    )(q_tt, kc, vc_t, k_slc, k_win, v_tt, v_tt, g_tt)


def _swa_kernel(q_ref, kp_ref, kc_ref, vp_ref, vc_ref, sink_ref, o_ref):
    qb = pl.program_id(1)
    t = SWA_WIN
    kband = jnp.concatenate([kp_ref[...], kc_ref[...]], axis=0)
    krel = lax.broadcasted_iota(jnp.int32, (2 * t, 1), 0) - t
    qrel = lax.broadcasted_iota(jnp.int32, (1, t), 1)
    lowest = jnp.where(qb > 0, -t, 0)
    mbias = jnp.where((krel <= qrel) & (krel > qrel - SWA_WIN) & (krel >= lowest), 0.0, NEG)
    outs = []
    for g in range(2):
        rows = slice(g * SWA_RPG * HEAD_DIM, (g + 1) * SWA_RPG * HEAD_DIM)
        qg = q_ref[0, rows, :]
        qcat = jnp.concatenate([qg[r * HEAD_DIM:(r + 1) * HEAD_DIM, :] for r in range(SWA_RPG)], axis=1)
        zq = jnp.zeros_like(qcat)
        qext = jnp.concatenate([qcat, zq] if g == 0 else [zq, qcat], axis=0)
        s = _dot(kband, qext)
        ps, dens = [], []
        for r in range(SWA_RPG):
            h = g * SWA_RPG + r
            sink = sink_ref[h:h + 1, :]
            sr = s[:, r * t:(r + 1) * t] + mbias
            mx = jnp.maximum(jnp.max(sr, axis=0, keepdims=True), sink)
            e = jnp.exp2(sr - mx)
            dens.append(jnp.sum(e, axis=0, keepdims=True) + jnp.exp2(sink - mx))
            ps.append(e.astype(BF16))
        vband = jnp.concatenate([vp_ref[0, g * HEAD_DIM:(g + 1) * HEAD_DIM, :],
                                 vc_ref[0, g * HEAD_DIM:(g + 1) * HEAD_DIM, :]], axis=1)
        og = _dot(vband, jnp.concatenate(ps, axis=1)) * (1.0 / jnp.concatenate(dens, axis=1))
        outs.append(jnp.concatenate([og[:, r * t:(r + 1) * t] for r in range(SWA_RPG)], axis=0).T)
    o_ref[...] = jnp.concatenate(outs, axis=1).astype(o_ref.dtype)


def _swa_mixer(q_tt, k_nat, v_tt, sinks, bsz, seq):
    nqb = seq // SWA_WIN
    sink_rows = jnp.broadcast_to((sinks.astype(F32) * math.log2(math.e))[:, None], (sinks.shape[0], LANES))
    prev = lambda b, q: b * nqb + jnp.maximum(q - 1, 0)
    return pl.pallas_call(
        _swa_kernel,
        grid=(bsz, nqb),
        in_specs=[
            pl.BlockSpec((1, SWA_Q, LANES), lambda b, q: (b * nqb + q, 0, 0)),
            pl.BlockSpec((SWA_WIN, SWA_KVW), lambda b, q: (prev(b, q), 0)),
            pl.BlockSpec((SWA_WIN, SWA_KVW), lambda b, q: (b * nqb + q, 0)),
            pl.BlockSpec((1, SWA_KVW, LANES), lambda b, q: (prev(b, q), 0, 0)),
            pl.BlockSpec((1, SWA_KVW, LANES), lambda b, q: (b * nqb + q, 0, 0)),
            _full(sink_rows.shape),
        ],
        out_specs=pl.BlockSpec((SWA_WIN, SWA_Q), lambda b, q: (b * nqb + q, 0)),
        out_shape=jax.ShapeDtypeStruct((bsz * seq, SWA_Q), BF16),
        compiler_params=_cparams("parallel", "parallel"),
        name="swa_attention",
    )(q_tt, k_nat, k_nat, v_tt, v_tt, sink_rows)


def _s5_params(a_re, a_im, log_dt, b_re, b_im, c_re, c_im, n_chunks):
    f = F32
    t = S5_CHUNK
    step = jnp.exp(log_dt.astype(f))[:, None]
    lr, li = a_re.astype(f), a_im.astype(f)

    def lam_pow(tau):
        tau = tau.astype(f)[:, None, None]
        mag = jnp.exp(lr * step * tau)
        ang = li * step * tau
        return mag * jnp.cos(ang), mag * jnp.sin(ang)

    lb_r, lb_i = (v[0] for v in lam_pow(jnp.ones((1,))))
    nr, ni = lb_r - 1.0, lb_i
    den = lr * lr + li * li
    fr, fi = (nr * lr + ni * li) / den, (ni * lr - nr * li) / den
    br, bi = b_re.astype(f), b_im.astype(f)
    bb_r = fr[..., None] * br - fi[..., None] * bi
    bb_i = fr[..., None] * bi + fi[..., None] * br
    cr, ci = c_re.astype(f), c_im.astype(f)

    pr, pi = lam_pow(jnp.arange(t + 1))
    cl_r = cr[None] * pr[:, :, None, :] - ci[None] * pi[:, :, None, :]
    cl_i = cr[None] * pi[:, :, None, :] + ci[None] * pr[:, :, None, :]
    kern_t = jnp.einsum("tghp,gpk->gkth", cl_r[:t], bb_r, precision="highest") - jnp.einsum(
        "tghp,gpk->gkth", cl_i[:t], bb_i, precision="highest")
    width = t * S5_GROUP_CH
    kpad = jnp.concatenate([jnp.zeros((S5_GROUPS, S5_GROUP_CH, width), f), kern_t.reshape(S5_GROUPS, S5_GROUP_CH, width)],
                           axis=2)
    toep = jnp.stack([kpad[:, :, width - s * S5_GROUP_CH:2 * width - s * S5_GROUP_CH] for s in range(t)], axis=1)
    toep = toep.reshape(S5_GROUPS, width, width)
    rr, ri = pr[t - 1 - jnp.arange(t)], pi[t - 1 - jnp.arange(t)]
    bs_r = rr[..., None] * bb_r[None] - ri[..., None] * bb_i[None]
    bs_i = rr[..., None] * bb_i[None] + ri[..., None] * bb_r[None]
    bs = jnp.concatenate([bs_r, bs_i], axis=2)
    bs = bs.transpose(1, 0, 3, 2).reshape(S5_GROUPS, t * S5_GROUP_CH, 2 * S5_STATE)
    cs = jnp.concatenate([cl_r[1:], -cl_i[1:]], axis=3)
    cs = cs.transpose(1, 3, 0, 2).reshape(S5_GROUPS, 2 * S5_STATE, t * S5_GROUP_CH)
    ar, ai = pr[t], pi[t]
    a1, a2 = [], []
    k = 1
    while k < n_chunks:
        a1.append(jnp.concatenate([ar, ar], axis=1))
        a2.append(jnp.concatenate([-ai, ai], axis=1))
        ar, ai = ar * ar - ai * ai, 2.0 * ar * ai
        k *= 2
    a1 = jnp.stack(a1, axis=1)
    a2 = jnp.stack(a2, axis=1)
    eye = jnp.eye(S5_OCT, dtype=BF16)
    noct = S5_GROUPS // S5_OCT
    hc, ns2 = S5_GROUP_CH, 2 * S5_STATE
    toep_o = jnp.einsum("ogskth,gq->osgktqh", toep.astype(BF16).reshape(noct, S5_OCT, t, hc, t, hc), eye)
    bs_o = jnp.einsum("ogskp,gq->osgkqp", bs.astype(BF16).reshape(noct, S5_OCT, t, hc, ns2), eye)
    cs_o = jnp.einsum("ogpth,gq->ogptqh", cs.astype(BF16).reshape(noct, S5_OCT, ns2, t, hc), eye)
    lanes = t * S5_OCT * hc

    def oct_rows(a):
        return a.reshape(noct, S5_OCT, -1, ns2).transpose(0, 2, 1, 3).reshape(noct, -1, S5_OCT * ns2)

    return (toep_o.reshape(noct, lanes, lanes), bs_o.reshape(noct, lanes, S5_OCT * ns2),
            cs_o.reshape(noct, S5_OCT * ns2, lanes), oct_rows(a1), oct_rows(a2))


def _s5_state_kernel(bsz, u_ref, bs_ref, a1_ref, a2_ref, hi_ref, lo_ref):
    sc = _dot(u_ref[...], bs_ref[0])
    n = sc.shape[0] // bsz
    width = sc.shape[1]
    rowi = lax.broadcasted_iota(jnp.int32, (n, width), 0)

    def swap_re_im(x):
        return jnp.concatenate([pltpu.roll(x[:, j * LANES:(j + 1) * LANES], S5_STATE, 1)
                                for j in range(width // LANES)], axis=1)

    h_in = []
    for b in range(bsz):
        x = sc[b * n:(b + 1) * n]
        k, step = 1, 0
        while k < n:
            xs = jnp.where(rowi >= k, pltpu.roll(x, k, 0), 0.0)
            x = x + a1_ref[0, step:step + 1, :] * xs + a2_ref[0, step:step + 1, :] * swap_re_im(xs)
            k *= 2
            step += 1
        h_in.append(jnp.where(rowi >= 1, pltpu.roll(x, 1, 0), 0.0))
    h_in = jnp.concatenate(h_in, axis=0)
    hi = h_in.astype(BF16)
    hi_ref[0] = hi
    lo_ref[0] = (h_in - hi.astype(F32)).astype(BF16)


def _s5_out_kernel(u_ref, toep_ref, hi_ref, lo_ref, cs_ref, o_ref):
    n = pl.program_id(1)
    cols = toep_ref.shape[2]
    off = _dot(hi_ref[0], cs_ref[0]) + _dot(lo_ref[0], cs_ref[0])
    for nn in range(toep_ref.shape[1] // cols):
        @pl.when(n == nn)
        def _(nn=nn):
            k = (nn + 1) * cols
            y = off + _dot(u_ref[:, 0:k], toep_ref[0, 0:k, :])
            for t8 in range(cols // LANES):
                o_ref[:, t8, :] = y[:, t8 * LANES:(t8 + 1) * LANES].astype(o_ref.dtype)


def _s5_glu_kernel(y_ref, u_ref, d_ref, w_ref, b_ref, o_ref):
    y = _gelu_tanh(y_ref[...] + d_ref[...] * u_ref[...])
    gate = _sigmoid(_dot(y.astype(BF16), w_ref[...]) + b_ref[...])
    o_ref[...] = (y * gate).astype(o_ref.dtype)


def _s5_mixer(u5, u_chunks, bsz, seq, a_re, a_im, log_dt, b_re, b_im, c_re, c_im, d_skip, glu_w, glu_b, tm=1024):
    m = u5.shape[0]
    t = S5_CHUNK
    nch = m // t
    noct = S5_GROUPS // S5_OCT
    lanes = t * LANES
    sw = S5_OCT * 2 * S5_STATE
    toep, bs, cs, a1, a2 = _s5_params(a_re, a_im, log_dt, b_re, b_im, c_re, c_im, nch // bsz)
    nsteps = a1.shape[1]
    h_hi, h_lo = pl.pallas_call(
        functools.partial(_s5_state_kernel, bsz),
        grid=(noct,),
        in_specs=[
            pl.BlockSpec((nch, lanes), lambda o: (0, o)),
            pl.BlockSpec((1, lanes, sw), lambda o: (o, 0, 0)),
            pl.BlockSpec((1, nsteps, sw), lambda o: (o, 0, 0)),
            pl.BlockSpec((1, nsteps, sw), lambda o: (o, 0, 0)),
        ],
        out_specs=[pl.BlockSpec((1, nch, sw), lambda o: (o, 0, 0))] * 2,
        out_shape=[jax.ShapeDtypeStruct((noct, nch, sw), BF16)] * 2,
        compiler_params=_cparams("parallel"),
        name="s5_state",
    )(u_chunks, bs, a1, a2)
    tsub = 8
    rows = nch // 2
    y = pl.pallas_call(
        _s5_out_kernel,
        grid=(noct, t // tsub, nch // rows),
        in_specs=[
            pl.BlockSpec((rows, lanes), lambda o, n, r: (r, o)),
            pl.BlockSpec((1, lanes, tsub * LANES), lambda o, n, r: (o, 0, n)),
            pl.BlockSpec((1, rows, sw), lambda o, n, r: (o, r, 0)),
            pl.BlockSpec((1, rows, sw), lambda o, n, r: (o, r, 0)),
            pl.BlockSpec((1, sw, tsub * LANES), lambda o, n, r: (o, 0, n)),
        ],
        out_specs=pl.BlockSpec((rows, tsub, LANES), lambda o, n, r: (r, n, o)),
        out_shape=jax.ShapeDtypeStruct((nch, t, S5_CH), BF16),
        compiler_params=_cparams("parallel", "parallel", "parallel"),
        name="s5_scan",
    )(u_chunks, toep, h_hi, h_lo, cs).reshape(m, S5_CH)
    return pl.pallas_call(
        _s5_glu_kernel,
        grid=(m // tm,),
        in_specs=[
            pl.BlockSpec((tm, S5_CH), lambda i: (i, 0)),
            pl.BlockSpec((tm, S5_CH), lambda i: (i, 0)),
            _full((1, S5_CH)), _full((S5_CH, S5_CH)), _full((1, S5_CH)),
        ],
        out_specs=pl.BlockSpec((tm, S5_CH), lambda i: (i, 0)),
        out_shape=jax.ShapeDtypeStruct((m, S5_CH), BF16),
        compiler_params=_cparams("parallel"),
        name="s5_glu",
    )(y, u5, d_skip.reshape(1, S5_CH).astype(F32), glu_w.astype(BF16), glu_b.reshape(1, S5_CH).astype(F32))


def _even_mixers(h, bsz, seq, g_mix, w_in, conv_w, conv_b, dt_bias, a_log, d_skip, norm_g, pe, w1, b1, w2, b2):
    d = h.shape[1]
    scale = HEAD_DIM ** -0.5 * math.log2(math.e)
    o = SSD_IN
    w_ssd = jnp.concatenate([w_in[:, :SSD_IN], jnp.zeros((d, LANES - SSD_HEADS), w_in.dtype)], axis=1)
    w_q = w_in[:, o:o + NSA_Q] * scale
    kv = [w_in[:, o + NSA_Q + i * NSA_KVW:o + NSA_Q + (i + 1) * NSA_KVW] for i in range(6)]
    w_gate = w_in[:, o + NSA_Q + 6 * NSA_KVW:].reshape(d, NSA_KV, NSA_RPG, 3).transpose(0, 1, 3, 2)
    w_gate = jnp.concatenate([w_gate.reshape(d, NSA_KV, 12), jnp.zeros((d, NSA_KV, 4), w_in.dtype)],
                             axis=2).reshape(d, NSA_KV * 16)

    def per_group_halves(w):
        wg = w.reshape(d, NSA_KV, HEAD_DIM)
        return jnp.concatenate([wg, jnp.zeros_like(wg)], axis=2).reshape(d, NSA_KV * NSA_KVW)
    segs = [
        ("nat", w_ssd, F32),
        ("tt", w_q, BF16),
        ("pc", jnp.concatenate([kv[0], kv[1]], axis=1), F32),
        ("nat+", per_group_halves(kv[2]), BF16, _nsa_block_onehot(NSA_KTILE)),
        ("nat", per_group_halves(kv[4]), BF16),
        ("tt", jnp.concatenate([kv[3], kv[5]], axis=1), BF16),
        ("tt", w_gate, F32),
    ]
    u_ssd, q_tt, kvc, k_slc, k_win, v_tt, g_tt = _norm_proj(h, g_mix, segs)
    ya = _ssd_mixer(u_ssd, bsz, seq, conv_w, conv_b, dt_bias, a_log, d_skip, norm_g)
    yb = _nsa_mixer(q_tt, kvc, k_slc, k_win, v_tt, g_tt, bsz, seq, pe, w1, b1, w2, b2)
    return ya, yb


def _odd_mixers(h, bsz, seq, g_mix, w_in, sinks, a_re, a_im, log_dt, b_re, b_im, c_re, c_im, d_skip, glu_w, glu_b):
    scale = HEAD_DIM ** -0.5 * math.log2(math.e)
    segs = [
        ("tt", w_in[:, :SWA_Q] * scale, BF16),
        ("nat", w_in[:, SWA_Q:SWA_Q + SWA_KVW], BF16),
        ("tt", w_in[:, SWA_Q + SWA_KVW:SWA_Q + 2 * SWA_KVW], BF16),
        ("nat+ch", w_in[:, SWA_Q + 2 * SWA_KVW:], F32),
    ]
    q_tt, k_nat, v_tt, u5, u_chunks = _norm_proj(h, g_mix, segs)
    yc = _swa_mixer(q_tt, k_nat, v_tt, sinks, bsz, seq)
    yd = _s5_mixer(u5, u_chunks, bsz, seq, a_re, a_im, log_dt, b_re, b_im, c_re, c_im, d_skip, glu_w, glu_b)
    return yc, yd


def kernel(x, norm_mix, norm_mlp, norm_final, mlp_w_up, mlp_w_down, ev_w_in, ev_w_out, ssd_conv_w, ssd_conv_b,
           ssd_dt_bias, ssd_a_log, ssd_d, ssd_norm, nsa_pe, nsa_cmp_w1, nsa_cmp_b1, nsa_cmp_w2, nsa_cmp_b2,
           od_w_in, od_w_out, swa_sinks, s5_a_re, s5_a_im, s5_log_dt, s5_b_re, s5_b_im, s5_c_re, s5_c_im,
           s5_d, s5_glu_w, s5_glu_b):
    bsz, seq, d = x.shape
    depth = norm_mix.shape[0]
    assert seq % (NSA_UNROLL * NSA_KTILE) == 0 and seq >= NSA_WIN + NSA_QBLK
    h = x.reshape(bsz * seq, d)
    for layer in range(depth):
        i = layer // 2
        if layer % 2 == 0:
            ya, yb = _even_mixers(h, bsz, seq, norm_mix[layer], ev_w_in[i], ssd_conv_w[i], ssd_conv_b[i],
                                  ssd_dt_bias[i], ssd_a_log[i], ssd_d[i], ssd_norm[i], nsa_pe[i],
                                  nsa_cmp_w1[i], nsa_cmp_b1[i], nsa_cmp_w2[i], nsa_cmp_b2[i])
            w_out = ev_w_out[i]
        else:
            ya, yb = _odd_mixers(h, bsz, seq, norm_mix[layer], od_w_in[i], swa_sinks[i], s5_a_re[i], s5_a_im[i],
                                 s5_log_dt[i], s5_b_re[i], s5_b_im[i], s5_c_re[i], s5_c_im[i], s5_d[i],
                                 s5_glu_w[i], s5_glu_b[i])
            w_out = od_w_out[i]
        h = _mix_out_mlp(h, ya, yb, w_out, norm_mlp[layer], mlp_w_up[layer], mlp_w_down[layer], norm_final,
                         final=(layer == depth - 1))
    return h.reshape(bsz, seq, d)
```

```python
import functools
import math

import jax
import jax.numpy as jnp
import numpy as np
from jax import lax
from jax.experimental import pallas as pl
from jax.experimental.pallas import tpu as pltpu

F32 = jnp.float32
BF16 = jnp.bfloat16

EPS = 1e-6
NEG = -1e30
FORCE = 1e9
HEAD_DIM = 64
LANES = 128
VMEM_LIMIT_BYTES = 56 * 1024 * 1024

SSD_HEADS = 8
SSD_INNER = 512
SSD_GROUPS = 2
SSD_STATE = 128
SSD_CONV = 4
SSD_CHUNK = 128
SSD_CONV_DIM = 1024
SSD_IN = SSD_INNER + SSD_CONV_DIM + SSD_HEADS

NSA_HEADS = 8
NSA_KV = 2
NSA_RPG = 4
NSA_CMP_LEN = 32
NSA_CMP_STRIDE = 16
NSA_SLC_LEN = 64
NSA_TOPK = 16
NSA_WIN = 512
NSA_CMP_HIDDEN = 256
NSA_QBLK = 128
NSA_Q = 512
NSA_KVW = 128
NSA_KTILE = 512
NSA_UNROLL = 2

SWA_RPG = 4
SWA_WIN = 128
SWA_Q = 512
SWA_KVW = 128

S5_CH = 512
S5_GROUP_CH = 16
S5_GROUPS = 32
S5_STATE = 64
S5_CHUNK = 32
S5_OCT = LANES // S5_GROUP_CH


def _cparams(*sem):
    return pltpu.CompilerParams(dimension_semantics=sem, vmem_limit_bytes=VMEM_LIMIT_BYTES)


def _full(shape):
    n = len(shape)
    return pl.BlockSpec(shape, lambda *_: (0,) * n)


def _dot(a, b):
    return jnp.dot(a, b, preferred_element_type=F32)


def _dot_nt(a, b):
    return lax.dot_general(a, b, (((1,), (1,)), ((), ())), preferred_element_type=F32)


def _split3(a):
    hi = a.astype(BF16)
    r1 = a - hi.astype(F32)
    mid = r1.astype(BF16)
    lo = (r1 - mid.astype(F32)).astype(BF16)
    return hi, mid, lo


def _dot_exact_rhs(a, b_exact):
    hi, mid, lo = _split3(a)
    return _dot(hi, b_exact) + _dot(mid, b_exact) + _dot(lo, b_exact)


def _dot_exact_lhs(a_exact, b):
    hi, mid, lo = _split3(b)
    return _dot(a_exact, hi) + _dot(a_exact, mid) + _dot(a_exact, lo)


def _rms(x, g):
    return x * lax.rsqrt(jnp.mean(x * x, axis=-1, keepdims=True) + EPS) * g


def _gelu_tanh(x):
    c = math.sqrt(2.0 / math.pi)
    return 0.5 * x * (1.0 + jnp.tanh(c * (x + 0.044715 * (x * x * x))))


def _sigmoid(x):
    return 1.0 / (1.0 + jnp.exp(-x))


def _proj_kernel(kinds, tm, h_ref, g_ref, *refs):
    n = len(kinds)
    n_add = sum(k == "nat+" for k in kinds)
    n_out = n + sum(k == "nat+ch" for k in kinds)
    w_refs, add_refs = refs[:n], list(refs[n:n + n_add])
    o_refs, scratch = list(refs[n + n_add:n + n_add + n_out]), list(refs[n + n_add + n_out:])
    yb = _rms(h_ref[...], g_ref[...]).astype(BF16)
    for kind, w_ref in zip(kinds, w_refs):
        o_ref = o_refs.pop(0)
        if kind == "nat":
            o_ref[...] = _dot(yb, w_ref[...]).astype(o_ref.dtype)
        elif kind == "nat+ch":
            ch_ref, ch_s = o_refs.pop(0), scratch.pop(0)
            res = _dot(yb, w_ref[...])
            o_ref[...] = res.astype(o_ref.dtype)
            for j in range(ch_s.shape[0]):
                ch_s[j] = res[:, j * LANES:(j + 1) * LANES]
            for t in range(S5_CHUNK):
                for j in range(ch_s.shape[0]):
                    col = (j * S5_CHUNK + t) * LANES
                    ch_ref[:, col:col + LANES] = ch_s[j, pl.ds(t, tm // S5_CHUNK, stride=S5_CHUNK), :].astype(ch_ref.dtype)
        elif kind == "nat+":
            o_ref[...] = _dot(yb, w_ref[...]).astype(o_ref.dtype) + add_refs.pop(0)[...]
        elif kind == "pc":
            pc_s = scratch.pop(0)
            res = _dot(yb, w_ref[...])
            flat = NSA_CMP_STRIDE * HEAD_DIM
            per_tile = LANES // HEAD_DIM
            for j in range(pc_s.shape[0]):
                pc_s[j] = res[:, j * LANES:(j + 1) * LANES]
            for p in range(NSA_CMP_STRIDE):
                for j in range(pc_s.shape[0]):
                    tok = pc_s[j, pl.ds(p, tm // NSA_CMP_STRIDE, stride=NSA_CMP_STRIDE), :]
                    for c in range(per_tile):
                        col = (j * per_tile + c) * flat + p * HEAD_DIM
                        o_ref[:, col:col + HEAD_DIM] = tok[:, c * HEAD_DIM:(c + 1) * HEAD_DIM].astype(o_ref.dtype)
        else:
            ot = _dot_nt(w_ref[...], yb)
            for j in range(tm // LANES):
                o_ref[j] = ot[:, j * LANES:(j + 1) * LANES].astype(o_ref.dtype)


def _norm_proj(h, g, segs, tm=512):
    m, d = h.shape
    kinds = tuple(s[0] for s in segs)
    adds = [s[3] for s in segs if s[0] == "nat+"]
    ws, w_specs, out_shapes, out_specs, scratch = [], [], [], [], []
    for kind, w, dt in (s[:3] for s in segs):
        n_out = w.shape[1]
        if kind == "pc":
            ws.append(w.astype(BF16))
            w_specs.append(_full((d, n_out)))
            out_shapes.append(jax.ShapeDtypeStruct((m // NSA_CMP_STRIDE, NSA_CMP_STRIDE * n_out), dt))
            out_specs.append(pl.BlockSpec((tm // NSA_CMP_STRIDE, NSA_CMP_STRIDE * n_out), lambda i: (i, 0)))
            scratch.append(pltpu.VMEM((n_out // LANES, tm, LANES), F32))
        elif kind in ("nat", "nat+", "nat+ch"):
            ws.append(w.astype(BF16))
            w_specs.append(_full((d, n_out)))
            out_shapes.append(jax.ShapeDtypeStruct((m, n_out), dt))
            out_specs.append(pl.BlockSpec((tm, n_out), lambda i: (i, 0)))
            if kind == "nat+ch":
                out_shapes.append(jax.ShapeDtypeStruct((m // S5_CHUNK, S5_CHUNK * n_out), BF16))
                out_specs.append(pl.BlockSpec((tm // S5_CHUNK, S5_CHUNK * n_out), lambda i: (i, 0)))
                scratch.append(pltpu.VMEM((n_out // LANES, tm, LANES), F32))
        else:
            ws.append(w.T.astype(BF16))
            w_specs.append(_full((n_out, d)))
            out_shapes.append(jax.ShapeDtypeStruct((m // LANES, n_out, LANES), dt))
            out_specs.append(pl.BlockSpec((tm // LANES, n_out, LANES), lambda i: (i, 0, 0)))
    return pl.pallas_call(
        functools.partial(_proj_kernel, kinds, tm),
        grid=(m // tm,),
        in_specs=[pl.BlockSpec((tm, d), lambda i: (i, 0)), _full((1, d))] + w_specs + [
            pl.BlockSpec((tm, a.shape[1]), functools.partial(lambda i, nb: (i % nb, 0), nb=a.shape[0] // tm))
            for a in adds],
        out_specs=out_specs,
        out_shape=out_shapes,
        scratch_shapes=scratch,
        compiler_params=_cparams("parallel"),
        name="norm_proj",
    )(h, g.reshape(1, d), *ws, *adds)


def _mlp_kernel(final, h_ref, ya_ref, yb_ref, woa_ref, wob_ref, gm_ref, wup_ref, wdn_ref, gf_ref,
                o_ref, h2_s, xn_s, acc_s):
    j = pl.program_id(1)

    @pl.when(j == 0)
    def _():
        h2 = h_ref[...] + _dot(ya_ref[...], woa_ref[...]) + _dot(yb_ref[...], wob_ref[...])
        h2_s[...] = h2
        xn_s[...] = _rms(h2, gm_ref[...]).astype(BF16)
        acc_s[...] = jnp.zeros_like(acc_s)

    hid = jnp.square(jnp.maximum(_dot(xn_s[...], wup_ref[...]), 0.0))
    acc_s[...] += _dot(hid.astype(BF16), wdn_ref[...])

    @pl.when(j == pl.num_programs(1) - 1)
    def _():
        out = h2_s[...] + acc_s[...]
        if final:
            out = _rms(out, gf_ref[...])
        o_ref[...] = out


def _mix_out_mlp(h, ya, yb, w_out, g_mlp, w_up, w_down, g_final, final, tm=1024, tf=1024):
    m, d = h.shape
    dff = w_up.shape[1]
    na = ya.shape[1]
    nb = yb.shape[1]
    return pl.pallas_call(
        functools.partial(_mlp_kernel, final),
        grid=(m // tm, dff // tf),
        in_specs=[
            pl.BlockSpec((tm, d), lambda i, j: (i, 0)),
            pl.BlockSpec((tm, na), lambda i, j: (i, 0)),
            pl.BlockSpec((tm, nb), lambda i, j: (i, 0)),
            _full((na, d)), _full((nb, d)), _full((1, d)),
            pl.BlockSpec((d, tf), lambda i, j: (0, j)),
            pl.BlockSpec((tf, d), lambda i, j: (j, 0)),
            _full((1, d)),
        ],
        out_specs=pl.BlockSpec((tm, d), lambda i, j: (i, 0)),
        out_shape=jax.ShapeDtypeStruct((m, d), F32),
        scratch_shapes=[pltpu.VMEM((tm, d), F32), pltpu.VMEM((tm, d), BF16), pltpu.VMEM((tm, d), F32)],
        compiler_params=_cparams("parallel", "arbitrary"),
        name="out_proj_mlp",
    )(h, ya, yb, w_out[:na].astype(BF16), w_out[na:].astype(BF16), g_mlp.reshape(1, d),
      w_up.astype(BF16), w_down.astype(BF16), g_final.reshape(1, d))


def _ssd_kernel(u_ref, cw_ref, cb_ref, dtb_ref, alog_ref, dsk_ref, ng_ref, o_ref, xext_s, st_s):
    t = SSD_CHUNK
    c = pl.program_id(1)

    @pl.when(c == 0)
    def _():
        xext_s[0:8, :] = jnp.zeros((8, SSD_CONV_DIM), F32)
        st_s[...] = jnp.zeros_like(st_s)

    z = u_ref[:, 0:SSD_INNER]
    dt_raw = u_ref[:, SSD_INNER + SSD_CONV_DIM:]
    xext_s[8:8 + t, :] = u_ref[:, SSD_INNER:SSD_INNER + SSD_CONV_DIM]
    conv = cb_ref[...] + cw_ref[0:1, :] * xext_s[5:5 + t, :]
    for k in range(1, SSD_CONV):
        conv = conv + cw_ref[k:k + 1, :] * xext_s[5 + k:5 + k + t, :]
    xext_s[0:8, :] = xext_s[t:t + 8, :]
    xc = conv * _sigmoid(conv)
    xs = xc[:, 0:SSD_INNER]
    gn = SSD_GROUPS * SSD_STATE

    dtp = dt_raw + dtb_ref[...]
    dt = jnp.maximum(dtp, 0.0) + jnp.log1p(jnp.exp(-jnp.abs(dtp)))
    a = -jnp.exp(alog_ref[...])
    da = dt * a

    row = lax.broadcasted_iota(jnp.int32, (t, t), 0)
    col = lax.broadcasted_iota(jnp.int32, (t, t), 1)
    causal = col <= row
    tril = jnp.where(causal, 1.0, 0.0).astype(BF16)
    a_cum = _dot_exact_lhs(tril, da)
    er = lax.broadcasted_iota(jnp.int32, (LANES, SSD_INNER), 0)
    ec = lax.broadcasted_iota(jnp.int32, (LANES, SSD_INNER), 1)
    expand = jnp.where((ec >> 6) == er, 1.0, 0.0).astype(BF16)
    a_cum_x = _dot_exact_rhs(a_cum, expand)
    dt_x = _dot_exact_rhs(dt, expand)
    a_cum_t = a_cum.T
    a_last_x = a_cum_x[t - 1:t, :]
    decay_end_x = jnp.exp(a_last_x - a_cum_x)
    decay_in_x = jnp.exp(a_cum_x)
    chunk_decay_x = jnp.exp(a_last_x)

    xd = xs * dt_x
    xd_end = (xd * decay_end_x).astype(BF16)
    xd_b = xd.astype(BF16)
    lane = lax.broadcasted_iota(jnp.int32, (t, LANES), 1)
    first_half = lane < HEAD_DIM

    pieces = []
    for g in range(SSD_GROUPS):
        bm = xc[:, SSD_INNER + g * SSD_STATE:SSD_INNER + (g + 1) * SSD_STATE]
        cm = xc[:, SSD_INNER + gn + g * SSD_STATE:SSD_INNER + gn + (g + 1) * SSD_STATE].astype(BF16)
        bm_t = bm.T.astype(BF16)
        cb = _dot_nt(cm, bm.astype(BF16))
        for pr in range(2):
            i = g * 2 + pr
            sl = slice(i * LANES, (i + 1) * LANES)
            ms = []
            for hh in range(2):
                h = 2 * i + hh
                seg = a_cum[:, h:h + 1] - a_cum_t[h:h + 1, :]
                dec = jnp.exp(jnp.where(causal, seg, NEG))
                ms.append((cb * dec).astype(BF16))
            y_diag = jnp.where(first_half, _dot(ms[0], xd_b[:, sl]), _dot(ms[1], xd_b[:, sl]))
            st = st_s[i]
            y_off = _dot(cm, st.astype(BF16)) * decay_in_x[:, sl]
            st_s[i] = st * chunk_decay_x[:, sl] + _dot(bm_t, xd_end[:, sl])
            pieces.append(y_diag + y_off)
    y = jnp.concatenate(pieces, axis=1) + xs * dsk_ref[...]
    y = y * (z * _sigmoid(z))
    half = SSD_INNER // SSD_GROUPS
    outs = [_rms(y[:, g * half:(g + 1) * half], ng_ref[:, g * half:(g + 1) * half]) for g in range(SSD_GROUPS)]
    o_ref[...] = jnp.concatenate(outs, axis=1).astype(o_ref.dtype)


def _ssd_mixer(u_ssd, bsz, seq, conv_w, conv_b, dt_bias, a_log, d_skip, norm_g):
    m, width = u_ssd.shape
    nch = seq // SSD_CHUNK
    pad = LANES - SSD_HEADS

    def padded(v):
        return jnp.concatenate([v.astype(F32), jnp.zeros((pad,), F32)]).reshape(1, LANES)

    return pl.pallas_call(
        _ssd_kernel,
        grid=(bsz, nch),
        in_specs=[
            pl.BlockSpec((SSD_CHUNK, width), lambda b, c: (b * nch + c, 0)),
            _full((SSD_CONV, SSD_CONV_DIM)), _full((1, SSD_CONV_DIM)),
            _full((1, LANES)), _full((1, LANES)), _full((1, SSD_INNER)), _full((1, SSD_INNER)),
        ],
        out_specs=pl.BlockSpec((SSD_CHUNK, SSD_INNER), lambda b, c: (b * nch + c, 0)),
        out_shape=jax.ShapeDtypeStruct((m, SSD_INNER), BF16),
        scratch_shapes=[pltpu.VMEM((SSD_CHUNK + 8, SSD_CONV_DIM), F32),
                        pltpu.VMEM((SSD_HEADS // 2, SSD_STATE, LANES), F32)],
        compiler_params=_cparams("arbitrary", "arbitrary"),
        name="ssd_mixer",
    )(u_ssd, conv_w.astype(F32), conv_b.reshape(1, -1).astype(F32), padded(dt_bias), padded(a_log),
      jnp.repeat(d_skip.astype(F32), HEAD_DIM).reshape(1, SSD_INNER), norm_g.reshape(1, SSD_INNER).astype(F32))


def _nsa_compress_kernel(x_ref, pe_ref, w1_ref, b1_ref, w2_ref, b2_ref, o_ref):
    x = x_ref[...]
    npc = x.shape[0]
    half = NSA_CMP_STRIDE * HEAD_DIM
    top = _dot((x + pe_ref[0, :, 0:half]).astype(BF16), w1_ref[0, 0:half, :])
    bot = _dot((x + pe_ref[0, :, half:]).astype(BF16), w1_ref[0, half:, :])
    pre = top + pltpu.roll(bot, npc - 1, 0) + b1_ref[0]
    hid = _gelu_tanh(pre).astype(BF16)
    out = _dot(hid, w2_ref[0]) + b2_ref[0]
    rowi = lax.broadcasted_iota(jnp.int32, out.shape, 0)
    o_ref[0, 0, 0] = jnp.where(rowi < npc - 1, out, 0.0)


def _nsa_compress(kvc_pieces, bsz, seq, pe, w1, b1, w2, b2):
    npc = seq // NSA_CMP_STRIDE
    flat = NSA_CMP_STRIDE * HEAD_DIM
    x = kvc_pieces
    return pl.pallas_call(
        _nsa_compress_kernel,
        grid=(bsz, 2, NSA_KV),
        in_specs=[
            pl.BlockSpec((npc, flat), lambda b, s, g: (b, s * NSA_KV + g)),
            pl.BlockSpec((1, 1, 2 * flat), lambda b, s, g: (s, 0, 0)),
            pl.BlockSpec((1, 2 * flat, NSA_CMP_HIDDEN), lambda b, s, g: (s, 0, 0)),
            pl.BlockSpec((1, 1, NSA_CMP_HIDDEN), lambda b, s, g: (s, 0, 0)),
            pl.BlockSpec((1, NSA_CMP_HIDDEN, HEAD_DIM), lambda b, s, g: (s, 0, 0)),
            pl.BlockSpec((1, 1, HEAD_DIM), lambda b, s, g: (s, 0, 0)),
        ],
        out_specs=pl.BlockSpec((1, 1, 1, npc, HEAD_DIM), lambda b, s, g: (b, s, g, 0, 0)),
        out_shape=jax.ShapeDtypeStruct((bsz, 2, NSA_KV, npc, HEAD_DIM), F32),
        compiler_params=_cparams("parallel", "parallel", "parallel"),
        name="nsa_compress",
    )(x, pe.reshape(2, 1, 2 * flat).astype(F32), w1.astype(BF16), b1.reshape(2, 1, -1).astype(F32),
      w2.astype(BF16), b2.reshape(2, 1, -1).astype(F32))


def _nsa_cmp_split(ns):
    return max(1, min(4, (4 * ns) // LANES))


def _nsa_block_onehot(rows):
    r = lax.broadcasted_iota(jnp.int32, (rows, 2 * NSA_KVW), 0)
    c = lax.broadcasted_iota(jnp.int32, (rows, 2 * NSA_KVW), 1)
    blk = (r % NSA_KTILE) // NSA_SLC_LEN
    return jnp.where((c % NSA_KVW) == HEAD_DIM + blk, 1.0, 0.0).astype(BF16)


def _nsa_kernel(q_ref, kc_ref, vc_ref, ks_ref, kw_ref, vs_ref, vw_ref, g_ref, o_ref,
                bias_s, qaug_s, qaug2_s, sa_s, sb_s, pa_s, pb_s):
    qb = pl.program_id(2)
    nqt = NSA_RPG * NSA_QBLK
    s0 = qb * NSA_QBLK
    ncp = kc_ref.shape[2]
    ns = ncp // 4
    heads = [slice(r * NSA_QBLK, (r + 1) * NSA_QBLK) for r in range(NSA_RPG)]

    qcat = jnp.concatenate([q_ref[0, r * HEAD_DIM:(r + 1) * HEAD_DIM, :] for r in range(NSA_RPG)], axis=1)
    qpos = s0 + lax.broadcasted_iota(jnp.int32, (1, NSA_QBLK), 1)
    qaug_s[0:HEAD_DIM, :] = qcat
    qaug_s[HEAD_DIM:, :] = jnp.zeros((HEAD_DIM, nqt), BF16)

    split = _nsa_cmp_split(ns)
    chunk = ns // split
    cvalid = jnp.where(qpos >= NSA_CMP_LEN - 1, 1.0, 0.0)
    cur = qpos // NSA_SLC_LEN
    taken = -3.0e38

    def cmp_and_select(nchunks):
        rows, jmax = nchunks * 4 * chunk, nchunks * chunk
        kc = kc_ref[0, 0, 0:rows, :]
        rc = lax.broadcasted_iota(jnp.int32, (rows, 1), 0)
        ncmp = 4 * ((rc // (4 * chunk)) * chunk + rc % chunk) + (rc % (4 * chunk)) // chunk
        cbias = jnp.where((ncmp * NSA_CMP_STRIDE + (NSA_CMP_LEN - 1)) <= qpos, 0.0, NEG)
        psum = jnp.zeros((rows, NSA_QBLK), F32)
        p_all = []
        for sl in heads:
            s = _dot(kc, qcat[:, sl]) + cbias
            e = jnp.exp2(s - jnp.max(s, axis=0, keepdims=True))
            p = e * (cvalid / jnp.sum(e, axis=0, keepdims=True))
            psum = psum + p
            p_all.append(p.astype(BF16))
        o_cmp = _dot(vc_ref[0, 0, :, 0:rows], jnp.concatenate(p_all, axis=1))

        tot, p3 = [], []
        for c in range(nchunks):
            part = [psum[(4 * c + i) * chunk:(4 * c + i + 1) * chunk] for i in range(4)]
            tot.append(part[0] + part[1] + part[2] + part[3])
            p3.append(part[3])
        tot, p3 = jnp.concatenate(tot, axis=0), jnp.concatenate(p3, axis=0)
        rj = lax.broadcasted_iota(jnp.int32, (jmax, NSA_QBLK), 0)
        imp = tot + jnp.where(rj >= 1, pltpu.roll(p3, 1, 0), 0.0)
        forced = (rj == 0) | (rj == cur) | (rj == cur - 1)
        rjf = rj.astype(F32)
        imp = jnp.where(forced, taken, jnp.where(rj <= cur, imp, -FORCE))
        for _ in range(min(NSA_TOPK, ns) - 3):
            mx = jnp.max(imp, axis=0, keepdims=True)
            first = jnp.min(jnp.where(imp == mx, rjf, float(ns)), axis=0, keepdims=True)
            imp = jnp.where(rjf == first, taken, imp)
        bias = jnp.where(imp == taken, 0.0, NEG)
        if jmax < ns:
            bias = jnp.concatenate([bias, jnp.full((ns - jmax, NSA_QBLK), NEG, F32)], axis=0)
        return o_cmp, bias

    last_block = (s0 + NSA_QBLK - 1) // NSA_SLC_LEN
    o_cmp, bias = lax.switch(last_block // chunk,
                             [functools.partial(cmp_and_select, n + 1) for n in range(split)])
    bias_s[...] = bias

    kt_diag = s0 // NSA_KTILE
    blocks_per_tile = NSA_KTILE // NSA_SLC_LEN
    vtiles = NSA_KTILE // LANES

    def qk_tile(kt, qaug_ref):
        k0 = pl.multiple_of(kt * NSA_KTILE, NSA_KTILE)
        b8 = bias_s[pl.ds(pl.multiple_of(kt * blocks_per_tile, blocks_per_tile), blocks_per_tile), :]
        b16 = jnp.concatenate([b8, jnp.zeros_like(b8)], axis=0).astype(BF16)
        qaug_ref[HEAD_DIM:HEAD_DIM + 16, :] = jnp.concatenate([b16] * NSA_RPG, axis=1)
        return _dot(ks_ref[pl.ds(k0, NSA_KTILE), :], qaug_ref[...])

    ones_rows = jnp.ones((16, NSA_KTILE), BF16)

    def pv_tile(kt, p):
        vt = jnp.concatenate([vs_ref[kt * vtiles + i] for i in range(vtiles)], axis=1)
        return _dot(jnp.concatenate([vt, ones_rows], axis=0), p)

    def softmax_tile(s, m_old):
        m_new = jnp.maximum(m_old, jnp.max(s, axis=0, keepdims=True))
        p = jnp.exp2((s - m_new).astype(BF16))
        return p, m_new, jnp.exp2(m_old - m_new)

    def visible(kt):
        kpos = kt * NSA_KTILE + lax.broadcasted_iota(jnp.int32, (NSA_KTILE, 1), 0)
        return jnp.concatenate([kpos <= qpos] * NSA_RPG, axis=1)

    def tile_group(i, carry, last):
        m_run, acc, alpha_prev = carry
        for k in range(NSA_UNROLL):
            t = NSA_UNROLL * i + k
            s_cur, p_cur, s_nxt, p_prv, qa = ((sa_s, pa_s, sb_s, pb_s, qaug_s) if k % 2 == 0 else
                                              (sb_s, pb_s, sa_s, pa_s, qaug2_s))
            acc = alpha_prev * acc + pv_tile(jnp.maximum(t - 1, 0), p_prv[...])
            s = jnp.where(visible(t), s_cur[...], NEG) if last else s_cur[...]
            p, m_run, alpha_prev = softmax_tile(s, m_run)
            p_cur[...] = p
            if last and k == NSA_UNROLL - 1:
                acc = alpha_prev * acc + pv_tile(t, p_cur[...])
            else:
                s_nxt[...] = qk_tile(t + 1, qa)
        return m_run, acc, alpha_prev

    qaug2_s[...] = qaug_s[...]
    pb_s[...] = jnp.zeros((NSA_KTILE, nqt), BF16)
    sa_s[...] = qk_tile(0, qaug2_s)

    span = NSA_WIN + NSA_QBLK
    start = pl.multiple_of(jnp.maximum(s0 - NSA_WIN, 0), NSA_QBLK)
    kwin = kw_ref[pl.ds(start, span), :]
    kp = start + lax.broadcasted_iota(jnp.int32, (span, 1), 0)
    wbias = jnp.where((kp <= qpos) & (kp > qpos - NSA_WIN), 0.0, NEG)
    pw, dens = [], []
    for sl in heads:
        s = _dot(kwin, qaug_s[:, sl]) + wbias
        e = jnp.exp2(s - jnp.max(s, axis=0, keepdims=True))
        dens.append(jnp.sum(e, axis=0, keepdims=True))
        pw.append(e.astype(BF16))
    sblk = start // LANES
    vwt = jnp.concatenate([vw_ref[sblk + i] for i in range(span // LANES)], axis=1)
    o_win = _dot(vwt, jnp.concatenate(pw, axis=1)) * (1.0 / jnp.concatenate(dens, axis=1))

    init = (jnp.full((1, nqt), NEG, F32), jnp.zeros((HEAD_DIM + 16, nqt), F32), jnp.ones((1, nqt), F32))
    group_diag = kt_diag // NSA_UNROLL
    carry = lax.fori_loop(0, group_diag, lambda i, c: tile_group(i, c, False), init)
    _, acc, _ = tile_group(group_diag, carry, True)
    o_slc = acc[0:HEAD_DIM] * (1.0 / acc[HEAD_DIM:HEAD_DIM + 1])

    gates = _sigmoid(g_ref[0])
    outs = [gates[r:r + 1, :] * o_cmp[:, sl] + gates[NSA_RPG + r:NSA_RPG + r + 1, :] * o_slc[:, sl]
            + gates[2 * NSA_RPG + r:2 * NSA_RPG + r + 1, :] * o_win[:, sl] for r, sl in enumerate(heads)]
    o_ref[...] = jnp.concatenate(outs, axis=0).T.astype(o_ref.dtype)


def _nsa_mixer(q_tt, kvc, k_slc, k_win, v_tt, g_tt, bsz, seq, pe, w1, b1, w2, b2):
    nqb = seq // NSA_QBLK
    npc = seq // NSA_CMP_STRIDE
    ns = seq // NSA_SLC_LEN
    cmp_out = _nsa_compress(kvc, bsz, seq, pe, w1, b1, w2, b2)
    split = _nsa_cmp_split(ns)
    perm = cmp_out.reshape(bsz, 2, NSA_KV, split, ns // split, 4, HEAD_DIM).transpose(0, 1, 2, 3, 5, 4, 6).reshape(
        bsz, 2, NSA_KV, npc, HEAD_DIM)
    kc = perm[:, 0].astype(BF16)
    vc_t = perm[:, 1].transpose(0, 1, 3, 2).astype(BF16)
    gq = NSA_RPG * HEAD_DIM
    return pl.pallas_call(
        _nsa_kernel,
        grid=(bsz, NSA_KV, nqb),
        in_specs=[
            pl.BlockSpec((1, gq, LANES), lambda b, g, q: (b * nqb + q, g, 0)),
            pl.BlockSpec((1, 1, npc, HEAD_DIM), lambda b, g, q: (b, g, 0, 0)),
            pl.BlockSpec((1, 1, HEAD_DIM, npc), lambda b, g, q: (b, g, 0, 0)),
            pl.BlockSpec((seq, NSA_KVW), lambda b, g, q: (b, g)),
            pl.BlockSpec((seq, NSA_KVW), lambda b, g, q: (b, g)),
            pl.BlockSpec((nqb, HEAD_DIM, LANES), lambda b, g, q: (b, g, 0)),
            pl.BlockSpec((nqb, HEAD_DIM, LANES), lambda b, g, q: (b, NSA_KV + g, 0)),
            pl.BlockSpec((1, 16, LANES), lambda b, g, q: (b * nqb + q, g, 0)),
        ],
        out_specs=pl.BlockSpec((NSA_QBLK, gq), lambda b, g, q: (b * nqb + q, g)),
        out_shape=jax.ShapeDtypeStruct((bsz * seq, NSA_Q), BF16),
        scratch_shapes=[pltpu.VMEM((ns, NSA_QBLK), F32),
                        pltpu.VMEM((2 * HEAD_DIM, NSA_RPG * NSA_QBLK), BF16),
                        pltpu.VMEM((2 * HEAD_DIM, NSA_RPG * NSA_QBLK), BF16),
                        pltpu.VMEM((NSA_KTILE, NSA_RPG * NSA_QBLK), F32),
                        pltpu.VMEM((NSA_KTILE, NSA_RPG * NSA_QBLK), F32),
                        pltpu.VMEM((NSA_KTILE, NSA_RPG * NSA_QBLK), BF16),
                        pltpu.VMEM((NSA_KTILE, NSA_RPG * NSA_QBLK), BF16)],
        compiler_params=_cparams("arbitrary", "arbitrary", "arbitrary"),
        name="nsa_attention",
    )(q_tt, kc, vc_t, k_slc, k_win, v_tt, v_tt, g_tt)


def _swa_kernel(q_ref, kp_ref, kc_ref, vp_ref, vc_ref, sink_ref, o_ref):
    qb = pl.program_id(1)
    t = SWA_WIN
    kband = jnp.concatenate([kp_ref[...], kc_ref[...]], axis=0)
    krel = lax.broadcasted_iota(jnp.int32, (2 * t, 1), 0) - t
    qrel = lax.broadcasted_iota(jnp.int32, (1, t), 1)
    lowest = jnp.where(qb > 0, -t, 0)
    mbias = jnp.where((krel <= qrel) & (krel > qrel - SWA_WIN) & (krel >= lowest), 0.0, NEG)
    outs = []
    for g in range(2):
        rows = slice(g * SWA_RPG * HEAD_DIM, (g + 1) * SWA_RPG * HEAD_DIM)
        qg = q_ref[0, rows, :]
        qcat = jnp.concatenate([qg[r * HEAD_DIM:(r + 1) * HEAD_DIM, :] for r in range(SWA_RPG)], axis=1)
        zq = jnp.zeros_like(qcat)
        qext = jnp.concatenate([qcat, zq] if g == 0 else [zq, qcat], axis=0)
        s = _dot(kband, qext)
        ps, dens = [], []
        for r in range(SWA_RPG):
            h = g * SWA_RPG + r
            sink = sink_ref[h:h + 1, :]
            sr = s[:, r * t:(r + 1) * t] + mbias
            mx = jnp.maximum(jnp.max(sr, axis=0, keepdims=True), sink)
            e = jnp.exp2(sr - mx)
            dens.append(jnp.sum(e, axis=0, keepdims=True) + jnp.exp2(sink - mx))
            ps.append(e.astype(BF16))
        vband = jnp.concatenate([vp_ref[0, g * HEAD_DIM:(g + 1) * HEAD_DIM, :],
                                 vc_ref[0, g * HEAD_DIM:(g + 1) * HEAD_DIM, :]], axis=1)
        og = _dot(vband, jnp.concatenate(ps, axis=1)) * (1.0 / jnp.concatenate(dens, axis=1))
        outs.append(jnp.concatenate([og[:, r * t:(r + 1) * t] for r in range(SWA_RPG)], axis=0).T)
    o_ref[...] = jnp.concatenate(outs, axis=1).astype(o_ref.dtype)


def _swa_mixer(q_tt, k_nat, v_tt, sinks, bsz, seq):
    nqb = seq // SWA_WIN
    sink_rows = jnp.broadcast_to((sinks.astype(F32) * math.log2(math.e))[:, None], (sinks.shape[0], LANES))
    prev = lambda b, q: b * nqb + jnp.maximum(q - 1, 0)
    return pl.pallas_call(
        _swa_kernel,
        grid=(bsz, nqb),
        in_specs=[
            pl.BlockSpec((1, SWA_Q, LANES), lambda b, q: (b * nqb + q, 0, 0)),
            pl.BlockSpec((SWA_WIN, SWA_KVW), lambda b, q: (prev(b, q), 0)),
            pl.BlockSpec((SWA_WIN, SWA_KVW), lambda b, q: (b * nqb + q, 0)),
            pl.BlockSpec((1, SWA_KVW, LANES), lambda b, q: (prev(b, q), 0, 0)),
            pl.BlockSpec((1, SWA_KVW, LANES), lambda b, q: (b * nqb + q, 0, 0)),
            _full(sink_rows.shape),
        ],
        out_specs=pl.BlockSpec((SWA_WIN, SWA_Q), lambda b, q: (b * nqb + q, 0)),
        out_shape=jax.ShapeDtypeStruct((bsz * seq, SWA_Q), BF16),
        compiler_params=_cparams("parallel", "parallel"),
        name="swa_attention",
    )(q_tt, k_nat, k_nat, v_tt, v_tt, sink_rows)


def _s5_params(a_re, a_im, log_dt, b_re, b_im, c_re, c_im, n_chunks):
    f = F32
    t = S5_CHUNK
    step = jnp.exp(log_dt.astype(f))[:, None]
    lr, li = a_re.astype(f), a_im.astype(f)

    def lam_pow(tau):
        tau = tau.astype(f)[:, None, None]
        mag = jnp.exp(lr * step * tau)
        ang = li * step * tau
        return mag * jnp.cos(ang), mag * jnp.sin(ang)

    lb_r, lb_i = (v[0] for v in lam_pow(jnp.ones((1,))))
    nr, ni = lb_r - 1.0, lb_i
    den = lr * lr + li * li
    fr, fi = (nr * lr + ni * li) / den, (ni * lr - nr * li) / den
    br, bi = b_re.astype(f), b_im.astype(f)
    bb_r = fr[..., None] * br - fi[..., None] * bi
    bb_i = fr[..., None] * bi + fi[..., None] * br
    cr, ci = c_re.astype(f), c_im.astype(f)

    pr, pi = lam_pow(jnp.arange(t + 1))
    cl_r = cr[None] * pr[:, :, None, :] - ci[None] * pi[:, :, None, :]
    cl_i = cr[None] * pi[:, :, None, :] + ci[None] * pr[:, :, None, :]
    kern_t = jnp.einsum("tghp,gpk->gkth", cl_r[:t], bb_r, precision="highest") - jnp.einsum(
        "tghp,gpk->gkth", cl_i[:t], bb_i, precision="highest")
    rr, ri = pr[t - 1 - jnp.arange(t)], pi[t - 1 - jnp.arange(t)]
    bs_r = rr[..., None] * bb_r[None] - ri[..., None] * bb_i[None]
    bs_i = rr[..., None] * bb_i[None] + ri[..., None] * bb_r[None]
    bs = jnp.concatenate([bs_r, bs_i], axis=2)
    bs = bs.transpose(1, 0, 3, 2).reshape(S5_GROUPS, t * S5_GROUP_CH, 2 * S5_STATE)
    cs = jnp.concatenate([cl_r[1:], -cl_i[1:]], axis=3)
    cs = cs.transpose(1, 3, 0, 2).reshape(S5_GROUPS, 2 * S5_STATE, t * S5_GROUP_CH)
    ar, ai = pr[t], pi[t]
    a1, a2 = [], []
    k = 1
    while k < n_chunks:
        a1.append(jnp.concatenate([ar, ar], axis=1))
        a2.append(jnp.concatenate([-ai, ai], axis=1))
        ar, ai = ar * ar - ai * ai, 2.0 * ar * ai
        k *= 2
    a1 = jnp.stack(a1, axis=1)
    a2 = jnp.stack(a2, axis=1)
    noct = S5_GROUPS // S5_OCT
    hc, ns2 = S5_GROUP_CH, 2 * S5_STATE
    width = t * hc
    lanes = t * LANES
    r_i, c_i = np.arange(width)[:, None], np.arange(lanes)[None, :]
    rep = jnp.asarray((r_i // hc == c_i // LANES) & (r_i % hc == c_i % hc), dtype=BF16)
    kern_o = kern_t.reshape(noct, LANES, width).astype(BF16)
    toep_o = pl.pallas_call(
        _s5_toeplitz_kernel,
        grid=(noct, t),
        in_specs=[pl.BlockSpec((1, LANES, width), lambda o, s: (o, 0, 0)), _full((width, lanes))],
        out_specs=pl.BlockSpec((1, LANES, lanes), lambda o, s: (o, s, 0)),
        out_shape=jax.ShapeDtypeStruct((noct, lanes, lanes), BF16),
        scratch_shapes=[pltpu.VMEM((t, LANES, LANES), F32)],
        compiler_params=_cparams("parallel", "arbitrary"),
        name="s5_toeplitz_table",
    )(kern_o, rep)
    cs_o = pl.pallas_call(
        _s5_readout_kernel,
        grid=(noct,),
        in_specs=[pl.BlockSpec((1, S5_OCT * ns2, width), lambda o: (o, 0, 0)), _full((width, lanes))],
        out_specs=pl.BlockSpec((1, S5_OCT * ns2, lanes), lambda o: (o, 0, 0)),
        out_shape=jax.ShapeDtypeStruct((noct, S5_OCT * ns2, lanes), BF16),
        compiler_params=_cparams("parallel"),
        name="s5_readout_table",
    )(cs.reshape(noct, S5_OCT * ns2, width).astype(BF16), rep)
    bs_slabs = bs.reshape(noct, S5_OCT, t, hc, ns2).transpose(0, 2, 1, 3, 4).reshape(noct, t, LANES, ns2).astype(BF16)
    bs_o = pl.pallas_call(
        _s5_state_in_kernel,
        grid=(noct, t),
        in_specs=[pl.BlockSpec((1, 1, LANES, ns2), lambda o, s: (o, s, 0, 0))],
        out_specs=pl.BlockSpec((1, LANES, S5_OCT * ns2), lambda o, s: (o, s, 0)),
        out_shape=jax.ShapeDtypeStruct((noct, lanes, S5_OCT * ns2), BF16),
        compiler_params=_cparams("parallel", "parallel"),
        name="s5_state_in_table",
    )(bs_slabs)

    def oct_rows(a):
        return a.reshape(noct, S5_OCT, -1, ns2).transpose(0, 2, 1, 3).reshape(noct, -1, S5_OCT * ns2)

    return toep_o, bs_o, cs_o, oct_rows(a1), oct_rows(a2)


def _same_group(shape, row_div, col_mod, col_div):
    r = lax.broadcasted_iota(jnp.int32, shape, 0)
    c = lax.broadcasted_iota(jnp.int32, shape, 1)
    return (r // row_div) == ((c % col_mod) // col_div)


def _s5_toeplitz_kernel(k_ref, rep_ref, o_ref, full_s):
    s = pl.program_id(1)

    nlag = full_s.shape[0]

    @pl.when(s == 0)
    def _():
        full = _dot(k_ref[0], rep_ref[...])
        full = jnp.where(_same_group(full.shape, S5_GROUP_CH, LANES, S5_GROUP_CH), full, 0.0)
        for lag in range(nlag):
            full_s[lag] = full[:, lag * LANES:(lag + 1) * LANES]

    for t in range(nlag):
        tile = full_s[jnp.maximum(t - s, 0)]
        o_ref[0, :, t * LANES:(t + 1) * LANES] = jnp.where(t >= s, tile, 0.0).astype(o_ref.dtype)


def _s5_readout_kernel(c_ref, rep_ref, o_ref):
    full = _dot(c_ref[0], rep_ref[...])
    keep = _same_group(full.shape, 2 * S5_STATE, LANES, S5_GROUP_CH)
    o_ref[0] = jnp.where(keep, full, 0.0).astype(o_ref.dtype)


def _s5_state_in_kernel(b_ref, o_ref):
    slab = b_ref[0, 0]
    full = jnp.concatenate([slab] * S5_OCT, axis=1)
    keep = _same_group(full.shape, S5_GROUP_CH, full.shape[1], 2 * S5_STATE)
    o_ref[0] = jnp.where(keep, full, jnp.zeros((), full.dtype))


def _s5_state_kernel(bsz, u_ref, bs_ref, a1_ref, a2_ref, hi_ref, lo_ref):
    sc = _dot(u_ref[...], bs_ref[0])
    n = sc.shape[0] // bsz
    width = sc.shape[1]
    rowi = lax.broadcasted_iota(jnp.int32, (n, width), 0)

    def swap_re_im(x):
        return jnp.concatenate([pltpu.roll(x[:, j * LANES:(j + 1) * LANES], S5_STATE, 1)
                                for j in range(width // LANES)], axis=1)

    h_in = []
    for b in range(bsz):
        x = sc[b * n:(b + 1) * n]
        k, step = 1, 0
        while k < n:
            xs = jnp.where(rowi >= k, pltpu.roll(x, k, 0), 0.0)
            x = x + a1_ref[0, step:step + 1, :] * xs + a2_ref[0, step:step + 1, :] * swap_re_im(xs)
            k *= 2
            step += 1
        h_in.append(jnp.where(rowi >= 1, pltpu.roll(x, 1, 0), 0.0))
    h_in = jnp.concatenate(h_in, axis=0)
    hi = h_in.astype(BF16)
    hi_ref[0] = hi
    lo_ref[0] = (h_in - hi.astype(F32)).astype(BF16)


def _s5_out_kernel(u_ref, toep_ref, hi_ref, lo_ref, cs_ref, o_ref):
    n = pl.program_id(1)
    cols = toep_ref.shape[2]
    off = _dot(hi_ref[0], cs_ref[0]) + _dot(lo_ref[0], cs_ref[0])
    for nn in range(toep_ref.shape[1] // cols):
        @pl.when(n == nn)
        def _(nn=nn):
            k = (nn + 1) * cols
            y = off + _dot(u_ref[:, 0:k], toep_ref[0, 0:k, :])
            for t8 in range(cols // LANES):
                o_ref[:, t8, :] = y[:, t8 * LANES:(t8 + 1) * LANES].astype(o_ref.dtype)


def _s5_glu_kernel(y_ref, u_ref, d_ref, w_ref, b_ref, o_ref):
    y = _gelu_tanh(y_ref[...] + d_ref[...] * u_ref[...])
    gate = _sigmoid(_dot(y.astype(BF16), w_ref[...]) + b_ref[...])
    o_ref[...] = (y * gate).astype(o_ref.dtype)


def _s5_mixer(u5, u_chunks, bsz, seq, a_re, a_im, log_dt, b_re, b_im, c_re, c_im, d_skip, glu_w, glu_b, tm=1024):
    m = u5.shape[0]
    t = S5_CHUNK
    nch = m // t
    noct = S5_GROUPS // S5_OCT
    lanes = t * LANES
    sw = S5_OCT * 2 * S5_STATE
    toep, bs, cs, a1, a2 = _s5_params(a_re, a_im, log_dt, b_re, b_im, c_re, c_im, nch // bsz)
    nsteps = a1.shape[1]
    h_hi, h_lo = pl.pallas_call(
        functools.partial(_s5_state_kernel, bsz),
        grid=(noct,),
        in_specs=[
            pl.BlockSpec((nch, lanes), lambda o: (0, o)),
            pl.BlockSpec((1, lanes, sw), lambda o: (o, 0, 0)),
            pl.BlockSpec((1, nsteps, sw), lambda o: (o, 0, 0)),
            pl.BlockSpec((1, nsteps, sw), lambda o: (o, 0, 0)),
        ],
        out_specs=[pl.BlockSpec((1, nch, sw), lambda o: (o, 0, 0))] * 2,
        out_shape=[jax.ShapeDtypeStruct((noct, nch, sw), BF16)] * 2,
        compiler_params=_cparams("parallel"),
        name="s5_state",
    )(u_chunks, bs, a1, a2)
    tsub = 8
    rows = nch // 2
    y = pl.pallas_call(
        _s5_out_kernel,
        grid=(noct, t // tsub, nch // rows),
        in_specs=[
            pl.BlockSpec((rows, lanes), lambda o, n, r: (r, o)),
            pl.BlockSpec((1, lanes, tsub * LANES), lambda o, n, r: (o, 0, n)),
            pl.BlockSpec((1, rows, sw), lambda o, n, r: (o, r, 0)),
            pl.BlockSpec((1, rows, sw), lambda o, n, r: (o, r, 0)),
            pl.BlockSpec((1, sw, tsub * LANES), lambda o, n, r: (o, 0, n)),
        ],
        out_specs=pl.BlockSpec((rows, tsub, LANES), lambda o, n, r: (r, n, o)),
        out_shape=jax.ShapeDtypeStruct((nch, t, S5_CH), BF16),
        compiler_params=_cparams("parallel", "parallel", "parallel"),
        name="s5_scan",
    )(u_chunks, toep, h_hi, h_lo, cs).reshape(m, S5_CH)
    return pl.pallas_call(
        _s5_glu_kernel,
        grid=(m // tm,),
        in_specs=[
            pl.BlockSpec((tm, S5_CH), lambda i: (i, 0)),
            pl.BlockSpec((tm, S5_CH), lambda i: (i, 0)),
            _full((1, S5_CH)), _full((S5_CH, S5_CH)), _full((1, S5_CH)),
        ],
        out_specs=pl.BlockSpec((tm, S5_CH), lambda i: (i, 0)),
        out_shape=jax.ShapeDtypeStruct((m, S5_CH), BF16),
        compiler_params=_cparams("parallel"),
        name="s5_glu",
    )(y, u5, d_skip.reshape(1, S5_CH).astype(F32), glu_w.astype(BF16), glu_b.reshape(1, S5_CH).astype(F32))


def _even_mixers(h, bsz, seq, g_mix, w_in, conv_w, conv_b, dt_bias, a_log, d_skip, norm_g, pe, w1, b1, w2, b2):
    d = h.shape[1]
    scale = HEAD_DIM ** -0.5 * math.log2(math.e)
    o = SSD_IN
    w_ssd = jnp.concatenate([w_in[:, :SSD_IN], jnp.zeros((d, LANES - SSD_HEADS), w_in.dtype)], axis=1)
    w_q = w_in[:, o:o + NSA_Q] * scale
    kv = [w_in[:, o + NSA_Q + i * NSA_KVW:o + NSA_Q + (i + 1) * NSA_KVW] for i in range(6)]
    w_gate = w_in[:, o + NSA_Q + 6 * NSA_KVW:].reshape(d, NSA_KV, NSA_RPG, 3).transpose(0, 1, 3, 2)
    w_gate = jnp.concatenate([w_gate.reshape(d, NSA_KV, 12), jnp.zeros((d, NSA_KV, 4), w_in.dtype)],
                             axis=2).reshape(d, NSA_KV * 16)

    def per_group_halves(w):
        wg = w.reshape(d, NSA_KV, HEAD_DIM)
        return jnp.concatenate([wg, jnp.zeros_like(wg)], axis=2).reshape(d, NSA_KV * NSA_KVW)
    segs = [
        ("nat", w_ssd, F32),
        ("tt", w_q, BF16),
        ("pc", jnp.concatenate([kv[0], kv[1]], axis=1), F32),
        ("nat+", per_group_halves(kv[2]), BF16, _nsa_block_onehot(NSA_KTILE)),
        ("nat", per_group_halves(kv[4]), BF16),
        ("tt", jnp.concatenate([kv[3], kv[5]], axis=1), BF16),
        ("tt", w_gate, F32),
    ]
    u_ssd, q_tt, kvc, k_slc, k_win, v_tt, g_tt = _norm_proj(h, g_mix, segs)
    ya = _ssd_mixer(u_ssd, bsz, seq, conv_w, conv_b, dt_bias, a_log, d_skip, norm_g)
    yb = _nsa_mixer(q_tt, kvc, k_slc, k_win, v_tt, g_tt, bsz, seq, pe, w1, b1, w2, b2)
    return ya, yb


def _odd_mixers(h, bsz, seq, g_mix, w_in, sinks, a_re, a_im, log_dt, b_re, b_im, c_re, c_im, d_skip, glu_w, glu_b):
    scale = HEAD_DIM ** -0.5 * math.log2(math.e)
    segs = [
        ("tt", w_in[:, :SWA_Q] * scale, BF16),
        ("nat", w_in[:, SWA_Q:SWA_Q + SWA_KVW], BF16),
        ("tt", w_in[:, SWA_Q + SWA_KVW:SWA_Q + 2 * SWA_KVW], BF16),
        ("nat+ch", w_in[:, SWA_Q + 2 * SWA_KVW:], F32),
    ]
    q_tt, k_nat, v_tt, u5, u_chunks = _norm_proj(h, g_mix, segs)
    yc = _swa_mixer(q_tt, k_nat, v_tt, sinks, bsz, seq)
    yd = _s5_mixer(u5, u_chunks, bsz, seq, a_re, a_im, log_dt, b_re, b_im, c_re, c_im, d_skip, glu_w, glu_b)
    return yc, yd


def kernel(x, norm_mix, norm_mlp, norm_final, mlp_w_up, mlp_w_down, ev_w_in, ev_w_out, ssd_conv_w, ssd_conv_b,
           ssd_dt_bias, ssd_a_log, ssd_d, ssd_norm, nsa_pe, nsa_cmp_w1, nsa_cmp_b1, nsa_cmp_w2, nsa_cmp_b2,
           od_w_in, od_w_out, swa_sinks, s5_a_re, s5_a_im, s5_log_dt, s5_b_re, s5_b_im, s5_c_re, s5_c_im,
           s5_d, s5_glu_w, s5_glu_b):
    bsz, seq, d = x.shape
    depth = norm_mix.shape[0]
    assert seq % (NSA_UNROLL * NSA_KTILE) == 0 and seq >= NSA_WIN + NSA_QBLK
    h = x.reshape(bsz * seq, d)
    for layer in range(depth):
        i = layer // 2
        if layer % 2 == 0:
            ya, yb = _even_mixers(h, bsz, seq, norm_mix[layer], ev_w_in[i], ssd_conv_w[i], ssd_conv_b[i],
                                  ssd_dt_bias[i], ssd_a_log[i], ssd_d[i], ssd_norm[i], nsa_pe[i],
                                  nsa_cmp_w1[i], nsa_cmp_b1[i], nsa_cmp_w2[i], nsa_cmp_b2[i])
            w_out = ev_w_out[i]
        else:
            ya, yb = _odd_mixers(h, bsz, seq, norm_mix[layer], od_w_in[i], swa_sinks[i], s5_a_re[i], s5_a_im[i],
                                 s5_log_dt[i], s5_b_re[i], s5_b_im[i], s5_c_re[i], s5_c_im[i], s5_d[i],
                                 s5_glu_w[i], s5_glu_b[i])
            w_out = od_w_out[i]
        h = _mix_out_mlp(h, ya, yb, w_out, norm_mlp[layer], mlp_w_up[layer], mlp_w_down[layer], norm_final,
                         final=(layer == depth - 1))
    return h.reshape(bsz, seq, d)
```

```python
import functools
import math

import jax
import jax.numpy as jnp
import numpy as np
from jax import lax
from jax.experimental import pallas as pl
from jax.experimental.pallas import tpu as pltpu

F32 = jnp.float32
BF16 = jnp.bfloat16

EPS = 1e-6
NEG = -1e30
FORCE = 1e9
HEAD_DIM = 64
LANES = 128
VMEM_LIMIT_BYTES = 56 * 1024 * 1024

SSD_HEADS = 8
SSD_INNER = 512
SSD_GROUPS = 2
SSD_STATE = 128
SSD_CONV = 4
SSD_CHUNK = 128
SSD_CONV_DIM = 1024
SSD_IN = SSD_INNER + SSD_CONV_DIM + SSD_HEADS

NSA_HEADS = 8
NSA_KV = 2
NSA_RPG = 4
NSA_CMP_LEN = 32
NSA_CMP_STRIDE = 16
NSA_SLC_LEN = 64
NSA_TOPK = 16
NSA_WIN = 512
NSA_CMP_HIDDEN = 256
NSA_QBLK = 128
NSA_Q = 512
NSA_KVW = 128
NSA_KTILE = 512
NSA_UNROLL = 2

SWA_RPG = 4
SWA_WIN = 128
SWA_Q = 512
SWA_KVW = 128

S5_CH = 512
S5_GROUP_CH = 16
S5_GROUPS = 32
S5_STATE = 64
S5_CHUNK = 32
S5_OCT = LANES // S5_GROUP_CH


def _cparams(*sem):
    return pltpu.CompilerParams(dimension_semantics=sem, vmem_limit_bytes=VMEM_LIMIT_BYTES)


def _full(shape):
    n = len(shape)
    return pl.BlockSpec(shape, lambda *_: (0,) * n)


def _dot(a, b):
    return jnp.dot(a, b, preferred_element_type=F32)


def _dot_nt(a, b):
    return lax.dot_general(a, b, (((1,), (1,)), ((), ())), preferred_element_type=F32)


def _split3(a):
    hi = a.astype(BF16)
    r1 = a - hi.astype(F32)
    mid = r1.astype(BF16)
    lo = (r1 - mid.astype(F32)).astype(BF16)
    return hi, mid, lo


def _dot_exact_rhs(a, b_exact):
    hi, mid, lo = _split3(a)
    return _dot(hi, b_exact) + _dot(mid, b_exact) + _dot(lo, b_exact)


def _dot_exact_lhs(a_exact, b):
    hi, mid, lo = _split3(b)
    return _dot(a_exact, hi) + _dot(a_exact, mid) + _dot(a_exact, lo)


def _rms(x, g):
    return x * lax.rsqrt(jnp.mean(x * x, axis=-1, keepdims=True) + EPS) * g


def _gelu_tanh(x):
    c = math.sqrt(2.0 / math.pi)
    return 0.5 * x * (1.0 + jnp.tanh(c * (x + 0.044715 * (x * x * x))))


def _sigmoid(x):
    return 1.0 / (1.0 + jnp.exp(-x))


def _proj_kernel(kinds, tm, h_ref, g_ref, *refs):
    n = len(kinds)
    n_add = sum(k == "nat+" for k in kinds)
    n_out = n + sum(k == "nat+ch" for k in kinds)
    w_refs, add_refs = refs[:n], list(refs[n:n + n_add])
    o_refs, scratch = list(refs[n + n_add:n + n_add + n_out]), list(refs[n + n_add + n_out:])
    yb = _rms(h_ref[...], g_ref[...]).astype(BF16)
    for kind, w_ref in zip(kinds, w_refs):
        o_ref = o_refs.pop(0)
        if kind == "nat":
            o_ref[...] = _dot(yb, w_ref[...]).astype(o_ref.dtype)
        elif kind == "nat+ch":
            ch_ref, ch_s = o_refs.pop(0), scratch.pop(0)
            res = _dot(yb, w_ref[...])
            o_ref[...] = res.astype(o_ref.dtype)
            for j in range(ch_s.shape[0]):
                ch_s[j] = res[:, j * LANES:(j + 1) * LANES]
            for t in range(S5_CHUNK):
                for j in range(ch_s.shape[0]):
                    col = (j * S5_CHUNK + t) * LANES
                    ch_ref[:, col:col + LANES] = ch_s[j, pl.ds(t, tm // S5_CHUNK, stride=S5_CHUNK), :].astype(ch_ref.dtype)
        elif kind == "nat+":
            o_ref[...] = _dot(yb, w_ref[...]).astype(o_ref.dtype) + add_refs.pop(0)[...]
        elif kind == "pc":
            pc_s = scratch.pop(0)
            res = _dot(yb, w_ref[...])
            flat = NSA_CMP_STRIDE * HEAD_DIM
            per_tile = LANES // HEAD_DIM
            for j in range(pc_s.shape[0]):
                pc_s[j] = res[:, j * LANES:(j + 1) * LANES]
            for p in range(NSA_CMP_STRIDE):
                for j in range(pc_s.shape[0]):
                    tok = pc_s[j, pl.ds(p, tm // NSA_CMP_STRIDE, stride=NSA_CMP_STRIDE), :]
                    for c in range(per_tile):
                        col = (j * per_tile + c) * flat + p * HEAD_DIM
                        o_ref[:, col:col + HEAD_DIM] = tok[:, c * HEAD_DIM:(c + 1) * HEAD_DIM].astype(o_ref.dtype)
        else:
            ot = _dot_nt(w_ref[...], yb)
            for j in range(tm // LANES):
                o_ref[j] = ot[:, j * LANES:(j + 1) * LANES].astype(o_ref.dtype)


def _norm_proj(h, g, segs, tm=512):
    m, d = h.shape
    kinds = tuple(s[0] for s in segs)
    adds = [s[3] for s in segs if s[0] == "nat+"]
    ws, w_specs, out_shapes, out_specs, scratch = [], [], [], [], []
    for kind, w, dt in (s[:3] for s in segs):
        n_out = w.shape[1]
        if kind == "pc":
            ws.append(w.astype(BF16))
            w_specs.append(_full((d, n_out)))
            out_shapes.append(jax.ShapeDtypeStruct((m // NSA_CMP_STRIDE, NSA_CMP_STRIDE * n_out), dt))
            out_specs.append(pl.BlockSpec((tm // NSA_CMP_STRIDE, NSA_CMP_STRIDE * n_out), lambda i: (i, 0)))
            scratch.append(pltpu.VMEM((n_out // LANES, tm, LANES), F32))
        elif kind in ("nat", "nat+", "nat+ch"):
            ws.append(w.astype(BF16))
            w_specs.append(_full((d, n_out)))
            out_shapes.append(jax.ShapeDtypeStruct((m, n_out), dt))
            out_specs.append(pl.BlockSpec((tm, n_out), lambda i: (i, 0)))
            if kind == "nat+ch":
                out_shapes.append(jax.ShapeDtypeStruct((m // S5_CHUNK, S5_CHUNK * n_out), BF16))
                out_specs.append(pl.BlockSpec((tm // S5_CHUNK, S5_CHUNK * n_out), lambda i: (i, 0)))
                scratch.append(pltpu.VMEM((n_out // LANES, tm, LANES), F32))
        else:
            ws.append(w.T.astype(BF16))
            w_specs.append(_full((n_out, d)))
            out_shapes.append(jax.ShapeDtypeStruct((m // LANES, n_out, LANES), dt))
            out_specs.append(pl.BlockSpec((tm // LANES, n_out, LANES), lambda i: (i, 0, 0)))
    return pl.pallas_call(
        functools.partial(_proj_kernel, kinds, tm),
        grid=(m // tm,),
        in_specs=[pl.BlockSpec((tm, d), lambda i: (i, 0)), _full((1, d))] + w_specs + [
            pl.BlockSpec((tm, a.shape[1]), functools.partial(lambda i, nb: (i % nb, 0), nb=a.shape[0] // tm))
            for a in adds],
        out_specs=out_specs,
        out_shape=out_shapes,
        scratch_shapes=scratch,
        compiler_params=_cparams("parallel"),
        name="norm_proj",
    )(h, g.reshape(1, d), *ws, *adds)


def _mlp_kernel(final, h_ref, ya_ref, yb_ref, woa_ref, wob_ref, gm_ref, wup_ref, wdn_ref, gf_ref,
                o_ref, h2_s, xn_s, acc_s):
    j = pl.program_id(1)

    @pl.when(j == 0)
    def _():
        h2 = h_ref[...] + _dot(ya_ref[...], woa_ref[...]) + _dot(yb_ref[...], wob_ref[...])
        h2_s[...] = h2
        xn_s[...] = _rms(h2, gm_ref[...]).astype(BF16)
        acc_s[...] = jnp.zeros_like(acc_s)

    hid = jnp.square(jnp.maximum(_dot(xn_s[...], wup_ref[...]), 0.0))
    acc_s[...] += _dot(hid.astype(BF16), wdn_ref[...])

    @pl.when(j == pl.num_programs(1) - 1)
    def _():
        out = h2_s[...] + acc_s[...]
        if final:
            out = _rms(out, gf_ref[...])
        o_ref[...] = out


def _mix_out_mlp(h, ya, yb, w_out, g_mlp, w_up, w_down, g_final, final, tm=1024, tf=1024):
    m, d = h.shape
    dff = w_up.shape[1]
    na = ya.shape[1]
    nb = yb.shape[1]
    return pl.pallas_call(
        functools.partial(_mlp_kernel, final),
        grid=(m // tm, dff // tf),
        in_specs=[
            pl.BlockSpec((tm, d), lambda i, j: (i, 0)),
            pl.BlockSpec((tm, na), lambda i, j: (i, 0)),
            pl.BlockSpec((tm, nb), lambda i, j: (i, 0)),
            _full((na, d)), _full((nb, d)), _full((1, d)),
            pl.BlockSpec((d, tf), lambda i, j: (0, j)),
            pl.BlockSpec((tf, d), lambda i, j: (j, 0)),
            _full((1, d)),
        ],
        out_specs=pl.BlockSpec((tm, d), lambda i, j: (i, 0)),
        out_shape=jax.ShapeDtypeStruct((m, d), F32),
        scratch_shapes=[pltpu.VMEM((tm, d), F32), pltpu.VMEM((tm, d), BF16), pltpu.VMEM((tm, d), F32)],
        compiler_params=_cparams("parallel", "arbitrary"),
        name="out_proj_mlp",
    )(h, ya, yb, w_out[:na].astype(BF16), w_out[na:].astype(BF16), g_mlp.reshape(1, d),
      w_up.astype(BF16), w_down.astype(BF16), g_final.reshape(1, d))


def _ssd_kernel(u_ref, cw_ref, cb_ref, dtb_ref, alog_ref, dsk_ref, ng_ref, o_ref, xext_s, st_s):
    t = SSD_CHUNK
    c = pl.program_id(1)

    @pl.when(c == 0)
    def _():
        xext_s[0:8, :] = jnp.zeros((8, SSD_CONV_DIM), F32)
        st_s[...] = jnp.zeros_like(st_s)

    z = u_ref[:, 0:SSD_INNER]
    dt_raw = u_ref[:, SSD_INNER + SSD_CONV_DIM:]
    xext_s[8:8 + t, :] = u_ref[:, SSD_INNER:SSD_INNER + SSD_CONV_DIM]
    xfull = xext_s[...]
    conv = cb_ref[...] + cw_ref[SSD_CONV - 1:SSD_CONV, :] * xfull[8:8 + t]
    for back in range(1, SSD_CONV):
        k = SSD_CONV - 1 - back
        conv = conv + cw_ref[k:k + 1, :] * pltpu.roll(xfull, back, 0)[8:8 + t]
    xext_s[0:8, :] = xext_s[t:t + 8, :]
    xc = conv * _sigmoid(conv)
    xs = xc[:, 0:SSD_INNER]
    gn = SSD_GROUPS * SSD_STATE

    dtp = dt_raw + dtb_ref[...]
    dt = jnp.maximum(dtp, 0.0) + jnp.log1p(jnp.exp(-jnp.abs(dtp)))
    a = -jnp.exp(alog_ref[...])
    da = dt * a

    row = lax.broadcasted_iota(jnp.int32, (t, t), 0)
    col = lax.broadcasted_iota(jnp.int32, (t, t), 1)
    causal = col <= row
    tril = jnp.where(causal, 1.0, 0.0).astype(BF16)
    a_cum = _dot_exact_lhs(tril, da)
    er = lax.broadcasted_iota(jnp.int32, (LANES, SSD_INNER), 0)
    ec = lax.broadcasted_iota(jnp.int32, (LANES, SSD_INNER), 1)
    expand = jnp.where((ec >> 6) == er, 1.0, 0.0).astype(BF16)
    a_cum_x = _dot_exact_rhs(a_cum, expand)
    dt_x = _dot_exact_rhs(dt, expand)
    a_cum_t = a_cum.T
    a_last_x = a_cum_x[t - 1:t, :]
    decay_end_x = jnp.exp(a_last_x - a_cum_x)
    decay_in_x = jnp.exp(a_cum_x)
    chunk_decay_x = jnp.exp(a_last_x)

    xd = xs * dt_x
    xd_end = (xd * decay_end_x).astype(BF16)
    xd_b = xd.astype(BF16)
    lane = lax.broadcasted_iota(jnp.int32, (t, LANES), 1)
    first_half = lane < HEAD_DIM

    pieces = []
    for g in range(SSD_GROUPS):
        bm = xc[:, SSD_INNER + g * SSD_STATE:SSD_INNER + (g + 1) * SSD_STATE]
        cm = xc[:, SSD_INNER + gn + g * SSD_STATE:SSD_INNER + gn + (g + 1) * SSD_STATE].astype(BF16)
        bm_t = bm.T.astype(BF16)
        cb = _dot_nt(cm, bm.astype(BF16))
        for pr in range(2):
            i = g * 2 + pr
            sl = slice(i * LANES, (i + 1) * LANES)
            ms = []
            for hh in range(2):
                h = 2 * i + hh
                seg = a_cum[:, h:h + 1] - a_cum_t[h:h + 1, :]
                dec = jnp.exp(jnp.where(causal, seg, NEG))
                ms.append((cb * dec).astype(BF16))
            y_diag = jnp.where(first_half, _dot(ms[0], xd_b[:, sl]), _dot(ms[1], xd_b[:, sl]))
            st = st_s[i]
            y_off = _dot(cm, st.astype(BF16)) * decay_in_x[:, sl]
            st_s[i] = st * chunk_decay_x[:, sl] + _dot(bm_t, xd_end[:, sl])
            pieces.append(y_diag + y_off)
    y = jnp.concatenate(pieces, axis=1) + xs * dsk_ref[...]
    y = y * (z * _sigmoid(z))
    half = SSD_INNER // SSD_GROUPS
    outs = [_rms(y[:, g * half:(g + 1) * half], ng_ref[:, g * half:(g + 1) * half]) for g in range(SSD_GROUPS)]
    o_ref[...] = jnp.concatenate(outs, axis=1).astype(o_ref.dtype)


def _ssd_mixer(u_ssd, bsz, seq, conv_w, conv_b, dt_bias, a_log, d_skip, norm_g):
    m, width = u_ssd.shape
    nch = seq // SSD_CHUNK
    pad = LANES - SSD_HEADS

    def padded(v):
        return jnp.concatenate([v.astype(F32), jnp.zeros((pad,), F32)]).reshape(1, LANES)

    return pl.pallas_call(
        _ssd_kernel,
        grid=(bsz, nch),
        in_specs=[
            pl.BlockSpec((SSD_CHUNK, width), lambda b, c: (b * nch + c, 0)),
            _full((SSD_CONV, SSD_CONV_DIM)), _full((1, SSD_CONV_DIM)),
            _full((1, LANES)), _full((1, LANES)), _full((1, SSD_INNER)), _full((1, SSD_INNER)),
        ],
        out_specs=pl.BlockSpec((SSD_CHUNK, SSD_INNER), lambda b, c: (b * nch + c, 0)),
        out_shape=jax.ShapeDtypeStruct((m, SSD_INNER), BF16),
        scratch_shapes=[pltpu.VMEM((SSD_CHUNK + 8, SSD_CONV_DIM), F32),
                        pltpu.VMEM((SSD_HEADS // 2, SSD_STATE, LANES), F32)],
        compiler_params=_cparams("arbitrary", "arbitrary"),
        name="ssd_mixer",
    )(u_ssd, conv_w.astype(F32), conv_b.reshape(1, -1).astype(F32), padded(dt_bias), padded(a_log),
      jnp.repeat(d_skip.astype(F32), HEAD_DIM).reshape(1, SSD_INNER), norm_g.reshape(1, SSD_INNER).astype(F32))


def _nsa_compress_kernel(x_ref, pe_ref, w1_ref, b1_ref, w2_ref, b2_ref, o_ref):
    x = x_ref[...]
    npc = x.shape[0]
    half = NSA_CMP_STRIDE * HEAD_DIM
    top = _dot((x + pe_ref[0, :, 0:half]).astype(BF16), w1_ref[0, 0:half, :])
    bot = _dot((x + pe_ref[0, :, half:]).astype(BF16), w1_ref[0, half:, :])
    pre = top + pltpu.roll(bot, npc - 1, 0) + b1_ref[0]
    hid = _gelu_tanh(pre).astype(BF16)
    out = _dot(hid, w2_ref[0]) + b2_ref[0]
    rowi = lax.broadcasted_iota(jnp.int32, out.shape, 0)
    o_ref[0, 0, 0] = jnp.where(rowi < npc - 1, out, 0.0)


def _nsa_compress(kvc_pieces, bsz, seq, pe, w1, b1, w2, b2):
    npc = seq // NSA_CMP_STRIDE
    flat = NSA_CMP_STRIDE * HEAD_DIM
    x = kvc_pieces
    return pl.pallas_call(
        _nsa_compress_kernel,
        grid=(bsz, 2, NSA_KV),
        in_specs=[
            pl.BlockSpec((npc, flat), lambda b, s, g: (b, s * NSA_KV + g)),
            pl.BlockSpec((1, 1, 2 * flat), lambda b, s, g: (s, 0, 0)),
            pl.BlockSpec((1, 2 * flat, NSA_CMP_HIDDEN), lambda b, s, g: (s, 0, 0)),
            pl.BlockSpec((1, 1, NSA_CMP_HIDDEN), lambda b, s, g: (s, 0, 0)),
            pl.BlockSpec((1, NSA_CMP_HIDDEN, HEAD_DIM), lambda b, s, g: (s, 0, 0)),
            pl.BlockSpec((1, 1, HEAD_DIM), lambda b, s, g: (s, 0, 0)),
        ],
        out_specs=pl.BlockSpec((1, 1, 1, npc, HEAD_DIM), lambda b, s, g: (b, s, g, 0, 0)),
        out_shape=jax.ShapeDtypeStruct((bsz, 2, NSA_KV, npc, HEAD_DIM), F32),
        compiler_params=_cparams("parallel", "parallel", "parallel"),
        name="nsa_compress",
    )(x, pe.reshape(2, 1, 2 * flat).astype(F32), w1.astype(BF16), b1.reshape(2, 1, -1).astype(F32),
      w2.astype(BF16), b2.reshape(2, 1, -1).astype(F32))


def _nsa_cmp_split(ns):
    return max(1, min(4, (4 * ns) // LANES))


def _nsa_block_onehot(rows):
    r = lax.broadcasted_iota(jnp.int32, (rows, 2 * NSA_KVW), 0)
    c = lax.broadcasted_iota(jnp.int32, (rows, 2 * NSA_KVW), 1)
    blk = (r % NSA_KTILE) // NSA_SLC_LEN
    return jnp.where((c % NSA_KVW) == HEAD_DIM + blk, 1.0, 0.0).astype(BF16)


def _nsa_kernel(q_ref, kc_ref, vc_ref, ks_ref, kw_ref, vs_ref, vw_ref, g_ref, o_ref,
                bias_s, qaug_s, qaug2_s, sa_s, sb_s, pa_s, pb_s):
    qb = pl.program_id(2)
    nqt = NSA_RPG * NSA_QBLK
    s0 = qb * NSA_QBLK
    ncp = kc_ref.shape[2]
    ns = ncp // 4
    heads = [slice(r * NSA_QBLK, (r + 1) * NSA_QBLK) for r in range(NSA_RPG)]

    qcat = jnp.concatenate([q_ref[0, r * HEAD_DIM:(r + 1) * HEAD_DIM, :] for r in range(NSA_RPG)], axis=1)
    qpos = s0 + lax.broadcasted_iota(jnp.int32, (1, NSA_QBLK), 1)
    qaug_s[0:HEAD_DIM, :] = qcat
    qaug_s[HEAD_DIM:, :] = jnp.zeros((HEAD_DIM, nqt), BF16)

    split = _nsa_cmp_split(ns)
    chunk = ns // split
    cvalid = jnp.where(qpos >= NSA_CMP_LEN - 1, 1.0, 0.0)
    cur = qpos // NSA_SLC_LEN
    taken = -3.0e38

    def cmp_and_select(nchunks):
        rows, jmax = nchunks * 4 * chunk, nchunks * chunk
        kc = kc_ref[0, 0, 0:rows, :]
        rc = lax.broadcasted_iota(jnp.int32, (rows, 1), 0)
        ncmp = 4 * ((rc // (4 * chunk)) * chunk + rc % chunk) + (rc % (4 * chunk)) // chunk
        cbias = jnp.where((ncmp * NSA_CMP_STRIDE + (NSA_CMP_LEN - 1)) <= qpos, 0.0, NEG)
        psum = jnp.zeros((rows, NSA_QBLK), F32)
        p_all = []
        for sl in heads:
            s = _dot(kc, qcat[:, sl]) + cbias
            e = jnp.exp2(s - jnp.max(s, axis=0, keepdims=True))
            p = e * (cvalid / jnp.sum(e, axis=0, keepdims=True))
            psum = psum + p
            p_all.append(p.astype(BF16))
        o_cmp = _dot(vc_ref[0, 0, :, 0:rows], jnp.concatenate(p_all, axis=1))

        tot, p3 = [], []
        for c in range(nchunks):
            part = [psum[(4 * c + i) * chunk:(4 * c + i + 1) * chunk] for i in range(4)]
            tot.append(part[0] + part[1] + part[2] + part[3])
            p3.append(part[3])
        tot, p3 = jnp.concatenate(tot, axis=0), jnp.concatenate(p3, axis=0)
        rj = lax.broadcasted_iota(jnp.int32, (jmax, NSA_QBLK), 0)
        imp = tot + jnp.where(rj >= 1, pltpu.roll(p3, 1, 0), 0.0)
        forced = (rj == 0) | (rj == cur) | (rj == cur - 1)
        rjf = rj.astype(F32)
        imp = jnp.where(forced, taken, jnp.where(rj <= cur, imp, -FORCE))
        for _ in range(min(NSA_TOPK, ns) - 3):
            mx = jnp.max(imp, axis=0, keepdims=True)
            first = jnp.min(jnp.where(imp == mx, rjf, float(ns)), axis=0, keepdims=True)
            imp = jnp.where(rjf == first, taken, imp)
        bias = jnp.where(imp == taken, 0.0, NEG)
        if jmax < ns:
            bias = jnp.concatenate([bias, jnp.full((ns - jmax, NSA_QBLK), NEG, F32)], axis=0)
        return o_cmp, bias

    last_block = (s0 + NSA_QBLK - 1) // NSA_SLC_LEN
    o_cmp, bias = lax.switch(last_block // chunk,
                             [functools.partial(cmp_and_select, n + 1) for n in range(split)])
    bias_s[...] = bias

    kt_diag = s0 // NSA_KTILE
    blocks_per_tile = NSA_KTILE // NSA_SLC_LEN
    vtiles = NSA_KTILE // LANES

    def qk_tile(kt, qaug_ref):
        k0 = pl.multiple_of(kt * NSA_KTILE, NSA_KTILE)
        b8 = bias_s[pl.ds(pl.multiple_of(kt * blocks_per_tile, blocks_per_tile), blocks_per_tile), :]
        b16 = jnp.concatenate([b8, jnp.zeros_like(b8)], axis=0).astype(BF16)
        qaug_ref[HEAD_DIM:HEAD_DIM + 16, :] = jnp.concatenate([b16] * NSA_RPG, axis=1)
        return _dot(ks_ref[pl.ds(k0, NSA_KTILE), :], qaug_ref[...])

    ones_rows = jnp.ones((16, NSA_KTILE), BF16)

    def pv_tile(kt, p):
        vt = jnp.concatenate([vs_ref[kt * vtiles + i] for i in range(vtiles)], axis=1)
        return _dot(jnp.concatenate([vt, ones_rows], axis=0), p)

    def softmax_tile(s, m_old):
        m_new = jnp.maximum(m_old, jnp.max(s, axis=0, keepdims=True))
        p = jnp.exp2((s - m_new).astype(BF16))
        return p, m_new, jnp.exp2(m_old - m_new)

    def visible(kt):
        kpos = kt * NSA_KTILE + lax.broadcasted_iota(jnp.int32, (NSA_KTILE, 1), 0)
        return jnp.concatenate([kpos <= qpos] * NSA_RPG, axis=1)

    def tile_group(i, carry, last):
        m_run, acc, alpha_prev = carry
        for k in range(NSA_UNROLL):
            t = NSA_UNROLL * i + k
            s_cur, p_cur, s_nxt, p_prv, qa = ((sa_s, pa_s, sb_s, pb_s, qaug_s) if k % 2 == 0 else
                                              (sb_s, pb_s, sa_s, pa_s, qaug2_s))
            acc = alpha_prev * acc + pv_tile(jnp.maximum(t - 1, 0), p_prv[...])
            s = jnp.where(visible(t), s_cur[...], NEG) if last else s_cur[...]
            p, m_run, alpha_prev = softmax_tile(s, m_run)
            p_cur[...] = p
            if last and k == NSA_UNROLL - 1:
                acc = alpha_prev * acc + pv_tile(t, p_cur[...])
            else:
                s_nxt[...] = qk_tile(t + 1, qa)
        return m_run, acc, alpha_prev

    qaug2_s[...] = qaug_s[...]
    pb_s[...] = jnp.zeros((NSA_KTILE, nqt), BF16)
    sa_s[...] = qk_tile(0, qaug2_s)

    span = NSA_WIN + NSA_QBLK
    start = pl.multiple_of(jnp.maximum(s0 - NSA_WIN, 0), NSA_QBLK)
    kwin = kw_ref[pl.ds(start, span), :]
    kp = start + lax.broadcasted_iota(jnp.int32, (span, 1), 0)
    wbias = jnp.where((kp <= qpos) & (kp > qpos - NSA_WIN), 0.0, NEG)
    pw, dens = [], []
    for sl in heads:
        s = _dot(kwin, qaug_s[:, sl]) + wbias
        e = jnp.exp2(s - jnp.max(s, axis=0, keepdims=True))
        dens.append(jnp.sum(e, axis=0, keepdims=True))
        pw.append(e.astype(BF16))
    sblk = start // LANES
    vwt = jnp.concatenate([vw_ref[sblk + i] for i in range(span // LANES)], axis=1)
    o_win = _dot(vwt, jnp.concatenate(pw, axis=1)) * (1.0 / jnp.concatenate(dens, axis=1))

    init = (jnp.full((1, nqt), NEG, F32), jnp.zeros((HEAD_DIM + 16, nqt), F32), jnp.ones((1, nqt), F32))
    group_diag = kt_diag // NSA_UNROLL
    carry = lax.fori_loop(0, group_diag, lambda i, c: tile_group(i, c, False), init)
    _, acc, _ = tile_group(group_diag, carry, True)
    o_slc = acc[0:HEAD_DIM] * (1.0 / acc[HEAD_DIM:HEAD_DIM + 1])

    gates = _sigmoid(g_ref[0])
    outs = [gates[r:r + 1, :] * o_cmp[:, sl] + gates[NSA_RPG + r:NSA_RPG + r + 1, :] * o_slc[:, sl]
            + gates[2 * NSA_RPG + r:2 * NSA_RPG + r + 1, :] * o_win[:, sl] for r, sl in enumerate(heads)]
    o_ref[...] = jnp.concatenate(outs, axis=0).T.astype(o_ref.dtype)


def _nsa_mixer(q_tt, kvc, k_slc, k_win, v_tt, g_tt, bsz, seq, pe, w1, b1, w2, b2):
    nqb = seq // NSA_QBLK
    npc = seq // NSA_CMP_STRIDE
    ns = seq // NSA_SLC_LEN
    cmp_out = _nsa_compress(kvc, bsz, seq, pe, w1, b1, w2, b2)
    split = _nsa_cmp_split(ns)
    perm = cmp_out.reshape(bsz, 2, NSA_KV, split, ns // split, 4, HEAD_DIM).transpose(0, 1, 2, 3, 5, 4, 6).reshape(
        bsz, 2, NSA_KV, npc, HEAD_DIM)
    kc = perm[:, 0].astype(BF16)
    vc_t = perm[:, 1].transpose(0, 1, 3, 2).astype(BF16)
    gq = NSA_RPG * HEAD_DIM
    return pl.pallas_call(
        _nsa_kernel,
        grid=(bsz, NSA_KV, nqb),
        in_specs=[
            pl.BlockSpec((1, gq, LANES), lambda b, g, q: (b * nqb + q, g, 0)),
            pl.BlockSpec((1, 1, npc, HEAD_DIM), lambda b, g, q: (b, g, 0, 0)),
            pl.BlockSpec((1, 1, HEAD_DIM, npc), lambda b, g, q: (b, g, 0, 0)),
            pl.BlockSpec((seq, NSA_KVW), lambda b, g, q: (b, g)),
            pl.BlockSpec((seq, NSA_KVW), lambda b, g, q: (b, g)),
            pl.BlockSpec((nqb, HEAD_DIM, LANES), lambda b, g, q: (b, g, 0)),
            pl.BlockSpec((nqb, HEAD_DIM, LANES), lambda b, g, q: (b, NSA_KV + g, 0)),
            pl.BlockSpec((1, 16, LANES), lambda b, g, q: (b * nqb + q, g, 0)),
        ],
        out_specs=pl.BlockSpec((NSA_QBLK, gq), lambda b, g, q: (b * nqb + q, g)),
        out_shape=jax.ShapeDtypeStruct((bsz * seq, NSA_Q), BF16),
        scratch_shapes=[pltpu.VMEM((ns, NSA_QBLK), F32),
                        pltpu.VMEM((2 * HEAD_DIM, NSA_RPG * NSA_QBLK), BF16),
                        pltpu.VMEM((2 * HEAD_DIM, NSA_RPG * NSA_QBLK), BF16),
                        pltpu.VMEM((NSA_KTILE, NSA_RPG * NSA_QBLK), F32),
                        pltpu.VMEM((NSA_KTILE, NSA_RPG * NSA_QBLK), F32),
                        pltpu.VMEM((NSA_KTILE, NSA_RPG * NSA_QBLK), BF16),
                        pltpu.VMEM((NSA_KTILE, NSA_RPG * NSA_QBLK), BF16)],
        compiler_params=_cparams("arbitrary", "arbitrary", "arbitrary"),
        name="nsa_attention",
    )(q_tt, kc, vc_t, k_slc, k_win, v_tt, v_tt, g_tt)


def _swa_kernel(q_ref, kp_ref, kc_ref, vp_ref, vc_ref, sink_ref, o_ref):
    qb = pl.program_id(1)
    t = SWA_WIN
    kband = jnp.concatenate([kp_ref[...], kc_ref[...]], axis=0)
    krel = lax.broadcasted_iota(jnp.int32, (2 * t, 1), 0) - t
    qrel = lax.broadcasted_iota(jnp.int32, (1, t), 1)
    lowest = jnp.where(qb > 0, -t, 0)
    mbias = jnp.where((krel <= qrel) & (krel > qrel - SWA_WIN) & (krel >= lowest), 0.0, NEG)
    outs = []
    for g in range(2):
        rows = slice(g * SWA_RPG * HEAD_DIM, (g + 1) * SWA_RPG * HEAD_DIM)
        qg = q_ref[0, rows, :]
        qcat = jnp.concatenate([qg[r * HEAD_DIM:(r + 1) * HEAD_DIM, :] for r in range(SWA_RPG)], axis=1)
        zq = jnp.zeros_like(qcat)
        qext = jnp.concatenate([qcat, zq] if g == 0 else [zq, qcat], axis=0)
        s = _dot(kband, qext)
        ps, dens = [], []
        for r in range(SWA_RPG):
            h = g * SWA_RPG + r
            sink = sink_ref[h:h + 1, :]
            sr = s[:, r * t:(r + 1) * t] + mbias
            mx = jnp.maximum(jnp.max(sr, axis=0, keepdims=True), sink)
            e = jnp.exp2(sr - mx)
            dens.append(jnp.sum(e, axis=0, keepdims=True) + jnp.exp2(sink - mx))
            ps.append(e.astype(BF16))
        vband = jnp.concatenate([vp_ref[0, g * HEAD_DIM:(g + 1) * HEAD_DIM, :],
                                 vc_ref[0, g * HEAD_DIM:(g + 1) * HEAD_DIM, :]], axis=1)
        og = _dot(vband, jnp.concatenate(ps, axis=1)) * (1.0 / jnp.concatenate(dens, axis=1))
        outs.append(jnp.concatenate([og[:, r * t:(r + 1) * t] for r in range(SWA_RPG)], axis=0).T)
    o_ref[...] = jnp.concatenate(outs, axis=1).astype(o_ref.dtype)


def _swa_mixer(q_tt, k_nat, v_tt, sinks, bsz, seq):
    nqb = seq // SWA_WIN
    sink_rows = jnp.broadcast_to((sinks.astype(F32) * math.log2(math.e))[:, None], (sinks.shape[0], LANES))
    prev = lambda b, q: b * nqb + jnp.maximum(q - 1, 0)
    return pl.pallas_call(
        _swa_kernel,
        grid=(bsz, nqb),
        in_specs=[
            pl.BlockSpec((1, SWA_Q, LANES), lambda b, q: (b * nqb + q, 0, 0)),
            pl.BlockSpec((SWA_WIN, SWA_KVW), lambda b, q: (prev(b, q), 0)),
            pl.BlockSpec((SWA_WIN, SWA_KVW), lambda b, q: (b * nqb + q, 0)),
            pl.BlockSpec((1, SWA_KVW, LANES), lambda b, q: (prev(b, q), 0, 0)),
            pl.BlockSpec((1, SWA_KVW, LANES), lambda b, q: (b * nqb + q, 0, 0)),
            _full(sink_rows.shape),
        ],
        out_specs=pl.BlockSpec((SWA_WIN, SWA_Q), lambda b, q: (b * nqb + q, 0)),
        out_shape=jax.ShapeDtypeStruct((bsz * seq, SWA_Q), BF16),
        compiler_params=_cparams("parallel", "parallel"),
        name="swa_attention",
    )(q_tt, k_nat, k_nat, v_tt, v_tt, sink_rows)


def _s5_params(a_re, a_im, log_dt, b_re, b_im, c_re, c_im, n_chunks):
    f = F32
    t = S5_CHUNK
    step = jnp.exp(log_dt.astype(f))[:, None]
    lr, li = a_re.astype(f), a_im.astype(f)

    def lam_pow(tau):
        tau = tau.astype(f)[:, None, None]
        mag = jnp.exp(lr * step * tau)
        ang = li * step * tau
        return mag * jnp.cos(ang), mag * jnp.sin(ang)

    lb_r, lb_i = (v[0] for v in lam_pow(jnp.ones((1,))))
    nr, ni = lb_r - 1.0, lb_i
    den = lr * lr + li * li
    fr, fi = (nr * lr + ni * li) / den, (ni * lr - nr * li) / den
    br, bi = b_re.astype(f), b_im.astype(f)
    bb_r = fr[..., None] * br - fi[..., None] * bi
    bb_i = fr[..., None] * bi + fi[..., None] * br
    cr, ci = c_re.astype(f), c_im.astype(f)

    pr, pi = lam_pow(jnp.arange(t + 1))
    cl_r = cr[None] * pr[:, :, None, :] - ci[None] * pi[:, :, None, :]
    cl_i = cr[None] * pi[:, :, None, :] + ci[None] * pr[:, :, None, :]
    kern_t = jnp.einsum("tghp,gpk->gkth", cl_r[:t], bb_r, precision="highest") - jnp.einsum(
        "tghp,gpk->gkth", cl_i[:t], bb_i, precision="highest")
    rr, ri = pr[t - 1 - jnp.arange(t)], pi[t - 1 - jnp.arange(t)]
    bs_r = rr[..., None] * bb_r[None] - ri[..., None] * bb_i[None]
    bs_i = rr[..., None] * bb_i[None] + ri[..., None] * bb_r[None]
    bs = jnp.concatenate([bs_r, bs_i], axis=2)
    bs = bs.transpose(1, 0, 3, 2).reshape(S5_GROUPS, t * S5_GROUP_CH, 2 * S5_STATE)
    cs = jnp.concatenate([cl_r[1:], -cl_i[1:]], axis=3)
    cs = cs.transpose(1, 3, 0, 2).reshape(S5_GROUPS, 2 * S5_STATE, t * S5_GROUP_CH)
    ar, ai = pr[t], pi[t]
    a1, a2 = [], []
    k = 1
    while k < n_chunks:
        a1.append(jnp.concatenate([ar, ar], axis=1))
        a2.append(jnp.concatenate([-ai, ai], axis=1))
        ar, ai = ar * ar - ai * ai, 2.0 * ar * ai
        k *= 2
    a1 = jnp.stack(a1, axis=1)
    a2 = jnp.stack(a2, axis=1)
    noct = S5_GROUPS // S5_OCT
    hc, ns2 = S5_GROUP_CH, 2 * S5_STATE
    width = t * hc
    lanes = t * LANES
    r_i, c_i = np.arange(width)[:, None], np.arange(lanes)[None, :]
    rep = jnp.asarray((r_i // hc == c_i // LANES) & (r_i % hc == c_i % hc), dtype=BF16)
    kern_o = kern_t.reshape(noct, LANES, width).astype(BF16)
    toep_o = pl.pallas_call(
        _s5_toeplitz_kernel,
        grid=(noct, t // 8),
        in_specs=[pl.BlockSpec((1, LANES, width), lambda o, s: (o, 0, 0)), _full((width, lanes))],
        out_specs=pl.BlockSpec((1, 8 * LANES, lanes), lambda o, s: (o, s, 0)),
        out_shape=jax.ShapeDtypeStruct((noct, lanes, lanes), BF16),
        scratch_shapes=[pltpu.VMEM((t, LANES, LANES), F32)],
        compiler_params=_cparams("parallel", "arbitrary"),
        name="s5_toeplitz_table",
    )(kern_o, rep)
    cs_o = pl.pallas_call(
        _s5_readout_kernel,
        grid=(noct,),
        in_specs=[pl.BlockSpec((1, S5_OCT * ns2, width), lambda o: (o, 0, 0)), _full((width, lanes))],
        out_specs=pl.BlockSpec((1, S5_OCT * ns2, lanes), lambda o: (o, 0, 0)),
        out_shape=jax.ShapeDtypeStruct((noct, S5_OCT * ns2, lanes), BF16),
        compiler_params=_cparams("parallel"),
        name="s5_readout_table",
    )(cs.reshape(noct, S5_OCT * ns2, width).astype(BF16), rep)
    bs_slabs = bs.reshape(noct, S5_OCT, t, hc, ns2).transpose(0, 2, 1, 3, 4).reshape(noct, t, LANES, ns2).astype(BF16)
    bs_o = pl.pallas_call(
        _s5_state_in_kernel,
        grid=(noct,),
        in_specs=[pl.BlockSpec((1, t, LANES, ns2), lambda o: (o, 0, 0, 0))],
        out_specs=pl.BlockSpec((1, lanes, S5_OCT * ns2), lambda o: (o, 0, 0)),
        out_shape=jax.ShapeDtypeStruct((noct, lanes, S5_OCT * ns2), BF16),
        compiler_params=_cparams("parallel"),
        name="s5_state_in_table",
    )(bs_slabs)

    def oct_rows(a):
        return a.reshape(noct, S5_OCT, -1, ns2).transpose(0, 2, 1, 3).reshape(noct, -1, S5_OCT * ns2)

    return toep_o, bs_o, cs_o, oct_rows(a1), oct_rows(a2)


def _same_group(shape, row_div, col_mod, col_div):
    r = lax.broadcasted_iota(jnp.int32, shape, 0)
    c = lax.broadcasted_iota(jnp.int32, shape, 1)
    return (r // row_div) == ((c % col_mod) // col_div)


def _s5_toeplitz_kernel(k_ref, rep_ref, o_ref, full_s):
    nlag = full_s.shape[0]
    steps_here = o_ref.shape[1] // LANES

    @pl.when(pl.program_id(1) == 0)
    def _():
        full = _dot(k_ref[0], rep_ref[...])
        full = jnp.where(_same_group(full.shape, S5_GROUP_CH, LANES, S5_GROUP_CH), full, 0.0)
        for lag in range(nlag):
            full_s[lag] = full[:, lag * LANES:(lag + 1) * LANES]

    for i in range(steps_here):
        s = pl.program_id(1) * steps_here + i
        for t in range(nlag):
            tile = full_s[jnp.maximum(t - s, 0)]
            o_ref[0, i * LANES:(i + 1) * LANES, t * LANES:(t + 1) * LANES] = (
                jnp.where(t >= s, tile, 0.0).astype(o_ref.dtype))


def _s5_readout_kernel(c_ref, rep_ref, o_ref):
    full = _dot(c_ref[0], rep_ref[...])
    keep = _same_group(full.shape, 2 * S5_STATE, LANES, S5_GROUP_CH)
    o_ref[0] = jnp.where(keep, full, 0.0).astype(o_ref.dtype)


def _s5_state_in_kernel(b_ref, o_ref):
    keep = _same_group((LANES, o_ref.shape[2]), S5_GROUP_CH, o_ref.shape[2], 2 * S5_STATE)
    for s in range(b_ref.shape[1]):
        full = jnp.concatenate([b_ref[0, s]] * S5_OCT, axis=1)
        o_ref[0, s * LANES:(s + 1) * LANES, :] = jnp.where(keep, full, jnp.zeros((), full.dtype))


def _s5_state_kernel(bsz, u_ref, bs_ref, a1_ref, a2_ref, hi_ref, lo_ref):
    sc = _dot(u_ref[...], bs_ref[0])
    n = sc.shape[0] // bsz
    width = sc.shape[1]
    rowi = lax.broadcasted_iota(jnp.int32, (n, width), 0)

    def swap_re_im(x):
        return jnp.concatenate([pltpu.roll(x[:, j * LANES:(j + 1) * LANES], S5_STATE, 1)
                                for j in range(width // LANES)], axis=1)

    h_in = []
    for b in range(bsz):
        x = sc[b * n:(b + 1) * n]
        k, step = 1, 0
        while k < n:
            xs = jnp.where(rowi >= k, pltpu.roll(x, k, 0), 0.0)
            x = x + a1_ref[0, step:step + 1, :] * xs + a2_ref[0, step:step + 1, :] * swap_re_im(xs)
            k *= 2
            step += 1
        h_in.append(jnp.where(rowi >= 1, pltpu.roll(x, 1, 0), 0.0))
    h_in = jnp.concatenate(h_in, axis=0)
    hi = h_in.astype(BF16)
    hi_ref[0] = hi
    lo_ref[0] = (h_in - hi.astype(F32)).astype(BF16)


def _s5_out_kernel(u_ref, toep_ref, hi_ref, lo_ref, cs_ref, o_ref):
    n = pl.program_id(1)
    cols = toep_ref.shape[2]
    off = _dot(hi_ref[0], cs_ref[0]) + _dot(lo_ref[0], cs_ref[0])
    for nn in range(toep_ref.shape[1] // cols):
        @pl.when(n == nn)
        def _(nn=nn):
            k = (nn + 1) * cols
            y = off + _dot(u_ref[:, 0:k], toep_ref[0, 0:k, :])
            for t8 in range(cols // LANES):
                o_ref[:, t8, :] = y[:, t8 * LANES:(t8 + 1) * LANES].astype(o_ref.dtype)


def _s5_glu_kernel(y_ref, u_ref, d_ref, w_ref, b_ref, o_ref):
    y = _gelu_tanh(y_ref[...] + d_ref[...] * u_ref[...])
    gate = _sigmoid(_dot(y.astype(BF16), w_ref[...]) + b_ref[...])
    o_ref[...] = (y * gate).astype(o_ref.dtype)


def _s5_mixer(u5, u_chunks, bsz, seq, a_re, a_im, log_dt, b_re, b_im, c_re, c_im, d_skip, glu_w, glu_b, tm=1024):
    m = u5.shape[0]
    t = S5_CHUNK
    nch = m // t
    noct = S5_GROUPS // S5_OCT
    lanes = t * LANES
    sw = S5_OCT * 2 * S5_STATE
    toep, bs, cs, a1, a2 = _s5_params(a_re, a_im, log_dt, b_re, b_im, c_re, c_im, nch // bsz)
    nsteps = a1.shape[1]
    h_hi, h_lo = pl.pallas_call(
        functools.partial(_s5_state_kernel, bsz),
        grid=(noct,),
        in_specs=[
            pl.BlockSpec((nch, lanes), lambda o: (0, o)),
            pl.BlockSpec((1, lanes, sw), lambda o: (o, 0, 0)),
            pl.BlockSpec((1, nsteps, sw), lambda o: (o, 0, 0)),
            pl.BlockSpec((1, nsteps, sw), lambda o: (o, 0, 0)),
        ],
        out_specs=[pl.BlockSpec((1, nch, sw), lambda o: (o, 0, 0))] * 2,
        out_shape=[jax.ShapeDtypeStruct((noct, nch, sw), BF16)] * 2,
        compiler_params=_cparams("parallel"),
        name="s5_state",
    )(u_chunks, bs, a1, a2)
    tsub = 8
    rows = nch // 2
    y = pl.pallas_call(
        _s5_out_kernel,
        grid=(noct, t // tsub, nch // rows),
        in_specs=[
            pl.BlockSpec((rows, lanes), lambda o, n, r: (r, o)),
            pl.BlockSpec((1, lanes, tsub * LANES), lambda o, n, r: (o, 0, n)),
            pl.BlockSpec((1, rows, sw), lambda o, n, r: (o, r, 0)),
            pl.BlockSpec((1, rows, sw), lambda o, n, r: (o, r, 0)),
            pl.BlockSpec((1, sw, tsub * LANES), lambda o, n, r: (o, 0, n)),
        ],
        out_specs=pl.BlockSpec((rows, tsub, LANES), lambda o, n, r: (r, n, o)),
        out_shape=jax.ShapeDtypeStruct((nch, t, S5_CH), BF16),
        compiler_params=_cparams("parallel", "parallel", "parallel"),
        name="s5_scan",
    )(u_chunks, toep, h_hi, h_lo, cs).reshape(m, S5_CH)
    return pl.pallas_call(
        _s5_glu_kernel,
        grid=(m // tm,),
        in_specs=[
            pl.BlockSpec((tm, S5_CH), lambda i: (i, 0)),
            pl.BlockSpec((tm, S5_CH), lambda i: (i, 0)),
            _full((1, S5_CH)), _full((S5_CH, S5_CH)), _full((1, S5_CH)),
        ],
        out_specs=pl.BlockSpec((tm, S5_CH), lambda i: (i, 0)),
        out_shape=jax.ShapeDtypeStruct((m, S5_CH), BF16),
        compiler_params=_cparams("parallel"),
        name="s5_glu",
    )(y, u5, d_skip.reshape(1, S5_CH).astype(F32), glu_w.astype(BF16), glu_b.reshape(1, S5_CH).astype(F32))


def _even_mixers(h, bsz, seq, g_mix, w_in, conv_w, conv_b, dt_bias, a_log, d_skip, norm_g, pe, w1, b1, w2, b2):
    d = h.shape[1]
    scale = HEAD_DIM ** -0.5 * math.log2(math.e)
    o = SSD_IN
    w_ssd = jnp.concatenate([w_in[:, :SSD_IN], jnp.zeros((d, LANES - SSD_HEADS), w_in.dtype)], axis=1)
    w_q = w_in[:, o:o + NSA_Q] * scale
    kv = [w_in[:, o + NSA_Q + i * NSA_KVW:o + NSA_Q + (i + 1) * NSA_KVW] for i in range(6)]
    w_gate = w_in[:, o + NSA_Q + 6 * NSA_KVW:].reshape(d, NSA_KV, NSA_RPG, 3).transpose(0, 1, 3, 2)
    w_gate = jnp.concatenate([w_gate.reshape(d, NSA_KV, 12), jnp.zeros((d, NSA_KV, 4), w_in.dtype)],
                             axis=2).reshape(d, NSA_KV * 16)

    def per_group_halves(w):
        wg = w.reshape(d, NSA_KV, HEAD_DIM)
        return jnp.concatenate([wg, jnp.zeros_like(wg)], axis=2).reshape(d, NSA_KV * NSA_KVW)
    segs = [
        ("nat", w_ssd, F32),
        ("tt", w_q, BF16),
        ("pc", jnp.concatenate([kv[0], kv[1]], axis=1), F32),
        ("nat+", per_group_halves(kv[2]), BF16, _nsa_block_onehot(NSA_KTILE)),
        ("nat", per_group_halves(kv[4]), BF16),
        ("tt", jnp.concatenate([kv[3], kv[5]], axis=1), BF16),
        ("tt", w_gate, F32),
    ]
    u_ssd, q_tt, kvc, k_slc, k_win, v_tt, g_tt = _norm_proj(h, g_mix, segs)
    ya = _ssd_mixer(u_ssd, bsz, seq, conv_w, conv_b, dt_bias, a_log, d_skip, norm_g)
    yb = _nsa_mixer(q_tt, kvc, k_slc, k_win, v_tt, g_tt, bsz, seq, pe, w1, b1, w2, b2)
    return ya, yb


def _odd_mixers(h, bsz, seq, g_mix, w_in, sinks, a_re, a_im, log_dt, b_re, b_im, c_re, c_im, d_skip, glu_w, glu_b):
    scale = HEAD_DIM ** -0.5 * math.log2(math.e)
    segs = [
        ("tt", w_in[:, :SWA_Q] * scale, BF16),
        ("nat", w_in[:, SWA_Q:SWA_Q + SWA_KVW], BF16),
        ("tt", w_in[:, SWA_Q + SWA_KVW:SWA_Q + 2 * SWA_KVW], BF16),
        ("nat+ch", w_in[:, SWA_Q + 2 * SWA_KVW:], F32),
    ]
    q_tt, k_nat, v_tt, u5, u_chunks = _norm_proj(h, g_mix, segs)
    yc = _swa_mixer(q_tt, k_nat, v_tt, sinks, bsz, seq)
    yd = _s5_mixer(u5, u_chunks, bsz, seq, a_re, a_im, log_dt, b_re, b_im, c_re, c_im, d_skip, glu_w, glu_b)
    return yc, yd


def kernel(x, norm_mix, norm_mlp, norm_final, mlp_w_up, mlp_w_down, ev_w_in, ev_w_out, ssd_conv_w, ssd_conv_b,
           ssd_dt_bias, ssd_a_log, ssd_d, ssd_norm, nsa_pe, nsa_cmp_w1, nsa_cmp_b1, nsa_cmp_w2, nsa_cmp_b2,
           od_w_in, od_w_out, swa_sinks, s5_a_re, s5_a_im, s5_log_dt, s5_b_re, s5_b_im, s5_c_re, s5_c_im,
           s5_d, s5_glu_w, s5_glu_b):
    bsz, seq, d = x.shape
    depth = norm_mix.shape[0]
    assert seq % (NSA_UNROLL * NSA_KTILE) == 0 and seq >= NSA_WIN + NSA_QBLK
    h = x.reshape(bsz * seq, d)
    for layer in range(depth):
        i = layer // 2
        if layer % 2 == 0:
            ya, yb = _even_mixers(h, bsz, seq, norm_mix[layer], ev_w_in[i], ssd_conv_w[i], ssd_conv_b[i],
                                  ssd_dt_bias[i], ssd_a_log[i], ssd_d[i], ssd_norm[i], nsa_pe[i],
                                  nsa_cmp_w1[i], nsa_cmp_b1[i], nsa_cmp_w2[i], nsa_cmp_b2[i])
            w_out = ev_w_out[i]
        else:
            ya, yb = _odd_mixers(h, bsz, seq, norm_mix[layer], od_w_in[i], swa_sinks[i], s5_a_re[i], s5_a_im[i],
                                 s5_log_dt[i], s5_b_re[i], s5_b_im[i], s5_c_re[i], s5_c_im[i], s5_d[i],
                                 s5_glu_w[i], s5_glu_b[i])
            w_out = od_w_out[i]
        h = _mix_out_mlp(h, ya, yb, w_out, norm_mlp[layer], mlp_w_up[layer], mlp_w_down[layer], norm_final,
                         final=(layer == depth - 1))
    return h.reshape(bsz, seq, d)
```

```python
import functools
import math

import jax
import jax.numpy as jnp
import numpy as np
from jax import lax
from jax.experimental import pallas as pl
from jax.experimental.pallas import tpu as pltpu

F32 = jnp.float32
BF16 = jnp.bfloat16

EPS = 1e-6
NEG = -1e30
FORCE = 1e9
HEAD_DIM = 64
LANES = 128
VMEM_LIMIT_BYTES = 56 * 1024 * 1024

SSD_HEADS = 8
SSD_INNER = 512
SSD_GROUPS = 2
SSD_STATE = 128
SSD_CONV = 4
SSD_CHUNK = 128
SSD_CONV_DIM = 1024
SSD_IN = SSD_INNER + SSD_CONV_DIM + SSD_HEADS

NSA_HEADS = 8
NSA_KV = 2
NSA_RPG = 4
NSA_CMP_LEN = 32
NSA_CMP_STRIDE = 16
NSA_SLC_LEN = 64
NSA_TOPK = 16
NSA_WIN = 512
NSA_CMP_HIDDEN = 256
NSA_QBLK = 128
NSA_Q = 512
NSA_KVW = 128
NSA_KTILE = 512
NSA_UNROLL = 2

SWA_RPG = 4
SWA_WIN = 128
SWA_Q = 512
SWA_KVW = 128

S5_CH = 512
S5_GROUP_CH = 16
S5_GROUPS = 32
S5_STATE = 64
S5_CHUNK = 32
S5_OCT = LANES // S5_GROUP_CH


def _cparams(*sem):
    return pltpu.CompilerParams(dimension_semantics=sem, vmem_limit_bytes=VMEM_LIMIT_BYTES)


def _full(shape):
    n = len(shape)
    return pl.BlockSpec(shape, lambda *_: (0,) * n)


def _dot(a, b):
    return jnp.dot(a, b, preferred_element_type=F32)


def _dot_nt(a, b):
    return lax.dot_general(a, b, (((1,), (1,)), ((), ())), preferred_element_type=F32)


def _split3(a):
    hi = a.astype(BF16)
    r1 = a - hi.astype(F32)
    mid = r1.astype(BF16)
    lo = (r1 - mid.astype(F32)).astype(BF16)
    return hi, mid, lo


def _dot_exact_rhs(a, b_exact):
    hi, mid, lo = _split3(a)
    return _dot(hi, b_exact) + _dot(mid, b_exact) + _dot(lo, b_exact)


def _dot_exact_lhs(a_exact, b):
    hi, mid, lo = _split3(b)
    return _dot(a_exact, hi) + _dot(a_exact, mid) + _dot(a_exact, lo)


def _rms(x, g):
    return x * lax.rsqrt(jnp.mean(x * x, axis=-1, keepdims=True) + EPS) * g


def _gelu_tanh(x):
    c = math.sqrt(2.0 / math.pi)
    return 0.5 * x * (1.0 + jnp.tanh(c * (x + 0.044715 * (x * x * x))))


def _sigmoid(x):
    return 1.0 / (1.0 + jnp.exp(-x))


def _proj_kernel(kinds, tm, h_ref, g_ref, *refs):
    n = len(kinds)
    n_add = sum(k == "nat+" for k in kinds)
    n_out = n + sum(k == "nat+ch" for k in kinds)
    w_refs, add_refs = refs[:n], list(refs[n:n + n_add])
    o_refs, scratch = list(refs[n + n_add:n + n_add + n_out]), list(refs[n + n_add + n_out:])
    yb = _rms(h_ref[...], g_ref[...]).astype(BF16)
    for kind, w_ref in zip(kinds, w_refs):
        o_ref = o_refs.pop(0)
        if kind == "nat":
            o_ref[...] = _dot(yb, w_ref[...]).astype(o_ref.dtype)
        elif kind == "nat+ch":
            ch_ref, ch_s = o_refs.pop(0), scratch.pop(0)
            res = _dot(yb, w_ref[...])
            o_ref[...] = res.astype(o_ref.dtype)
            for j in range(ch_s.shape[0]):
                ch_s[j] = res[:, j * LANES:(j + 1) * LANES]
            for t in range(S5_CHUNK):
                for j in range(ch_s.shape[0]):
                    col = (j * S5_CHUNK + t) * LANES
                    ch_ref[:, col:col + LANES] = ch_s[j, pl.ds(t, tm // S5_CHUNK, stride=S5_CHUNK), :].astype(ch_ref.dtype)
        elif kind == "nat+":
            o_ref[...] = _dot(yb, w_ref[...]).astype(o_ref.dtype) + add_refs.pop(0)[...]
        elif kind == "pc":
            pc_s = scratch.pop(0)
            res = _dot(yb, w_ref[...])
            flat = NSA_CMP_STRIDE * HEAD_DIM
            per_tile = LANES // HEAD_DIM
            for j in range(pc_s.shape[0]):
                pc_s[j] = res[:, j * LANES:(j + 1) * LANES]
            for p in range(NSA_CMP_STRIDE):
                for j in range(pc_s.shape[0]):
                    tok = pc_s[j, pl.ds(p, tm // NSA_CMP_STRIDE, stride=NSA_CMP_STRIDE), :]
                    for c in range(per_tile):
                        col = (j * per_tile + c) * flat + p * HEAD_DIM
                        o_ref[:, col:col + HEAD_DIM] = tok[:, c * HEAD_DIM:(c + 1) * HEAD_DIM].astype(o_ref.dtype)
        else:
            ot = _dot_nt(w_ref[...], yb)
            for j in range(tm // LANES):
                o_ref[j] = ot[:, j * LANES:(j + 1) * LANES].astype(o_ref.dtype)


def _norm_proj(h, g, segs, tm=512):
    m, d = h.shape
    kinds = tuple(s[0] for s in segs)
    adds = [s[3] for s in segs if s[0] == "nat+"]
    ws, w_specs, out_shapes, out_specs, scratch = [], [], [], [], []
    for kind, w, dt in (s[:3] for s in segs):
        n_out = w.shape[1]
        if kind == "pc":
            ws.append(w.astype(BF16))
            w_specs.append(_full((d, n_out)))
            out_shapes.append(jax.ShapeDtypeStruct((m // NSA_CMP_STRIDE, NSA_CMP_STRIDE * n_out), dt))
            out_specs.append(pl.BlockSpec((tm // NSA_CMP_STRIDE, NSA_CMP_STRIDE * n_out), lambda i: (i, 0)))
            scratch.append(pltpu.VMEM((n_out // LANES, tm, LANES), F32))
        elif kind in ("nat", "nat+", "nat+ch"):
            ws.append(w.astype(BF16))
            w_specs.append(_full((d, n_out)))
            out_shapes.append(jax.ShapeDtypeStruct((m, n_out), dt))
            out_specs.append(pl.BlockSpec((tm, n_out), lambda i: (i, 0)))
            if kind == "nat+ch":
                out_shapes.append(jax.ShapeDtypeStruct((m // S5_CHUNK, S5_CHUNK * n_out), BF16))
                out_specs.append(pl.BlockSpec((tm // S5_CHUNK, S5_CHUNK * n_out), lambda i: (i, 0)))
                scratch.append(pltpu.VMEM((n_out // LANES, tm, LANES), F32))
        else:
            ws.append(w.T.astype(BF16))
            w_specs.append(_full((n_out, d)))
            out_shapes.append(jax.ShapeDtypeStruct((m // LANES, n_out, LANES), dt))
            out_specs.append(pl.BlockSpec((tm // LANES, n_out, LANES), lambda i: (i, 0, 0)))
    return pl.pallas_call(
        functools.partial(_proj_kernel, kinds, tm),
        grid=(m // tm,),
        in_specs=[pl.BlockSpec((tm, d), lambda i: (i, 0)), _full((1, d))] + w_specs + [
            pl.BlockSpec((tm, a.shape[1]), functools.partial(lambda i, nb: (i % nb, 0), nb=a.shape[0] // tm))
            for a in adds],
        out_specs=out_specs,
        out_shape=out_shapes,
        scratch_shapes=scratch,
        compiler_params=_cparams("parallel"),
        name="norm_proj",
    )(h, g.reshape(1, d), *ws, *adds)


def _mlp_kernel(final, h_ref, ya_ref, yb_ref, woa_ref, wob_ref, gm_ref, wup_ref, wdn_ref, gf_ref,
                o_ref, h2_s, xn_s, acc_s):
    j = pl.program_id(1)

    @pl.when(j == 0)
    def _():
        h2 = h_ref[...] + _dot(ya_ref[...], woa_ref[...]) + _dot(yb_ref[...], wob_ref[...])
        h2_s[...] = h2
        xn_s[...] = _rms(h2, gm_ref[...]).astype(BF16)
        acc_s[...] = jnp.zeros_like(acc_s)

    hid = jnp.square(jnp.maximum(_dot(xn_s[...], wup_ref[...]), 0.0))
    acc_s[...] += _dot(hid.astype(BF16), wdn_ref[...])

    @pl.when(j == pl.num_programs(1) - 1)
    def _():
        out = h2_s[...] + acc_s[...]
        if final:
            out = _rms(out, gf_ref[...])
        o_ref[...] = out


def _mix_out_mlp(h, ya, yb, w_out, g_mlp, w_up, w_down, g_final, final, tm=1024, tf=1024):
    m, d = h.shape
    dff = w_up.shape[1]
    na = ya.shape[1]
    nb = yb.shape[1]
    return pl.pallas_call(
        functools.partial(_mlp_kernel, final),
        grid=(m // tm, dff // tf),
        in_specs=[
            pl.BlockSpec((tm, d), lambda i, j: (i, 0)),
            pl.BlockSpec((tm, na), lambda i, j: (i, 0)),
            pl.BlockSpec((tm, nb), lambda i, j: (i, 0)),
            _full((na, d)), _full((nb, d)), _full((1, d)),
            pl.BlockSpec((d, tf), lambda i, j: (0, j)),
            pl.BlockSpec((tf, d), lambda i, j: (j, 0)),
            _full((1, d)),
        ],
        out_specs=pl.BlockSpec((tm, d), lambda i, j: (i, 0)),
        out_shape=jax.ShapeDtypeStruct((m, d), F32),
        scratch_shapes=[pltpu.VMEM((tm, d), F32), pltpu.VMEM((tm, d), BF16), pltpu.VMEM((tm, d), F32)],
        compiler_params=_cparams("parallel", "arbitrary"),
        name="out_proj_mlp",
    )(h, ya, yb, w_out[:na].astype(BF16), w_out[na:].astype(BF16), g_mlp.reshape(1, d),
      w_up.astype(BF16), w_down.astype(BF16), g_final.reshape(1, d))


def _ssd_kernel(u_ref, cw_ref, cb_ref, dtb_ref, alog_ref, dsk_ref, ng_ref, o_ref, xext_s, st_s):
    t = SSD_CHUNK
    c = pl.program_id(1)

    @pl.when(c == 0)
    def _():
        xext_s[0:8, :] = jnp.zeros((8, SSD_CONV_DIM), F32)
        st_s[...] = jnp.zeros_like(st_s)

    z = u_ref[:, 0:SSD_INNER]
    dt_raw = u_ref[:, SSD_INNER + SSD_CONV_DIM:]
    xext_s[8:8 + t, :] = u_ref[:, SSD_INNER:SSD_INNER + SSD_CONV_DIM]
    xfull = xext_s[...]
    conv = cb_ref[...] + cw_ref[SSD_CONV - 1:SSD_CONV, :] * xfull[8:8 + t]
    for back in range(1, SSD_CONV):
        k = SSD_CONV - 1 - back
        conv = conv + cw_ref[k:k + 1, :] * pltpu.roll(xfull, back, 0)[8:8 + t]
    xext_s[0:8, :] = xext_s[t:t + 8, :]
    xc = conv * _sigmoid(conv)
    xs = xc[:, 0:SSD_INNER]
    gn = SSD_GROUPS * SSD_STATE

    dtp = dt_raw + dtb_ref[...]
    dt = jnp.maximum(dtp, 0.0) + jnp.log1p(jnp.exp(-jnp.abs(dtp)))
    a = -jnp.exp(alog_ref[...])
    da = dt * a

    row = lax.broadcasted_iota(jnp.int32, (t, t), 0)
    col = lax.broadcasted_iota(jnp.int32, (t, t), 1)
    causal = col <= row
    tril = jnp.where(causal, 1.0, 0.0).astype(BF16)
    a_cum = _dot_exact_lhs(tril, da)
    er = lax.broadcasted_iota(jnp.int32, (LANES, SSD_INNER), 0)
    ec = lax.broadcasted_iota(jnp.int32, (LANES, SSD_INNER), 1)
    expand = jnp.where((ec >> 6) == er, 1.0, 0.0).astype(BF16)
    a_cum_x = _dot_exact_rhs(a_cum, expand)
    dt_x = _dot_exact_rhs(dt, expand)
    a_cum_t = a_cum.T
    a_last_x = a_cum_x[t - 1:t, :]
    decay_end_x = jnp.exp(a_last_x - a_cum_x)
    decay_in_x = jnp.exp(a_cum_x)
    chunk_decay_x = jnp.exp(a_last_x)

    xd = xs * dt_x
    xd_end = (xd * decay_end_x).astype(BF16)
    xd_b = xd.astype(BF16)
    lane = lax.broadcasted_iota(jnp.int32, (t, LANES), 1)
    first_half = lane < HEAD_DIM

    pieces = []
    for g in range(SSD_GROUPS):
        bm = xc[:, SSD_INNER + g * SSD_STATE:SSD_INNER + (g + 1) * SSD_STATE]
        cm = xc[:, SSD_INNER + gn + g * SSD_STATE:SSD_INNER + gn + (g + 1) * SSD_STATE].astype(BF16)
        bm_t = bm.T.astype(BF16)
        cb = _dot_nt(cm, bm.astype(BF16))
        for pr in range(2):
            i = g * 2 + pr
            sl = slice(i * LANES, (i + 1) * LANES)
            ms = []
            for hh in range(2):
                h = 2 * i + hh
                seg = a_cum[:, h:h + 1] - a_cum_t[h:h + 1, :]
                dec = jnp.exp(jnp.where(causal, seg, NEG))
                ms.append((cb * dec).astype(BF16))
            y_diag = jnp.where(first_half, _dot(ms[0], xd_b[:, sl]), _dot(ms[1], xd_b[:, sl]))
            st = st_s[i]
            y_off = _dot(cm, st.astype(BF16)) * decay_in_x[:, sl]
            st_s[i] = st * chunk_decay_x[:, sl] + _dot(bm_t, xd_end[:, sl])
            pieces.append(y_diag + y_off)
    y = jnp.concatenate(pieces, axis=1) + xs * dsk_ref[...]
    y = y * (z * _sigmoid(z))
    half = SSD_INNER // SSD_GROUPS
    outs = [_rms(y[:, g * half:(g + 1) * half], ng_ref[:, g * half:(g + 1) * half]) for g in range(SSD_GROUPS)]
    o_ref[...] = jnp.concatenate(outs, axis=1).astype(o_ref.dtype)


def _ssd_mixer(u_ssd, bsz, seq, conv_w, conv_b, dt_bias, a_log, d_skip, norm_g):
    m, width = u_ssd.shape
    nch = seq // SSD_CHUNK
    pad = LANES - SSD_HEADS

    def padded(v):
        return jnp.concatenate([v.astype(F32), jnp.zeros((pad,), F32)]).reshape(1, LANES)

    return pl.pallas_call(
        _ssd_kernel,
        grid=(bsz, nch),
        in_specs=[
            pl.BlockSpec((SSD_CHUNK, width), lambda b, c: (b * nch + c, 0)),
            _full((SSD_CONV, SSD_CONV_DIM)), _full((1, SSD_CONV_DIM)),
            _full((1, LANES)), _full((1, LANES)), _full((1, SSD_INNER)), _full((1, SSD_INNER)),
        ],
        out_specs=pl.BlockSpec((SSD_CHUNK, SSD_INNER), lambda b, c: (b * nch + c, 0)),
        out_shape=jax.ShapeDtypeStruct((m, SSD_INNER), BF16),
        scratch_shapes=[pltpu.VMEM((SSD_CHUNK + 8, SSD_CONV_DIM), F32),
                        pltpu.VMEM((SSD_HEADS // 2, SSD_STATE, LANES), F32)],
        compiler_params=_cparams("arbitrary", "arbitrary"),
        name="ssd_mixer",
    )(u_ssd, conv_w.astype(F32), conv_b.reshape(1, -1).astype(F32), padded(dt_bias), padded(a_log),
      jnp.repeat(d_skip.astype(F32), HEAD_DIM).reshape(1, SSD_INNER), norm_g.reshape(1, SSD_INNER).astype(F32))


def _nsa_compress_kernel(x_ref, pe_ref, w1_ref, b1_ref, w2_ref, b2_ref, o_ref):
    x = x_ref[...]
    npc = x.shape[0]
    half = NSA_CMP_STRIDE * HEAD_DIM
    top = _dot((x + pe_ref[0, :, 0:half]).astype(BF16), w1_ref[0, 0:half, :])
    bot = _dot((x + pe_ref[0, :, half:]).astype(BF16), w1_ref[0, half:, :])
    pre = top + pltpu.roll(bot, npc - 1, 0) + b1_ref[0]
    hid = _gelu_tanh(pre).astype(BF16)
    out = _dot(hid, w2_ref[0]) + b2_ref[0]
    rowi = lax.broadcasted_iota(jnp.int32, out.shape, 0)
    o_ref[0, 0, 0] = jnp.where(rowi < npc - 1, out, 0.0)


def _nsa_compress(kvc_pieces, bsz, seq, pe, w1, b1, w2, b2):
    npc = seq // NSA_CMP_STRIDE
    flat = NSA_CMP_STRIDE * HEAD_DIM
    x = kvc_pieces
    return pl.pallas_call(
        _nsa_compress_kernel,
        grid=(bsz, 2, NSA_KV),
        in_specs=[
            pl.BlockSpec((npc, flat), lambda b, s, g: (b, s * NSA_KV + g)),
            pl.BlockSpec((1, 1, 2 * flat), lambda b, s, g: (s, 0, 0)),
            pl.BlockSpec((1, 2 * flat, NSA_CMP_HIDDEN), lambda b, s, g: (s, 0, 0)),
            pl.BlockSpec((1, 1, NSA_CMP_HIDDEN), lambda b, s, g: (s, 0, 0)),
            pl.BlockSpec((1, NSA_CMP_HIDDEN, HEAD_DIM), lambda b, s, g: (s, 0, 0)),
            pl.BlockSpec((1, 1, HEAD_DIM), lambda b, s, g: (s, 0, 0)),
        ],
        out_specs=pl.BlockSpec((1, 1, 1, npc, HEAD_DIM), lambda b, s, g: (b, s, g, 0, 0)),
        out_shape=jax.ShapeDtypeStruct((bsz, 2, NSA_KV, npc, HEAD_DIM), F32),
        compiler_params=_cparams("parallel", "parallel", "parallel"),
        name="nsa_compress",
    )(x, pe.reshape(2, 1, 2 * flat).astype(F32), w1.astype(BF16), b1.reshape(2, 1, -1).astype(F32),
      w2.astype(BF16), b2.reshape(2, 1, -1).astype(F32))


def _nsa_cmp_split(ns):
    return max(1, min(4, (4 * ns) // LANES))


def _nsa_block_onehot(rows):
    r = lax.broadcasted_iota(jnp.int32, (rows, 2 * NSA_KVW), 0)
    c = lax.broadcasted_iota(jnp.int32, (rows, 2 * NSA_KVW), 1)
    blk = (r % NSA_KTILE) // NSA_SLC_LEN
    return jnp.where((c % NSA_KVW) == HEAD_DIM + blk, 1.0, 0.0).astype(BF16)


def _nsa_kernel(q_ref, kc_ref, vc_ref, ks_ref, kw_ref, vs_ref, vw_ref, g_ref, o_ref,
                bias_s, qaug_s, qaug2_s, sa_s, sb_s, pa_s, pb_s):
    qb = pl.program_id(2)
    nqt = NSA_RPG * NSA_QBLK
    s0 = qb * NSA_QBLK
    ncp = kc_ref.shape[2]
    ns = ncp // 4
    heads = [slice(r * NSA_QBLK, (r + 1) * NSA_QBLK) for r in range(NSA_RPG)]

    qtiles = NSA_QBLK // LANES
    qcat = jnp.concatenate([q_ref[j, r * HEAD_DIM:(r + 1) * HEAD_DIM, :]
                            for r in range(NSA_RPG) for j in range(qtiles)], axis=1)
    qpos = s0 + lax.broadcasted_iota(jnp.int32, (1, NSA_QBLK), 1)
    qaug_s[0:HEAD_DIM, :] = qcat
    qaug_s[HEAD_DIM:, :] = jnp.zeros((HEAD_DIM, nqt), BF16)

    split = _nsa_cmp_split(ns)
    chunk = ns // split
    cvalid = jnp.where(qpos >= NSA_CMP_LEN - 1, 1.0, 0.0)
    cur = qpos // NSA_SLC_LEN
    taken = -3.0e38

    def cmp_and_select(nchunks):
        rows, jmax = nchunks * 4 * chunk, nchunks * chunk
        kc = kc_ref[0, 0, 0:rows, :]
        rc = lax.broadcasted_iota(jnp.int32, (rows, 1), 0)
        ncmp = 4 * ((rc // (4 * chunk)) * chunk + rc % chunk) + (rc % (4 * chunk)) // chunk
        cbias = jnp.where((ncmp * NSA_CMP_STRIDE + (NSA_CMP_LEN - 1)) <= qpos, 0.0, NEG)
        psum = jnp.zeros((rows, NSA_QBLK), F32)
        p_all = []
        for sl in heads:
            s = _dot(kc, qcat[:, sl]) + cbias
            e = jnp.exp2(s - jnp.max(s, axis=0, keepdims=True))
            p = e * (cvalid / jnp.sum(e, axis=0, keepdims=True))
            psum = psum + p
            p_all.append(p.astype(BF16))
        o_cmp = _dot(vc_ref[0, 0, :, 0:rows], jnp.concatenate(p_all, axis=1))

        tot, p3 = [], []
        for c in range(nchunks):
            part = [psum[(4 * c + i) * chunk:(4 * c + i + 1) * chunk] for i in range(4)]
            tot.append(part[0] + part[1] + part[2] + part[3])
            p3.append(part[3])
        tot, p3 = jnp.concatenate(tot, axis=0), jnp.concatenate(p3, axis=0)
        rj = lax.broadcasted_iota(jnp.int32, (jmax, NSA_QBLK), 0)
        imp = tot + jnp.where(rj >= 1, pltpu.roll(p3, 1, 0), 0.0)
        forced = (rj == 0) | (rj == cur) | (rj == cur - 1)
        rjf = rj.astype(F32)
        imp = jnp.where(forced, taken, jnp.where(rj <= cur, imp, -FORCE))
        for _ in range(min(NSA_TOPK, ns) - 3):
            mx = jnp.max(imp, axis=0, keepdims=True)
            first = jnp.min(jnp.where(imp == mx, rjf, float(ns)), axis=0, keepdims=True)
            imp = jnp.where(rjf == first, taken, imp)
        bias = jnp.where(imp == taken, 0.0, NEG)
        if jmax < ns:
            bias = jnp.concatenate([bias, jnp.full((ns - jmax, NSA_QBLK), NEG, F32)], axis=0)
        return o_cmp, bias

    last_block = (s0 + NSA_QBLK - 1) // NSA_SLC_LEN
    o_cmp, bias = lax.switch(last_block // chunk,
                             [functools.partial(cmp_and_select, n + 1) for n in range(split)])
    bias_s[...] = bias

    kt_diag = s0 // NSA_KTILE
    blocks_per_tile = NSA_KTILE // NSA_SLC_LEN
    vtiles = NSA_KTILE // LANES

    def qk_tile(kt, qaug_ref):
        k0 = pl.multiple_of(kt * NSA_KTILE, NSA_KTILE)
        b8 = bias_s[pl.ds(pl.multiple_of(kt * blocks_per_tile, blocks_per_tile), blocks_per_tile), :]
        b16 = jnp.concatenate([b8, jnp.zeros_like(b8)], axis=0).astype(BF16)
        qaug_ref[HEAD_DIM:HEAD_DIM + 16, :] = jnp.concatenate([b16] * NSA_RPG, axis=1)
        return _dot(ks_ref[pl.ds(k0, NSA_KTILE), :], qaug_ref[...])

    ones_rows = jnp.ones((16, NSA_KTILE), BF16)

    def pv_tile(kt, p):
        vt = jnp.concatenate([vs_ref[kt * vtiles + i] for i in range(vtiles)], axis=1)
        return _dot(jnp.concatenate([vt, ones_rows], axis=0), p)

    def softmax_tile(s, m_old):
        m_new = jnp.maximum(m_old, jnp.max(s, axis=0, keepdims=True))
        p = jnp.exp2((s - m_new).astype(BF16))
        return p, m_new, jnp.exp2(m_old - m_new)

    def visible(kt):
        kpos = kt * NSA_KTILE + lax.broadcasted_iota(jnp.int32, (NSA_KTILE, 1), 0)
        return jnp.concatenate([kpos <= qpos] * NSA_RPG, axis=1)

    def tile_group(i, carry, last):
        m_run, acc, alpha_prev = carry
        for k in range(NSA_UNROLL):
            t = NSA_UNROLL * i + k
            s_cur, p_cur, s_nxt, p_prv, qa = ((sa_s, pa_s, sb_s, pb_s, qaug_s) if k % 2 == 0 else
                                              (sb_s, pb_s, sa_s, pa_s, qaug2_s))
            acc = alpha_prev * acc + pv_tile(jnp.maximum(t - 1, 0), p_prv[...])
            s = jnp.where(visible(t), s_cur[...], NEG) if last else s_cur[...]
            p, m_run, alpha_prev = softmax_tile(s, m_run)
            p_cur[...] = p
            if last and k == NSA_UNROLL - 1:
                acc = alpha_prev * acc + pv_tile(t, p_cur[...])
            else:
                s_nxt[...] = qk_tile(t + 1, qa)
        return m_run, acc, alpha_prev

    qaug2_s[...] = qaug_s[...]
    pb_s[...] = jnp.zeros((NSA_KTILE, nqt), BF16)
    sa_s[...] = qk_tile(0, qaug2_s)

    span = NSA_WIN + NSA_QBLK
    start = pl.multiple_of(jnp.maximum(s0 - NSA_WIN, 0), NSA_QBLK)
    kwin = kw_ref[pl.ds(start, span), :]
    kp = start + lax.broadcasted_iota(jnp.int32, (span, 1), 0)
    wbias = jnp.where((kp <= qpos) & (kp > qpos - NSA_WIN), 0.0, NEG)
    pw, dens = [], []
    for sl in heads:
        s = _dot(kwin, qaug_s[:, sl]) + wbias
        e = jnp.exp2(s - jnp.max(s, axis=0, keepdims=True))
        dens.append(jnp.sum(e, axis=0, keepdims=True))
        pw.append(e.astype(BF16))
    sblk = start // LANES
    vwt = jnp.concatenate([vw_ref[sblk + i] for i in range(span // LANES)], axis=1)
    o_win = _dot(vwt, jnp.concatenate(pw, axis=1)) * (1.0 / jnp.concatenate(dens, axis=1))

    init = (jnp.full((1, nqt), NEG, F32), jnp.zeros((HEAD_DIM + 16, nqt), F32), jnp.ones((1, nqt), F32))
    group_diag = kt_diag // NSA_UNROLL
    carry = lax.fori_loop(0, group_diag, lambda i, c: tile_group(i, c, False), init)
    _, acc, _ = tile_group(group_diag, carry, True)
    o_slc = acc[0:HEAD_DIM] * (1.0 / acc[HEAD_DIM:HEAD_DIM + 1])

    gates = _sigmoid(jnp.concatenate([g_ref[j] for j in range(qtiles)], axis=1))
    outs = [gates[r:r + 1, :] * o_cmp[:, sl] + gates[NSA_RPG + r:NSA_RPG + r + 1, :] * o_slc[:, sl]
            + gates[2 * NSA_RPG + r:2 * NSA_RPG + r + 1, :] * o_win[:, sl] for r, sl in enumerate(heads)]
    o_ref[...] = jnp.concatenate(outs, axis=0).T.astype(o_ref.dtype)


def _nsa_mixer(q_tt, kvc, k_slc, k_win, v_tt, g_tt, bsz, seq, pe, w1, b1, w2, b2):
    nqb = seq // NSA_QBLK
    qtiles = NSA_QBLK // LANES
    ntile = seq // LANES
    npc = seq // NSA_CMP_STRIDE
    ns = seq // NSA_SLC_LEN
    cmp_out = _nsa_compress(kvc, bsz, seq, pe, w1, b1, w2, b2)
    split = _nsa_cmp_split(ns)
    perm = cmp_out.reshape(bsz, 2, NSA_KV, split, ns // split, 4, HEAD_DIM).transpose(0, 1, 2, 3, 5, 4, 6).reshape(
        bsz, 2, NSA_KV, npc, HEAD_DIM)
    kc = perm[:, 0].astype(BF16)
    vc_t = perm[:, 1].transpose(0, 1, 3, 2).astype(BF16)
    gq = NSA_RPG * HEAD_DIM
    return pl.pallas_call(
        _nsa_kernel,
        grid=(bsz, NSA_KV, nqb),
        in_specs=[
            pl.BlockSpec((qtiles, gq, LANES), lambda b, g, q: (b * nqb + q, g, 0)),
            pl.BlockSpec((1, 1, npc, HEAD_DIM), lambda b, g, q: (b, g, 0, 0)),
            pl.BlockSpec((1, 1, HEAD_DIM, npc), lambda b, g, q: (b, g, 0, 0)),
            pl.BlockSpec((seq, NSA_KVW), lambda b, g, q: (b, g)),
            pl.BlockSpec((seq, NSA_KVW), lambda b, g, q: (b, g)),
            pl.BlockSpec((ntile, HEAD_DIM, LANES), lambda b, g, q: (b, g, 0)),
            pl.BlockSpec((ntile, HEAD_DIM, LANES), lambda b, g, q: (b, NSA_KV + g, 0)),
            pl.BlockSpec((qtiles, 16, LANES), lambda b, g, q: (b * nqb + q, g, 0)),
        ],
        out_specs=pl.BlockSpec((NSA_QBLK, gq), lambda b, g, q: (b * nqb + q, g)),
        out_shape=jax.ShapeDtypeStruct((bsz * seq, NSA_Q), BF16),
        scratch_shapes=[pltpu.VMEM((ns, NSA_QBLK), F32),
                        pltpu.VMEM((2 * HEAD_DIM, NSA_RPG * NSA_QBLK), BF16),
                        pltpu.VMEM((2 * HEAD_DIM, NSA_RPG * NSA_QBLK), BF16),
                        pltpu.VMEM((NSA_KTILE, NSA_RPG * NSA_QBLK), F32),
                        pltpu.VMEM((NSA_KTILE, NSA_RPG * NSA_QBLK), F32),
                        pltpu.VMEM((NSA_KTILE, NSA_RPG * NSA_QBLK), BF16),
                        pltpu.VMEM((NSA_KTILE, NSA_RPG * NSA_QBLK), BF16)],
        compiler_params=_cparams("arbitrary", "arbitrary", "arbitrary"),
        name="nsa_attention",
    )(q_tt, kc, vc_t, k_slc, k_win, v_tt, v_tt, g_tt)


def _swa_kernel(q_ref, kp_ref, k0_ref, k1_ref, vp_ref, v0_ref, v1_ref, sink_ref, o_ref):
    step = pl.program_id(1)
    t = SWA_WIN
    krel = lax.broadcasted_iota(jnp.int32, (2 * t, 1), 0) - t
    qrel = lax.broadcasted_iota(jnp.int32, (1, t), 1)
    in_band = (krel <= qrel) & (krel > qrel - SWA_WIN)
    lowest = jnp.where(step > 0, -t, 0)
    mbias = [jnp.where(in_band & (krel >= lowest), 0.0, NEG), jnp.where(in_band, 0.0, NEG)]
    k_blocks = [kp_ref[...], k0_ref[...], k1_ref[...]]
    v_tiles = [vp_ref, v0_ref, v1_ref]
    scores = {}
    for blk in range(2):
        kband = jnp.concatenate([k_blocks[blk], k_blocks[blk + 1]], axis=0)
        for g in range(2):
            rows = slice(g * SWA_RPG * HEAD_DIM, (g + 1) * SWA_RPG * HEAD_DIM)
            qg = q_ref[blk, rows, :]
            qcat = jnp.concatenate([qg[r * HEAD_DIM:(r + 1) * HEAD_DIM, :] for r in range(SWA_RPG)], axis=1)
            zq = jnp.zeros_like(qcat)
            qext = jnp.concatenate([qcat, zq] if g == 0 else [zq, qcat], axis=0)
            scores[blk, g] = _dot(kband, qext)
    for blk in range(2):
        outs = []
        for g in range(2):
            s = scores[blk, g]
            ps, dens = [], []
            for r in range(SWA_RPG):
                h = g * SWA_RPG + r
                sink = sink_ref[h:h + 1, :]
                sr = s[:, r * t:(r + 1) * t] + mbias[blk]
                mx = jnp.maximum(jnp.max(sr, axis=0, keepdims=True), sink)
                e = jnp.exp2(sr - mx)
                dens.append(jnp.sum(e, axis=0, keepdims=True) + jnp.exp2(sink - mx))
                ps.append(e.astype(BF16))
            vband = jnp.concatenate([v_tiles[blk][0, g * HEAD_DIM:(g + 1) * HEAD_DIM, :],
                                     v_tiles[blk + 1][0, g * HEAD_DIM:(g + 1) * HEAD_DIM, :]], axis=1)
            og = _dot(vband, jnp.concatenate(ps, axis=1)) * (1.0 / jnp.concatenate(dens, axis=1))
            outs.append(jnp.concatenate([og[:, r * t:(r + 1) * t] for r in range(SWA_RPG)], axis=0).T)
        o_ref[blk * t:(blk + 1) * t, :] = jnp.concatenate(outs, axis=1).astype(o_ref.dtype)


def _swa_mixer(q_tt, k_nat, v_tt, sinks, bsz, seq):
    nblk = seq // SWA_WIN
    nstep = nblk // 2
    sink_rows = jnp.broadcast_to((sinks.astype(F32) * math.log2(math.e))[:, None], (sinks.shape[0], LANES))
    prev = lambda b, q: b * nblk + jnp.maximum(2 * q - 1, 0)
    cur0 = lambda b, q: b * nblk + 2 * q
    cur1 = lambda b, q: b * nblk + 2 * q + 1
    k_spec = lambda f: pl.BlockSpec((SWA_WIN, SWA_KVW), lambda b, q: (f(b, q), 0))
    v_spec = lambda f: pl.BlockSpec((1, SWA_KVW, LANES), lambda b, q: (f(b, q), 0, 0))
    return pl.pallas_call(
        _swa_kernel,
        grid=(bsz, nstep),
        in_specs=[
            pl.BlockSpec((2, SWA_Q, LANES), lambda b, q: (b * nstep + q, 0, 0)),
            k_spec(prev), k_spec(cur0), k_spec(cur1), v_spec(prev), v_spec(cur0), v_spec(cur1),
            _full(sink_rows.shape),
        ],
        out_specs=pl.BlockSpec((2 * SWA_WIN, SWA_Q), lambda b, q: (b * nstep + q, 0)),
        out_shape=jax.ShapeDtypeStruct((bsz * seq, SWA_Q), BF16),
        compiler_params=_cparams("parallel", "parallel"),
        name="swa_attention",
    )(q_tt, k_nat, k_nat, k_nat, v_tt, v_tt, v_tt, sink_rows)


def _s5_params(a_re, a_im, log_dt, b_re, b_im, c_re, c_im, n_chunks):
    f = F32
    t = S5_CHUNK
    step = jnp.exp(log_dt.astype(f))[:, None]
    lr, li = a_re.astype(f), a_im.astype(f)

    def lam_pow(tau):
        tau = tau.astype(f)[:, None, None]
        mag = jnp.exp(lr * step * tau)
        ang = li * step * tau
        return mag * jnp.cos(ang), mag * jnp.sin(ang)

    lb_r, lb_i = (v[0] for v in lam_pow(jnp.ones((1,))))
    nr, ni = lb_r - 1.0, lb_i
    den = lr * lr + li * li
    fr, fi = (nr * lr + ni * li) / den, (ni * lr - nr * li) / den
    br, bi = b_re.astype(f), b_im.astype(f)
    bb_r = fr[..., None] * br - fi[..., None] * bi
    bb_i = fr[..., None] * bi + fi[..., None] * br
    cr, ci = c_re.astype(f), c_im.astype(f)

    pr, pi = lam_pow(jnp.arange(t + 1))
    cl_r = cr[None] * pr[:, :, None, :] - ci[None] * pi[:, :, None, :]
    cl_i = cr[None] * pi[:, :, None, :] + ci[None] * pr[:, :, None, :]
    kern_t = jnp.einsum("tghp,gpk->gkth", cl_r[:t], bb_r, precision="highest") - jnp.einsum(
        "tghp,gpk->gkth", cl_i[:t], bb_i, precision="highest")
    rr, ri = pr[t - 1 - jnp.arange(t)], pi[t - 1 - jnp.arange(t)]
    bs_r = rr[..., None] * bb_r[None] - ri[..., None] * bb_i[None]
    bs_i = rr[..., None] * bb_i[None] + ri[..., None] * bb_r[None]
    bs = jnp.concatenate([bs_r, bs_i], axis=2)
    bs = bs.transpose(1, 0, 3, 2).reshape(S5_GROUPS, t * S5_GROUP_CH, 2 * S5_STATE)
    cs = jnp.concatenate([cl_r[1:], -cl_i[1:]], axis=3)
    cs = cs.transpose(1, 3, 0, 2).reshape(S5_GROUPS, 2 * S5_STATE, t * S5_GROUP_CH)
    ar, ai = pr[t], pi[t]
    a1, a2 = [], []
    k = 1
    while k < n_chunks:
        a1.append(jnp.concatenate([ar, ar], axis=1))
        a2.append(jnp.concatenate([-ai, ai], axis=1))
        ar, ai = ar * ar - ai * ai, 2.0 * ar * ai
        k *= 2
    a1 = jnp.stack(a1, axis=1)
    a2 = jnp.stack(a2, axis=1)
    noct = S5_GROUPS // S5_OCT
    hc, ns2 = S5_GROUP_CH, 2 * S5_STATE
    width = t * hc
    lanes = t * LANES
    r_i, c_i = np.arange(width)[:, None], np.arange(lanes)[None, :]
    rep = jnp.asarray((r_i // hc == c_i // LANES) & (r_i % hc == c_i % hc), dtype=BF16)
    kern_o = kern_t.reshape(noct, LANES, width).astype(BF16)
    toep_o = pl.pallas_call(
        _s5_toeplitz_kernel,
        grid=(noct, t // 8),
        in_specs=[pl.BlockSpec((1, LANES, width), lambda o, s: (o, 0, 0)), _full((width, lanes))],
        out_specs=pl.BlockSpec((1, 8 * LANES, lanes), lambda o, s: (o, s, 0)),
        out_shape=jax.ShapeDtypeStruct((noct, lanes, lanes), BF16),
        scratch_shapes=[pltpu.VMEM((t, LANES, LANES), F32)],
        compiler_params=_cparams("parallel", "arbitrary"),
        name="s5_toeplitz_table",
    )(kern_o, rep)
    cs_o = pl.pallas_call(
        _s5_readout_kernel,
        grid=(noct,),
        in_specs=[pl.BlockSpec((1, S5_OCT * ns2, width), lambda o: (o, 0, 0)), _full((width, lanes))],
        out_specs=pl.BlockSpec((1, S5_OCT * ns2, lanes), lambda o: (o, 0, 0)),
        out_shape=jax.ShapeDtypeStruct((noct, S5_OCT * ns2, lanes), BF16),
        compiler_params=_cparams("parallel"),
        name="s5_readout_table",
    )(cs.reshape(noct, S5_OCT * ns2, width).astype(BF16), rep)
    bs_slabs = bs.reshape(noct, S5_OCT, t, hc, ns2).transpose(0, 2, 1, 3, 4).reshape(noct, t, LANES, ns2).astype(BF16)
    bs_o = pl.pallas_call(
        _s5_state_in_kernel,
        grid=(noct,),
        in_specs=[pl.BlockSpec((1, t, LANES, ns2), lambda o: (o, 0, 0, 0))],
        out_specs=pl.BlockSpec((1, lanes, S5_OCT * ns2), lambda o: (o, 0, 0)),
        out_shape=jax.ShapeDtypeStruct((noct, lanes, S5_OCT * ns2), BF16),
        compiler_params=_cparams("parallel"),
        name="s5_state_in_table",
    )(bs_slabs)

    def oct_rows(a):
        return a.reshape(noct, S5_OCT, -1, ns2).transpose(0, 2, 1, 3).reshape(noct, -1, S5_OCT * ns2)

    return toep_o, bs_o, cs_o, oct_rows(a1), oct_rows(a2)


def _same_group(shape, row_div, col_mod, col_div):
    r = lax.broadcasted_iota(jnp.int32, shape, 0)
    c = lax.broadcasted_iota(jnp.int32, shape, 1)
    return (r // row_div) == ((c % col_mod) // col_div)


def _s5_toeplitz_kernel(k_ref, rep_ref, o_ref, full_s):
    nlag = full_s.shape[0]
    steps_here = o_ref.shape[1] // LANES

    @pl.when(pl.program_id(1) == 0)
    def _():
        full = _dot(k_ref[0], rep_ref[...])
        full = jnp.where(_same_group(full.shape, S5_GROUP_CH, LANES, S5_GROUP_CH), full, 0.0)
        for lag in range(nlag):
            full_s[lag] = full[:, lag * LANES:(lag + 1) * LANES]

    for i in range(steps_here):
        s = pl.program_id(1) * steps_here + i
        for t in range(nlag):
            tile = full_s[jnp.maximum(t - s, 0)]
            o_ref[0, i * LANES:(i + 1) * LANES, t * LANES:(t + 1) * LANES] = (
                jnp.where(t >= s, tile, 0.0).astype(o_ref.dtype))


def _s5_readout_kernel(c_ref, rep_ref, o_ref):
    full = _dot(c_ref[0], rep_ref[...])
    keep = _same_group(full.shape, 2 * S5_STATE, LANES, S5_GROUP_CH)
    o_ref[0] = jnp.where(keep, full, 0.0).astype(o_ref.dtype)


def _s5_state_in_kernel(b_ref, o_ref):
    keep = _same_group((LANES, o_ref.shape[2]), S5_GROUP_CH, o_ref.shape[2], 2 * S5_STATE)
    for s in range(b_ref.shape[1]):
        full = jnp.concatenate([b_ref[0, s]] * S5_OCT, axis=1)
        o_ref[0, s * LANES:(s + 1) * LANES, :] = jnp.where(keep, full, jnp.zeros((), full.dtype))


def _s5_state_kernel(bsz, u_ref, bs_ref, a1_ref, a2_ref, hi_ref, lo_ref):
    sc = _dot(u_ref[...], bs_ref[0])
    n = sc.shape[0] // bsz
    width = sc.shape[1]
    rowi = lax.broadcasted_iota(jnp.int32, (n, width), 0)

    def swap_re_im(x):
        return jnp.concatenate([pltpu.roll(x[:, j * LANES:(j + 1) * LANES], S5_STATE, 1)
                                for j in range(width // LANES)], axis=1)

    h_in = []
    for b in range(bsz):
        x = sc[b * n:(b + 1) * n]
        k, step = 1, 0
        while k < n:
            xs = jnp.where(rowi >= k, pltpu.roll(x, k, 0), 0.0)
            x = x + a1_ref[0, step:step + 1, :] * xs + a2_ref[0, step:step + 1, :] * swap_re_im(xs)
            k *= 2
            step += 1
        h_in.append(jnp.where(rowi >= 1, pltpu.roll(x, 1, 0), 0.0))
    h_in = jnp.concatenate(h_in, axis=0)
    hi = h_in.astype(BF16)
    hi_ref[0] = hi
    lo_ref[0] = (h_in - hi.astype(F32)).astype(BF16)


def _s5_out_kernel(u_ref, toep_ref, hi_ref, lo_ref, cs_ref, o_ref):
    n = pl.program_id(1)
    cols = toep_ref.shape[2]
    off = _dot(hi_ref[0], cs_ref[0]) + _dot(lo_ref[0], cs_ref[0])
    for nn in range(toep_ref.shape[1] // cols):
        @pl.when(n == nn)
        def _(nn=nn):
            k = (nn + 1) * cols
            y = off + _dot(u_ref[:, 0:k], toep_ref[0, 0:k, :])
            for t8 in range(cols // LANES):
                o_ref[:, t8, :] = y[:, t8 * LANES:(t8 + 1) * LANES].astype(o_ref.dtype)


def _s5_glu_kernel(y_ref, u_ref, d_ref, w_ref, b_ref, o_ref):
    y = _gelu_tanh(y_ref[...] + d_ref[...] * u_ref[...])
    gate = _sigmoid(_dot(y.astype(BF16), w_ref[...]) + b_ref[...])
    o_ref[...] = (y * gate).astype(o_ref.dtype)


def _s5_mixer(u5, u_chunks, bsz, seq, a_re, a_im, log_dt, b_re, b_im, c_re, c_im, d_skip, glu_w, glu_b, tm=1024):
    m = u5.shape[0]
    t = S5_CHUNK
    nch = m // t
    noct = S5_GROUPS // S5_OCT
    lanes = t * LANES
    sw = S5_OCT * 2 * S5_STATE
    toep, bs, cs, a1, a2 = _s5_params(a_re, a_im, log_dt, b_re, b_im, c_re, c_im, nch // bsz)
    nsteps = a1.shape[1]
    h_hi, h_lo = pl.pallas_call(
        functools.partial(_s5_state_kernel, bsz),
        grid=(noct,),
        in_specs=[
            pl.BlockSpec((nch, lanes), lambda o: (0, o)),
            pl.BlockSpec((1, lanes, sw), lambda o: (o, 0, 0)),
            pl.BlockSpec((1, nsteps, sw), lambda o: (o, 0, 0)),
            pl.BlockSpec((1, nsteps, sw), lambda o: (o, 0, 0)),
        ],
        out_specs=[pl.BlockSpec((1, nch, sw), lambda o: (o, 0, 0))] * 2,
        out_shape=[jax.ShapeDtypeStruct((noct, nch, sw), BF16)] * 2,
        compiler_params=_cparams("parallel"),
        name="s5_state",
    )(u_chunks, bs, a1, a2)
    tsub = 8
    rows = nch // 2
    y = pl.pallas_call(
        _s5_out_kernel,
        grid=(noct, t // tsub, nch // rows),
        in_specs=[
            pl.BlockSpec((rows, lanes), lambda o, n, r: (r, o)),
            pl.BlockSpec((1, lanes, tsub * LANES), lambda o, n, r: (o, 0, n)),
            pl.BlockSpec((1, rows, sw), lambda o, n, r: (o, r, 0)),
            pl.BlockSpec((1, rows, sw), lambda o, n, r: (o, r, 0)),
            pl.BlockSpec((1, sw, tsub * LANES), lambda o, n, r: (o, 0, n)),
        ],
        out_specs=pl.BlockSpec((rows, tsub, LANES), lambda o, n, r: (r, n, o)),
        out_shape=jax.ShapeDtypeStruct((nch, t, S5_CH), BF16),
        compiler_params=_cparams("parallel", "parallel", "parallel"),
        name="s5_scan",
    )(u_chunks, toep, h_hi, h_lo, cs).reshape(m, S5_CH)
    return pl.pallas_call(
        _s5_glu_kernel,
        grid=(m // tm,),
        in_specs=[
            pl.BlockSpec((tm, S5_CH), lambda i: (i, 0)),
            pl.BlockSpec((tm, S5_CH), lambda i: (i, 0)),
            _full((1, S5_CH)), _full((S5_CH, S5_CH)), _full((1, S5_CH)),
        ],
        out_specs=pl.BlockSpec((tm, S5_CH), lambda i: (i, 0)),
        out_shape=jax.ShapeDtypeStruct((m, S5_CH), BF16),
        compiler_params=_cparams("parallel"),
        name="s5_glu",
    )(y, u5, d_skip.reshape(1, S5_CH).astype(F32), glu_w.astype(BF16), glu_b.reshape(1, S5_CH).astype(F32))


def _even_mixers(h, bsz, seq, g_mix, w_in, conv_w, conv_b, dt_bias, a_log, d_skip, norm_g, pe, w1, b1, w2, b2):
    d = h.shape[1]
    scale = HEAD_DIM ** -0.5 * math.log2(math.e)
    o = SSD_IN
    w_ssd = jnp.concatenate([w_in[:, :SSD_IN], jnp.zeros((d, LANES - SSD_HEADS), w_in.dtype)], axis=1)
    w_q = w_in[:, o:o + NSA_Q] * scale
    kv = [w_in[:, o + NSA_Q + i * NSA_KVW:o + NSA_Q + (i + 1) * NSA_KVW] for i in range(6)]
    w_gate = w_in[:, o + NSA_Q + 6 * NSA_KVW:].reshape(d, NSA_KV, NSA_RPG, 3).transpose(0, 1, 3, 2)
    w_gate = jnp.concatenate([w_gate.reshape(d, NSA_KV, 12), jnp.zeros((d, NSA_KV, 4), w_in.dtype)],
                             axis=2).reshape(d, NSA_KV * 16)

    def per_group_halves(w):
        wg = w.reshape(d, NSA_KV, HEAD_DIM)
        return jnp.concatenate([wg, jnp.zeros_like(wg)], axis=2).reshape(d, NSA_KV * NSA_KVW)
    segs = [
        ("nat", w_ssd, F32),
        ("tt", w_q, BF16),
        ("pc", jnp.concatenate([kv[0], kv[1]], axis=1), F32),
        ("nat+", per_group_halves(kv[2]), BF16, _nsa_block_onehot(NSA_KTILE)),
        ("nat", per_group_halves(kv[4]), BF16),
        ("tt", jnp.concatenate([kv[3], kv[5]], axis=1), BF16),
        ("tt", w_gate, F32),
    ]
    u_ssd, q_tt, kvc, k_slc, k_win, v_tt, g_tt = _norm_proj(h, g_mix, segs)
    ya = _ssd_mixer(u_ssd, bsz, seq, conv_w, conv_b, dt_bias, a_log, d_skip, norm_g)
    yb = _nsa_mixer(q_tt, kvc, k_slc, k_win, v_tt, g_tt, bsz, seq, pe, w1, b1, w2, b2)
    return ya, yb


def _odd_mixers(h, bsz, seq, g_mix, w_in, sinks, a_re, a_im, log_dt, b_re, b_im, c_re, c_im, d_skip, glu_w, glu_b):
    scale = HEAD_DIM ** -0.5 * math.log2(math.e)
    segs = [
        ("tt", w_in[:, :SWA_Q] * scale, BF16),
        ("nat", w_in[:, SWA_Q:SWA_Q + SWA_KVW], BF16),
        ("tt", w_in[:, SWA_Q + SWA_KVW:SWA_Q + 2 * SWA_KVW], BF16),
        ("nat+ch", w_in[:, SWA_Q + 2 * SWA_KVW:], F32),
    ]
    q_tt, k_nat, v_tt, u5, u_chunks = _norm_proj(h, g_mix, segs)
    yc = _swa_mixer(q_tt, k_nat, v_tt, sinks, bsz, seq)
    yd = _s5_mixer(u5, u_chunks, bsz, seq, a_re, a_im, log_dt, b_re, b_im, c_re, c_im, d_skip, glu_w, glu_b)
    return yc, yd


def kernel(x, norm_mix, norm_mlp, norm_final, mlp_w_up, mlp_w_down, ev_w_in, ev_w_out, ssd_conv_w, ssd_conv_b,
           ssd_dt_bias, ssd_a_log, ssd_d, ssd_norm, nsa_pe, nsa_cmp_w1, nsa_cmp_b1, nsa_cmp_w2, nsa_cmp_b2,
           od_w_in, od_w_out, swa_sinks, s5_a_re, s5_a_im, s5_log_dt, s5_b_re, s5_b_im, s5_c_re, s5_c_im,
           s5_d, s5_glu_w, s5_glu_b):
    bsz, seq, d = x.shape
    depth = norm_mix.shape[0]
    assert seq % (NSA_UNROLL * NSA_KTILE) == 0 and seq >= NSA_WIN + NSA_QBLK
    h = x.reshape(bsz * seq, d)
    for layer in range(depth):
        i = layer // 2
        if layer % 2 == 0:
            ya, yb = _even_mixers(h, bsz, seq, norm_mix[layer], ev_w_in[i], ssd_conv_w[i], ssd_conv_b[i],
                                  ssd_dt_bias[i], ssd_a_log[i], ssd_d[i], ssd_norm[i], nsa_pe[i],
                                  nsa_cmp_w1[i], nsa_cmp_b1[i], nsa_cmp_w2[i], nsa_cmp_b2[i])
            w_out = ev_w_out[i]
        else:
            ya, yb = _odd_mixers(h, bsz, seq, norm_mix[layer], od_w_in[i], swa_sinks[i], s5_a_re[i], s5_a_im[i],
                                 s5_log_dt[i], s5_b_re[i], s5_b_im[i], s5_c_re[i], s5_c_im[i], s5_d[i],
                                 s5_glu_w[i], s5_glu_b[i])
            w_out = od_w_out[i]
        h = _mix_out_mlp(h, ya, yb, w_out, norm_mlp[layer], mlp_w_up[layer], mlp_w_down[layer], norm_final,
                         final=(layer == depth - 1))
    return h.reshape(bsz, seq, d)
```

```python
import functools
import math

import jax
import jax.numpy as jnp
import numpy as np
from jax import lax
from jax.experimental import pallas as pl
from jax.experimental.pallas import tpu as pltpu

F32 = jnp.float32
BF16 = jnp.bfloat16

EPS = 1e-6
NEG = -1e30
FORCE = 1e9
HEAD_DIM = 64
LANES = 128
VMEM_LIMIT_BYTES = 56 * 1024 * 1024

SSD_HEADS = 8
SSD_INNER = 512
SSD_GROUPS = 2
SSD_STATE = 128
SSD_CONV = 4
SSD_CHUNK = 128
SSD_CONV_DIM = 1024
SSD_IN = SSD_INNER + SSD_CONV_DIM + SSD_HEADS

NSA_HEADS = 8
NSA_KV = 2
NSA_RPG = 4
NSA_CMP_LEN = 32
NSA_CMP_STRIDE = 16
NSA_SLC_LEN = 64
NSA_TOPK = 16
NSA_WIN = 512
NSA_CMP_HIDDEN = 256
NSA_QBLK = 128
NSA_Q = 512
NSA_KVW = 128
NSA_KTILE = 512
NSA_UNROLL = 2

SWA_RPG = 4
SWA_WIN = 128
SWA_Q = 512
SWA_KVW = 128

S5_CH = 512
S5_GROUP_CH = 16
S5_GROUPS = 32
S5_STATE = 64
S5_CHUNK = 32
S5_OCT = LANES // S5_GROUP_CH


def _cparams(*sem):
    return pltpu.CompilerParams(dimension_semantics=sem, vmem_limit_bytes=VMEM_LIMIT_BYTES)


def _full(shape):
    n = len(shape)
    return pl.BlockSpec(shape, lambda *_: (0,) * n)


def _dot(a, b):
    return jnp.dot(a, b, preferred_element_type=F32)


def _dot_nt(a, b):
    return lax.dot_general(a, b, (((1,), (1,)), ((), ())), preferred_element_type=F32)


def _split3(a):
    hi = a.astype(BF16)
    r1 = a - hi.astype(F32)
    mid = r1.astype(BF16)
    lo = (r1 - mid.astype(F32)).astype(BF16)
    return hi, mid, lo


def _dot_exact_rhs(a, b_exact):
    hi, mid, lo = _split3(a)
    return _dot(hi, b_exact) + _dot(mid, b_exact) + _dot(lo, b_exact)


def _dot_exact_lhs(a_exact, b):
    hi, mid, lo = _split3(b)
    return _dot(a_exact, hi) + _dot(a_exact, mid) + _dot(a_exact, lo)


def _rms(x, g):
    return x * lax.rsqrt(jnp.mean(x * x, axis=-1, keepdims=True) + EPS) * g


def _gelu_tanh(x):
    c = math.sqrt(2.0 / math.pi)
    return 0.5 * x * (1.0 + jnp.tanh(c * (x + 0.044715 * (x * x * x))))


def _sigmoid(x):
    return 1.0 / (1.0 + jnp.exp(-x))


def _proj_kernel(kinds, tm, h_ref, g_ref, *refs):
    n = len(kinds)
    n_add = sum(k == "nat+" for k in kinds)
    n_out = n + sum(k == "nat+ch" for k in kinds)
    w_refs, add_refs = refs[:n], list(refs[n:n + n_add])
    o_refs, scratch = list(refs[n + n_add:n + n_add + n_out]), list(refs[n + n_add + n_out:])
    yb = _rms(h_ref[...], g_ref[...]).astype(BF16)
    for kind, w_ref in zip(kinds, w_refs):
        o_ref = o_refs.pop(0)
        if kind == "nat":
            o_ref[...] = _dot(yb, w_ref[...]).astype(o_ref.dtype)
        elif kind == "nat+ch":
            ch_ref, ch_s = o_refs.pop(0), scratch.pop(0)
            res = _dot(yb, w_ref[...])
            o_ref[...] = res.astype(o_ref.dtype)
            for j in range(ch_s.shape[0]):
                ch_s[j] = res[:, j * LANES:(j + 1) * LANES]
            for t in range(S5_CHUNK):
                for j in range(ch_s.shape[0]):
                    col = (j * S5_CHUNK + t) * LANES
                    ch_ref[:, col:col + LANES] = ch_s[j, pl.ds(t, tm // S5_CHUNK, stride=S5_CHUNK), :].astype(ch_ref.dtype)
        elif kind == "nat+":
            o_ref[...] = _dot(yb, w_ref[...]).astype(o_ref.dtype) + add_refs.pop(0)[...]
        elif kind == "pc":
            pc_s = scratch.pop(0)
            res = _dot(yb, w_ref[...])
            flat = NSA_CMP_STRIDE * HEAD_DIM
            per_tile = LANES // HEAD_DIM
            for j in range(pc_s.shape[0]):
                pc_s[j] = res[:, j * LANES:(j + 1) * LANES]
            for p in range(NSA_CMP_STRIDE):
                for j in range(pc_s.shape[0]):
                    tok = pc_s[j, pl.ds(p, tm // NSA_CMP_STRIDE, stride=NSA_CMP_STRIDE), :]
                    for c in range(per_tile):
                        col = (j * per_tile + c) * flat + p * HEAD_DIM
                        o_ref[:, col:col + HEAD_DIM] = tok[:, c * HEAD_DIM:(c + 1) * HEAD_DIM].astype(o_ref.dtype)
        else:
            ot = _dot_nt(w_ref[...], yb)
            for j in range(tm // LANES):
                o_ref[j] = ot[:, j * LANES:(j + 1) * LANES].astype(o_ref.dtype)


def _norm_proj(h, g, segs, tm=512):
    m, d = h.shape
    kinds = tuple(s[0] for s in segs)
    adds = [s[3] for s in segs if s[0] == "nat+"]
    ws, w_specs, out_shapes, out_specs, scratch = [], [], [], [], []
    for kind, w, dt in (s[:3] for s in segs):
        n_out = w.shape[1]
        if kind == "pc":
            ws.append(w.astype(BF16))
            w_specs.append(_full((d, n_out)))
            out_shapes.append(jax.ShapeDtypeStruct((m // NSA_CMP_STRIDE, NSA_CMP_STRIDE * n_out), dt))
            out_specs.append(pl.BlockSpec((tm // NSA_CMP_STRIDE, NSA_CMP_STRIDE * n_out), lambda i: (i, 0)))
            scratch.append(pltpu.VMEM((n_out // LANES, tm, LANES), F32))
        elif kind in ("nat", "nat+", "nat+ch"):
            ws.append(w.astype(BF16))
            w_specs.append(_full((d, n_out)))
            out_shapes.append(jax.ShapeDtypeStruct((m, n_out), dt))
            out_specs.append(pl.BlockSpec((tm, n_out), lambda i: (i, 0)))
            if kind == "nat+ch":
                out_shapes.append(jax.ShapeDtypeStruct((m // S5_CHUNK, S5_CHUNK * n_out), BF16))
                out_specs.append(pl.BlockSpec((tm // S5_CHUNK, S5_CHUNK * n_out), lambda i: (i, 0)))
                scratch.append(pltpu.VMEM((n_out // LANES, tm, LANES), F32))
        else:
            ws.append(w.T.astype(BF16))
            w_specs.append(_full((n_out, d)))
            out_shapes.append(jax.ShapeDtypeStruct((m // LANES, n_out, LANES), dt))
            out_specs.append(pl.BlockSpec((tm // LANES, n_out, LANES), lambda i: (i, 0, 0)))
    return pl.pallas_call(
        functools.partial(_proj_kernel, kinds, tm),
        grid=(m // tm,),
        in_specs=[pl.BlockSpec((tm, d), lambda i: (i, 0)), _full((1, d))] + w_specs + [
            pl.BlockSpec((tm, a.shape[1]), functools.partial(lambda i, nb: (i % nb, 0), nb=a.shape[0] // tm))
            for a in adds],
        out_specs=out_specs,
        out_shape=out_shapes,
        scratch_shapes=scratch,
        compiler_params=_cparams("parallel"),
        name="norm_proj",
    )(h, g.reshape(1, d), *ws, *adds)


def _mlp_kernel(final, h_ref, ya_ref, yb_ref, woa_ref, wob_ref, gm_ref, wup_ref, wdn_ref, gf_ref,
                o_ref, h2_s, xn_s, acc_s):
    j = pl.program_id(1)

    @pl.when(j == 0)
    def _():
        h2 = h_ref[...] + _dot(ya_ref[...], woa_ref[...]) + _dot(yb_ref[...], wob_ref[...])
        h2_s[...] = h2
        xn_s[...] = _rms(h2, gm_ref[...]).astype(BF16)
        acc_s[...] = jnp.zeros_like(acc_s)

    hid = jnp.square(jnp.maximum(_dot(xn_s[...], wup_ref[...]), 0.0))
    acc_s[...] += _dot(hid.astype(BF16), wdn_ref[...])

    @pl.when(j == pl.num_programs(1) - 1)
    def _():
        out = h2_s[...] + acc_s[...]
        if final:
            out = _rms(out, gf_ref[...])
        o_ref[...] = out


def _mix_out_mlp(h, ya, yb, w_out, g_mlp, w_up, w_down, g_final, final, tm=1024, tf=1024):
    m, d = h.shape
    dff = w_up.shape[1]
    na = ya.shape[1]
    nb = yb.shape[1]
    return pl.pallas_call(
        functools.partial(_mlp_kernel, final),
        grid=(m // tm, dff // tf),
        in_specs=[
            pl.BlockSpec((tm, d), lambda i, j: (i, 0)),
            pl.BlockSpec((tm, na), lambda i, j: (i, 0)),
            pl.BlockSpec((tm, nb), lambda i, j: (i, 0)),
            _full((na, d)), _full((nb, d)), _full((1, d)),
            pl.BlockSpec((d, tf), lambda i, j: (0, j)),
            pl.BlockSpec((tf, d), lambda i, j: (j, 0)),
            _full((1, d)),
        ],
        out_specs=pl.BlockSpec((tm, d), lambda i, j: (i, 0)),
        out_shape=jax.ShapeDtypeStruct((m, d), F32),
        scratch_shapes=[pltpu.VMEM((tm, d), F32), pltpu.VMEM((tm, d), BF16), pltpu.VMEM((tm, d), F32)],
        compiler_params=_cparams("parallel", "arbitrary"),
        name="out_proj_mlp",
    )(h, ya, yb, w_out[:na].astype(BF16), w_out[na:].astype(BF16), g_mlp.reshape(1, d),
      w_up.astype(BF16), w_down.astype(BF16), g_final.reshape(1, d))


def _ssd_kernel(u_ref, cw_ref, cb_ref, dtb_ref, alog_ref, dsk_ref, ng_ref, o_ref, xext_s, st_s):
    t = SSD_CHUNK
    c = pl.program_id(1)

    @pl.when(c == 0)
    def _():
        xext_s[0:8, :] = jnp.zeros((8, SSD_CONV_DIM), F32)
        st_s[...] = jnp.zeros_like(st_s)

    z = u_ref[:, 0:SSD_INNER]
    dt_raw = u_ref[:, SSD_INNER + SSD_CONV_DIM:]
    xext_s[8:8 + t, :] = u_ref[:, SSD_INNER:SSD_INNER + SSD_CONV_DIM]
    xfull = xext_s[...]
    conv = cb_ref[...] + cw_ref[SSD_CONV - 1:SSD_CONV, :] * xfull[8:8 + t]
    for back in range(1, SSD_CONV):
        k = SSD_CONV - 1 - back
        conv = conv + cw_ref[k:k + 1, :] * pltpu.roll(xfull, back, 0)[8:8 + t]
    xext_s[0:8, :] = xext_s[t:t + 8, :]
    xc = conv * _sigmoid(conv)
    xs = xc[:, 0:SSD_INNER]
    gn = SSD_GROUPS * SSD_STATE

    dtp = dt_raw + dtb_ref[...]
    dt = jnp.maximum(dtp, 0.0) + jnp.log1p(jnp.exp(-jnp.abs(dtp)))
    a = -jnp.exp(alog_ref[...])
    da = dt * a

    row = lax.broadcasted_iota(jnp.int32, (t, t), 0)
    col = lax.broadcasted_iota(jnp.int32, (t, t), 1)
    causal = col <= row
    tril = jnp.where(causal, 1.0, 0.0).astype(BF16)
    a_cum = _dot_exact_lhs(tril, da)
    er = lax.broadcasted_iota(jnp.int32, (LANES, SSD_INNER), 0)
    ec = lax.broadcasted_iota(jnp.int32, (LANES, SSD_INNER), 1)
    expand = jnp.where((ec >> 6) == er, 1.0, 0.0).astype(BF16)
    a_cum_x = _dot_exact_rhs(a_cum, expand)
    dt_x = _dot_exact_rhs(dt, expand)
    a_cum_t = a_cum.T
    a_last_x = a_cum_x[t - 1:t, :]
    decay_end_x = jnp.exp(a_last_x - a_cum_x)
    decay_in_x = jnp.exp(a_cum_x)
    chunk_decay_x = jnp.exp(a_last_x)

    xd = xs * dt_x
    xd_end = (xd * decay_end_x).astype(BF16)
    xd_b = xd.astype(BF16)
    lane = lax.broadcasted_iota(jnp.int32, (t, LANES), 1)
    first_half = lane < HEAD_DIM

    pieces = []
    for g in range(SSD_GROUPS):
        bm = xc[:, SSD_INNER + g * SSD_STATE:SSD_INNER + (g + 1) * SSD_STATE]
        cm = xc[:, SSD_INNER + gn + g * SSD_STATE:SSD_INNER + gn + (g + 1) * SSD_STATE].astype(BF16)
        bm_t = bm.T.astype(BF16)
        cb = _dot_nt(cm, bm.astype(BF16))
        for pr in range(2):
            i = g * 2 + pr
            sl = slice(i * LANES, (i + 1) * LANES)
            ms = []
            for hh in range(2):
                h = 2 * i + hh
                seg = a_cum[:, h:h + 1] - a_cum_t[h:h + 1, :]
                dec = jnp.exp(jnp.where(causal, seg, NEG))
                ms.append((cb * dec).astype(BF16))
            y_diag = jnp.where(first_half, _dot(ms[0], xd_b[:, sl]), _dot(ms[1], xd_b[:, sl]))
            st = st_s[i]
            y_off = _dot(cm, st.astype(BF16)) * decay_in_x[:, sl]
            st_s[i] = st * chunk_decay_x[:, sl] + _dot(bm_t, xd_end[:, sl])
            pieces.append(y_diag + y_off)
    y = jnp.concatenate(pieces, axis=1) + xs * dsk_ref[...]
    y = y * (z * _sigmoid(z))
    half = SSD_INNER // SSD_GROUPS
    outs = [_rms(y[:, g * half:(g + 1) * half], ng_ref[:, g * half:(g + 1) * half]) for g in range(SSD_GROUPS)]
    o_ref[...] = jnp.concatenate(outs, axis=1).astype(o_ref.dtype)


def _ssd_mixer(u_ssd, bsz, seq, conv_w, conv_b, dt_bias, a_log, d_skip, norm_g):
    m, width = u_ssd.shape
    nch = seq // SSD_CHUNK
    pad = LANES - SSD_HEADS

    def padded(v):
        return jnp.concatenate([v.astype(F32), jnp.zeros((pad,), F32)]).reshape(1, LANES)

    return pl.pallas_call(
        _ssd_kernel,
        grid=(bsz, nch),
        in_specs=[
            pl.BlockSpec((SSD_CHUNK, width), lambda b, c: (b * nch + c, 0)),
            _full((SSD_CONV, SSD_CONV_DIM)), _full((1, SSD_CONV_DIM)),
            _full((1, LANES)), _full((1, LANES)), _full((1, SSD_INNER)), _full((1, SSD_INNER)),
        ],
        out_specs=pl.BlockSpec((SSD_CHUNK, SSD_INNER), lambda b, c: (b * nch + c, 0)),
        out_shape=jax.ShapeDtypeStruct((m, SSD_INNER), BF16),
        scratch_shapes=[pltpu.VMEM((SSD_CHUNK + 8, SSD_CONV_DIM), F32),
                        pltpu.VMEM((SSD_HEADS // 2, SSD_STATE, LANES), F32)],
        compiler_params=_cparams("arbitrary", "arbitrary"),
        name="ssd_mixer",
    )(u_ssd, conv_w.astype(F32), conv_b.reshape(1, -1).astype(F32), padded(dt_bias), padded(a_log),
      jnp.repeat(d_skip.astype(F32), HEAD_DIM).reshape(1, SSD_INNER), norm_g.reshape(1, SSD_INNER).astype(F32))


def _nsa_compress_kernel(x_ref, pe_ref, w1_ref, b1_ref, w2_ref, b2_ref, o_ref):
    x = x_ref[...]
    npc = x.shape[0]
    half = NSA_CMP_STRIDE * HEAD_DIM
    top = _dot((x + pe_ref[0, :, 0:half]).astype(BF16), w1_ref[0, 0:half, :])
    bot = _dot((x + pe_ref[0, :, half:]).astype(BF16), w1_ref[0, half:, :])
    pre = top + pltpu.roll(bot, npc - 1, 0) + b1_ref[0]
    hid = _gelu_tanh(pre).astype(BF16)
    out = _dot(hid, w2_ref[0]) + b2_ref[0]
    rowi = lax.broadcasted_iota(jnp.int32, out.shape, 0)
    o_ref[0, 0, 0] = jnp.where(rowi < npc - 1, out, 0.0)


def _nsa_compress(kvc_pieces, bsz, seq, pe, w1, b1, w2, b2):
    npc = seq // NSA_CMP_STRIDE
    flat = NSA_CMP_STRIDE * HEAD_DIM
    x = kvc_pieces
    return pl.pallas_call(
        _nsa_compress_kernel,
        grid=(bsz, 2, NSA_KV),
        in_specs=[
            pl.BlockSpec((npc, flat), lambda b, s, g: (b, s * NSA_KV + g)),
            pl.BlockSpec((1, 1, 2 * flat), lambda b, s, g: (s, 0, 0)),
            pl.BlockSpec((1, 2 * flat, NSA_CMP_HIDDEN), lambda b, s, g: (s, 0, 0)),
            pl.BlockSpec((1, 1, NSA_CMP_HIDDEN), lambda b, s, g: (s, 0, 0)),
            pl.BlockSpec((1, NSA_CMP_HIDDEN, HEAD_DIM), lambda b, s, g: (s, 0, 0)),
            pl.BlockSpec((1, 1, HEAD_DIM), lambda b, s, g: (s, 0, 0)),
        ],
        out_specs=pl.BlockSpec((1, 1, 1, npc, HEAD_DIM), lambda b, s, g: (b, s, g, 0, 0)),
        out_shape=jax.ShapeDtypeStruct((bsz, 2, NSA_KV, npc, HEAD_DIM), F32),
        compiler_params=_cparams("parallel", "parallel", "parallel"),
        name="nsa_compress",
    )(x, pe.reshape(2, 1, 2 * flat).astype(F32), w1.astype(BF16), b1.reshape(2, 1, -1).astype(F32),
      w2.astype(BF16), b2.reshape(2, 1, -1).astype(F32))


def _nsa_cmp_split(ns):
    return max(1, min(4, (4 * ns) // LANES))


def _nsa_block_onehot(rows):
    r = lax.broadcasted_iota(jnp.int32, (rows, 2 * NSA_KVW), 0)
    c = lax.broadcasted_iota(jnp.int32, (rows, 2 * NSA_KVW), 1)
    blk = (r % NSA_KTILE) // NSA_SLC_LEN
    return jnp.where((c % NSA_KVW) == HEAD_DIM + blk, 1.0, 0.0).astype(BF16)


def _nsa_kernel(q_ref, kc_ref, vc_ref, ks_ref, kw_ref, vs_ref, vw_ref, g_ref, o_ref, *scratch):
    groups = range(NSA_KV)
    per = len(scratch) // NSA_KV
    bias_s, qaug_s, qaug2_s, sa_s, sb_s, pa_s, pb_s = (
        tuple(scratch[g * per + i] for g in groups) for i in range(per))
    qb = pl.program_id(1)
    nqt = NSA_RPG * NSA_QBLK
    s0 = qb * NSA_QBLK
    ncp = kc_ref.shape[2]
    ns = ncp // 4
    heads = [slice(r * NSA_QBLK, (r + 1) * NSA_QBLK) for r in range(NSA_RPG)]
    klanes = [slice(g * NSA_KVW, (g + 1) * NSA_KVW) for g in groups]
    vrows = [slice(g * HEAD_DIM, (g + 1) * HEAD_DIM) for g in groups]

    qtiles = NSA_QBLK // LANES
    qcat = [jnp.concatenate([q_ref[j, (g * NSA_RPG + r) * HEAD_DIM:(g * NSA_RPG + r + 1) * HEAD_DIM, :]
                             for r in range(NSA_RPG) for j in range(qtiles)], axis=1) for g in groups]
    qpos = s0 + lax.broadcasted_iota(jnp.int32, (1, NSA_QBLK), 1)
    for g in groups:
        qaug_s[g][0:HEAD_DIM, :] = qcat[g]
        qaug_s[g][HEAD_DIM:, :] = jnp.zeros((HEAD_DIM, nqt), BF16)

    split = _nsa_cmp_split(ns)
    chunk = ns // split
    cvalid = jnp.where(qpos >= NSA_CMP_LEN - 1, 1.0, 0.0)
    cur = qpos // NSA_SLC_LEN
    taken = -3.0e38

    def cmp_and_select(nchunks):
        out = []
        for g in groups:
            out.extend(cmp_and_select_group(g, nchunks))
        return tuple(out)

    def cmp_and_select_group(g, nchunks):
        rows, jmax = nchunks * 4 * chunk, nchunks * chunk
        kc = kc_ref[0, g, 0:rows, :]
        rc = lax.broadcasted_iota(jnp.int32, (rows, 1), 0)
        ncmp = 4 * ((rc // (4 * chunk)) * chunk + rc % chunk) + (rc % (4 * chunk)) // chunk
        cbias = jnp.where((ncmp * NSA_CMP_STRIDE + (NSA_CMP_LEN - 1)) <= qpos, 0.0, NEG)
        psum = jnp.zeros((rows, NSA_QBLK), F32)
        p_all = []
        for sl in heads:
            s = _dot(kc, qcat[g][:, sl]) + cbias
            e = jnp.exp2(s - jnp.max(s, axis=0, keepdims=True))
            p = e * (cvalid / jnp.sum(e, axis=0, keepdims=True))
            psum = psum + p
            p_all.append(p.astype(BF16))
        o_cmp = _dot(vc_ref[0, g, :, 0:rows], jnp.concatenate(p_all, axis=1))

        tot, p3 = [], []
        for c in range(nchunks):
            part = [psum[(4 * c + i) * chunk:(4 * c + i + 1) * chunk] for i in range(4)]
            tot.append(part[0] + part[1] + part[2] + part[3])
            p3.append(part[3])
        tot, p3 = jnp.concatenate(tot, axis=0), jnp.concatenate(p3, axis=0)
        rj = lax.broadcasted_iota(jnp.int32, (jmax, NSA_QBLK), 0)
        imp = tot + jnp.where(rj >= 1, pltpu.roll(p3, 1, 0), 0.0)
        forced = (rj == 0) | (rj == cur) | (rj == cur - 1)
        rjf = rj.astype(F32)
        imp = jnp.where(forced, taken, jnp.where(rj <= cur, imp, -FORCE))
        for _ in range(min(NSA_TOPK, ns) - 3):
            mx = jnp.max(imp, axis=0, keepdims=True)
            first = jnp.min(jnp.where(imp == mx, rjf, float(ns)), axis=0, keepdims=True)
            imp = jnp.where(rjf == first, taken, imp)
        bias = jnp.where(imp == taken, 0.0, NEG)
        if jmax < ns:
            bias = jnp.concatenate([bias, jnp.full((ns - jmax, NSA_QBLK), NEG, F32)], axis=0)
        return o_cmp, bias

    last_block = (s0 + NSA_QBLK - 1) // NSA_SLC_LEN
    selected = lax.switch(last_block // chunk, [functools.partial(cmp_and_select, n + 1) for n in range(split)])
    o_cmp = [selected[2 * g] for g in groups]
    for g in groups:
        bias_s[g][...] = selected[2 * g + 1]

    kt_diag = s0 // NSA_KTILE
    blocks_per_tile = NSA_KTILE // NSA_SLC_LEN
    vtiles = NSA_KTILE // LANES

    def qk_tile(g, kt, qaug_ref):
        k0 = pl.multiple_of(kt * NSA_KTILE, NSA_KTILE)
        b8 = bias_s[g][pl.ds(pl.multiple_of(kt * blocks_per_tile, blocks_per_tile), blocks_per_tile), :]
        b16 = jnp.concatenate([b8, jnp.zeros_like(b8)], axis=0).astype(BF16)
        qaug_ref[HEAD_DIM:HEAD_DIM + 16, :] = jnp.concatenate([b16] * NSA_RPG, axis=1)
        return _dot(ks_ref[pl.ds(k0, NSA_KTILE), klanes[g]], qaug_ref[...])

    ones_rows = jnp.ones((16, NSA_KTILE), BF16)

    def pv_tile(g, kt, p):
        vt = jnp.concatenate([vs_ref[kt * vtiles + i, vrows[g], :] for i in range(vtiles)], axis=1)
        return _dot(jnp.concatenate([vt, ones_rows], axis=0), p)

    def softmax_tile(s, m_old):
        m_new = jnp.maximum(m_old, jnp.max(s, axis=0, keepdims=True))
        p = jnp.exp2((s - m_new).astype(BF16))
        return p, m_new, jnp.exp2(m_old - m_new)

    def visible(kt):
        kpos = kt * NSA_KTILE + lax.broadcasted_iota(jnp.int32, (NSA_KTILE, 1), 0)
        return jnp.concatenate([kpos <= qpos] * NSA_RPG, axis=1)

    def tile_group(i, carry, last):
        carry = list(carry)
        for k in range(NSA_UNROLL):
            t = NSA_UNROLL * i + k
            for g in groups:
                m_run, acc, alpha_prev = carry[g]
                s_cur, p_cur, s_nxt, p_prv, qa = ((sa_s[g], pa_s[g], sb_s[g], pb_s[g], qaug_s[g]) if k % 2 == 0 else
                                                  (sb_s[g], pb_s[g], sa_s[g], pa_s[g], qaug2_s[g]))
                acc = alpha_prev * acc + pv_tile(g, jnp.maximum(t - 1, 0), p_prv[...])
                s = jnp.where(visible(t), s_cur[...], NEG) if last else s_cur[...]
                p, m_run, alpha_prev = softmax_tile(s, m_run)
                p_cur[...] = p
                if last and k == NSA_UNROLL - 1:
                    acc = alpha_prev * acc + pv_tile(g, t, p_cur[...])
                else:
                    s_nxt[...] = qk_tile(g, t + 1, qa)
                carry[g] = (m_run, acc, alpha_prev)
        return tuple(carry)

    for g in groups:
        qaug2_s[g][...] = qaug_s[g][...]
        pb_s[g][...] = jnp.zeros((NSA_KTILE, nqt), BF16)
        sa_s[g][...] = qk_tile(g, 0, qaug2_s[g])

    span = NSA_WIN + NSA_QBLK
    start = pl.multiple_of(jnp.maximum(s0 - NSA_WIN, 0), NSA_QBLK)
    kp = start + lax.broadcasted_iota(jnp.int32, (span, 1), 0)
    wbias = jnp.where((kp <= qpos) & (kp > qpos - NSA_WIN), 0.0, NEG)
    sblk = start // LANES
    o_win = []
    for g in groups:
        kwin = kw_ref[pl.ds(start, span), klanes[g]]
        pw, dens = [], []
        for sl in heads:
            s = _dot(kwin, qaug_s[g][:, sl]) + wbias
            e = jnp.exp2(s - jnp.max(s, axis=0, keepdims=True))
            dens.append(jnp.sum(e, axis=0, keepdims=True))
            pw.append(e.astype(BF16))
        vwt = jnp.concatenate([vw_ref[sblk + i, vrows[g], :] for i in range(span // LANES)], axis=1)
        o_win.append(_dot(vwt, jnp.concatenate(pw, axis=1)) * (1.0 / jnp.concatenate(dens, axis=1)))

    init = tuple((jnp.full((1, nqt), NEG, F32), jnp.zeros((HEAD_DIM + 16, nqt), F32), jnp.ones((1, nqt), F32))
                 for _ in groups)
    group_diag = kt_diag // NSA_UNROLL
    carry = lax.fori_loop(0, group_diag, lambda i, c: tile_group(i, c, False), init)
    carry = tile_group(group_diag, carry, True)

    gates = _sigmoid(jnp.concatenate([g_ref[j] for j in range(qtiles)], axis=1))
    outs = []
    for g in groups:
        acc = carry[g][1]
        o_slc = acc[0:HEAD_DIM] * (1.0 / acc[HEAD_DIM:HEAD_DIM + 1])
        for r, sl in enumerate(heads):
            row = g * 16 + r
            outs.append(gates[row:row + 1, :] * o_cmp[g][:, sl]
                        + gates[row + NSA_RPG:row + NSA_RPG + 1, :] * o_slc[:, sl]
                        + gates[row + 2 * NSA_RPG:row + 2 * NSA_RPG + 1, :] * o_win[g][:, sl])
    o_ref[...] = jnp.concatenate(outs, axis=0).T.astype(o_ref.dtype)


def _nsa_mixer(q_tt, kvc, k_slc, k_win, v_tt, g_tt, bsz, seq, pe, w1, b1, w2, b2):
    nqb = seq // NSA_QBLK
    qtiles = NSA_QBLK // LANES
    ntile = seq // LANES
    npc = seq // NSA_CMP_STRIDE
    ns = seq // NSA_SLC_LEN
    cmp_out = _nsa_compress(kvc, bsz, seq, pe, w1, b1, w2, b2)
    split = _nsa_cmp_split(ns)
    perm = cmp_out.reshape(bsz, 2, NSA_KV, split, ns // split, 4, HEAD_DIM).transpose(0, 1, 2, 3, 5, 4, 6).reshape(
        bsz, 2, NSA_KV, npc, HEAD_DIM)
    kc = perm[:, 0].astype(BF16)
    vc_t = perm[:, 1].transpose(0, 1, 3, 2).astype(BF16)
    nqt = NSA_RPG * NSA_QBLK
    resident = pl.Buffered(1)
    group_scratch = [pltpu.VMEM((ns, NSA_QBLK), F32),
                     pltpu.VMEM((2 * HEAD_DIM, nqt), BF16), pltpu.VMEM((2 * HEAD_DIM, nqt), BF16),
                     pltpu.VMEM((NSA_KTILE, nqt), F32), pltpu.VMEM((NSA_KTILE, nqt), F32),
                     pltpu.VMEM((NSA_KTILE, nqt), BF16), pltpu.VMEM((NSA_KTILE, nqt), BF16)]
    return pl.pallas_call(
        _nsa_kernel,
        grid=(bsz, nqb),
        in_specs=[
            pl.BlockSpec((qtiles, NSA_Q, LANES), lambda b, q: (b * nqb + q, 0, 0)),
            pl.BlockSpec((1, NSA_KV, npc, HEAD_DIM), lambda b, q: (b, 0, 0, 0), pipeline_mode=resident),
            pl.BlockSpec((1, NSA_KV, HEAD_DIM, npc), lambda b, q: (b, 0, 0, 0), pipeline_mode=resident),
            pl.BlockSpec((seq, NSA_KV * NSA_KVW), lambda b, q: (b, 0), pipeline_mode=resident),
            pl.BlockSpec((seq, NSA_KV * NSA_KVW), lambda b, q: (b, 0), pipeline_mode=resident),
            pl.BlockSpec((ntile, NSA_KVW, LANES), lambda b, q: (b, 0, 0), pipeline_mode=resident),
            pl.BlockSpec((ntile, NSA_KVW, LANES), lambda b, q: (b, 1, 0), pipeline_mode=resident),
            pl.BlockSpec((qtiles, NSA_KV * 16, LANES), lambda b, q: (b * nqb + q, 0, 0)),
        ],
        out_specs=pl.BlockSpec((NSA_QBLK, NSA_Q), lambda b, q: (b * nqb + q, 0)),
        out_shape=jax.ShapeDtypeStruct((bsz * seq, NSA_Q), BF16),
        scratch_shapes=group_scratch * NSA_KV,
        compiler_params=_cparams("arbitrary", "arbitrary"),
        name="nsa_attention",
    )(q_tt, kc, vc_t, k_slc, k_win, v_tt, v_tt, g_tt)


def _swa_kernel(q_ref, kp_ref, k0_ref, k1_ref, vp_ref, v0_ref, v1_ref, sink_ref, o_ref):
    step = pl.program_id(1)
    t = SWA_WIN
    krel = lax.broadcasted_iota(jnp.int32, (2 * t, 1), 0) - t
    qrel = lax.broadcasted_iota(jnp.int32, (1, t), 1)
    in_band = (krel <= qrel) & (krel > qrel - SWA_WIN)
    lowest = jnp.where(step > 0, -t, 0)
    mbias = [jnp.where(in_band & (krel >= lowest), 0.0, NEG), jnp.where(in_band, 0.0, NEG)]
    k_blocks = [kp_ref[...], k0_ref[...], k1_ref[...]]
    v_tiles = [vp_ref, v0_ref, v1_ref]
    scores = {}
    for blk in range(2):
        kband = jnp.concatenate([k_blocks[blk], k_blocks[blk + 1]], axis=0)
        for g in range(2):
            rows = slice(g * SWA_RPG * HEAD_DIM, (g + 1) * SWA_RPG * HEAD_DIM)
            qg = q_ref[blk, rows, :]
            qcat = jnp.concatenate([qg[r * HEAD_DIM:(r + 1) * HEAD_DIM, :] for r in range(SWA_RPG)], axis=1)
            zq = jnp.zeros_like(qcat)
            qext = jnp.concatenate([qcat, zq] if g == 0 else [zq, qcat], axis=0)
            scores[blk, g] = _dot(kband, qext)
    for blk in range(2):
        outs = []
        for g in range(2):
            s = scores[blk, g]
            ps, dens = [], []
            for r in range(SWA_RPG):
                h = g * SWA_RPG + r
                sink = sink_ref[h:h + 1, :]
                sr = s[:, r * t:(r + 1) * t] + mbias[blk]
                mx = jnp.maximum(jnp.max(sr, axis=0, keepdims=True), sink)
                e = jnp.exp2(sr - mx)
                dens.append(jnp.sum(e, axis=0, keepdims=True) + jnp.exp2(sink - mx))
                ps.append(e.astype(BF16))
            vband = jnp.concatenate([v_tiles[blk][0, g * HEAD_DIM:(g + 1) * HEAD_DIM, :],
                                     v_tiles[blk + 1][0, g * HEAD_DIM:(g + 1) * HEAD_DIM, :]], axis=1)
            og = _dot(vband, jnp.concatenate(ps, axis=1)) * (1.0 / jnp.concatenate(dens, axis=1))
            outs.append(jnp.concatenate([og[:, r * t:(r + 1) * t] for r in range(SWA_RPG)], axis=0).T)
        o_ref[blk * t:(blk + 1) * t, :] = jnp.concatenate(outs, axis=1).astype(o_ref.dtype)


def _swa_mixer(q_tt, k_nat, v_tt, sinks, bsz, seq):
    nblk = seq // SWA_WIN
    nstep = nblk // 2
    sink_rows = jnp.broadcast_to((sinks.astype(F32) * math.log2(math.e))[:, None], (sinks.shape[0], LANES))
    prev = lambda b, q: b * nblk + jnp.maximum(2 * q - 1, 0)
    cur0 = lambda b, q: b * nblk + 2 * q
    cur1 = lambda b, q: b * nblk + 2 * q + 1
    k_spec = lambda f: pl.BlockSpec((SWA_WIN, SWA_KVW), lambda b, q: (f(b, q), 0))
    v_spec = lambda f: pl.BlockSpec((1, SWA_KVW, LANES), lambda b, q: (f(b, q), 0, 0))
    return pl.pallas_call(
        _swa_kernel,
        grid=(bsz, nstep),
        in_specs=[
            pl.BlockSpec((2, SWA_Q, LANES), lambda b, q: (b * nstep + q, 0, 0)),
            k_spec(prev), k_spec(cur0), k_spec(cur1), v_spec(prev), v_spec(cur0), v_spec(cur1),
            _full(sink_rows.shape),
        ],
        out_specs=pl.BlockSpec((2 * SWA_WIN, SWA_Q), lambda b, q: (b * nstep + q, 0)),
        out_shape=jax.ShapeDtypeStruct((bsz * seq, SWA_Q), BF16),
        compiler_params=_cparams("parallel", "parallel"),
        name="swa_attention",
    )(q_tt, k_nat, k_nat, k_nat, v_tt, v_tt, v_tt, sink_rows)


def _s5_params(a_re, a_im, log_dt, b_re, b_im, c_re, c_im, n_chunks):
    f = F32
    t = S5_CHUNK
    step = jnp.exp(log_dt.astype(f))[:, None]
    lr, li = a_re.astype(f), a_im.astype(f)

    def lam_pow(tau):
        tau = tau.astype(f)[:, None, None]
        mag = jnp.exp(lr * step * tau)
        ang = li * step * tau
        return mag * jnp.cos(ang), mag * jnp.sin(ang)

    lb_r, lb_i = (v[0] for v in lam_pow(jnp.ones((1,))))
    nr, ni = lb_r - 1.0, lb_i
    den = lr * lr + li * li
    fr, fi = (nr * lr + ni * li) / den, (ni * lr - nr * li) / den
    br, bi = b_re.astype(f), b_im.astype(f)
    bb_r = fr[..., None] * br - fi[..., None] * bi
    bb_i = fr[..., None] * bi + fi[..., None] * br
    cr, ci = c_re.astype(f), c_im.astype(f)

    pr, pi = lam_pow(jnp.arange(t + 1))
    cl_r = cr[None] * pr[:, :, None, :] - ci[None] * pi[:, :, None, :]
    cl_i = cr[None] * pi[:, :, None, :] + ci[None] * pr[:, :, None, :]
    kern_t = jnp.einsum("tghp,gpk->gkth", cl_r[:t], bb_r, precision="highest") - jnp.einsum(
        "tghp,gpk->gkth", cl_i[:t], bb_i, precision="highest")
    rr, ri = pr[t - 1 - jnp.arange(t)], pi[t - 1 - jnp.arange(t)]
    bs_r = rr[..., None] * bb_r[None] - ri[..., None] * bb_i[None]
    bs_i = rr[..., None] * bb_i[None] + ri[..., None] * bb_r[None]
    bs = jnp.concatenate([bs_r, bs_i], axis=2)
    bs = bs.transpose(1, 0, 3, 2).reshape(S5_GROUPS, t * S5_GROUP_CH, 2 * S5_STATE)
    cs = jnp.concatenate([cl_r[1:], -cl_i[1:]], axis=3)
    cs = cs.transpose(1, 3, 0, 2).reshape(S5_GROUPS, 2 * S5_STATE, t * S5_GROUP_CH)
    ar, ai = pr[t], pi[t]
    a1, a2 = [], []
    k = 1
    while k < n_chunks:
        a1.append(jnp.concatenate([ar, ar], axis=1))
        a2.append(jnp.concatenate([-ai, ai], axis=1))
        ar, ai = ar * ar - ai * ai, 2.0 * ar * ai
        k *= 2
    a1 = jnp.stack(a1, axis=1)
    a2 = jnp.stack(a2, axis=1)
    noct = S5_GROUPS // S5_OCT
    hc, ns2 = S5_GROUP_CH, 2 * S5_STATE
    width = t * hc
    lanes = t * LANES
    r_i, c_i = np.arange(width)[:, None], np.arange(lanes)[None, :]
    rep = jnp.asarray((r_i // hc == c_i // LANES) & (r_i % hc == c_i % hc), dtype=BF16)
    kern_o = kern_t.reshape(noct, LANES, width).astype(BF16)
    toep_o = pl.pallas_call(
        _s5_toeplitz_kernel,
        grid=(noct, t // 8),
        in_specs=[pl.BlockSpec((1, LANES, width), lambda o, s: (o, 0, 0)), _full((width, lanes))],
        out_specs=pl.BlockSpec((1, 8 * LANES, lanes), lambda o, s: (o, s, 0)),
        out_shape=jax.ShapeDtypeStruct((noct, lanes, lanes), BF16),
        scratch_shapes=[pltpu.VMEM((t, LANES, LANES), F32)],
        compiler_params=_cparams("parallel", "arbitrary"),
        name="s5_toeplitz_table",
    )(kern_o, rep)
    cs_o = pl.pallas_call(
        _s5_readout_kernel,
        grid=(noct,),
        in_specs=[pl.BlockSpec((1, S5_OCT * ns2, width), lambda o: (o, 0, 0)), _full((width, lanes))],
        out_specs=pl.BlockSpec((1, S5_OCT * ns2, lanes), lambda o: (o, 0, 0)),
        out_shape=jax.ShapeDtypeStruct((noct, S5_OCT * ns2, lanes), BF16),
        compiler_params=_cparams("parallel"),
        name="s5_readout_table",
    )(cs.reshape(noct, S5_OCT * ns2, width).astype(BF16), rep)
    bs_slabs = bs.reshape(noct, S5_OCT, t, hc, ns2).transpose(0, 2, 1, 3, 4).reshape(noct, t, LANES, ns2).astype(BF16)
    bs_o = pl.pallas_call(
        _s5_state_in_kernel,
        grid=(noct,),
        in_specs=[pl.BlockSpec((1, t, LANES, ns2), lambda o: (o, 0, 0, 0))],
        out_specs=pl.BlockSpec((1, lanes, S5_OCT * ns2), lambda o: (o, 0, 0)),
        out_shape=jax.ShapeDtypeStruct((noct, lanes, S5_OCT * ns2), BF16),
        compiler_params=_cparams("parallel"),
        name="s5_state_in_table",
    )(bs_slabs)

    def oct_rows(a):
        return a.reshape(noct, S5_OCT, -1, ns2).transpose(0, 2, 1, 3).reshape(noct, -1, S5_OCT * ns2)

    return toep_o, bs_o, cs_o, oct_rows(a1), oct_rows(a2)


def _same_group(shape, row_div, col_mod, col_div):
    r = lax.broadcasted_iota(jnp.int32, shape, 0)
    c = lax.broadcasted_iota(jnp.int32, shape, 1)
    return (r // row_div) == ((c % col_mod) // col_div)


def _s5_toeplitz_kernel(k_ref, rep_ref, o_ref, full_s):
    nlag = full_s.shape[0]
    steps_here = o_ref.shape[1] // LANES

    @pl.when(pl.program_id(1) == 0)
    def _():
        full = _dot(k_ref[0], rep_ref[...])
        full = jnp.where(_same_group(full.shape, S5_GROUP_CH, LANES, S5_GROUP_CH), full, 0.0)
        for lag in range(nlag):
            full_s[lag] = full[:, lag * LANES:(lag + 1) * LANES]

    for i in range(steps_here):
        s = pl.program_id(1) * steps_here + i
        for t in range(nlag):
            tile = full_s[jnp.maximum(t - s, 0)]
            o_ref[0, i * LANES:(i + 1) * LANES, t * LANES:(t + 1) * LANES] = (
                jnp.where(t >= s, tile, 0.0).astype(o_ref.dtype))


def _s5_readout_kernel(c_ref, rep_ref, o_ref):
    full = _dot(c_ref[0], rep_ref[...])
    keep = _same_group(full.shape, 2 * S5_STATE, LANES, S5_GROUP_CH)
    o_ref[0] = jnp.where(keep, full, 0.0).astype(o_ref.dtype)


def _s5_state_in_kernel(b_ref, o_ref):
    keep = _same_group((LANES, o_ref.shape[2]), S5_GROUP_CH, o_ref.shape[2], 2 * S5_STATE)
    for s in range(b_ref.shape[1]):
        full = jnp.concatenate([b_ref[0, s]] * S5_OCT, axis=1)
        o_ref[0, s * LANES:(s + 1) * LANES, :] = jnp.where(keep, full, jnp.zeros((), full.dtype))


def _s5_state_kernel(bsz, u_ref, bs_ref, a1_ref, a2_ref, hi_ref, lo_ref):
    sc = _dot(u_ref[...], bs_ref[0])
    n = sc.shape[0] // bsz
    width = sc.shape[1]
    rowi = lax.broadcasted_iota(jnp.int32, (n, width), 0)

    def swap_re_im(x):
        return jnp.concatenate([pltpu.roll(x[:, j * LANES:(j + 1) * LANES], S5_STATE, 1)
                                for j in range(width // LANES)], axis=1)

    h_in = []
    for b in range(bsz):
        x = sc[b * n:(b + 1) * n]
        k, step = 1, 0
        while k < n:
            xs = jnp.where(rowi >= k, pltpu.roll(x, k, 0), 0.0)
            x = x + a1_ref[0, step:step + 1, :] * xs + a2_ref[0, step:step + 1, :] * swap_re_im(xs)
            k *= 2
            step += 1
        h_in.append(jnp.where(rowi >= 1, pltpu.roll(x, 1, 0), 0.0))
    h_in = jnp.concatenate(h_in, axis=0)
    hi = h_in.astype(BF16)
    hi_ref[0] = hi
    lo_ref[0] = (h_in - hi.astype(F32)).astype(BF16)


def _s5_out_kernel(u_ref, toep_ref, hi_ref, lo_ref, cs_ref, o_ref):
    n = pl.program_id(1)
    cols = toep_ref.shape[2]
    off = _dot(hi_ref[0], cs_ref[0]) + _dot(lo_ref[0], cs_ref[0])
    for nn in range(toep_ref.shape[1] // cols):
        @pl.when(n == nn)
        def _(nn=nn):
            k = (nn + 1) * cols
            y = off + _dot(u_ref[:, 0:k], toep_ref[0, 0:k, :])
            for t8 in range(cols // LANES):
                o_ref[:, t8, :] = y[:, t8 * LANES:(t8 + 1) * LANES].astype(o_ref.dtype)


def _s5_glu_kernel(y_ref, u_ref, d_ref, w_ref, b_ref, o_ref):
    y = _gelu_tanh(y_ref[...] + d_ref[...] * u_ref[...])
    gate = _sigmoid(_dot(y.astype(BF16), w_ref[...]) + b_ref[...])
    o_ref[...] = (y * gate).astype(o_ref.dtype)


def _s5_mixer(u5, u_chunks, bsz, seq, a_re, a_im, log_dt, b_re, b_im, c_re, c_im, d_skip, glu_w, glu_b, tm=1024):
    m = u5.shape[0]
    t = S5_CHUNK
    nch = m // t
    noct = S5_GROUPS // S5_OCT
    lanes = t * LANES
    sw = S5_OCT * 2 * S5_STATE
    toep, bs, cs, a1, a2 = _s5_params(a_re, a_im, log_dt, b_re, b_im, c_re, c_im, nch // bsz)
    nsteps = a1.shape[1]
    h_hi, h_lo = pl.pallas_call(
        functools.partial(_s5_state_kernel, bsz),
        grid=(noct,),
        in_specs=[
            pl.BlockSpec((nch, lanes), lambda o: (0, o)),
            pl.BlockSpec((1, lanes, sw), lambda o: (o, 0, 0)),
            pl.BlockSpec((1, nsteps, sw), lambda o: (o, 0, 0)),
            pl.BlockSpec((1, nsteps, sw), lambda o: (o, 0, 0)),
        ],
        out_specs=[pl.BlockSpec((1, nch, sw), lambda o: (o, 0, 0))] * 2,
        out_shape=[jax.ShapeDtypeStruct((noct, nch, sw), BF16)] * 2,
        compiler_params=_cparams("parallel"),
        name="s5_state",
    )(u_chunks, bs, a1, a2)
    tsub = 8
    rows = nch // 2
    y = pl.pallas_call(
        _s5_out_kernel,
        grid=(noct, t // tsub, nch // rows),
        in_specs=[
            pl.BlockSpec((rows, lanes), lambda o, n, r: (r, o)),
            pl.BlockSpec((1, lanes, tsub * LANES), lambda o, n, r: (o, 0, n)),
            pl.BlockSpec((1, rows, sw), lambda o, n, r: (o, r, 0)),
            pl.BlockSpec((1, rows, sw), lambda o, n, r: (o, r, 0)),
            pl.BlockSpec((1, sw, tsub * LANES), lambda o, n, r: (o, 0, n)),
        ],
        out_specs=pl.BlockSpec((rows, tsub, LANES), lambda o, n, r: (r, n, o)),
        out_shape=jax.ShapeDtypeStruct((nch, t, S5_CH), BF16),
        compiler_params=_cparams("parallel", "parallel", "parallel"),
        name="s5_scan",
    )(u_chunks, toep, h_hi, h_lo, cs).reshape(m, S5_CH)
    return pl.pallas_call(
        _s5_glu_kernel,
        grid=(m // tm,),
        in_specs=[
            pl.BlockSpec((tm, S5_CH), lambda i: (i, 0)),
            pl.BlockSpec((tm, S5_CH), lambda i: (i, 0)),
            _full((1, S5_CH)), _full((S5_CH, S5_CH)), _full((1, S5_CH)),
        ],
        out_specs=pl.BlockSpec((tm, S5_CH), lambda i: (i, 0)),
        out_shape=jax.ShapeDtypeStruct((m, S5_CH), BF16),
        compiler_params=_cparams("parallel"),
        name="s5_glu",
    )(y, u5, d_skip.reshape(1, S5_CH).astype(F32), glu_w.astype(BF16), glu_b.reshape(1, S5_CH).astype(F32))


def _even_mixers(h, bsz, seq, g_mix, w_in, conv_w, conv_b, dt_bias, a_log, d_skip, norm_g, pe, w1, b1, w2, b2):
    d = h.shape[1]
    scale = HEAD_DIM ** -0.5 * math.log2(math.e)
    o = SSD_IN
    w_ssd = jnp.concatenate([w_in[:, :SSD_IN], jnp.zeros((d, LANES - SSD_HEADS), w_in.dtype)], axis=1)
    w_q = w_in[:, o:o + NSA_Q] * scale
    kv = [w_in[:, o + NSA_Q + i * NSA_KVW:o + NSA_Q + (i + 1) * NSA_KVW] for i in range(6)]
    w_gate = w_in[:, o + NSA_Q + 6 * NSA_KVW:].reshape(d, NSA_KV, NSA_RPG, 3).transpose(0, 1, 3, 2)
    w_gate = jnp.concatenate([w_gate.reshape(d, NSA_KV, 12), jnp.zeros((d, NSA_KV, 4), w_in.dtype)],
                             axis=2).reshape(d, NSA_KV * 16)

    def per_group_halves(w):
        wg = w.reshape(d, NSA_KV, HEAD_DIM)
        return jnp.concatenate([wg, jnp.zeros_like(wg)], axis=2).reshape(d, NSA_KV * NSA_KVW)
    segs = [
        ("nat", w_ssd, F32),
        ("tt", w_q, BF16),
        ("pc", jnp.concatenate([kv[0], kv[1]], axis=1), F32),
        ("nat+", per_group_halves(kv[2]), BF16, _nsa_block_onehot(NSA_KTILE)),
        ("nat", per_group_halves(kv[4]), BF16),
        ("tt", jnp.concatenate([kv[3], kv[5]], axis=1), BF16),
        ("tt", w_gate, F32),
    ]
    u_ssd, q_tt, kvc, k_slc, k_win, v_tt, g_tt = _norm_proj(h, g_mix, segs)
    ya = _ssd_mixer(u_ssd, bsz, seq, conv_w, conv_b, dt_bias, a_log, d_skip, norm_g)
    yb = _nsa_mixer(q_tt, kvc, k_slc, k_win, v_tt, g_tt, bsz, seq, pe, w1, b1, w2, b2)
    return ya, yb


def _odd_mixers(h, bsz, seq, g_mix, w_in, sinks, a_re, a_im, log_dt, b_re, b_im, c_re, c_im, d_skip, glu_w, glu_b):
    scale = HEAD_DIM ** -0.5 * math.log2(math.e)
    segs = [
        ("tt", w_in[:, :SWA_Q] * scale, BF16),
        ("nat", w_in[:, SWA_Q:SWA_Q + SWA_KVW], BF16),
        ("tt", w_in[:, SWA_Q + SWA_KVW:SWA_Q + 2 * SWA_KVW], BF16),
        ("nat+ch", w_in[:, SWA_Q + 2 * SWA_KVW:], F32),
    ]
    q_tt, k_nat, v_tt, u5, u_chunks = _norm_proj(h, g_mix, segs)
    yc = _swa_mixer(q_tt, k_nat, v_tt, sinks, bsz, seq)
    yd = _s5_mixer(u5, u_chunks, bsz, seq, a_re, a_im, log_dt, b_re, b_im, c_re, c_im, d_skip, glu_w, glu_b)
    return yc, yd


def kernel(x, norm_mix, norm_mlp, norm_final, mlp_w_up, mlp_w_down, ev_w_in, ev_w_out, ssd_conv_w, ssd_conv_b,
           ssd_dt_bias, ssd_a_log, ssd_d, ssd_norm, nsa_pe, nsa_cmp_w1, nsa_cmp_b1, nsa_cmp_w2, nsa_cmp_b2,
           od_w_in, od_w_out, swa_sinks, s5_a_re, s5_a_im, s5_log_dt, s5_b_re, s5_b_im, s5_c_re, s5_c_im,
           s5_d, s5_glu_w, s5_glu_b):
    bsz, seq, d = x.shape
    depth = norm_mix.shape[0]
    assert seq % (NSA_UNROLL * NSA_KTILE) == 0 and seq >= NSA_WIN + NSA_QBLK
    h = x.reshape(bsz * seq, d)
    for layer in range(depth):
        i = layer // 2
        if layer % 2 == 0:
            ya, yb = _even_mixers(h, bsz, seq, norm_mix[layer], ev_w_in[i], ssd_conv_w[i], ssd_conv_b[i],
                                  ssd_dt_bias[i], ssd_a_log[i], ssd_d[i], ssd_norm[i], nsa_pe[i],
                                  nsa_cmp_w1[i], nsa_cmp_b1[i], nsa_cmp_w2[i], nsa_cmp_b2[i])
            w_out = ev_w_out[i]
        else:
            ya, yb = _odd_mixers(h, bsz, seq, norm_mix[layer], od_w_in[i], swa_sinks[i], s5_a_re[i], s5_a_im[i],
                                 s5_log_dt[i], s5_b_re[i], s5_b_im[i], s5_c_re[i], s5_c_im[i], s5_d[i],
                                 s5_glu_w[i], s5_glu_b[i])
            w_out = od_w_out[i]
        h = _mix_out_mlp(h, ya, yb, w_out, norm_mlp[layer], mlp_w_up[layer], mlp_w_down[layer], norm_final,
                         final=(layer == depth - 1))
    return h.reshape(bsz, seq, d)
```

```python
import functools
import math

import jax
import jax.numpy as jnp
import numpy as np
from jax import lax
from jax.experimental import pallas as pl
from jax.experimental.pallas import tpu as pltpu

F32 = jnp.float32
BF16 = jnp.bfloat16

EPS = 1e-6
NEG = -1e30
FORCE = 1e9
HEAD_DIM = 64
LANES = 128
VMEM_LIMIT_BYTES = 56 * 1024 * 1024

SSD_HEADS = 8
SSD_INNER = 512
SSD_GROUPS = 2
SSD_STATE = 128
SSD_CONV = 4
SSD_CHUNK = 128
SSD_CONV_DIM = 1024
SSD_IN = SSD_INNER + SSD_CONV_DIM + SSD_HEADS

NSA_HEADS = 8
NSA_KV = 2
NSA_RPG = 4
NSA_CMP_LEN = 32
NSA_CMP_STRIDE = 16
NSA_SLC_LEN = 64
NSA_TOPK = 16
NSA_WIN = 512
NSA_CMP_HIDDEN = 256
NSA_QBLK = 128
NSA_Q = 512
NSA_KVW = 128
NSA_KTILE = 512
NSA_UNROLL = 2

SWA_RPG = 4
SWA_WIN = 128
SWA_Q = 512
SWA_KVW = 128
SWA_STEP_BLOCKS = 4

S5_CH = 512
S5_GROUP_CH = 16
S5_GROUPS = 32
S5_STATE = 64
S5_CHUNK = 32
S5_OCT = LANES // S5_GROUP_CH


def _cparams(*sem):
    return pltpu.CompilerParams(dimension_semantics=sem, vmem_limit_bytes=VMEM_LIMIT_BYTES)


def _full(shape):
    n = len(shape)
    return pl.BlockSpec(shape, lambda *_: (0,) * n)


def _dot(a, b):
    return jnp.dot(a, b, preferred_element_type=F32)


def _dot_nt(a, b):
    return lax.dot_general(a, b, (((1,), (1,)), ((), ())), preferred_element_type=F32)


def _split3(a):
    hi = a.astype(BF16)
    r1 = a - hi.astype(F32)
    mid = r1.astype(BF16)
    lo = (r1 - mid.astype(F32)).astype(BF16)
    return hi, mid, lo


def _dot_exact_rhs(a, b_exact):
    hi, mid, lo = _split3(a)
    return _dot(hi, b_exact) + _dot(mid, b_exact) + _dot(lo, b_exact)


def _dot_exact_lhs(a_exact, b):
    hi, mid, lo = _split3(b)
    return _dot(a_exact, hi) + _dot(a_exact, mid) + _dot(a_exact, lo)


def _rms(x, g):
    return x * lax.rsqrt(jnp.mean(x * x, axis=-1, keepdims=True) + EPS) * g


def _gelu_tanh(x):
    c = math.sqrt(2.0 / math.pi)
    return 0.5 * x * (1.0 + jnp.tanh(c * (x + 0.044715 * (x * x * x))))


def _sigmoid(x):
    return 1.0 / (1.0 + jnp.exp(-x))


def _proj_kernel(kinds, tm, h_ref, g_ref, *refs):
    n = len(kinds)
    n_add = sum(k == "nat+" for k in kinds)
    n_out = n + sum(k == "nat+ch" for k in kinds)
    w_refs, add_refs = refs[:n], list(refs[n:n + n_add])
    o_refs, scratch = list(refs[n + n_add:n + n_add + n_out]), list(refs[n + n_add + n_out:])
    yb = _rms(h_ref[...], g_ref[...]).astype(BF16)
    for kind, w_ref in zip(kinds, w_refs):
        o_ref = o_refs.pop(0)
        if kind == "nat":
            o_ref[...] = _dot(yb, w_ref[...]).astype(o_ref.dtype)
        elif kind == "nat+ch":
            ch_ref, ch_s = o_refs.pop(0), scratch.pop(0)
            res = _dot(yb, w_ref[...])
            o_ref[...] = res.astype(o_ref.dtype)
            for j in range(ch_s.shape[0]):
                ch_s[j] = res[:, j * LANES:(j + 1) * LANES]
            for t in range(S5_CHUNK):
                for j in range(ch_s.shape[0]):
                    col = (j * S5_CHUNK + t) * LANES
                    ch_ref[:, col:col + LANES] = ch_s[j, pl.ds(t, tm // S5_CHUNK, stride=S5_CHUNK), :].astype(ch_ref.dtype)
        elif kind == "nat+":
            o_ref[...] = _dot(yb, w_ref[...]).astype(o_ref.dtype) + add_refs.pop(0)[...]
        elif kind == "pc":
            pc_s = scratch.pop(0)
            res = _dot(yb, w_ref[...])
            flat = NSA_CMP_STRIDE * HEAD_DIM
            per_tile = LANES // HEAD_DIM
            for j in range(pc_s.shape[0]):
                pc_s[j] = res[:, j * LANES:(j + 1) * LANES]
            for p in range(NSA_CMP_STRIDE):
                for j in range(pc_s.shape[0]):
                    tok = pc_s[j, pl.ds(p, tm // NSA_CMP_STRIDE, stride=NSA_CMP_STRIDE), :]
                    for c in range(per_tile):
                        col = (j * per_tile + c) * flat + p * HEAD_DIM
                        o_ref[:, col:col + HEAD_DIM] = tok[:, c * HEAD_DIM:(c + 1) * HEAD_DIM].astype(o_ref.dtype)
        else:
            ot = _dot_nt(w_ref[...], yb)
            for j in range(tm // LANES):
                o_ref[j] = ot[:, j * LANES:(j + 1) * LANES].astype(o_ref.dtype)


def _norm_proj(h, g, segs, tm=512):
    m, d = h.shape
    kinds = tuple(s[0] for s in segs)
    adds = [s[3] for s in segs if s[0] == "nat+"]
    ws, w_specs, out_shapes, out_specs, scratch = [], [], [], [], []
    for kind, w, dt in (s[:3] for s in segs):
        n_out = w.shape[1]
        if kind == "pc":
            ws.append(w.astype(BF16))
            w_specs.append(_full((d, n_out)))
            out_shapes.append(jax.ShapeDtypeStruct((m // NSA_CMP_STRIDE, NSA_CMP_STRIDE * n_out), dt))
            out_specs.append(pl.BlockSpec((tm // NSA_CMP_STRIDE, NSA_CMP_STRIDE * n_out), lambda i: (i, 0)))
            scratch.append(pltpu.VMEM((n_out // LANES, tm, LANES), F32))
        elif kind in ("nat", "nat+", "nat+ch"):
            ws.append(w.astype(BF16))
            w_specs.append(_full((d, n_out)))
            out_shapes.append(jax.ShapeDtypeStruct((m, n_out), dt))
            out_specs.append(pl.BlockSpec((tm, n_out), lambda i: (i, 0)))
            if kind == "nat+ch":
                out_shapes.append(jax.ShapeDtypeStruct((m // S5_CHUNK, S5_CHUNK * n_out), BF16))
                out_specs.append(pl.BlockSpec((tm // S5_CHUNK, S5_CHUNK * n_out), lambda i: (i, 0)))
                scratch.append(pltpu.VMEM((n_out // LANES, tm, LANES), F32))
        else:
            ws.append(w.T.astype(BF16))
            w_specs.append(_full((n_out, d)))
            out_shapes.append(jax.ShapeDtypeStruct((m // LANES, n_out, LANES), dt))
            out_specs.append(pl.BlockSpec((tm // LANES, n_out, LANES), lambda i: (i, 0, 0)))
    return pl.pallas_call(
        functools.partial(_proj_kernel, kinds, tm),
        grid=(m // tm,),
        in_specs=[pl.BlockSpec((tm, d), lambda i: (i, 0)), _full((1, d))] + w_specs + [
            pl.BlockSpec((tm, a.shape[1]), functools.partial(lambda i, nb: (i % nb, 0), nb=a.shape[0] // tm))
            for a in adds],
        out_specs=out_specs,
        out_shape=out_shapes,
        scratch_shapes=scratch,
        compiler_params=_cparams("parallel"),
        name="norm_proj",
    )(h, g.reshape(1, d), *ws, *adds)


def _mlp_kernel(final, h_ref, ya_ref, yb_ref, woa_ref, wob_ref, gm_ref, wup_ref, wdn_ref, gf_ref,
                o_ref, h2_s, xn_s, acc_s):
    j = pl.program_id(1)

    @pl.when(j == 0)
    def _():
        h2 = h_ref[...] + _dot(ya_ref[...], woa_ref[...]) + _dot(yb_ref[...], wob_ref[...])
        h2_s[...] = h2
        xn_s[...] = _rms(h2, gm_ref[...]).astype(BF16)
        acc_s[...] = jnp.zeros_like(acc_s)

    hid = jnp.square(jnp.maximum(_dot(xn_s[...], wup_ref[...]), 0.0))
    acc_s[...] += _dot(hid.astype(BF16), wdn_ref[...])

    @pl.when(j == pl.num_programs(1) - 1)
    def _():
        out = h2_s[...] + acc_s[...]
        if final:
            out = _rms(out, gf_ref[...])
        o_ref[...] = out


def _mix_out_mlp(h, ya, yb, w_out, g_mlp, w_up, w_down, g_final, final, tm=1024, tf=1024):
    m, d = h.shape
    dff = w_up.shape[1]
    na = ya.shape[1]
    nb = yb.shape[1]
    return pl.pallas_call(
        functools.partial(_mlp_kernel, final),
        grid=(m // tm, dff // tf),
        in_specs=[
            pl.BlockSpec((tm, d), lambda i, j: (i, 0)),
            pl.BlockSpec((tm, na), lambda i, j: (i, 0)),
            pl.BlockSpec((tm, nb), lambda i, j: (i, 0)),
            _full((na, d)), _full((nb, d)), _full((1, d)),
            pl.BlockSpec((d, tf), lambda i, j: (0, j)),
            pl.BlockSpec((tf, d), lambda i, j: (j, 0)),
            _full((1, d)),
        ],
        out_specs=pl.BlockSpec((tm, d), lambda i, j: (i, 0)),
        out_shape=jax.ShapeDtypeStruct((m, d), F32),
        scratch_shapes=[pltpu.VMEM((tm, d), F32), pltpu.VMEM((tm, d), BF16), pltpu.VMEM((tm, d), F32)],
        compiler_params=_cparams("parallel", "arbitrary"),
        name="out_proj_mlp",
    )(h, ya, yb, w_out[:na].astype(BF16), w_out[na:].astype(BF16), g_mlp.reshape(1, d),
      w_up.astype(BF16), w_down.astype(BF16), g_final.reshape(1, d))


def _ssd_kernel(u_ref, cw_ref, cb_ref, dtb_ref, alog_ref, dsk_ref, ng_ref, o_ref, xext_s, st_s):
    t = SSD_CHUNK
    c = pl.program_id(1)

    @pl.when(c == 0)
    def _():
        xext_s[0:8, :] = jnp.zeros((8, SSD_CONV_DIM), F32)
        st_s[...] = jnp.zeros_like(st_s)

    z = u_ref[:, 0:SSD_INNER]
    dt_raw = u_ref[:, SSD_INNER + SSD_CONV_DIM:]
    xext_s[8:8 + t, :] = u_ref[:, SSD_INNER:SSD_INNER + SSD_CONV_DIM]
    xfull = xext_s[...]
    conv = cb_ref[...] + cw_ref[SSD_CONV - 1:SSD_CONV, :] * xfull[8:8 + t]
    for back in range(1, SSD_CONV):
        k = SSD_CONV - 1 - back
        conv = conv + cw_ref[k:k + 1, :] * pltpu.roll(xfull, back, 0)[8:8 + t]
    xext_s[0:8, :] = xext_s[t:t + 8, :]
    xc = conv * _sigmoid(conv)
    xs = xc[:, 0:SSD_INNER]
    gn = SSD_GROUPS * SSD_STATE

    dtp = dt_raw + dtb_ref[...]
    dt = jnp.maximum(dtp, 0.0) + jnp.log1p(jnp.exp(-jnp.abs(dtp)))
    a = -jnp.exp(alog_ref[...])
    da = dt * a

    row = lax.broadcasted_iota(jnp.int32, (t, t), 0)
    col = lax.broadcasted_iota(jnp.int32, (t, t), 1)
    causal = col <= row
    tril = jnp.where(causal, 1.0, 0.0).astype(BF16)
    a_cum = _dot_exact_lhs(tril, da)
    er = lax.broadcasted_iota(jnp.int32, (LANES, SSD_INNER), 0)
    ec = lax.broadcasted_iota(jnp.int32, (LANES, SSD_INNER), 1)
    expand = jnp.where((ec >> 6) == er, 1.0, 0.0).astype(BF16)
    a_cum_x = _dot_exact_rhs(a_cum, expand)
    dt_x = _dot_exact_rhs(dt, expand)
    a_cum_t = a_cum.T
    a_last_x = a_cum_x[t - 1:t, :]
    decay_end_x = jnp.exp(a_last_x - a_cum_x)
    decay_in_x = jnp.exp(a_cum_x)
    chunk_decay_x = jnp.exp(a_last_x)

    xd = xs * dt_x
    xd_end = (xd * decay_end_x).astype(BF16)
    xd_b = xd.astype(BF16)
    lane = lax.broadcasted_iota(jnp.int32, (t, LANES), 1)
    first_half = lane < HEAD_DIM

    pieces = []
    for g in range(SSD_GROUPS):
        bm = xc[:, SSD_INNER + g * SSD_STATE:SSD_INNER + (g + 1) * SSD_STATE]
        cm = xc[:, SSD_INNER + gn + g * SSD_STATE:SSD_INNER + gn + (g + 1) * SSD_STATE].astype(BF16)
        bm_t = bm.T.astype(BF16)
        cb = _dot_nt(cm, bm.astype(BF16))
        for pr in range(2):
            i = g * 2 + pr
            sl = slice(i * LANES, (i + 1) * LANES)
            ms = []
            for hh in range(2):
                h = 2 * i + hh
                seg = a_cum[:, h:h + 1] - a_cum_t[h:h + 1, :]
                dec = jnp.exp(jnp.where(causal, seg, NEG))
                ms.append((cb * dec).astype(BF16))
            y_diag = jnp.where(first_half, _dot(ms[0], xd_b[:, sl]), _dot(ms[1], xd_b[:, sl]))
            st = st_s[i]
            y_off = _dot(cm, st.astype(BF16)) * decay_in_x[:, sl]
            st_s[i] = st * chunk_decay_x[:, sl] + _dot(bm_t, xd_end[:, sl])
            pieces.append(y_diag + y_off)
    y = jnp.concatenate(pieces, axis=1) + xs * dsk_ref[...]
    y = y * (z * _sigmoid(z))
    half = SSD_INNER // SSD_GROUPS
    outs = [_rms(y[:, g * half:(g + 1) * half], ng_ref[:, g * half:(g + 1) * half]) for g in range(SSD_GROUPS)]
    o_ref[...] = jnp.concatenate(outs, axis=1).astype(o_ref.dtype)


def _ssd_mixer(u_ssd, bsz, seq, conv_w, conv_b, dt_bias, a_log, d_skip, norm_g):
    m, width = u_ssd.shape
    nch = seq // SSD_CHUNK
    pad = LANES - SSD_HEADS

    def padded(v):
        return jnp.concatenate([v.astype(F32), jnp.zeros((pad,), F32)]).reshape(1, LANES)

    return pl.pallas_call(
        _ssd_kernel,
        grid=(bsz, nch),
        in_specs=[
            pl.BlockSpec((SSD_CHUNK, width), lambda b, c: (b * nch + c, 0)),
            _full((SSD_CONV, SSD_CONV_DIM)), _full((1, SSD_CONV_DIM)),
            _full((1, LANES)), _full((1, LANES)), _full((1, SSD_INNER)), _full((1, SSD_INNER)),
        ],
        out_specs=pl.BlockSpec((SSD_CHUNK, SSD_INNER), lambda b, c: (b * nch + c, 0)),
        out_shape=jax.ShapeDtypeStruct((m, SSD_INNER), BF16),
        scratch_shapes=[pltpu.VMEM((SSD_CHUNK + 8, SSD_CONV_DIM), F32),
                        pltpu.VMEM((SSD_HEADS // 2, SSD_STATE, LANES), F32)],
        compiler_params=_cparams("arbitrary", "arbitrary"),
        name="ssd_mixer",
    )(u_ssd, conv_w.astype(F32), conv_b.reshape(1, -1).astype(F32), padded(dt_bias), padded(a_log),
      jnp.repeat(d_skip.astype(F32), HEAD_DIM).reshape(1, SSD_INNER), norm_g.reshape(1, SSD_INNER).astype(F32))


def _nsa_compress_kernel(x_ref, pe_ref, w1_ref, b1_ref, w2_ref, b2_ref, o_ref):
    x = x_ref[...]
    npc = x.shape[0]
    half = NSA_CMP_STRIDE * HEAD_DIM
    top = _dot((x + pe_ref[0, :, 0:half]).astype(BF16), w1_ref[0, 0:half, :])
    bot = _dot((x + pe_ref[0, :, half:]).astype(BF16), w1_ref[0, half:, :])
    pre = top + pltpu.roll(bot, npc - 1, 0) + b1_ref[0]
    hid = _gelu_tanh(pre).astype(BF16)
    out = _dot(hid, w2_ref[0]) + b2_ref[0]
    rowi = lax.broadcasted_iota(jnp.int32, out.shape, 0)
    o_ref[0, 0, 0] = jnp.where(rowi < npc - 1, out, 0.0)


def _nsa_compress(kvc_pieces, bsz, seq, pe, w1, b1, w2, b2):
    npc = seq // NSA_CMP_STRIDE
    flat = NSA_CMP_STRIDE * HEAD_DIM
    x = kvc_pieces
    return pl.pallas_call(
        _nsa_compress_kernel,
        grid=(bsz, 2, NSA_KV),
        in_specs=[
            pl.BlockSpec((npc, flat), lambda b, s, g: (b, s * NSA_KV + g)),
            pl.BlockSpec((1, 1, 2 * flat), lambda b, s, g: (s, 0, 0)),
            pl.BlockSpec((1, 2 * flat, NSA_CMP_HIDDEN), lambda b, s, g: (s, 0, 0)),
            pl.BlockSpec((1, 1, NSA_CMP_HIDDEN), lambda b, s, g: (s, 0, 0)),
            pl.BlockSpec((1, NSA_CMP_HIDDEN, HEAD_DIM), lambda b, s, g: (s, 0, 0)),
            pl.BlockSpec((1, 1, HEAD_DIM), lambda b, s, g: (s, 0, 0)),
        ],
        out_specs=pl.BlockSpec((1, 1, 1, npc, HEAD_DIM), lambda b, s, g: (b, s, g, 0, 0)),
        out_shape=jax.ShapeDtypeStruct((bsz, 2, NSA_KV, npc, HEAD_DIM), F32),
        compiler_params=_cparams("parallel", "parallel", "parallel"),
        name="nsa_compress",
    )(x, pe.reshape(2, 1, 2 * flat).astype(F32), w1.astype(BF16), b1.reshape(2, 1, -1).astype(F32),
      w2.astype(BF16), b2.reshape(2, 1, -1).astype(F32))


def _nsa_cmp_split(ns):
    return max(1, min(4, (4 * ns) // LANES))


def _nsa_block_onehot(rows):
    r = lax.broadcasted_iota(jnp.int32, (rows, 2 * NSA_KVW), 0)
    c = lax.broadcasted_iota(jnp.int32, (rows, 2 * NSA_KVW), 1)
    blk = (r % NSA_KTILE) // NSA_SLC_LEN
    return jnp.where((c % NSA_KVW) == HEAD_DIM + blk, 1.0, 0.0).astype(BF16)


def _nsa_kernel(q_ref, kc_ref, vc_ref, ks_ref, kw_ref, vs_ref, vw_ref, g_ref, o_ref, *scratch):
    groups = range(NSA_KV)
    per = len(scratch) // NSA_KV
    bias_s, qaug_s, qaug2_s, sa_s, sb_s, pa_s, pb_s = (
        tuple(scratch[g * per + i] for g in groups) for i in range(per))
    qb = pl.program_id(1)
    nqt = NSA_RPG * NSA_QBLK
    s0 = qb * NSA_QBLK
    ncp = kc_ref.shape[2]
    ns = ncp // 4
    heads = [slice(r * NSA_QBLK, (r + 1) * NSA_QBLK) for r in range(NSA_RPG)]
    klanes = [slice(g * NSA_KVW, (g + 1) * NSA_KVW) for g in groups]
    vrows = [slice(g * HEAD_DIM, (g + 1) * HEAD_DIM) for g in groups]

    qtiles = NSA_QBLK // LANES
    qcat = [jnp.concatenate([q_ref[j, (g * NSA_RPG + r) * HEAD_DIM:(g * NSA_RPG + r + 1) * HEAD_DIM, :]
                             for r in range(NSA_RPG) for j in range(qtiles)], axis=1) for g in groups]
    qpos = s0 + lax.broadcasted_iota(jnp.int32, (1, NSA_QBLK), 1)
    for g in groups:
        qaug_s[g][0:HEAD_DIM, :] = qcat[g]
        qaug_s[g][HEAD_DIM:, :] = jnp.zeros((HEAD_DIM, nqt), BF16)

    split = _nsa_cmp_split(ns)
    chunk = ns // split
    cvalid = jnp.where(qpos >= NSA_CMP_LEN - 1, 1.0, 0.0)
    cur = qpos // NSA_SLC_LEN
    taken = -3.0e38

    def cmp_and_select(nchunks):
        out = []
        for g in groups:
            out.extend(cmp_and_select_group(g, nchunks))
        return tuple(out)

    def cmp_and_select_group(g, nchunks):
        rows, jmax = nchunks * 4 * chunk, nchunks * chunk
        kc = kc_ref[0, g, 0:rows, :]
        rc = lax.broadcasted_iota(jnp.int32, (rows, 1), 0)
        ncmp = 4 * ((rc // (4 * chunk)) * chunk + rc % chunk) + (rc % (4 * chunk)) // chunk
        cbias = jnp.where((ncmp * NSA_CMP_STRIDE + (NSA_CMP_LEN - 1)) <= qpos, 0.0, NEG)
        psum = jnp.zeros((rows, NSA_QBLK), F32)
        p_all = []
        for sl in heads:
            s = _dot(kc, qcat[g][:, sl]) + cbias
            e = jnp.exp2(s - jnp.max(s, axis=0, keepdims=True))
            p = e * (cvalid / jnp.sum(e, axis=0, keepdims=True))
            psum = psum + p
            p_all.append(p.astype(BF16))
        o_cmp = _dot(vc_ref[0, g, :, 0:rows], jnp.concatenate(p_all, axis=1))

        tot, p3 = [], []
        for c in range(nchunks):
            part = [psum[(4 * c + i) * chunk:(4 * c + i + 1) * chunk] for i in range(4)]
            tot.append(part[0] + part[1] + part[2] + part[3])
            p3.append(part[3])
        tot, p3 = jnp.concatenate(tot, axis=0), jnp.concatenate(p3, axis=0)
        rj = lax.broadcasted_iota(jnp.int32, (jmax, NSA_QBLK), 0)
        imp = tot + jnp.where(rj >= 1, pltpu.roll(p3, 1, 0), 0.0)
        forced = (rj == 0) | (rj == cur) | (rj == cur - 1)
        rjf = rj.astype(F32)
        imp = jnp.where(forced, taken, jnp.where(rj <= cur, imp, -FORCE))
        for _ in range(min(NSA_TOPK, ns) - 3):
            mx = jnp.max(imp, axis=0, keepdims=True)
            first = jnp.min(jnp.where(imp == mx, rjf, float(ns)), axis=0, keepdims=True)
            imp = jnp.where(rjf == first, taken, imp)
        bias = jnp.where(imp == taken, 0.0, NEG)
        if jmax < ns:
            bias = jnp.concatenate([bias, jnp.full((ns - jmax, NSA_QBLK), NEG, F32)], axis=0)
        return o_cmp, bias

    last_block = (s0 + NSA_QBLK - 1) // NSA_SLC_LEN
    selected = lax.switch(last_block // chunk, [functools.partial(cmp_and_select, n + 1) for n in range(split)])
    o_cmp = [selected[2 * g] for g in groups]
    for g in groups:
        bias_s[g][...] = selected[2 * g + 1]

    kt_diag = s0 // NSA_KTILE
    blocks_per_tile = NSA_KTILE // NSA_SLC_LEN
    vtiles = NSA_KTILE // LANES

    def qk_tile(g, kt, qaug_ref):
        k0 = pl.multiple_of(kt * NSA_KTILE, NSA_KTILE)
        b8 = bias_s[g][pl.ds(pl.multiple_of(kt * blocks_per_tile, blocks_per_tile), blocks_per_tile), :]
        b16 = jnp.concatenate([b8, jnp.zeros_like(b8)], axis=0).astype(BF16)
        qaug_ref[HEAD_DIM:HEAD_DIM + 16, :] = jnp.concatenate([b16] * NSA_RPG, axis=1)
        return _dot(ks_ref[pl.ds(k0, NSA_KTILE), klanes[g]], qaug_ref[...])

    ones_rows = jnp.ones((16, NSA_KTILE), BF16)

    def pv_tile(g, kt, p):
        vt = jnp.concatenate([vs_ref[kt * vtiles + i, vrows[g], :] for i in range(vtiles)], axis=1)
        return _dot(jnp.concatenate([vt, ones_rows], axis=0), p)

    def softmax_tile(s, m_old):
        m_new = jnp.maximum(m_old, jnp.max(s, axis=0, keepdims=True))
        p = jnp.exp2((s - m_new).astype(BF16))
        return p, m_new, jnp.exp2(m_old - m_new)

    def visible(kt):
        kpos = kt * NSA_KTILE + lax.broadcasted_iota(jnp.int32, (NSA_KTILE, 1), 0)
        return jnp.concatenate([kpos <= qpos] * NSA_RPG, axis=1)

    def tile_group(i, carry, last):
        carry = list(carry)
        for k in range(NSA_UNROLL):
            t = NSA_UNROLL * i + k
            for g in groups:
                m_run, acc, alpha_prev = carry[g]
                s_cur, p_cur, s_nxt, p_prv, qa = ((sa_s[g], pa_s[g], sb_s[g], pb_s[g], qaug_s[g]) if k % 2 == 0 else
                                                  (sb_s[g], pb_s[g], sa_s[g], pa_s[g], qaug2_s[g]))
                acc = alpha_prev * acc + pv_tile(g, jnp.maximum(t - 1, 0), p_prv[...])
                s = jnp.where(visible(t), s_cur[...], NEG) if last else s_cur[...]
                p, m_run, alpha_prev = softmax_tile(s, m_run)
                p_cur[...] = p
                if last and k == NSA_UNROLL - 1:
                    acc = alpha_prev * acc + pv_tile(g, t, p_cur[...])
                else:
                    s_nxt[...] = qk_tile(g, t + 1, qa)
                carry[g] = (m_run, acc, alpha_prev)
        return tuple(carry)

    for g in groups:
        qaug2_s[g][...] = qaug_s[g][...]
        pb_s[g][...] = jnp.zeros((NSA_KTILE, nqt), BF16)
        sa_s[g][...] = qk_tile(g, 0, qaug2_s[g])

    span = NSA_WIN + NSA_QBLK
    start = pl.multiple_of(jnp.maximum(s0 - NSA_WIN, 0), NSA_QBLK)
    kp = start + lax.broadcasted_iota(jnp.int32, (span, 1), 0)
    wbias = jnp.where((kp <= qpos) & (kp > qpos - NSA_WIN), 0.0, NEG)
    sblk = start // LANES
    o_win = []
    for g in groups:
        kwin = kw_ref[pl.ds(start, span), klanes[g]]
        pw, dens = [], []
        for sl in heads:
            s = _dot(kwin, qaug_s[g][:, sl]) + wbias
            e = jnp.exp2(s - jnp.max(s, axis=0, keepdims=True))
            dens.append(jnp.sum(e, axis=0, keepdims=True))
            pw.append(e.astype(BF16))
        vwt = jnp.concatenate([vw_ref[sblk + i, vrows[g], :] for i in range(span // LANES)], axis=1)
        o_win.append(_dot(vwt, jnp.concatenate(pw, axis=1)) * (1.0 / jnp.concatenate(dens, axis=1)))

    init = tuple((jnp.full((1, nqt), NEG, F32), jnp.zeros((HEAD_DIM + 16, nqt), F32), jnp.ones((1, nqt), F32))
                 for _ in groups)
    group_diag = kt_diag // NSA_UNROLL
    carry = lax.fori_loop(0, group_diag, lambda i, c: tile_group(i, c, False), init)
    carry = tile_group(group_diag, carry, True)

    gates = _sigmoid(jnp.concatenate([g_ref[j] for j in range(qtiles)], axis=1))
    outs = []
    for g in groups:
        acc = carry[g][1]
        o_slc = acc[0:HEAD_DIM] * (1.0 / acc[HEAD_DIM:HEAD_DIM + 1])
        for r, sl in enumerate(heads):
            row = g * 16 + r
            outs.append(gates[row:row + 1, :] * o_cmp[g][:, sl]
                        + gates[row + NSA_RPG:row + NSA_RPG + 1, :] * o_slc[:, sl]
                        + gates[row + 2 * NSA_RPG:row + 2 * NSA_RPG + 1, :] * o_win[g][:, sl])
    o_ref[...] = jnp.concatenate(outs, axis=0).T.astype(o_ref.dtype)


def _nsa_mixer(q_tt, kvc, k_slc, k_win, v_tt, g_tt, bsz, seq, pe, w1, b1, w2, b2):
    nqb = seq // NSA_QBLK
    qtiles = NSA_QBLK // LANES
    ntile = seq // LANES
    npc = seq // NSA_CMP_STRIDE
    ns = seq // NSA_SLC_LEN
    cmp_out = _nsa_compress(kvc, bsz, seq, pe, w1, b1, w2, b2)
    split = _nsa_cmp_split(ns)
    perm = cmp_out.reshape(bsz, 2, NSA_KV, split, ns // split, 4, HEAD_DIM).transpose(0, 1, 2, 3, 5, 4, 6).reshape(
        bsz, 2, NSA_KV, npc, HEAD_DIM)
    kc = perm[:, 0].astype(BF16)
    vc_t = perm[:, 1].transpose(0, 1, 3, 2).astype(BF16)
    nqt = NSA_RPG * NSA_QBLK
    resident = pl.Buffered(1)
    group_scratch = [pltpu.VMEM((ns, NSA_QBLK), F32),
                     pltpu.VMEM((2 * HEAD_DIM, nqt), BF16), pltpu.VMEM((2 * HEAD_DIM, nqt), BF16),
                     pltpu.VMEM((NSA_KTILE, nqt), F32), pltpu.VMEM((NSA_KTILE, nqt), F32),
                     pltpu.VMEM((NSA_KTILE, nqt), BF16), pltpu.VMEM((NSA_KTILE, nqt), BF16)]
    return pl.pallas_call(
        _nsa_kernel,
        grid=(bsz, nqb),
        in_specs=[
            pl.BlockSpec((qtiles, NSA_Q, LANES), lambda b, q: (b * nqb + q, 0, 0)),
            pl.BlockSpec((1, NSA_KV, npc, HEAD_DIM), lambda b, q: (b, 0, 0, 0), pipeline_mode=resident),
            pl.BlockSpec((1, NSA_KV, HEAD_DIM, npc), lambda b, q: (b, 0, 0, 0), pipeline_mode=resident),
            pl.BlockSpec((seq, NSA_KV * NSA_KVW), lambda b, q: (b, 0), pipeline_mode=resident),
            pl.BlockSpec((seq, NSA_KV * NSA_KVW), lambda b, q: (b, 0), pipeline_mode=resident),
            pl.BlockSpec((ntile, NSA_KVW, LANES), lambda b, q: (b, 0, 0), pipeline_mode=resident),
            pl.BlockSpec((ntile, NSA_KVW, LANES), lambda b, q: (b, 1, 0), pipeline_mode=resident),
            pl.BlockSpec((qtiles, NSA_KV * 16, LANES), lambda b, q: (b * nqb + q, 0, 0)),
        ],
        out_specs=pl.BlockSpec((NSA_QBLK, NSA_Q), lambda b, q: (b * nqb + q, 0)),
        out_shape=jax.ShapeDtypeStruct((bsz * seq, NSA_Q), BF16),
        scratch_shapes=group_scratch * NSA_KV,
        compiler_params=_cparams("arbitrary", "arbitrary"),
        name="nsa_attention",
    )(q_tt, kc, vc_t, k_slc, k_win, v_tt, v_tt, g_tt)


def _swa_kernel(q_ref, kp_ref, kc_ref, vp_ref, vc_ref, sink_ref, o_ref):
    step = pl.program_id(1)
    t = SWA_WIN
    nb = SWA_STEP_BLOCKS
    krel = lax.broadcasted_iota(jnp.int32, (2 * t, 1), 0) - t
    qrel = lax.broadcasted_iota(jnp.int32, (1, t), 1)
    in_band = (krel <= qrel) & (krel > qrel - SWA_WIN)
    lowest = jnp.where(step > 0, -t, 0)
    mbias = [jnp.where(in_band & (krel >= lowest), 0.0, NEG)] + [jnp.where(in_band, 0.0, NEG)] * (nb - 1)
    k_all = jnp.concatenate([kp_ref[...], kc_ref[...]], axis=0)
    v_tiles = [vp_ref[0]] + [vc_ref[i] for i in range(nb)]
    scores = {}
    for blk in range(nb):
        kband = k_all[blk * t:(blk + 2) * t]
        for g in range(2):
            rows = slice(g * SWA_RPG * HEAD_DIM, (g + 1) * SWA_RPG * HEAD_DIM)
            qg = q_ref[blk, rows, :]
            qcat = jnp.concatenate([qg[r * HEAD_DIM:(r + 1) * HEAD_DIM, :] for r in range(SWA_RPG)], axis=1)
            zq = jnp.zeros_like(qcat)
            qext = jnp.concatenate([qcat, zq] if g == 0 else [zq, qcat], axis=0)
            scores[blk, g] = _dot(kband, qext)
    for blk in range(nb):
        outs = []
        for g in range(2):
            s = scores[blk, g]
            ps, dens = [], []
            for r in range(SWA_RPG):
                h = g * SWA_RPG + r
                sink = sink_ref[h:h + 1, :]
                sr = s[:, r * t:(r + 1) * t] + mbias[blk]
                mx = jnp.maximum(jnp.max(sr, axis=0, keepdims=True), sink)
                e = jnp.exp2(sr - mx)
                dens.append(jnp.sum(e, axis=0, keepdims=True) + jnp.exp2(sink - mx))
                ps.append(e.astype(BF16))
            vband = jnp.concatenate([v_tiles[blk][g * HEAD_DIM:(g + 1) * HEAD_DIM, :],
                                     v_tiles[blk + 1][g * HEAD_DIM:(g + 1) * HEAD_DIM, :]], axis=1)
            og = _dot(vband, jnp.concatenate(ps, axis=1)) * (1.0 / jnp.concatenate(dens, axis=1))
            outs.append(jnp.concatenate([og[:, r * t:(r + 1) * t] for r in range(SWA_RPG)], axis=0).T)
        o_ref[blk * t:(blk + 1) * t, :] = jnp.concatenate(outs, axis=1).astype(o_ref.dtype)


def _swa_mixer(q_tt, k_nat, v_tt, sinks, bsz, seq):
    nb = SWA_STEP_BLOCKS
    nblk = seq // SWA_WIN
    nstep = nblk // nb
    sink_rows = jnp.broadcast_to((sinks.astype(F32) * math.log2(math.e))[:, None], (sinks.shape[0], LANES))
    prev = lambda b, q: b * nblk + jnp.maximum(nb * q - 1, 0)
    return pl.pallas_call(
        _swa_kernel,
        grid=(bsz, nstep),
        in_specs=[
            pl.BlockSpec((nb, SWA_Q, LANES), lambda b, q: (b * nstep + q, 0, 0)),
            pl.BlockSpec((SWA_WIN, SWA_KVW), lambda b, q: (prev(b, q), 0)),
            pl.BlockSpec((nb * SWA_WIN, SWA_KVW), lambda b, q: (b * nstep + q, 0)),
            pl.BlockSpec((1, SWA_KVW, LANES), lambda b, q: (prev(b, q), 0, 0)),
            pl.BlockSpec((nb, SWA_KVW, LANES), lambda b, q: (b * nstep + q, 0, 0)),
            _full(sink_rows.shape),
        ],
        out_specs=pl.BlockSpec((nb * SWA_WIN, SWA_Q), lambda b, q: (b * nstep + q, 0)),
        out_shape=jax.ShapeDtypeStruct((bsz * seq, SWA_Q), BF16),
        compiler_params=_cparams("parallel", "parallel"),
        name="swa_attention",
    )(q_tt, k_nat, k_nat, v_tt, v_tt, sink_rows)


def _s5_params(a_re, a_im, log_dt, b_re, b_im, c_re, c_im, n_chunks):
    f = F32
    t = S5_CHUNK
    step = jnp.exp(log_dt.astype(f))[:, None]
    lr, li = a_re.astype(f), a_im.astype(f)

    def lam_pow(tau):
        tau = tau.astype(f)[:, None, None]
        mag = jnp.exp(lr * step * tau)
        ang = li * step * tau
        return mag * jnp.cos(ang), mag * jnp.sin(ang)

    lb_r, lb_i = (v[0] for v in lam_pow(jnp.ones((1,))))
    nr, ni = lb_r - 1.0, lb_i
    den = lr * lr + li * li
    fr, fi = (nr * lr + ni * li) / den, (ni * lr - nr * li) / den
    br, bi = b_re.astype(f), b_im.astype(f)
    bb_r = fr[..., None] * br - fi[..., None] * bi
    bb_i = fr[..., None] * bi + fi[..., None] * br
    cr, ci = c_re.astype(f), c_im.astype(f)

    pr, pi = lam_pow(jnp.arange(t + 1))
    cl_r = cr[None] * pr[:, :, None, :] - ci[None] * pi[:, :, None, :]
    cl_i = cr[None] * pi[:, :, None, :] + ci[None] * pr[:, :, None, :]
    kern_t = jnp.einsum("tghp,gpk->gkth", cl_r[:t], bb_r, precision="highest") - jnp.einsum(
        "tghp,gpk->gkth", cl_i[:t], bb_i, precision="highest")
    rr, ri = pr[t - 1 - jnp.arange(t)], pi[t - 1 - jnp.arange(t)]
    bs_r = rr[..., None] * bb_r[None] - ri[..., None] * bb_i[None]
    bs_i = rr[..., None] * bb_i[None] + ri[..., None] * bb_r[None]
    bs = jnp.concatenate([bs_r, bs_i], axis=2)
    bs = bs.transpose(1, 0, 3, 2).reshape(S5_GROUPS, t * S5_GROUP_CH, 2 * S5_STATE)
    cs = jnp.concatenate([cl_r[1:], -cl_i[1:]], axis=3)
    cs = cs.transpose(1, 3, 0, 2).reshape(S5_GROUPS, 2 * S5_STATE, t * S5_GROUP_CH)
    ar, ai = pr[t], pi[t]
    a1, a2 = [], []
    k = 1
    while k < n_chunks:
        a1.append(jnp.concatenate([ar, ar], axis=1))
        a2.append(jnp.concatenate([-ai, ai], axis=1))
        ar, ai = ar * ar - ai * ai, 2.0 * ar * ai
        k *= 2
    a1 = jnp.stack(a1, axis=1)
    a2 = jnp.stack(a2, axis=1)
    noct = S5_GROUPS // S5_OCT
    hc, ns2 = S5_GROUP_CH, 2 * S5_STATE
    width = t * hc
    lanes = t * LANES
    r_i, c_i = np.arange(width)[:, None], np.arange(lanes)[None, :]
    rep = jnp.asarray((r_i // hc == c_i // LANES) & (r_i % hc == c_i % hc), dtype=BF16)
    kern_o = kern_t.reshape(noct, LANES, width).astype(BF16)
    toep_o = pl.pallas_call(
        _s5_toeplitz_kernel,
        grid=(noct, t // 8),
        in_specs=[pl.BlockSpec((1, LANES, width), lambda o, s: (o, 0, 0)), _full((width, lanes))],
        out_specs=pl.BlockSpec((1, 8 * LANES, lanes), lambda o, s: (o, s, 0)),
        out_shape=jax.ShapeDtypeStruct((noct, lanes, lanes), BF16),
        scratch_shapes=[pltpu.VMEM((t, LANES, LANES), F32)],
        compiler_params=_cparams("parallel", "arbitrary"),
        name="s5_toeplitz_table",
    )(kern_o, rep)
    cs_o = pl.pallas_call(
        _s5_readout_kernel,
        grid=(noct,),
        in_specs=[pl.BlockSpec((1, S5_OCT * ns2, width), lambda o: (o, 0, 0)), _full((width, lanes))],
        out_specs=pl.BlockSpec((1, S5_OCT * ns2, lanes), lambda o: (o, 0, 0)),
        out_shape=jax.ShapeDtypeStruct((noct, S5_OCT * ns2, lanes), BF16),
        compiler_params=_cparams("parallel"),
        name="s5_readout_table",
    )(cs.reshape(noct, S5_OCT * ns2, width).astype(BF16), rep)
    bs_slabs = bs.reshape(noct, S5_OCT, t, hc, ns2).transpose(0, 2, 1, 3, 4).reshape(noct, t, LANES, ns2).astype(BF16)
    bs_o = pl.pallas_call(
        _s5_state_in_kernel,
        grid=(noct,),
        in_specs=[pl.BlockSpec((1, t, LANES, ns2), lambda o: (o, 0, 0, 0))],
        out_specs=pl.BlockSpec((1, lanes, S5_OCT * ns2), lambda o: (o, 0, 0)),
        out_shape=jax.ShapeDtypeStruct((noct, lanes, S5_OCT * ns2), BF16),
        compiler_params=_cparams("parallel"),
        name="s5_state_in_table",
    )(bs_slabs)

    def oct_rows(a):
        return a.reshape(noct, S5_OCT, -1, ns2).transpose(0, 2, 1, 3).reshape(noct, -1, S5_OCT * ns2)

    return toep_o, bs_o, cs_o, oct_rows(a1), oct_rows(a2)


def _same_group(shape, row_div, col_mod, col_div):
    r = lax.broadcasted_iota(jnp.int32, shape, 0)
    c = lax.broadcasted_iota(jnp.int32, shape, 1)
    return (r // row_div) == ((c % col_mod) // col_div)


def _s5_toeplitz_kernel(k_ref, rep_ref, o_ref, full_s):
    nlag = full_s.shape[0]
    steps_here = o_ref.shape[1] // LANES

    @pl.when(pl.program_id(1) == 0)
    def _():
        full = _dot(k_ref[0], rep_ref[...])
        full = jnp.where(_same_group(full.shape, S5_GROUP_CH, LANES, S5_GROUP_CH), full, 0.0)
        for lag in range(nlag):
            full_s[lag] = full[:, lag * LANES:(lag + 1) * LANES]

    for i in range(steps_here):
        s = pl.program_id(1) * steps_here + i
        for t in range(nlag):
            tile = full_s[jnp.maximum(t - s, 0)]
            o_ref[0, i * LANES:(i + 1) * LANES, t * LANES:(t + 1) * LANES] = (
                jnp.where(t >= s, tile, 0.0).astype(o_ref.dtype))


def _s5_readout_kernel(c_ref, rep_ref, o_ref):
    full = _dot(c_ref[0], rep_ref[...])
    keep = _same_group(full.shape, 2 * S5_STATE, LANES, S5_GROUP_CH)
    o_ref[0] = jnp.where(keep, full, 0.0).astype(o_ref.dtype)


def _s5_state_in_kernel(b_ref, o_ref):
    keep = _same_group((LANES, o_ref.shape[2]), S5_GROUP_CH, o_ref.shape[2], 2 * S5_STATE)
    for s in range(b_ref.shape[1]):
        full = jnp.concatenate([b_ref[0, s]] * S5_OCT, axis=1)
        o_ref[0, s * LANES:(s + 1) * LANES, :] = jnp.where(keep, full, jnp.zeros((), full.dtype))


def _s5_state_kernel(bsz, u_ref, bs_ref, a1_ref, a2_ref, hi_ref, lo_ref):
    sc = _dot(u_ref[...], bs_ref[0])
    n = sc.shape[0] // bsz
    width = sc.shape[1]
    rowi = lax.broadcasted_iota(jnp.int32, (n, width), 0)

    def swap_re_im(x):
        return jnp.concatenate([pltpu.roll(x[:, j * LANES:(j + 1) * LANES], S5_STATE, 1)
                                for j in range(width // LANES)], axis=1)

    h_in = []
    for b in range(bsz):
        x = sc[b * n:(b + 1) * n]
        k, step = 1, 0
        while k < n:
            xs = jnp.where(rowi >= k, pltpu.roll(x, k, 0), 0.0)
            x = x + a1_ref[0, step:step + 1, :] * xs + a2_ref[0, step:step + 1, :] * swap_re_im(xs)
            k *= 2
            step += 1
        h_in.append(jnp.where(rowi >= 1, pltpu.roll(x, 1, 0), 0.0))
    h_in = jnp.concatenate(h_in, axis=0)
    hi = h_in.astype(BF16)
    hi_ref[0] = hi
    lo_ref[0] = (h_in - hi.astype(F32)).astype(BF16)


def _s5_out_kernel(u_ref, toep_ref, hi_ref, lo_ref, cs_ref, o_ref):
    n = pl.program_id(1)
    cols = toep_ref.shape[2]
    off = _dot(hi_ref[0], cs_ref[0]) + _dot(lo_ref[0], cs_ref[0])
    for nn in range(toep_ref.shape[1] // cols):
        @pl.when(n == nn)
        def _(nn=nn):
            k = (nn + 1) * cols
            y = off + _dot(u_ref[:, 0:k], toep_ref[0, 0:k, :])
            for t8 in range(cols // LANES):
                o_ref[:, t8, :] = y[:, t8 * LANES:(t8 + 1) * LANES].astype(o_ref.dtype)


def _s5_glu_kernel(y_ref, u_ref, d_ref, w_ref, b_ref, o_ref):
    y = _gelu_tanh(y_ref[...] + d_ref[...] * u_ref[...])
    gate = _sigmoid(_dot(y.astype(BF16), w_ref[...]) + b_ref[...])
    o_ref[...] = (y * gate).astype(o_ref.dtype)


def _s5_mixer(u5, u_chunks, bsz, seq, a_re, a_im, log_dt, b_re, b_im, c_re, c_im, d_skip, glu_w, glu_b, tm=1024):
    m = u5.shape[0]
    t = S5_CHUNK
    nch = m // t
    noct = S5_GROUPS // S5_OCT
    lanes = t * LANES
    sw = S5_OCT * 2 * S5_STATE
    toep, bs, cs, a1, a2 = _s5_params(a_re, a_im, log_dt, b_re, b_im, c_re, c_im, nch // bsz)
    nsteps = a1.shape[1]
    h_hi, h_lo = pl.pallas_call(
        functools.partial(_s5_state_kernel, bsz),
        grid=(noct,),
        in_specs=[
            pl.BlockSpec((nch, lanes), lambda o: (0, o)),
            pl.BlockSpec((1, lanes, sw), lambda o: (o, 0, 0)),
            pl.BlockSpec((1, nsteps, sw), lambda o: (o, 0, 0)),
            pl.BlockSpec((1, nsteps, sw), lambda o: (o, 0, 0)),
        ],
        out_specs=[pl.BlockSpec((1, nch, sw), lambda o: (o, 0, 0))] * 2,
        out_shape=[jax.ShapeDtypeStruct((noct, nch, sw), BF16)] * 2,
        compiler_params=_cparams("parallel"),
        name="s5_state",
    )(u_chunks, bs, a1, a2)
    tsub = 8
    rows = nch // 2
    y = pl.pallas_call(
        _s5_out_kernel,
        grid=(noct, t // tsub, nch // rows),
        in_specs=[
            pl.BlockSpec((rows, lanes), lambda o, n, r: (r, o)),
            pl.BlockSpec((1, lanes, tsub * LANES), lambda o, n, r: (o, 0, n)),
            pl.BlockSpec((1, rows, sw), lambda o, n, r: (o, r, 0)),
            pl.BlockSpec((1, rows, sw), lambda o, n, r: (o, r, 0)),
            pl.BlockSpec((1, sw, tsub * LANES), lambda o, n, r: (o, 0, n)),
        ],
        out_specs=pl.BlockSpec((rows, tsub, LANES), lambda o, n, r: (r, n, o)),
        out_shape=jax.ShapeDtypeStruct((nch, t, S5_CH), BF16),
        compiler_params=_cparams("parallel", "parallel", "parallel"),
        name="s5_scan",
    )(u_chunks, toep, h_hi, h_lo, cs).reshape(m, S5_CH)
    return pl.pallas_call(
        _s5_glu_kernel,
        grid=(m // tm,),
        in_specs=[
            pl.BlockSpec((tm, S5_CH), lambda i: (i, 0)),
            pl.BlockSpec((tm, S5_CH), lambda i: (i, 0)),
            _full((1, S5_CH)), _full((S5_CH, S5_CH)), _full((1, S5_CH)),
        ],
        out_specs=pl.BlockSpec((tm, S5_CH), lambda i: (i, 0)),
        out_shape=jax.ShapeDtypeStruct((m, S5_CH), BF16),
        compiler_params=_cparams("parallel"),
        name="s5_glu",
    )(y, u5, d_skip.reshape(1, S5_CH).astype(F32), glu_w.astype(BF16), glu_b.reshape(1, S5_CH).astype(F32))


def _even_mixers(h, bsz, seq, g_mix, w_in, conv_w, conv_b, dt_bias, a_log, d_skip, norm_g, pe, w1, b1, w2, b2):
    d = h.shape[1]
    scale = HEAD_DIM ** -0.5 * math.log2(math.e)
    o = SSD_IN
    w_ssd = jnp.concatenate([w_in[:, :SSD_IN], jnp.zeros((d, LANES - SSD_HEADS), w_in.dtype)], axis=1)
    w_q = w_in[:, o:o + NSA_Q] * scale
    kv = [w_in[:, o + NSA_Q + i * NSA_KVW:o + NSA_Q + (i + 1) * NSA_KVW] for i in range(6)]
    w_gate = w_in[:, o + NSA_Q + 6 * NSA_KVW:].reshape(d, NSA_KV, NSA_RPG, 3).transpose(0, 1, 3, 2)
    w_gate = jnp.concatenate([w_gate.reshape(d, NSA_KV, 12), jnp.zeros((d, NSA_KV, 4), w_in.dtype)],
                             axis=2).reshape(d, NSA_KV * 16)

    def per_group_halves(w):
        wg = w.reshape(d, NSA_KV, HEAD_DIM)
        return jnp.concatenate([wg, jnp.zeros_like(wg)], axis=2).reshape(d, NSA_KV * NSA_KVW)
    segs = [
        ("nat", w_ssd, F32),
        ("tt", w_q, BF16),
        ("pc", jnp.concatenate([kv[0], kv[1]], axis=1), F32),
        ("nat+", per_group_halves(kv[2]), BF16, _nsa_block_onehot(NSA_KTILE)),
        ("nat", per_group_halves(kv[4]), BF16),
        ("tt", jnp.concatenate([kv[3], kv[5]], axis=1), BF16),
        ("tt", w_gate, F32),
    ]
    u_ssd, q_tt, kvc, k_slc, k_win, v_tt, g_tt = _norm_proj(h, g_mix, segs)
    ya = _ssd_mixer(u_ssd, bsz, seq, conv_w, conv_b, dt_bias, a_log, d_skip, norm_g)
    yb = _nsa_mixer(q_tt, kvc, k_slc, k_win, v_tt, g_tt, bsz, seq, pe, w1, b1, w2, b2)
    return ya, yb


def _odd_mixers(h, bsz, seq, g_mix, w_in, sinks, a_re, a_im, log_dt, b_re, b_im, c_re, c_im, d_skip, glu_w, glu_b):
    scale = HEAD_DIM ** -0.5 * math.log2(math.e)
    segs = [
        ("tt", w_in[:, :SWA_Q] * scale, BF16),
        ("nat", w_in[:, SWA_Q:SWA_Q + SWA_KVW], BF16),
        ("tt", w_in[:, SWA_Q + SWA_KVW:SWA_Q + 2 * SWA_KVW], BF16),
        ("nat+ch", w_in[:, SWA_Q + 2 * SWA_KVW:], F32),
    ]
    q_tt, k_nat, v_tt, u5, u_chunks = _norm_proj(h, g_mix, segs)
    yc = _swa_mixer(q_tt, k_nat, v_tt, sinks, bsz, seq)
    yd = _s5_mixer(u5, u_chunks, bsz, seq, a_re, a_im, log_dt, b_re, b_im, c_re, c_im, d_skip, glu_w, glu_b)
    return yc, yd


def kernel(x, norm_mix, norm_mlp, norm_final, mlp_w_up, mlp_w_down, ev_w_in, ev_w_out, ssd_conv_w, ssd_conv_b,
           ssd_dt_bias, ssd_a_log, ssd_d, ssd_norm, nsa_pe, nsa_cmp_w1, nsa_cmp_b1, nsa_cmp_w2, nsa_cmp_b2,
           od_w_in, od_w_out, swa_sinks, s5_a_re, s5_a_im, s5_log_dt, s5_b_re, s5_b_im, s5_c_re, s5_c_im,
           s5_d, s5_glu_w, s5_glu_b):
    bsz, seq, d = x.shape
    depth = norm_mix.shape[0]
    assert seq % (NSA_UNROLL * NSA_KTILE) == 0 and seq >= NSA_WIN + NSA_QBLK
    h = x.reshape(bsz * seq, d)
    for layer in range(depth):
        i = layer // 2
        if layer % 2 == 0:
            ya, yb = _even_mixers(h, bsz, seq, norm_mix[layer], ev_w_in[i], ssd_conv_w[i], ssd_conv_b[i],
                                  ssd_dt_bias[i], ssd_a_log[i], ssd_d[i], ssd_norm[i], nsa_pe[i],
                                  nsa_cmp_w1[i], nsa_cmp_b1[i], nsa_cmp_w2[i], nsa_cmp_b2[i])
            w_out = ev_w_out[i]
        else:
            ya, yb = _odd_mixers(h, bsz, seq, norm_mix[layer], od_w_in[i], swa_sinks[i], s5_a_re[i], s5_a_im[i],
                                 s5_log_dt[i], s5_b_re[i], s5_b_im[i], s5_c_re[i], s5_c_im[i], s5_d[i],
                                 s5_glu_w[i], s5_glu_b[i])
            w_out = od_w_out[i]
        h = _mix_out_mlp(h, ya, yb, w_out, norm_mlp[layer], mlp_w_up[layer], mlp_w_down[layer], norm_final,
                         final=(layer == depth - 1))
    return h.reshape(bsz, seq, d)
```

```python
import functools
import math

import jax
import jax.numpy as jnp
import numpy as np
from jax import lax
from jax.experimental import pallas as pl
from jax.experimental.pallas import tpu as pltpu

F32 = jnp.float32
BF16 = jnp.bfloat16

EPS = 1e-6
NEG = -1e30
FORCE = 1e9
HEAD_DIM = 64
LANES = 128
BF16_ROWS = 16
VMEM_LIMIT_BYTES = 56 * 1024 * 1024

SSD_HEADS = 8
SSD_INNER = 512
SSD_GROUPS = 2
SSD_STATE = 128
SSD_CONV = 4
SSD_CHUNK = 128
SSD_CONV_DIM = 1024
SSD_IN = SSD_INNER + SSD_CONV_DIM + SSD_HEADS

NSA_KV = 2
NSA_RPG = 4
NSA_CMP_LEN = 32
NSA_CMP_STRIDE = 16
NSA_SLC_LEN = 64
NSA_TOPK = 16
NSA_WIN = 512
NSA_CMP_HIDDEN = 256
NSA_QBLK = 128
NSA_Q = 512
NSA_KVW = 128
NSA_KTILE = 512
NSA_UNROLL = 2
NSA_GATE_ROWS = 16

SWA_RPG = 4
SWA_WIN = 128
SWA_Q = 512
SWA_KVW = 128
SWA_STEP_BLOCKS = 4

S5_CH = 512
S5_GROUP_CH = 16
S5_GROUPS = 32
S5_STATE = 64
S5_CHUNK = 32
S5_OCT = LANES // S5_GROUP_CH


def _cparams(*sem):
    return pltpu.CompilerParams(dimension_semantics=sem, vmem_limit_bytes=VMEM_LIMIT_BYTES)


def _full(shape):
    n = len(shape)
    return pl.BlockSpec(shape, lambda *_: (0,) * n)


def _dot(a, b):
    return jnp.dot(a, b, preferred_element_type=F32)


def _dot_nt(a, b):
    return lax.dot_general(a, b, (((1,), (1,)), ((), ())), preferred_element_type=F32)


def _split3(a):
    hi = a.astype(BF16)
    r1 = a - hi.astype(F32)
    mid = r1.astype(BF16)
    lo = (r1 - mid.astype(F32)).astype(BF16)
    return hi, mid, lo


def _dot_exact_rhs(a, b_exact):
    hi, mid, lo = _split3(a)
    return _dot(hi, b_exact) + _dot(mid, b_exact) + _dot(lo, b_exact)


def _dot_exact_lhs(a_exact, b):
    hi, mid, lo = _split3(b)
    return _dot(a_exact, hi) + _dot(a_exact, mid) + _dot(a_exact, lo)


def _rms(x, g):
    return x * lax.rsqrt(jnp.mean(x * x, axis=-1, keepdims=True) + EPS) * g


def _gelu_tanh(x):
    c = math.sqrt(2.0 / math.pi)
    return 0.5 * x * (1.0 + jnp.tanh(c * (x + 0.044715 * (x * x * x))))


def _sigmoid(x):
    return 1.0 / (1.0 + jnp.exp(-x))


def _proj_kernel(kinds, tm, h_ref, g_ref, *refs):
    n = len(kinds)
    n_add = sum(k == "nat+" for k in kinds)
    n_out = n + sum(k == "nat+ch" for k in kinds)
    w_refs, add_refs = refs[:n], list(refs[n:n + n_add])
    o_refs, scratch = list(refs[n + n_add:n + n_add + n_out]), list(refs[n + n_add + n_out:])
    yb = _rms(h_ref[...], g_ref[...]).astype(BF16)
    for kind, w_ref in zip(kinds, w_refs):
        o_ref = o_refs.pop(0)
        if kind == "nat":
            o_ref[...] = _dot(yb, w_ref[...]).astype(o_ref.dtype)
        elif kind == "nat+ch":
            ch_ref, ch_s = o_refs.pop(0), scratch.pop(0)
            res = _dot(yb, w_ref[...])
            o_ref[...] = res.astype(o_ref.dtype)
            for j in range(ch_s.shape[0]):
                ch_s[j] = res[:, j * LANES:(j + 1) * LANES]
            for t in range(S5_CHUNK):
                for j in range(ch_s.shape[0]):
                    col = (j * S5_CHUNK + t) * LANES
                    ch_ref[:, col:col + LANES] = ch_s[j, pl.ds(t, tm // S5_CHUNK, stride=S5_CHUNK), :].astype(ch_ref.dtype)
        elif kind == "nat+":
            o_ref[...] = _dot(yb, w_ref[...]).astype(o_ref.dtype) + add_refs.pop(0)[...]
        elif kind == "pc":
            pc_s = scratch.pop(0)
            res = _dot(yb, w_ref[...])
            flat = NSA_CMP_STRIDE * HEAD_DIM
            per_tile = LANES // HEAD_DIM
            for j in range(pc_s.shape[0]):
                pc_s[j] = res[:, j * LANES:(j + 1) * LANES]
            for p in range(NSA_CMP_STRIDE):
                for j in range(pc_s.shape[0]):
                    tok = pc_s[j, pl.ds(p, tm // NSA_CMP_STRIDE, stride=NSA_CMP_STRIDE), :]
                    for c in range(per_tile):
                        col = (j * per_tile + c) * flat + p * HEAD_DIM
                        o_ref[:, col:col + HEAD_DIM] = tok[:, c * HEAD_DIM:(c + 1) * HEAD_DIM].astype(o_ref.dtype)
        else:
            ot = _dot_nt(w_ref[...], yb)
            for j in range(tm // LANES):
                o_ref[j] = ot[:, j * LANES:(j + 1) * LANES].astype(o_ref.dtype)


def _norm_proj(h, g, segs, tm=512):
    m, d = h.shape
    kinds = tuple(s[0] for s in segs)
    adds = [s[3] for s in segs if s[0] == "nat+"]
    ws, w_specs, out_shapes, out_specs, scratch = [], [], [], [], []
    for kind, w, dt in (s[:3] for s in segs):
        n_out = w.shape[1]
        if kind == "pc":
            ws.append(w.astype(BF16))
            w_specs.append(_full((d, n_out)))
            out_shapes.append(jax.ShapeDtypeStruct((m // NSA_CMP_STRIDE, NSA_CMP_STRIDE * n_out), dt))
            out_specs.append(pl.BlockSpec((tm // NSA_CMP_STRIDE, NSA_CMP_STRIDE * n_out), lambda i: (i, 0)))
            scratch.append(pltpu.VMEM((n_out // LANES, tm, LANES), F32))
        elif kind in ("nat", "nat+", "nat+ch"):
            ws.append(w.astype(BF16))
            w_specs.append(_full((d, n_out)))
            out_shapes.append(jax.ShapeDtypeStruct((m, n_out), dt))
            out_specs.append(pl.BlockSpec((tm, n_out), lambda i: (i, 0)))
            if kind == "nat+ch":
                out_shapes.append(jax.ShapeDtypeStruct((m // S5_CHUNK, S5_CHUNK * n_out), BF16))
                out_specs.append(pl.BlockSpec((tm // S5_CHUNK, S5_CHUNK * n_out), lambda i: (i, 0)))
                scratch.append(pltpu.VMEM((n_out // LANES, tm, LANES), F32))
        else:
            ws.append(w.T.astype(BF16))
            w_specs.append(_full((n_out, d)))
            out_shapes.append(jax.ShapeDtypeStruct((m // LANES, n_out, LANES), dt))
            out_specs.append(pl.BlockSpec((tm // LANES, n_out, LANES), lambda i: (i, 0, 0)))
    return pl.pallas_call(
        functools.partial(_proj_kernel, kinds, tm),
        grid=(m // tm,),
        in_specs=[pl.BlockSpec((tm, d), lambda i: (i, 0)), _full((1, d))] + w_specs + [
            pl.BlockSpec((tm, a.shape[1]), functools.partial(lambda i, nb: (i % nb, 0), nb=a.shape[0] // tm))
            for a in adds],
        out_specs=out_specs,
        out_shape=out_shapes,
        scratch_shapes=scratch,
        compiler_params=_cparams("parallel"),
        name="norm_proj",
    )(h, g.reshape(1, d), *ws, *adds)


def _mlp_kernel(final, h_ref, ya_ref, yb_ref, woa_ref, wob_ref, gm_ref, wup_ref, wdn_ref, gf_ref,
                o_ref, h2_s, xn_s, acc_s):
    j = pl.program_id(1)

    @pl.when(j == 0)
    def _():
        h2 = h_ref[...] + _dot(ya_ref[...], woa_ref[...]) + _dot(yb_ref[...], wob_ref[...])
        h2_s[...] = h2
        xn_s[...] = _rms(h2, gm_ref[...]).astype(BF16)
        acc_s[...] = jnp.zeros_like(acc_s)

    hid = jnp.square(jnp.maximum(_dot(xn_s[...], wup_ref[...]), 0.0))
    acc_s[...] += _dot(hid.astype(BF16), wdn_ref[...])

    @pl.when(j == pl.num_programs(1) - 1)
    def _():
        out = h2_s[...] + acc_s[...]
        if final:
            out = _rms(out, gf_ref[...])
        o_ref[...] = out


def _mix_out_mlp(h, ya, yb, w_out, g_mlp, w_up, w_down, g_final, final, tm=1024, tf=1024):
    m, d = h.shape
    dff = w_up.shape[1]
    na = ya.shape[1]
    nb = yb.shape[1]
    return pl.pallas_call(
        functools.partial(_mlp_kernel, final),
        grid=(m // tm, dff // tf),
        in_specs=[
            pl.BlockSpec((tm, d), lambda i, j: (i, 0)),
            pl.BlockSpec((tm, na), lambda i, j: (i, 0)),
            pl.BlockSpec((tm, nb), lambda i, j: (i, 0)),
            _full((na, d)), _full((nb, d)), _full((1, d)),
            pl.BlockSpec((d, tf), lambda i, j: (0, j)),
            pl.BlockSpec((tf, d), lambda i, j: (j, 0)),
            _full((1, d)),
        ],
        out_specs=pl.BlockSpec((tm, d), lambda i, j: (i, 0)),
        out_shape=jax.ShapeDtypeStruct((m, d), F32),
        scratch_shapes=[pltpu.VMEM((tm, d), F32), pltpu.VMEM((tm, d), BF16), pltpu.VMEM((tm, d), F32)],
        compiler_params=_cparams("parallel", "arbitrary"),
        name="out_proj_mlp",
    )(h, ya, yb, w_out[:na].astype(BF16), w_out[na:].astype(BF16), g_mlp.reshape(1, d),
      w_up.astype(BF16), w_down.astype(BF16), g_final.reshape(1, d))


def _ssd_kernel(u_ref, cw_ref, cb_ref, dtb_ref, alog_ref, dsk_ref, ng_ref, o_ref, xext_s, st_s):
    t = SSD_CHUNK
    c = pl.program_id(1)

    @pl.when(c == 0)
    def _():
        xext_s[0:8, :] = jnp.zeros((8, SSD_CONV_DIM), F32)
        st_s[...] = jnp.zeros_like(st_s)

    z = u_ref[:, 0:SSD_INNER]
    dt_raw = u_ref[:, SSD_INNER + SSD_CONV_DIM:]
    xext_s[8:8 + t, :] = u_ref[:, SSD_INNER:SSD_INNER + SSD_CONV_DIM]
    xfull = xext_s[...]
    conv = cb_ref[...] + cw_ref[SSD_CONV - 1:SSD_CONV, :] * xfull[8:8 + t]
    for back in range(1, SSD_CONV):
        k = SSD_CONV - 1 - back
        conv = conv + cw_ref[k:k + 1, :] * pltpu.roll(xfull, back, 0)[8:8 + t]
    xext_s[0:8, :] = xext_s[t:t + 8, :]
    xc = conv * _sigmoid(conv)
    xs = xc[:, 0:SSD_INNER]
    gn = SSD_GROUPS * SSD_STATE

    dtp = dt_raw + dtb_ref[...]
    dt = jnp.maximum(dtp, 0.0) + jnp.log1p(jnp.exp(-jnp.abs(dtp)))
    a = -jnp.exp(alog_ref[...])
    da = dt * a

    row = lax.broadcasted_iota(jnp.int32, (t, t), 0)
    col = lax.broadcasted_iota(jnp.int32, (t, t), 1)
    causal = col <= row
    tril = jnp.where(causal, 1.0, 0.0).astype(BF16)
    a_cum = _dot_exact_lhs(tril, da)
    er = lax.broadcasted_iota(jnp.int32, (LANES, SSD_INNER), 0)
    ec = lax.broadcasted_iota(jnp.int32, (LANES, SSD_INNER), 1)
    expand = jnp.where((ec >> 6) == er, 1.0, 0.0).astype(BF16)
    a_cum_x = _dot_exact_rhs(a_cum, expand)
    dt_x = _dot_exact_rhs(dt, expand)
    a_cum_t = a_cum.T
    a_last_x = a_cum_x[t - 1:t, :]
    decay_end_x = jnp.exp(a_last_x - a_cum_x)
    decay_in_x = jnp.exp(a_cum_x)
    chunk_decay_x = jnp.exp(a_last_x)

    xd = xs * dt_x
    xd_end = (xd * decay_end_x).astype(BF16)
    xd_b = xd.astype(BF16)
    lane = lax.broadcasted_iota(jnp.int32, (t, LANES), 1)
    first_half = lane < HEAD_DIM

    pieces = []
    for g in range(SSD_GROUPS):
        bm = xc[:, SSD_INNER + g * SSD_STATE:SSD_INNER + (g + 1) * SSD_STATE]
        cm = xc[:, SSD_INNER + gn + g * SSD_STATE:SSD_INNER + gn + (g + 1) * SSD_STATE].astype(BF16)
        bm_t = bm.T.astype(BF16)
        cb = _dot_nt(cm, bm.astype(BF16))
        for pr in range(2):
            i = g * 2 + pr
            sl = slice(i * LANES, (i + 1) * LANES)
            ms = []
            for hh in range(2):
                h = 2 * i + hh
                seg = a_cum[:, h:h + 1] - a_cum_t[h:h + 1, :]
                dec = jnp.exp(jnp.where(causal, seg, NEG))
                ms.append((cb * dec).astype(BF16))
            y_diag = jnp.where(first_half, _dot(ms[0], xd_b[:, sl]), _dot(ms[1], xd_b[:, sl]))
            st = st_s[i]
            y_off = _dot(cm, st.astype(BF16)) * decay_in_x[:, sl]
            st_s[i] = st * chunk_decay_x[:, sl] + _dot(bm_t, xd_end[:, sl])
            pieces.append(y_diag + y_off)
    y = jnp.concatenate(pieces, axis=1) + xs * dsk_ref[...]
    y = y * (z * _sigmoid(z))
    half = SSD_INNER // SSD_GROUPS
    outs = [_rms(y[:, g * half:(g + 1) * half], ng_ref[:, g * half:(g + 1) * half]) for g in range(SSD_GROUPS)]
    o_ref[...] = jnp.concatenate(outs, axis=1).astype(o_ref.dtype)


def _ssd_mixer(u_ssd, bsz, seq, conv_w, conv_b, dt_bias, a_log, d_skip, norm_g):
    m, width = u_ssd.shape
    nch = seq // SSD_CHUNK
    pad = LANES - SSD_HEADS

    def padded(v):
        return jnp.concatenate([v.astype(F32), jnp.zeros((pad,), F32)]).reshape(1, LANES)

    return pl.pallas_call(
        _ssd_kernel,
        grid=(bsz, nch),
        in_specs=[
            pl.BlockSpec((SSD_CHUNK, width), lambda b, c: (b * nch + c, 0)),
            _full((SSD_CONV, SSD_CONV_DIM)), _full((1, SSD_CONV_DIM)),
            _full((1, LANES)), _full((1, LANES)), _full((1, SSD_INNER)), _full((1, SSD_INNER)),
        ],
        out_specs=pl.BlockSpec((SSD_CHUNK, SSD_INNER), lambda b, c: (b * nch + c, 0)),
        out_shape=jax.ShapeDtypeStruct((m, SSD_INNER), BF16),
        scratch_shapes=[pltpu.VMEM((SSD_CHUNK + 8, SSD_CONV_DIM), F32),
                        pltpu.VMEM((SSD_HEADS // 2, SSD_STATE, LANES), F32)],
        compiler_params=_cparams("arbitrary", "arbitrary"),
        name="ssd_mixer",
    )(u_ssd, conv_w.astype(F32), conv_b.reshape(1, -1).astype(F32), padded(dt_bias), padded(a_log),
      jnp.repeat(d_skip.astype(F32), HEAD_DIM).reshape(1, SSD_INNER), norm_g.reshape(1, SSD_INNER).astype(F32))


def _nsa_compress_kernel(x_ref, pe_ref, w1_ref, b1_ref, w2_ref, b2_ref, o_ref):
    x = x_ref[...]
    npc = x.shape[0]
    half = NSA_CMP_STRIDE * HEAD_DIM
    top = _dot((x + pe_ref[0, :, 0:half]).astype(BF16), w1_ref[0, 0:half, :])
    bot = _dot((x + pe_ref[0, :, half:]).astype(BF16), w1_ref[0, half:, :])
    pre = top + pltpu.roll(bot, npc - 1, 0) + b1_ref[0]
    hid = _gelu_tanh(pre).astype(BF16)
    out = _dot(hid, w2_ref[0]) + b2_ref[0]
    rowi = lax.broadcasted_iota(jnp.int32, out.shape, 0)
    o_ref[0, 0, 0] = jnp.where(rowi < npc - 1, out, 0.0)


def _nsa_compress(kvc_pieces, bsz, seq, pe, w1, b1, w2, b2):
    npc = seq // NSA_CMP_STRIDE
    flat = NSA_CMP_STRIDE * HEAD_DIM
    x = kvc_pieces
    return pl.pallas_call(
        _nsa_compress_kernel,
        grid=(bsz, 2, NSA_KV),
        in_specs=[
            pl.BlockSpec((npc, flat), lambda b, s, g: (b, s * NSA_KV + g)),
            pl.BlockSpec((1, 1, 2 * flat), lambda b, s, g: (s, 0, 0)),
            pl.BlockSpec((1, 2 * flat, NSA_CMP_HIDDEN), lambda b, s, g: (s, 0, 0)),
            pl.BlockSpec((1, 1, NSA_CMP_HIDDEN), lambda b, s, g: (s, 0, 0)),
            pl.BlockSpec((1, NSA_CMP_HIDDEN, HEAD_DIM), lambda b, s, g: (s, 0, 0)),
            pl.BlockSpec((1, 1, HEAD_DIM), lambda b, s, g: (s, 0, 0)),
        ],
        out_specs=pl.BlockSpec((1, 1, 1, npc, HEAD_DIM), lambda b, s, g: (b, s, g, 0, 0)),
        out_shape=jax.ShapeDtypeStruct((bsz, 2, NSA_KV, npc, HEAD_DIM), F32),
        compiler_params=_cparams("parallel", "parallel", "parallel"),
        name="nsa_compress",
    )(x, pe.reshape(2, 1, 2 * flat).astype(F32), w1.astype(BF16), b1.reshape(2, 1, -1).astype(F32),
      w2.astype(BF16), b2.reshape(2, 1, -1).astype(F32))


def _nsa_cmp_split(ns):
    return max(1, min(4, (4 * ns) // LANES))


def _nsa_block_onehot(rows):
    r = lax.broadcasted_iota(jnp.int32, (rows, 2 * NSA_KVW), 0)
    c = lax.broadcasted_iota(jnp.int32, (rows, 2 * NSA_KVW), 1)
    blk = (r % NSA_KTILE) // NSA_SLC_LEN
    return jnp.where((c % NSA_KVW) == HEAD_DIM + blk, 1.0, 0.0).astype(BF16)


def _nsa_kernel(q_ref, kc_ref, vc_ref, ks_ref, kw_ref, vs_ref, vw_ref, g_ref, o_ref, *scratch):
    groups = range(NSA_KV)
    per = len(scratch) // NSA_KV
    bias_s, qaug_s, qaug2_s, sa_s, sb_s, pa_s, pb_s = (
        tuple(scratch[g * per + i] for g in groups) for i in range(per))
    qb = pl.program_id(1)
    nqt = NSA_RPG * NSA_QBLK
    s0 = qb * NSA_QBLK
    ncp = kc_ref.shape[2]
    ns = ncp // 4
    heads = [slice(r * NSA_QBLK, (r + 1) * NSA_QBLK) for r in range(NSA_RPG)]
    klanes = [slice(g * NSA_KVW, (g + 1) * NSA_KVW) for g in groups]
    vrows = [slice(g * HEAD_DIM, (g + 1) * HEAD_DIM) for g in groups]

    qtiles = NSA_QBLK // LANES
    qcat = [jnp.concatenate([q_ref[j, (g * NSA_RPG + r) * HEAD_DIM:(g * NSA_RPG + r + 1) * HEAD_DIM, :]
                             for r in range(NSA_RPG) for j in range(qtiles)], axis=1) for g in groups]
    qpos = s0 + lax.broadcasted_iota(jnp.int32, (1, NSA_QBLK), 1)
    for g in groups:
        qaug_s[g][0:HEAD_DIM, :] = qcat[g]
        qaug_s[g][HEAD_DIM:, :] = jnp.zeros((HEAD_DIM, nqt), BF16)

    split = _nsa_cmp_split(ns)
    chunk = ns // split
    cvalid = jnp.where(qpos >= NSA_CMP_LEN - 1, 1.0, 0.0)
    cur = qpos // NSA_SLC_LEN
    taken = -3.0e38

    def cmp_and_select(nchunks):
        out = []
        for g in groups:
            out.extend(cmp_and_select_group(g, nchunks))
        return tuple(out)

    def cmp_and_select_group(g, nchunks):
        rows, jmax = nchunks * 4 * chunk, nchunks * chunk
        kc = kc_ref[0, g, 0:rows, :]
        rc = lax.broadcasted_iota(jnp.int32, (rows, 1), 0)
        ncmp = 4 * ((rc // (4 * chunk)) * chunk + rc % chunk) + (rc % (4 * chunk)) // chunk
        cbias = jnp.where((ncmp * NSA_CMP_STRIDE + (NSA_CMP_LEN - 1)) <= qpos, 0.0, NEG)
        psum = jnp.zeros((rows, NSA_QBLK), F32)
        p_all = []
        for sl in heads:
            s = _dot(kc, qcat[g][:, sl]) + cbias
            e = jnp.exp2(s - jnp.max(s, axis=0, keepdims=True))
            p = e * (cvalid / jnp.sum(e, axis=0, keepdims=True))
            psum = psum + p
            p_all.append(p.astype(BF16))
        o_cmp = _dot(vc_ref[0, g, :, 0:rows], jnp.concatenate(p_all, axis=1))

        tot, p3 = [], []
        for c in range(nchunks):
            part = [psum[(4 * c + i) * chunk:(4 * c + i + 1) * chunk] for i in range(4)]
            tot.append(part[0] + part[1] + part[2] + part[3])
            p3.append(part[3])
        tot, p3 = jnp.concatenate(tot, axis=0), jnp.concatenate(p3, axis=0)
        rj = lax.broadcasted_iota(jnp.int32, (jmax, NSA_QBLK), 0)
        imp = tot + jnp.where(rj >= 1, pltpu.roll(p3, 1, 0), 0.0)
        forced = (rj == 0) | (rj == cur) | (rj == cur - 1)
        rjf = rj.astype(F32)
        imp = jnp.where(forced, taken, jnp.where(rj <= cur, imp, -FORCE))
        for _ in range(min(NSA_TOPK, ns) - 3):
            mx = jnp.max(imp, axis=0, keepdims=True)
            first = jnp.min(jnp.where(imp == mx, rjf, float(ns)), axis=0, keepdims=True)
            imp = jnp.where(rjf == first, taken, imp)
        bias = jnp.where(imp == taken, 0.0, NEG)
        if jmax < ns:
            bias = jnp.concatenate([bias, jnp.full((ns - jmax, NSA_QBLK), NEG, F32)], axis=0)
        return o_cmp, bias

    last_block = (s0 + NSA_QBLK - 1) // NSA_SLC_LEN
    selected = lax.switch(last_block // chunk, [functools.partial(cmp_and_select, n + 1) for n in range(split)])
    o_cmp = [selected[2 * g] for g in groups]
    for g in groups:
        bias_s[g][...] = selected[2 * g + 1]

    kt_diag = s0 // NSA_KTILE
    blocks_per_tile = NSA_KTILE // NSA_SLC_LEN
    vtiles = NSA_KTILE // LANES

    def qk_tile(g, kt, qaug_ref):
        k0 = pl.multiple_of(kt * NSA_KTILE, NSA_KTILE)
        b8 = bias_s[g][pl.ds(pl.multiple_of(kt * blocks_per_tile, blocks_per_tile), blocks_per_tile), :]
        b16 = jnp.concatenate([b8, jnp.zeros_like(b8)], axis=0).astype(BF16)
        qaug_ref[HEAD_DIM:HEAD_DIM + BF16_ROWS, :] = jnp.concatenate([b16] * NSA_RPG, axis=1)
        return _dot(ks_ref[pl.ds(k0, NSA_KTILE), klanes[g]], qaug_ref[...])

    ones_rows = jnp.ones((BF16_ROWS, NSA_KTILE), BF16)

    def pv_tile(g, kt, p):
        vt = jnp.concatenate([vs_ref[kt * vtiles + i, vrows[g], :] for i in range(vtiles)], axis=1)
        return _dot(jnp.concatenate([vt, ones_rows], axis=0), p)

    def softmax_tile(s, m_old):
        m_new = jnp.maximum(m_old, jnp.max(s, axis=0, keepdims=True))
        p = jnp.exp2((s - m_new).astype(BF16))
        return p, m_new, jnp.exp2(m_old - m_new)

    def visible(kt):
        kpos = kt * NSA_KTILE + lax.broadcasted_iota(jnp.int32, (NSA_KTILE, 1), 0)
        return jnp.concatenate([kpos <= qpos] * NSA_RPG, axis=1)

    def tile_group(i, carry, last):
        carry = list(carry)
        for k in range(NSA_UNROLL):
            t = NSA_UNROLL * i + k
            for g in groups:
                m_run, acc, alpha_prev = carry[g]
                s_cur, p_cur, s_nxt, p_prv, qa = ((sa_s[g], pa_s[g], sb_s[g], pb_s[g], qaug_s[g]) if k % 2 == 0 else
                                                  (sb_s[g], pb_s[g], sa_s[g], pa_s[g], qaug2_s[g]))
                acc = alpha_prev * acc + pv_tile(g, jnp.maximum(t - 1, 0), p_prv[...])
                s = jnp.where(visible(t), s_cur[...], NEG) if last else s_cur[...]
                p, m_run, alpha_prev = softmax_tile(s, m_run)
                p_cur[...] = p
                if last and k == NSA_UNROLL - 1:
                    acc = alpha_prev * acc + pv_tile(g, t, p_cur[...])
                else:
                    s_nxt[...] = qk_tile(g, t + 1, qa)
                carry[g] = (m_run, acc, alpha_prev)
        return tuple(carry)

    for g in groups:
        qaug2_s[g][...] = qaug_s[g][...]
        pb_s[g][...] = jnp.zeros((NSA_KTILE, nqt), BF16)
        sa_s[g][...] = qk_tile(g, 0, qaug2_s[g])

    span = NSA_WIN + NSA_QBLK
    start = pl.multiple_of(jnp.maximum(s0 - NSA_WIN, 0), NSA_QBLK)
    kp = start + lax.broadcasted_iota(jnp.int32, (span, 1), 0)
    wbias = jnp.where((kp <= qpos) & (kp > qpos - NSA_WIN), 0.0, NEG)
    sblk = start // LANES
    o_win = []
    for g in groups:
        kwin = kw_ref[pl.ds(start, span), klanes[g]]
        pw, dens = [], []
        for sl in heads:
            s = _dot(kwin, qaug_s[g][:, sl]) + wbias
            e = jnp.exp2(s - jnp.max(s, axis=0, keepdims=True))
            dens.append(jnp.sum(e, axis=0, keepdims=True))
            pw.append(e.astype(BF16))
        vwt = jnp.concatenate([vw_ref[sblk + i, vrows[g], :] for i in range(span // LANES)], axis=1)
        o_win.append(_dot(vwt, jnp.concatenate(pw, axis=1)) * (1.0 / jnp.concatenate(dens, axis=1)))

    init = tuple((jnp.full((1, nqt), NEG, F32), jnp.zeros((HEAD_DIM + BF16_ROWS, nqt), F32), jnp.ones((1, nqt), F32))
                 for _ in groups)
    group_diag = kt_diag // NSA_UNROLL
    carry = lax.fori_loop(0, group_diag, lambda i, c: tile_group(i, c, False), init)
    carry = tile_group(group_diag, carry, True)

    gates = _sigmoid(jnp.concatenate([g_ref[j] for j in range(qtiles)], axis=1))
    outs = []
    for g in groups:
        acc = carry[g][1]
        o_slc = acc[0:HEAD_DIM] * (1.0 / acc[HEAD_DIM:HEAD_DIM + 1])
        for r, sl in enumerate(heads):
            row = g * NSA_GATE_ROWS + r
            outs.append(gates[row:row + 1, :] * o_cmp[g][:, sl]
                        + gates[row + NSA_RPG:row + NSA_RPG + 1, :] * o_slc[:, sl]
                        + gates[row + 2 * NSA_RPG:row + 2 * NSA_RPG + 1, :] * o_win[g][:, sl])
    o_ref[...] = jnp.concatenate(outs, axis=0).T.astype(o_ref.dtype)


def _nsa_mixer(q_tt, kvc, k_slc, k_win, v_tt, g_tt, bsz, seq, pe, w1, b1, w2, b2):
    nqb = seq // NSA_QBLK
    qtiles = NSA_QBLK // LANES
    ntile = seq // LANES
    npc = seq // NSA_CMP_STRIDE
    ns = seq // NSA_SLC_LEN
    cmp_out = _nsa_compress(kvc, bsz, seq, pe, w1, b1, w2, b2)
    split = _nsa_cmp_split(ns)
    perm = cmp_out.reshape(bsz, 2, NSA_KV, split, ns // split, 4, HEAD_DIM).transpose(0, 1, 2, 3, 5, 4, 6).reshape(
        bsz, 2, NSA_KV, npc, HEAD_DIM)
    kc = perm[:, 0].astype(BF16)
    vc_t = perm[:, 1].transpose(0, 1, 3, 2).astype(BF16)
    nqt = NSA_RPG * NSA_QBLK
    resident = pl.Buffered(1)
    group_scratch = [pltpu.VMEM((ns, NSA_QBLK), F32),
                     pltpu.VMEM((2 * HEAD_DIM, nqt), BF16), pltpu.VMEM((2 * HEAD_DIM, nqt), BF16),
                     pltpu.VMEM((NSA_KTILE, nqt), F32), pltpu.VMEM((NSA_KTILE, nqt), F32),
                     pltpu.VMEM((NSA_KTILE, nqt), BF16), pltpu.VMEM((NSA_KTILE, nqt), BF16)]
    return pl.pallas_call(
        _nsa_kernel,
        grid=(bsz, nqb),
        in_specs=[
            pl.BlockSpec((qtiles, NSA_Q, LANES), lambda b, q: (b * nqb + q, 0, 0)),
            pl.BlockSpec((1, NSA_KV, npc, HEAD_DIM), lambda b, q: (b, 0, 0, 0), pipeline_mode=resident),
            pl.BlockSpec((1, NSA_KV, HEAD_DIM, npc), lambda b, q: (b, 0, 0, 0), pipeline_mode=resident),
            pl.BlockSpec((seq, NSA_KV * NSA_KVW), lambda b, q: (b, 0), pipeline_mode=resident),
            pl.BlockSpec((seq, NSA_KV * NSA_KVW), lambda b, q: (b, 0), pipeline_mode=resident),
            pl.BlockSpec((ntile, NSA_KVW, LANES), lambda b, q: (b, 0, 0), pipeline_mode=resident),
            pl.BlockSpec((ntile, NSA_KVW, LANES), lambda b, q: (b, 1, 0), pipeline_mode=resident),
            pl.BlockSpec((qtiles, NSA_KV * NSA_GATE_ROWS, LANES), lambda b, q: (b * nqb + q, 0, 0)),
        ],
        out_specs=pl.BlockSpec((NSA_QBLK, NSA_Q), lambda b, q: (b * nqb + q, 0)),
        out_shape=jax.ShapeDtypeStruct((bsz * seq, NSA_Q), BF16),
        scratch_shapes=group_scratch * NSA_KV,
        compiler_params=_cparams("arbitrary", "arbitrary"),
        name="nsa_attention",
    )(q_tt, kc, vc_t, k_slc, k_win, v_tt, v_tt, g_tt)


def _swa_kernel(q_ref, kp_ref, kc_ref, vp_ref, vc_ref, sink_ref, o_ref):
    step = pl.program_id(1)
    t = SWA_WIN
    nb = SWA_STEP_BLOCKS
    krel = lax.broadcasted_iota(jnp.int32, (2 * t, 1), 0) - t
    qrel = lax.broadcasted_iota(jnp.int32, (1, t), 1)
    in_band = (krel <= qrel) & (krel > qrel - SWA_WIN)
    lowest = jnp.where(step > 0, -t, 0)
    mbias = [jnp.where(in_band & (krel >= lowest), 0.0, NEG)] + [jnp.where(in_band, 0.0, NEG)] * (nb - 1)
    k_all = jnp.concatenate([kp_ref[...], kc_ref[...]], axis=0)
    v_tiles = [vp_ref[0]] + [vc_ref[i] for i in range(nb)]
    scores = {}
    for blk in range(nb):
        kband = k_all[blk * t:(blk + 2) * t]
        for g in range(2):
            rows = slice(g * SWA_RPG * HEAD_DIM, (g + 1) * SWA_RPG * HEAD_DIM)
            qg = q_ref[blk, rows, :]
            qcat = jnp.concatenate([qg[r * HEAD_DIM:(r + 1) * HEAD_DIM, :] for r in range(SWA_RPG)], axis=1)
            zq = jnp.zeros_like(qcat)
            qext = jnp.concatenate([qcat, zq] if g == 0 else [zq, qcat], axis=0)
            scores[blk, g] = _dot(kband, qext)
    for blk in range(nb):
        outs = []
        for g in range(2):
            s = scores[blk, g]
            ps, dens = [], []
            for r in range(SWA_RPG):
                h = g * SWA_RPG + r
                sink = sink_ref[h:h + 1, :]
                sr = s[:, r * t:(r + 1) * t] + mbias[blk]
                mx = jnp.maximum(jnp.max(sr, axis=0, keepdims=True), sink)
                e = jnp.exp2(sr - mx)
                dens.append(jnp.sum(e, axis=0, keepdims=True) + jnp.exp2(sink - mx))
                ps.append(e.astype(BF16))
            vband = jnp.concatenate([v_tiles[blk][g * HEAD_DIM:(g + 1) * HEAD_DIM, :],
                                     v_tiles[blk + 1][g * HEAD_DIM:(g + 1) * HEAD_DIM, :]], axis=1)
            og = _dot(vband, jnp.concatenate(ps, axis=1)) * (1.0 / jnp.concatenate(dens, axis=1))
            outs.append(jnp.concatenate([og[:, r * t:(r + 1) * t] for r in range(SWA_RPG)], axis=0).T)
        o_ref[blk * t:(blk + 1) * t, :] = jnp.concatenate(outs, axis=1).astype(o_ref.dtype)


def _swa_mixer(q_tt, k_nat, v_tt, sinks, bsz, seq):
    nb = SWA_STEP_BLOCKS
    nblk = seq // SWA_WIN
    nstep = nblk // nb
    sink_rows = jnp.broadcast_to((sinks.astype(F32) * math.log2(math.e))[:, None], (sinks.shape[0], LANES))
    prev = lambda b, q: b * nblk + jnp.maximum(nb * q - 1, 0)
    return pl.pallas_call(
        _swa_kernel,
        grid=(bsz, nstep),
        in_specs=[
            pl.BlockSpec((nb, SWA_Q, LANES), lambda b, q: (b * nstep + q, 0, 0)),
            pl.BlockSpec((SWA_WIN, SWA_KVW), lambda b, q: (prev(b, q), 0)),
            pl.BlockSpec((nb * SWA_WIN, SWA_KVW), lambda b, q: (b * nstep + q, 0)),
            pl.BlockSpec((1, SWA_KVW, LANES), lambda b, q: (prev(b, q), 0, 0)),
            pl.BlockSpec((nb, SWA_KVW, LANES), lambda b, q: (b * nstep + q, 0, 0)),
            _full(sink_rows.shape),
        ],
        out_specs=pl.BlockSpec((nb * SWA_WIN, SWA_Q), lambda b, q: (b * nstep + q, 0)),
        out_shape=jax.ShapeDtypeStruct((bsz * seq, SWA_Q), BF16),
        compiler_params=_cparams("parallel", "parallel"),
        name="swa_attention",
    )(q_tt, k_nat, k_nat, v_tt, v_tt, sink_rows)


def _s5_params(a_re, a_im, log_dt, b_re, b_im, c_re, c_im, n_chunks):
    f = F32
    t = S5_CHUNK
    step = jnp.exp(log_dt.astype(f))[:, None]
    lr, li = a_re.astype(f), a_im.astype(f)

    def lam_pow(tau):
        tau = tau.astype(f)[:, None, None]
        mag = jnp.exp(lr * step * tau)
        ang = li * step * tau
        return mag * jnp.cos(ang), mag * jnp.sin(ang)

    lb_r, lb_i = (v[0] for v in lam_pow(jnp.ones((1,))))
    nr, ni = lb_r - 1.0, lb_i
    den = lr * lr + li * li
    fr, fi = (nr * lr + ni * li) / den, (ni * lr - nr * li) / den
    br, bi = b_re.astype(f), b_im.astype(f)
    bb_r = fr[..., None] * br - fi[..., None] * bi
    bb_i = fr[..., None] * bi + fi[..., None] * br
    cr, ci = c_re.astype(f), c_im.astype(f)

    pr, pi = lam_pow(jnp.arange(t + 1))
    cl_r = cr[None] * pr[:, :, None, :] - ci[None] * pi[:, :, None, :]
    cl_i = cr[None] * pi[:, :, None, :] + ci[None] * pr[:, :, None, :]
    kern_t = jnp.einsum("tghp,gpk->gkth", cl_r[:t], bb_r, precision="highest") - jnp.einsum(
        "tghp,gpk->gkth", cl_i[:t], bb_i, precision="highest")
    rr, ri = pr[t - 1 - jnp.arange(t)], pi[t - 1 - jnp.arange(t)]
    bs_r = rr[..., None] * bb_r[None] - ri[..., None] * bb_i[None]
    bs_i = rr[..., None] * bb_i[None] + ri[..., None] * bb_r[None]
    bs = jnp.concatenate([bs_r, bs_i], axis=2)
    bs = bs.transpose(1, 0, 3, 2).reshape(S5_GROUPS, t * S5_GROUP_CH, 2 * S5_STATE)
    cs = jnp.concatenate([cl_r[1:], -cl_i[1:]], axis=3)
    cs = cs.transpose(1, 3, 0, 2).reshape(S5_GROUPS, 2 * S5_STATE, t * S5_GROUP_CH)
    ar, ai = pr[t], pi[t]
    a1, a2 = [], []
    k = 1
    while k < n_chunks:
        a1.append(jnp.concatenate([ar, ar], axis=1))
        a2.append(jnp.concatenate([-ai, ai], axis=1))
        ar, ai = ar * ar - ai * ai, 2.0 * ar * ai
        k *= 2
    a1 = jnp.stack(a1, axis=1)
    a2 = jnp.stack(a2, axis=1)
    noct = S5_GROUPS // S5_OCT
    hc, ns2 = S5_GROUP_CH, 2 * S5_STATE
    width = t * hc
    lanes = t * LANES
    r_i, c_i = np.arange(width)[:, None], np.arange(lanes)[None, :]
    rep = jnp.asarray((r_i // hc == c_i // LANES) & (r_i % hc == c_i % hc), dtype=BF16)
    kern_o = kern_t.reshape(noct, LANES, width).astype(BF16)
    toep_o = pl.pallas_call(
        _s5_toeplitz_kernel,
        grid=(noct, t // 8),
        in_specs=[pl.BlockSpec((1, LANES, width), lambda o, s: (o, 0, 0)), _full((width, lanes))],
        out_specs=pl.BlockSpec((1, 8 * LANES, lanes), lambda o, s: (o, s, 0)),
        out_shape=jax.ShapeDtypeStruct((noct, lanes, lanes), BF16),
        scratch_shapes=[pltpu.VMEM((t, LANES, LANES), F32)],
        compiler_params=_cparams("parallel", "arbitrary"),
        name="s5_toeplitz_table",
    )(kern_o, rep)
    cs_o = pl.pallas_call(
        _s5_readout_kernel,
        grid=(noct,),
        in_specs=[pl.BlockSpec((1, S5_OCT * ns2, width), lambda o: (o, 0, 0)), _full((width, lanes))],
        out_specs=pl.BlockSpec((1, S5_OCT * ns2, lanes), lambda o: (o, 0, 0)),
        out_shape=jax.ShapeDtypeStruct((noct, S5_OCT * ns2, lanes), BF16),
        compiler_params=_cparams("parallel"),
        name="s5_readout_table",
    )(cs.reshape(noct, S5_OCT * ns2, width).astype(BF16), rep)
    bs_slabs = bs.reshape(noct, S5_OCT, t, hc, ns2).transpose(0, 2, 1, 3, 4).reshape(noct, t, LANES, ns2).astype(BF16)
    bs_o = pl.pallas_call(
        _s5_state_in_kernel,
        grid=(noct,),
        in_specs=[pl.BlockSpec((1, t, LANES, ns2), lambda o: (o, 0, 0, 0))],
        out_specs=pl.BlockSpec((1, lanes, S5_OCT * ns2), lambda o: (o, 0, 0)),
        out_shape=jax.ShapeDtypeStruct((noct, lanes, S5_OCT * ns2), BF16),
        compiler_params=_cparams("parallel"),
        name="s5_state_in_table",
    )(bs_slabs)

    def oct_rows(a):
        return a.reshape(noct, S5_OCT, -1, ns2).transpose(0, 2, 1, 3).reshape(noct, -1, S5_OCT * ns2)

    return toep_o, bs_o, cs_o, oct_rows(a1), oct_rows(a2)


def _same_group(shape, row_div, col_mod, col_div):
    r = lax.broadcasted_iota(jnp.int32, shape, 0)
    c = lax.broadcasted_iota(jnp.int32, shape, 1)
    return (r // row_div) == ((c % col_mod) // col_div)


def _s5_toeplitz_kernel(k_ref, rep_ref, o_ref, full_s):
    nlag = full_s.shape[0]
    steps_here = o_ref.shape[1] // LANES

    @pl.when(pl.program_id(1) == 0)
    def _():
        full = _dot(k_ref[0], rep_ref[...])
        full = jnp.where(_same_group(full.shape, S5_GROUP_CH, LANES, S5_GROUP_CH), full, 0.0)
        for lag in range(nlag):
            full_s[lag] = full[:, lag * LANES:(lag + 1) * LANES]

    for i in range(steps_here):
        s = pl.program_id(1) * steps_here + i
        for t in range(nlag):
            tile = full_s[jnp.maximum(t - s, 0)]
            o_ref[0, i * LANES:(i + 1) * LANES, t * LANES:(t + 1) * LANES] = (
                jnp.where(t >= s, tile, 0.0).astype(o_ref.dtype))


def _s5_readout_kernel(c_ref, rep_ref, o_ref):
    full = _dot(c_ref[0], rep_ref[...])
    keep = _same_group(full.shape, 2 * S5_STATE, LANES, S5_GROUP_CH)
    o_ref[0] = jnp.where(keep, full, 0.0).astype(o_ref.dtype)


def _s5_state_in_kernel(b_ref, o_ref):
    keep = _same_group((LANES, o_ref.shape[2]), S5_GROUP_CH, o_ref.shape[2], 2 * S5_STATE)
    for s in range(b_ref.shape[1]):
        full = jnp.concatenate([b_ref[0, s]] * S5_OCT, axis=1)
        o_ref[0, s * LANES:(s + 1) * LANES, :] = jnp.where(keep, full, jnp.zeros((), full.dtype))


def _s5_state_kernel(bsz, u_ref, bs_ref, a1_ref, a2_ref, hi_ref, lo_ref):
    sc = _dot(u_ref[...], bs_ref[0])
    n = sc.shape[0] // bsz
    width = sc.shape[1]
    rowi = lax.broadcasted_iota(jnp.int32, (n, width), 0)

    def swap_re_im(x):
        return jnp.concatenate([pltpu.roll(x[:, j * LANES:(j + 1) * LANES], S5_STATE, 1)
                                for j in range(width // LANES)], axis=1)

    h_in = []
    for b in range(bsz):
        x = sc[b * n:(b + 1) * n]
        k, step = 1, 0
        while k < n:
            xs = jnp.where(rowi >= k, pltpu.roll(x, k, 0), 0.0)
            x = x + a1_ref[0, step:step + 1, :] * xs + a2_ref[0, step:step + 1, :] * swap_re_im(xs)
            k *= 2
            step += 1
        h_in.append(jnp.where(rowi >= 1, pltpu.roll(x, 1, 0), 0.0))
    h_in = jnp.concatenate(h_in, axis=0)
    hi = h_in.astype(BF16)
    hi_ref[0] = hi
    lo_ref[0] = (h_in - hi.astype(F32)).astype(BF16)


def _s5_out_kernel(u_ref, toep_ref, hi_ref, lo_ref, cs_ref, o_ref):
    n = pl.program_id(1)
    cols = toep_ref.shape[2]
    off = _dot(hi_ref[0], cs_ref[0]) + _dot(lo_ref[0], cs_ref[0])
    for nn in range(toep_ref.shape[1] // cols):
        @pl.when(n == nn)
        def _(nn=nn):
            k = (nn + 1) * cols
            y = off + _dot(u_ref[:, 0:k], toep_ref[0, 0:k, :])
            for t8 in range(cols // LANES):
                o_ref[:, t8, :] = y[:, t8 * LANES:(t8 + 1) * LANES].astype(o_ref.dtype)


def _s5_glu_kernel(y_ref, u_ref, d_ref, w_ref, b_ref, o_ref):
    y = _gelu_tanh(y_ref[...] + d_ref[...] * u_ref[...])
    gate = _sigmoid(_dot(y.astype(BF16), w_ref[...]) + b_ref[...])
    o_ref[...] = (y * gate).astype(o_ref.dtype)


def _s5_mixer(u5, u_chunks, bsz, seq, a_re, a_im, log_dt, b_re, b_im, c_re, c_im, d_skip, glu_w, glu_b, tm=1024):
    m = u5.shape[0]
    t = S5_CHUNK
    nch = m // t
    noct = S5_GROUPS // S5_OCT
    lanes = t * LANES
    sw = S5_OCT * 2 * S5_STATE
    toep, bs, cs, a1, a2 = _s5_params(a_re, a_im, log_dt, b_re, b_im, c_re, c_im, nch // bsz)
    nsteps = a1.shape[1]
    h_hi, h_lo = pl.pallas_call(
        functools.partial(_s5_state_kernel, bsz),
        grid=(noct,),
        in_specs=[
            pl.BlockSpec((nch, lanes), lambda o: (0, o)),
            pl.BlockSpec((1, lanes, sw), lambda o: (o, 0, 0)),
            pl.BlockSpec((1, nsteps, sw), lambda o: (o, 0, 0)),
            pl.BlockSpec((1, nsteps, sw), lambda o: (o, 0, 0)),
        ],
        out_specs=[pl.BlockSpec((1, nch, sw), lambda o: (o, 0, 0))] * 2,
        out_shape=[jax.ShapeDtypeStruct((noct, nch, sw), BF16)] * 2,
        compiler_params=_cparams("parallel"),
        name="s5_state",
    )(u_chunks, bs, a1, a2)
    tsub = 8
    rows = nch // 2
    y = pl.pallas_call(
        _s5_out_kernel,
        grid=(noct, t // tsub, nch // rows),
        in_specs=[
            pl.BlockSpec((rows, lanes), lambda o, n, r: (r, o)),
            pl.BlockSpec((1, lanes, tsub * LANES), lambda o, n, r: (o, 0, n)),
            pl.BlockSpec((1, rows, sw), lambda o, n, r: (o, r, 0)),
            pl.BlockSpec((1, rows, sw), lambda o, n, r: (o, r, 0)),
            pl.BlockSpec((1, sw, tsub * LANES), lambda o, n, r: (o, 0, n)),
        ],
        out_specs=pl.BlockSpec((rows, tsub, LANES), lambda o, n, r: (r, n, o)),
        out_shape=jax.ShapeDtypeStruct((nch, t, S5_CH), BF16),
        compiler_params=_cparams("parallel", "parallel", "parallel"),
        name="s5_scan",
    )(u_chunks, toep, h_hi, h_lo, cs).reshape(m, S5_CH)
    return pl.pallas_call(
        _s5_glu_kernel,
        grid=(m // tm,),
        in_specs=[
            pl.BlockSpec((tm, S5_CH), lambda i: (i, 0)),
            pl.BlockSpec((tm, S5_CH), lambda i: (i, 0)),
            _full((1, S5_CH)), _full((S5_CH, S5_CH)), _full((1, S5_CH)),
        ],
        out_specs=pl.BlockSpec((tm, S5_CH), lambda i: (i, 0)),
        out_shape=jax.ShapeDtypeStruct((m, S5_CH), BF16),
        compiler_params=_cparams("parallel"),
        name="s5_glu",
    )(y, u5, d_skip.reshape(1, S5_CH).astype(F32), glu_w.astype(BF16), glu_b.reshape(1, S5_CH).astype(F32))


def _even_mixers(h, bsz, seq, g_mix, w_in, conv_w, conv_b, dt_bias, a_log, d_skip, norm_g, pe, w1, b1, w2, b2):
    d = h.shape[1]
    scale = HEAD_DIM ** -0.5 * math.log2(math.e)
    o = SSD_IN
    w_ssd = jnp.concatenate([w_in[:, :SSD_IN], jnp.zeros((d, LANES - SSD_HEADS), w_in.dtype)], axis=1)
    w_q = w_in[:, o:o + NSA_Q] * scale
    kv = [w_in[:, o + NSA_Q + i * NSA_KVW:o + NSA_Q + (i + 1) * NSA_KVW] for i in range(6)]
    w_gate = w_in[:, o + NSA_Q + 6 * NSA_KVW:].reshape(d, NSA_KV, NSA_RPG, 3).transpose(0, 1, 3, 2)
    w_gate = jnp.concatenate([w_gate.reshape(d, NSA_KV, 3 * NSA_RPG),
                              jnp.zeros((d, NSA_KV, NSA_GATE_ROWS - 3 * NSA_RPG), w_in.dtype)],
                             axis=2).reshape(d, NSA_KV * NSA_GATE_ROWS)

    def per_group_halves(w):
        wg = w.reshape(d, NSA_KV, HEAD_DIM)
        return jnp.concatenate([wg, jnp.zeros_like(wg)], axis=2).reshape(d, NSA_KV * NSA_KVW)
    segs = [
        ("nat", w_ssd, F32),
        ("tt", w_q, BF16),
        ("pc", jnp.concatenate([kv[0], kv[1]], axis=1), F32),
        ("nat+", per_group_halves(kv[2]), BF16, _nsa_block_onehot(NSA_KTILE)),
        ("nat", per_group_halves(kv[4]), BF16),
        ("tt", jnp.concatenate([kv[3], kv[5]], axis=1), BF16),
        ("tt", w_gate, F32),
    ]
    u_ssd, q_tt, kvc, k_slc, k_win, v_tt, g_tt = _norm_proj(h, g_mix, segs)
    ya = _ssd_mixer(u_ssd, bsz, seq, conv_w, conv_b, dt_bias, a_log, d_skip, norm_g)
    yb = _nsa_mixer(q_tt, kvc, k_slc, k_win, v_tt, g_tt, bsz, seq, pe, w1, b1, w2, b2)
    return ya, yb


def _odd_mixers(h, bsz, seq, g_mix, w_in, sinks, a_re, a_im, log_dt, b_re, b_im, c_re, c_im, d_skip, glu_w, glu_b):
    scale = HEAD_DIM ** -0.5 * math.log2(math.e)
    segs = [
        ("tt", w_in[:, :SWA_Q] * scale, BF16),
        ("nat", w_in[:, SWA_Q:SWA_Q + SWA_KVW], BF16),
        ("tt", w_in[:, SWA_Q + SWA_KVW:SWA_Q + 2 * SWA_KVW], BF16),
        ("nat+ch", w_in[:, SWA_Q + 2 * SWA_KVW:], F32),
    ]
    q_tt, k_nat, v_tt, u5, u_chunks = _norm_proj(h, g_mix, segs)
    yc = _swa_mixer(q_tt, k_nat, v_tt, sinks, bsz, seq)
    yd = _s5_mixer(u5, u_chunks, bsz, seq, a_re, a_im, log_dt, b_re, b_im, c_re, c_im, d_skip, glu_w, glu_b)
    return yc, yd


def kernel(x, norm_mix, norm_mlp, norm_final, mlp_w_up, mlp_w_down, ev_w_in, ev_w_out, ssd_conv_w, ssd_conv_b,
           ssd_dt_bias, ssd_a_log, ssd_d, ssd_norm, nsa_pe, nsa_cmp_w1, nsa_cmp_b1, nsa_cmp_w2, nsa_cmp_b2,
           od_w_in, od_w_out, swa_sinks, s5_a_re, s5_a_im, s5_log_dt, s5_b_re, s5_b_im, s5_c_re, s5_c_im,
           s5_d, s5_glu_w, s5_glu_b):
    bsz, seq, d = x.shape
    depth = norm_mix.shape[0]
    assert seq % (NSA_UNROLL * NSA_KTILE) == 0 and seq >= NSA_WIN + NSA_QBLK
    assert seq % (SWA_STEP_BLOCKS * SWA_WIN) == 0 and (bsz * seq) % 1024 == 0 and d % LANES == 0
    h = x.reshape(bsz * seq, d)
    for layer in range(depth):
        i = layer // 2
        if layer % 2 == 0:
            ya, yb = _even_mixers(h, bsz, seq, norm_mix[layer], ev_w_in[i], ssd_conv_w[i], ssd_conv_b[i],
                                  ssd_dt_bias[i], ssd_a_log[i], ssd_d[i], ssd_norm[i], nsa_pe[i],
                                  nsa_cmp_w1[i], nsa_cmp_b1[i], nsa_cmp_w2[i], nsa_cmp_b2[i])
            w_out = ev_w_out[i]
        else:
            ya, yb = _odd_mixers(h, bsz, seq, norm_mix[layer], od_w_in[i], swa_sinks[i], s5_a_re[i], s5_a_im[i],
                                 s5_log_dt[i], s5_b_re[i], s5_b_im[i], s5_c_re[i], s5_c_im[i], s5_d[i],
                                 s5_glu_w[i], s5_glu_b[i])
            w_out = od_w_out[i]
        h = _mix_out_mlp(h, ya, yb, w_out, norm_mlp[layer], mlp_w_up[layer], mlp_w_down[layer], norm_final,
                         final=(layer == depth - 1))
    return h.reshape(bsz, seq, d)
```

```python
import functools
import math

import jax
import jax.numpy as jnp
import numpy as np
from jax import lax
from jax.experimental import pallas as pl
from jax.experimental.pallas import tpu as pltpu

F32 = jnp.float32
BF16 = jnp.bfloat16

EPS = 1e-6
NEG = -1e30
FORCE = 1e9
HEAD_DIM = 64
LANES = 128
BF16_ROWS = 16
VMEM_LIMIT_BYTES = 56 * 1024 * 1024

SSD_HEADS = 8
SSD_INNER = 512
SSD_GROUPS = 2
SSD_STATE = 128
SSD_CONV = 4
SSD_CHUNK = 128
SSD_CONV_DIM = 1024
SSD_IN = SSD_INNER + SSD_CONV_DIM + SSD_HEADS

NSA_KV = 2
NSA_RPG = 4
NSA_CMP_LEN = 32
NSA_CMP_STRIDE = 16
NSA_SLC_LEN = 64
NSA_TOPK = 16
NSA_WIN = 512
NSA_CMP_HIDDEN = 256
NSA_QBLK = 128
NSA_Q = 512
NSA_KVW = 128
NSA_KTILE = 512
NSA_UNROLL = 2
NSA_GATE_ROWS = 16

SWA_RPG = 4
SWA_WIN = 128
SWA_Q = 512
SWA_KVW = 128
SWA_STEP_BLOCKS = 4

S5_CH = 512
S5_GROUP_CH = 16
S5_GROUPS = 32
S5_STATE = 64
S5_CHUNK = 32
S5_OCT = LANES // S5_GROUP_CH


def _cparams(*sem):
    return pltpu.CompilerParams(dimension_semantics=sem, vmem_limit_bytes=VMEM_LIMIT_BYTES)


def _full(shape):
    n = len(shape)
    return pl.BlockSpec(shape, lambda *_: (0,) * n)


def _dot(a, b):
    return jnp.dot(a, b, preferred_element_type=F32)


def _dot_nt(a, b):
    return lax.dot_general(a, b, (((1,), (1,)), ((), ())), preferred_element_type=F32)


def _split3(a):
    hi = a.astype(BF16)
    r1 = a - hi.astype(F32)
    mid = r1.astype(BF16)
    lo = (r1 - mid.astype(F32)).astype(BF16)
    return hi, mid, lo


def _dot_exact_rhs(a, b_exact):
    hi, mid, lo = _split3(a)
    return _dot(hi, b_exact) + _dot(mid, b_exact) + _dot(lo, b_exact)


def _dot_exact_lhs(a_exact, b):
    hi, mid, lo = _split3(b)
    return _dot(a_exact, hi) + _dot(a_exact, mid) + _dot(a_exact, lo)


def _rms(x, g):
    return x * lax.rsqrt(jnp.mean(x * x, axis=-1, keepdims=True) + EPS) * g


def _gelu_tanh(x):
    c = math.sqrt(2.0 / math.pi)
    return 0.5 * x * (1.0 + jnp.tanh(c * (x + 0.044715 * (x * x * x))))


def _sigmoid(x):
    return 1.0 / (1.0 + jnp.exp(-x))


def _proj_kernel(kinds, tm, h_ref, g_ref, *refs):
    n = len(kinds)
    n_add = sum(k == "nat+" for k in kinds)
    n_out = n + sum(k == "nat+ch" for k in kinds)
    w_refs, add_refs = refs[:n], list(refs[n:n + n_add])
    o_refs, scratch = list(refs[n + n_add:n + n_add + n_out]), list(refs[n + n_add + n_out:])
    yb = _rms(h_ref[...], g_ref[...]).astype(BF16)
    for kind, w_ref in zip(kinds, w_refs):
        o_ref = o_refs.pop(0)
        if kind == "nat":
            o_ref[...] = _dot(yb, w_ref[...]).astype(o_ref.dtype)
        elif kind == "nat+ch":
            ch_ref, ch_s = o_refs.pop(0), scratch.pop(0)
            res = _dot(yb, w_ref[...])
            o_ref[...] = res.astype(o_ref.dtype)
            for j in range(ch_s.shape[0]):
                ch_s[j] = res[:, j * LANES:(j + 1) * LANES]
            for t in range(S5_CHUNK):
                for j in range(ch_s.shape[0]):
                    col = (j * S5_CHUNK + t) * LANES
                    ch_ref[:, col:col + LANES] = ch_s[j, pl.ds(t, tm // S5_CHUNK, stride=S5_CHUNK), :].astype(ch_ref.dtype)
        elif kind == "nat+":
            o_ref[...] = _dot(yb, w_ref[...]).astype(o_ref.dtype) + add_refs.pop(0)[...]
        elif kind == "pc":
            pc_s = scratch.pop(0)
            res = _dot(yb, w_ref[...])
            flat = NSA_CMP_STRIDE * HEAD_DIM
            per_tile = LANES // HEAD_DIM
            for j in range(pc_s.shape[0]):
                pc_s[j] = res[:, j * LANES:(j + 1) * LANES]
            for p in range(NSA_CMP_STRIDE):
                for j in range(pc_s.shape[0]):
                    tok = pc_s[j, pl.ds(p, tm // NSA_CMP_STRIDE, stride=NSA_CMP_STRIDE), :]
                    for c in range(per_tile):
                        col = (j * per_tile + c) * flat + p * HEAD_DIM
                        o_ref[:, col:col + HEAD_DIM] = tok[:, c * HEAD_DIM:(c + 1) * HEAD_DIM].astype(o_ref.dtype)
        else:
            ot = _dot_nt(w_ref[...], yb)
            for j in range(tm // LANES):
                o_ref[j] = ot[:, j * LANES:(j + 1) * LANES].astype(o_ref.dtype)


def _norm_proj(h, g, segs, tm=512):
    m, d = h.shape
    kinds = tuple(s[0] for s in segs)
    adds = [s[3] for s in segs if s[0] == "nat+"]
    ws, w_specs, out_shapes, out_specs, scratch = [], [], [], [], []
    for kind, w, dt in (s[:3] for s in segs):
        n_out = w.shape[1]
        if kind == "pc":
            ws.append(w.astype(BF16))
            w_specs.append(_full((d, n_out)))
            out_shapes.append(jax.ShapeDtypeStruct((m // NSA_CMP_STRIDE, NSA_CMP_STRIDE * n_out), dt))
            out_specs.append(pl.BlockSpec((tm // NSA_CMP_STRIDE, NSA_CMP_STRIDE * n_out), lambda i: (i, 0)))
            scratch.append(pltpu.VMEM((n_out // LANES, tm, LANES), F32))
        elif kind in ("nat", "nat+", "nat+ch"):
            ws.append(w.astype(BF16))
            w_specs.append(_full((d, n_out)))
            out_shapes.append(jax.ShapeDtypeStruct((m, n_out), dt))
            out_specs.append(pl.BlockSpec((tm, n_out), lambda i: (i, 0)))
            if kind == "nat+ch":
                out_shapes.append(jax.ShapeDtypeStruct((m // S5_CHUNK, S5_CHUNK * n_out), BF16))
                out_specs.append(pl.BlockSpec((tm // S5_CHUNK, S5_CHUNK * n_out), lambda i: (i, 0)))
                scratch.append(pltpu.VMEM((n_out // LANES, tm, LANES), F32))
        else:
            ws.append(w.T.astype(BF16))
            w_specs.append(_full((n_out, d)))
            out_shapes.append(jax.ShapeDtypeStruct((m // LANES, n_out, LANES), dt))
            out_specs.append(pl.BlockSpec((tm // LANES, n_out, LANES), lambda i: (i, 0, 0)))
    return pl.pallas_call(
        functools.partial(_proj_kernel, kinds, tm),
        grid=(m // tm,),
        in_specs=[pl.BlockSpec((tm, d), lambda i: (i, 0)), _full((1, d))] + w_specs + [
            pl.BlockSpec((tm, a.shape[1]), functools.partial(lambda i, nb: (i % nb, 0), nb=a.shape[0] // tm))
            for a in adds],
        out_specs=out_specs,
        out_shape=out_shapes,
        scratch_shapes=scratch,
        compiler_params=_cparams("parallel"),
        name="norm_proj",
    )(h, g.reshape(1, d), *ws, *adds)


def _mlp_kernel(final, h_ref, ya_ref, yb_ref, woa_ref, wob_ref, gm_ref, wup_ref, wdn_ref, gf_ref,
                o_ref, h2_s, xn_s, acc_s):
    j = pl.program_id(1)

    @pl.when(j == 0)
    def _():
        h2 = h_ref[...] + _dot(ya_ref[...], woa_ref[...]) + _dot(yb_ref[...], wob_ref[...])
        h2_s[...] = h2
        xn_s[...] = _rms(h2, gm_ref[...]).astype(BF16)
        acc_s[...] = jnp.zeros_like(acc_s)

    hid = jnp.square(jnp.maximum(_dot(xn_s[...], wup_ref[...]), 0.0))
    acc_s[...] += _dot(hid.astype(BF16), wdn_ref[...])

    @pl.when(j == pl.num_programs(1) - 1)
    def _():
        out = h2_s[...] + acc_s[...]
        if final:
            out = _rms(out, gf_ref[...])
        o_ref[...] = out


def _mix_out_mlp(h, ya, yb, w_out, g_mlp, w_up, w_down, g_final, final, tm=1024, tf=1024):
    m, d = h.shape
    dff = w_up.shape[1]
    na = ya.shape[1]
    nb = yb.shape[1]
    return pl.pallas_call(
        functools.partial(_mlp_kernel, final),
        grid=(m // tm, dff // tf),
        in_specs=[
            pl.BlockSpec((tm, d), lambda i, j: (i, 0)),
            pl.BlockSpec((tm, na), lambda i, j: (i, 0)),
            pl.BlockSpec((tm, nb), lambda i, j: (i, 0)),
            _full((na, d)), _full((nb, d)), _full((1, d)),
            pl.BlockSpec((d, tf), lambda i, j: (0, j)),
            pl.BlockSpec((tf, d), lambda i, j: (j, 0)),
            _full((1, d)),
        ],
        out_specs=pl.BlockSpec((tm, d), lambda i, j: (i, 0)),
        out_shape=jax.ShapeDtypeStruct((m, d), F32),
        scratch_shapes=[pltpu.VMEM((tm, d), F32), pltpu.VMEM((tm, d), BF16), pltpu.VMEM((tm, d), F32)],
        compiler_params=_cparams("parallel", "arbitrary"),
        name="out_proj_mlp",
    )(h, ya, yb, w_out[:na].astype(BF16), w_out[na:].astype(BF16), g_mlp.reshape(1, d),
      w_up.astype(BF16), w_down.astype(BF16), g_final.reshape(1, d))


def _ssd_kernel(u_ref, cw_ref, cb_ref, dtb_ref, alog_ref, dsk_ref, ng_ref, o_ref, xext_s, st_s):
    t = SSD_CHUNK
    c = pl.program_id(1)

    @pl.when(c == 0)
    def _():
        xext_s[0:8, :] = jnp.zeros((8, SSD_CONV_DIM), F32)
        st_s[...] = jnp.zeros_like(st_s)

    z = u_ref[:, 0:SSD_INNER]
    dt_raw = u_ref[:, SSD_INNER + SSD_CONV_DIM:]
    xext_s[8:8 + t, :] = u_ref[:, SSD_INNER:SSD_INNER + SSD_CONV_DIM]
    xfull = xext_s[...]
    conv = cb_ref[...] + cw_ref[SSD_CONV - 1:SSD_CONV, :] * xfull[8:8 + t]
    for back in range(1, SSD_CONV):
        k = SSD_CONV - 1 - back
        conv = conv + cw_ref[k:k + 1, :] * pltpu.roll(xfull, back, 0)[8:8 + t]
    xext_s[0:8, :] = xext_s[t:t + 8, :]
    xc = conv * _sigmoid(conv)
    xs = xc[:, 0:SSD_INNER]
    gn = SSD_GROUPS * SSD_STATE

    dtp = dt_raw + dtb_ref[...]
    dt = jnp.maximum(dtp, 0.0) + jnp.log1p(jnp.exp(-jnp.abs(dtp)))
    a = -jnp.exp(alog_ref[...])
    da = dt * a

    row = lax.broadcasted_iota(jnp.int32, (t, t), 0)
    col = lax.broadcasted_iota(jnp.int32, (t, t), 1)
    causal = col <= row
    tril = jnp.where(causal, 1.0, 0.0).astype(BF16)
    a_cum = _dot_exact_lhs(tril, da)
    er = lax.broadcasted_iota(jnp.int32, (LANES, SSD_INNER), 0)
    ec = lax.broadcasted_iota(jnp.int32, (LANES, SSD_INNER), 1)
    expand = jnp.where((ec >> 6) == er, 1.0, 0.0).astype(BF16)
    a_cum_x = _dot_exact_rhs(a_cum, expand)
    dt_x = _dot_exact_rhs(dt, expand)
    a_cum_t = a_cum.T
    a_last_x = a_cum_x[t - 1:t, :]
    decay_end_x = jnp.exp(a_last_x - a_cum_x)
    decay_in_x = jnp.exp(a_cum_x)
    chunk_decay_x = jnp.exp(a_last_x)

    xd = xs * dt_x
    xd_end = (xd * decay_end_x).astype(BF16)
    xd_b = xd.astype(BF16)
    lane = lax.broadcasted_iota(jnp.int32, (t, LANES), 1)
    first_half = lane < HEAD_DIM

    pieces = []
    for g in range(SSD_GROUPS):
        bm = xc[:, SSD_INNER + g * SSD_STATE:SSD_INNER + (g + 1) * SSD_STATE]
        cm = xc[:, SSD_INNER + gn + g * SSD_STATE:SSD_INNER + gn + (g + 1) * SSD_STATE].astype(BF16)
        bm_t = bm.T.astype(BF16)
        cb = _dot_nt(cm, bm.astype(BF16))
        for pr in range(2):
            i = g * 2 + pr
            sl = slice(i * LANES, (i + 1) * LANES)
            ms = []
            for hh in range(2):
                h = 2 * i + hh
                seg = a_cum[:, h:h + 1] - a_cum_t[h:h + 1, :]
                dec = jnp.exp(jnp.where(causal, seg, NEG))
                ms.append((cb * dec).astype(BF16))
            y_diag = jnp.where(first_half, _dot(ms[0], xd_b[:, sl]), _dot(ms[1], xd_b[:, sl]))
            st = st_s[i]
            y_off = _dot(cm, st.astype(BF16)) * decay_in_x[:, sl]
            st_s[i] = st * chunk_decay_x[:, sl] + _dot(bm_t, xd_end[:, sl])
            pieces.append(y_diag + y_off)
    y = jnp.concatenate(pieces, axis=1) + xs * dsk_ref[...]
    y = y * (z * _sigmoid(z))
    half = SSD_INNER // SSD_GROUPS
    outs = [_rms(y[:, g * half:(g + 1) * half], ng_ref[:, g * half:(g + 1) * half]) for g in range(SSD_GROUPS)]
    o_ref[...] = jnp.concatenate(outs, axis=1).astype(o_ref.dtype)


def _ssd_mixer(u_ssd, bsz, seq, conv_w, conv_b, dt_bias, a_log, d_skip, norm_g):
    m, width = u_ssd.shape
    nch = seq // SSD_CHUNK
    pad = LANES - SSD_HEADS

    def padded(v):
        return jnp.concatenate([v.astype(F32), jnp.zeros((pad,), F32)]).reshape(1, LANES)

    return pl.pallas_call(
        _ssd_kernel,
        grid=(bsz, nch),
        in_specs=[
            pl.BlockSpec((SSD_CHUNK, width), lambda b, c: (b * nch + c, 0)),
            _full((SSD_CONV, SSD_CONV_DIM)), _full((1, SSD_CONV_DIM)),
            _full((1, LANES)), _full((1, LANES)), _full((1, SSD_INNER)), _full((1, SSD_INNER)),
        ],
        out_specs=pl.BlockSpec((SSD_CHUNK, SSD_INNER), lambda b, c: (b * nch + c, 0)),
        out_shape=jax.ShapeDtypeStruct((m, SSD_INNER), BF16),
        scratch_shapes=[pltpu.VMEM((SSD_CHUNK + 8, SSD_CONV_DIM), F32),
                        pltpu.VMEM((SSD_HEADS // 2, SSD_STATE, LANES), F32)],
        compiler_params=_cparams("arbitrary", "arbitrary"),
        name="ssd_mixer",
    )(u_ssd, conv_w.astype(F32), conv_b.reshape(1, -1).astype(F32), padded(dt_bias), padded(a_log),
      jnp.repeat(d_skip.astype(F32), HEAD_DIM).reshape(1, SSD_INNER), norm_g.reshape(1, SSD_INNER).astype(F32))


def _nsa_compress_kernel(x_ref, pe_ref, w1_ref, b1_ref, w2_ref, b2_ref, o_ref):
    x = x_ref[...]
    npc = x.shape[0]
    half = NSA_CMP_STRIDE * HEAD_DIM
    top = _dot((x + pe_ref[0, :, 0:half]).astype(BF16), w1_ref[0, 0:half, :])
    bot = _dot((x + pe_ref[0, :, half:]).astype(BF16), w1_ref[0, half:, :])
    pre = top + pltpu.roll(bot, npc - 1, 0) + b1_ref[0]
    hid = _gelu_tanh(pre).astype(BF16)
    out = _dot(hid, w2_ref[0]) + b2_ref[0]
    rowi = lax.broadcasted_iota(jnp.int32, out.shape, 0)
    o_ref[0, 0, 0] = jnp.where(rowi < npc - 1, out, 0.0)


def _nsa_compress(kvc_pieces, bsz, seq, pe, w1, b1, w2, b2):
    npc = seq // NSA_CMP_STRIDE
    flat = NSA_CMP_STRIDE * HEAD_DIM
    x = kvc_pieces
    return pl.pallas_call(
        _nsa_compress_kernel,
        grid=(bsz, 2, NSA_KV),
        in_specs=[
            pl.BlockSpec((npc, flat), lambda b, s, g: (b, s * NSA_KV + g)),
            pl.BlockSpec((1, 1, 2 * flat), lambda b, s, g: (s, 0, 0)),
            pl.BlockSpec((1, 2 * flat, NSA_CMP_HIDDEN), lambda b, s, g: (s, 0, 0)),
            pl.BlockSpec((1, 1, NSA_CMP_HIDDEN), lambda b, s, g: (s, 0, 0)),
            pl.BlockSpec((1, NSA_CMP_HIDDEN, HEAD_DIM), lambda b, s, g: (s, 0, 0)),
            pl.BlockSpec((1, 1, HEAD_DIM), lambda b, s, g: (s, 0, 0)),
        ],
        out_specs=pl.BlockSpec((1, 1, 1, npc, HEAD_DIM), lambda b, s, g: (b, s, g, 0, 0)),
        out_shape=jax.ShapeDtypeStruct((bsz, 2, NSA_KV, npc, HEAD_DIM), F32),
        compiler_params=_cparams("parallel", "parallel", "parallel"),
        name="nsa_compress",
    )(x, pe.reshape(2, 1, 2 * flat).astype(F32), w1.astype(BF16), b1.reshape(2, 1, -1).astype(F32),
      w2.astype(BF16), b2.reshape(2, 1, -1).astype(F32))


def _nsa_cmp_split(ns):
    return max(1, min(4, (4 * ns) // LANES))


def _nsa_block_onehot(rows):
    r = lax.broadcasted_iota(jnp.int32, (rows, 2 * NSA_KVW), 0)
    c = lax.broadcasted_iota(jnp.int32, (rows, 2 * NSA_KVW), 1)
    blk = (r % NSA_KTILE) // NSA_SLC_LEN
    return jnp.where((c % NSA_KVW) == HEAD_DIM + blk, 1.0, 0.0).astype(BF16)


def _nsa_kernel(q_ref, kc_ref, vc_ref, ks_ref, kw_ref, vs_ref, vw_ref, g_ref, o_ref, *scratch):
    groups = range(NSA_KV)
    per = len(scratch) // NSA_KV
    bias_s, qaug_s, qaug2_s, sa_s, sb_s, pa_s, pb_s = (
        tuple(scratch[g * per + i] for g in groups) for i in range(per))
    qb = pl.program_id(1)
    nqt = NSA_RPG * NSA_QBLK
    s0 = qb * NSA_QBLK
    ncp = kc_ref.shape[2]
    ns = ncp // 4
    heads = [slice(r * NSA_QBLK, (r + 1) * NSA_QBLK) for r in range(NSA_RPG)]
    klanes = [slice(g * NSA_KVW, (g + 1) * NSA_KVW) for g in groups]
    vrows = [slice(g * HEAD_DIM, (g + 1) * HEAD_DIM) for g in groups]

    qtiles = NSA_QBLK // LANES
    qcat = [jnp.concatenate([q_ref[j, (g * NSA_RPG + r) * HEAD_DIM:(g * NSA_RPG + r + 1) * HEAD_DIM, :]
                             for r in range(NSA_RPG) for j in range(qtiles)], axis=1) for g in groups]
    qpos = s0 + lax.broadcasted_iota(jnp.int32, (1, NSA_QBLK), 1)
    for g in groups:
        qaug_s[g][0:HEAD_DIM, :] = qcat[g]
        qaug_s[g][HEAD_DIM:, :] = jnp.zeros((HEAD_DIM, nqt), BF16)

    split = _nsa_cmp_split(ns)
    chunk = ns // split
    cvalid = jnp.where(qpos >= NSA_CMP_LEN - 1, 1.0, 0.0)
    cur = qpos // NSA_SLC_LEN
    taken = -3.0e38

    def cmp_and_select(nchunks):
        out = []
        for g in groups:
            out.extend(cmp_and_select_group(g, nchunks))
        return tuple(out)

    def cmp_and_select_group(g, nchunks):
        rows, jmax = nchunks * 4 * chunk, nchunks * chunk
        kc = kc_ref[0, g, 0:rows, :]
        rc = lax.broadcasted_iota(jnp.int32, (rows, 1), 0)
        ncmp = 4 * ((rc // (4 * chunk)) * chunk + rc % chunk) + (rc % (4 * chunk)) // chunk
        cbias = jnp.where((ncmp * NSA_CMP_STRIDE + (NSA_CMP_LEN - 1)) <= qpos, 0.0, NEG)
        psum = jnp.zeros((rows, NSA_QBLK), F32)
        p_all = []
        for sl in heads:
            s = _dot(kc, qcat[g][:, sl]) + cbias
            e = jnp.exp2(s - jnp.max(s, axis=0, keepdims=True))
            p = e * (cvalid / jnp.sum(e, axis=0, keepdims=True))
            psum = psum + p
            p_all.append(p.astype(BF16))
        o_cmp = _dot(vc_ref[0, g, :, 0:rows], jnp.concatenate(p_all, axis=1))

        tot, p3 = [], []
        for c in range(nchunks):
            part = [psum[(4 * c + i) * chunk:(4 * c + i + 1) * chunk] for i in range(4)]
            tot.append(part[0] + part[1] + part[2] + part[3])
            p3.append(part[3])
        tot, p3 = jnp.concatenate(tot, axis=0), jnp.concatenate(p3, axis=0)
        rj = lax.broadcasted_iota(jnp.int32, (jmax, NSA_QBLK), 0)
        imp = tot + jnp.where(rj >= 1, pltpu.roll(p3, 1, 0), 0.0)
        forced = (rj == 0) | (rj == cur) | (rj == cur - 1)
        rjf = rj.astype(F32)
        imp = jnp.where(forced, taken, jnp.where(rj <= cur, imp, -FORCE))
        for _ in range(min(NSA_TOPK, ns) - 3):
            mx = jnp.max(imp, axis=0, keepdims=True)
            first = jnp.min(jnp.where(imp == mx, rjf, float(ns)), axis=0, keepdims=True)
            imp = jnp.where(rjf == first, taken, imp)
        bias = jnp.where(imp == taken, 0.0, NEG)
        if jmax < ns:
            bias = jnp.concatenate([bias, jnp.full((ns - jmax, NSA_QBLK), NEG, F32)], axis=0)
        return o_cmp, bias

    last_block = (s0 + NSA_QBLK - 1) // NSA_SLC_LEN
    selected = lax.switch(last_block // chunk, [functools.partial(cmp_and_select, n + 1) for n in range(split)])
    o_cmp = [selected[2 * g] for g in groups]
    for g in groups:
        bias_s[g][...] = selected[2 * g + 1]

    kt_diag = s0 // NSA_KTILE
    blocks_per_tile = NSA_KTILE // NSA_SLC_LEN
    vtiles = NSA_KTILE // LANES

    def qk_tile(g, kt, qaug_ref):
        k0 = pl.multiple_of(kt * NSA_KTILE, NSA_KTILE)
        b8 = bias_s[g][pl.ds(pl.multiple_of(kt * blocks_per_tile, blocks_per_tile), blocks_per_tile), :]
        b16 = jnp.concatenate([b8, jnp.zeros_like(b8)], axis=0).astype(BF16)
        qaug_ref[HEAD_DIM:HEAD_DIM + BF16_ROWS, :] = jnp.concatenate([b16] * NSA_RPG, axis=1)
        return _dot(ks_ref[pl.ds(k0, NSA_KTILE), klanes[g]], qaug_ref[...])

    ones_rows = jnp.ones((BF16_ROWS, NSA_KTILE), BF16)

    def pv_tile(g, kt, p):
        vt = jnp.concatenate([vs_ref[kt * vtiles + i, vrows[g], :] for i in range(vtiles)], axis=1)
        return _dot(jnp.concatenate([vt, ones_rows], axis=0), p)

    def softmax_tile(s, m_old):
        m_new = jnp.maximum(m_old, jnp.max(s, axis=0, keepdims=True))
        p = jnp.exp2((s - m_new).astype(BF16))
        return p, m_new, jnp.exp2(m_old - m_new)

    def visible(kt):
        kpos = kt * NSA_KTILE + lax.broadcasted_iota(jnp.int32, (NSA_KTILE, 1), 0)
        return jnp.concatenate([kpos <= qpos] * NSA_RPG, axis=1)

    def tile_group(i, carry, last):
        carry = list(carry)
        for k in range(NSA_UNROLL):
            t = NSA_UNROLL * i + k
            for g in groups:
                m_run, acc, alpha_prev = carry[g]
                s_cur, p_cur, s_nxt, p_prv, qa = ((sa_s[g], pa_s[g], sb_s[g], pb_s[g], qaug_s[g]) if k % 2 == 0 else
                                                  (sb_s[g], pb_s[g], sa_s[g], pa_s[g], qaug2_s[g]))
                acc = alpha_prev * acc + pv_tile(g, jnp.maximum(t - 1, 0), p_prv[...])
                s = jnp.where(visible(t), s_cur[...], NEG) if last else s_cur[...]
                p, m_run, alpha_prev = softmax_tile(s, m_run)
                p_cur[...] = p
                if last and k == NSA_UNROLL - 1:
                    acc = alpha_prev * acc + pv_tile(g, t, p_cur[...])
                else:
                    s_nxt[...] = qk_tile(g, t + 1, qa)
                carry[g] = (m_run, acc, alpha_prev)
        return tuple(carry)

    for g in groups:
        qaug2_s[g][...] = qaug_s[g][...]
        pb_s[g][...] = jnp.zeros((NSA_KTILE, nqt), BF16)
        sa_s[g][...] = qk_tile(g, 0, qaug2_s[g])

    span = NSA_WIN + NSA_QBLK
    start = pl.multiple_of(jnp.maximum(s0 - NSA_WIN, 0), NSA_QBLK)
    kp = start + lax.broadcasted_iota(jnp.int32, (span, 1), 0)
    wbias = jnp.where((kp <= qpos) & (kp > qpos - NSA_WIN), 0.0, NEG)
    sblk = start // LANES
    o_win = []
    for g in groups:
        kwin = kw_ref[pl.ds(start, span), klanes[g]]
        pw, dens = [], []
        for sl in heads:
            s = _dot(kwin, qaug_s[g][:, sl]) + wbias
            e = jnp.exp2(s - jnp.max(s, axis=0, keepdims=True))
            dens.append(jnp.sum(e, axis=0, keepdims=True))
            pw.append(e.astype(BF16))
        vwt = jnp.concatenate([vw_ref[sblk + i, vrows[g], :] for i in range(span // LANES)], axis=1)
        o_win.append(_dot(vwt, jnp.concatenate(pw, axis=1)) * (1.0 / jnp.concatenate(dens, axis=1)))

    init = tuple((jnp.full((1, nqt), NEG, F32), jnp.zeros((HEAD_DIM + BF16_ROWS, nqt), F32), jnp.ones((1, nqt), F32))
                 for _ in groups)
    group_diag = kt_diag // NSA_UNROLL
    carry = lax.fori_loop(0, group_diag, lambda i, c: tile_group(i, c, False), init)
    carry = tile_group(group_diag, carry, True)

    gates = _sigmoid(jnp.concatenate([g_ref[j] for j in range(qtiles)], axis=1))
    outs = []
    for g in groups:
        acc = carry[g][1]
        o_slc = acc[0:HEAD_DIM] * (1.0 / acc[HEAD_DIM:HEAD_DIM + 1])
        for r, sl in enumerate(heads):
            row = g * NSA_GATE_ROWS + r
            outs.append(gates[row:row + 1, :] * o_cmp[g][:, sl]
                        + gates[row + NSA_RPG:row + NSA_RPG + 1, :] * o_slc[:, sl]
                        + gates[row + 2 * NSA_RPG:row + 2 * NSA_RPG + 1, :] * o_win[g][:, sl])
    o_ref[...] = jnp.concatenate(outs, axis=0).T.astype(o_ref.dtype)


def _nsa_mixer(q_tt, kvc, k_slc, k_win, v_tt, g_tt, bsz, seq, pe, w1, b1, w2, b2):
    nqb = seq // NSA_QBLK
    qtiles = NSA_QBLK // LANES
    ntile = seq // LANES
    npc = seq // NSA_CMP_STRIDE
    ns = seq // NSA_SLC_LEN
    cmp_out = _nsa_compress(kvc, bsz, seq, pe, w1, b1, w2, b2)
    split = _nsa_cmp_split(ns)
    perm = cmp_out.reshape(bsz, 2, NSA_KV, split, ns // split, 4, HEAD_DIM).transpose(0, 1, 2, 3, 5, 4, 6).reshape(
        bsz, 2, NSA_KV, npc, HEAD_DIM)
    kc = perm[:, 0].astype(BF16)
    vc_t = perm[:, 1].transpose(0, 1, 3, 2).astype(BF16)
    nqt = NSA_RPG * NSA_QBLK
    resident = pl.Buffered(1)
    group_scratch = [pltpu.VMEM((ns, NSA_QBLK), F32),
                     pltpu.VMEM((2 * HEAD_DIM, nqt), BF16), pltpu.VMEM((2 * HEAD_DIM, nqt), BF16),
                     pltpu.VMEM((NSA_KTILE, nqt), F32), pltpu.VMEM((NSA_KTILE, nqt), F32),
                     pltpu.VMEM((NSA_KTILE, nqt), BF16), pltpu.VMEM((NSA_KTILE, nqt), BF16)]
    return pl.pallas_call(
        _nsa_kernel,
        grid=(bsz, nqb),
        in_specs=[
            pl.BlockSpec((qtiles, NSA_Q, LANES), lambda b, q: (b * nqb + q, 0, 0)),
            pl.BlockSpec((1, NSA_KV, npc, HEAD_DIM), lambda b, q: (b, 0, 0, 0), pipeline_mode=resident),
            pl.BlockSpec((1, NSA_KV, HEAD_DIM, npc), lambda b, q: (b, 0, 0, 0), pipeline_mode=resident),
            pl.BlockSpec((seq, NSA_KV * NSA_KVW), lambda b, q: (b, 0), pipeline_mode=resident),
            pl.BlockSpec((seq, NSA_KV * NSA_KVW), lambda b, q: (b, 0), pipeline_mode=resident),
            pl.BlockSpec((ntile, NSA_KVW, LANES), lambda b, q: (b, 0, 0), pipeline_mode=resident),
            pl.BlockSpec((ntile, NSA_KVW, LANES), lambda b, q: (b, 1, 0), pipeline_mode=resident),
            pl.BlockSpec((qtiles, NSA_KV * NSA_GATE_ROWS, LANES), lambda b, q: (b * nqb + q, 0, 0)),
        ],
        out_specs=pl.BlockSpec((NSA_QBLK, NSA_Q), lambda b, q: (b * nqb + q, 0)),
        out_shape=jax.ShapeDtypeStruct((bsz * seq, NSA_Q), BF16),
        scratch_shapes=group_scratch * NSA_KV,
        compiler_params=_cparams("arbitrary", "arbitrary"),
        name="nsa_attention",
    )(q_tt, kc, vc_t, k_slc, k_win, v_tt, v_tt, g_tt)


def _swa_kernel(q_ref, kp_ref, kc_ref, vp_ref, vc_ref, sink_ref, o_ref):
    step = pl.program_id(1)
    t = SWA_WIN
    nb = SWA_STEP_BLOCKS
    krel = lax.broadcasted_iota(jnp.int32, (2 * t, 1), 0) - t
    qrel = lax.broadcasted_iota(jnp.int32, (1, t), 1)
    in_band = (krel <= qrel) & (krel > qrel - SWA_WIN)
    lowest = jnp.where(step > 0, -t, 0)
    mbias = [jnp.where(in_band & (krel >= lowest), 0.0, NEG)] + [jnp.where(in_band, 0.0, NEG)] * (nb - 1)
    k_all = jnp.concatenate([kp_ref[...], kc_ref[...]], axis=0)
    v_tiles = [vp_ref[0]] + [vc_ref[i] for i in range(nb)]
    scores = {}
    for blk in range(nb):
        kband = k_all[blk * t:(blk + 2) * t]
        for g in range(2):
            rows = slice(g * SWA_RPG * HEAD_DIM, (g + 1) * SWA_RPG * HEAD_DIM)
            qg = q_ref[blk, rows, :]
            qcat = jnp.concatenate([qg[r * HEAD_DIM:(r + 1) * HEAD_DIM, :] for r in range(SWA_RPG)], axis=1)
            zq = jnp.zeros_like(qcat)
            qext = jnp.concatenate([qcat, zq] if g == 0 else [zq, qcat], axis=0)
            scores[blk, g] = _dot(kband, qext)
    for blk in range(nb):
        outs = []
        for g in range(2):
            s = scores[blk, g]
            ps, dens = [], []
            for r in range(SWA_RPG):
                h = g * SWA_RPG + r
                sink = sink_ref[h:h + 1, :]
                sr = s[:, r * t:(r + 1) * t] + mbias[blk]
                mx = jnp.maximum(jnp.max(sr, axis=0, keepdims=True), sink)
                e = jnp.exp2(sr - mx)
                dens.append(jnp.sum(e, axis=0, keepdims=True) + jnp.exp2(sink - mx))
                ps.append(e.astype(BF16))
            vband = jnp.concatenate([v_tiles[blk][g * HEAD_DIM:(g + 1) * HEAD_DIM, :],
                                     v_tiles[blk + 1][g * HEAD_DIM:(g + 1) * HEAD_DIM, :]], axis=1)
            og = _dot(vband, jnp.concatenate(ps, axis=1)) * (1.0 / jnp.concatenate(dens, axis=1))
            outs.append(jnp.concatenate([og[:, r * t:(r + 1) * t] for r in range(SWA_RPG)], axis=0).T)
        o_ref[blk * t:(blk + 1) * t, :] = jnp.concatenate(outs, axis=1).astype(o_ref.dtype)


def _swa_mixer(q_tt, k_nat, v_tt, sinks, bsz, seq):
    nb = SWA_STEP_BLOCKS
    nblk = seq // SWA_WIN
    nstep = nblk // nb
    sink_rows = jnp.broadcast_to((sinks.astype(F32) * math.log2(math.e))[:, None], (sinks.shape[0], LANES))
    prev = lambda b, q: b * nblk + jnp.maximum(nb * q - 1, 0)
    return pl.pallas_call(
        _swa_kernel,
        grid=(bsz, nstep),
        in_specs=[
            pl.BlockSpec((nb, SWA_Q, LANES), lambda b, q: (b * nstep + q, 0, 0)),
            pl.BlockSpec((SWA_WIN, SWA_KVW), lambda b, q: (prev(b, q), 0)),
            pl.BlockSpec((nb * SWA_WIN, SWA_KVW), lambda b, q: (b * nstep + q, 0)),
            pl.BlockSpec((1, SWA_KVW, LANES), lambda b, q: (prev(b, q), 0, 0)),
            pl.BlockSpec((nb, SWA_KVW, LANES), lambda b, q: (b * nstep + q, 0, 0)),
            _full(sink_rows.shape),
        ],
        out_specs=pl.BlockSpec((nb * SWA_WIN, SWA_Q), lambda b, q: (b * nstep + q, 0)),
        out_shape=jax.ShapeDtypeStruct((bsz * seq, SWA_Q), BF16),
        compiler_params=_cparams("parallel", "parallel"),
        name="swa_attention",
    )(q_tt, k_nat, k_nat, v_tt, v_tt, sink_rows)


def _s5_params(a_re, a_im, log_dt, b_re, b_im, c_re, c_im, n_chunks):
    f = F32
    t = S5_CHUNK
    step = jnp.exp(log_dt.astype(f))[:, None]
    lr, li = a_re.astype(f), a_im.astype(f)

    def lam_pow(tau):
        tau = tau.astype(f)[:, None, None]
        mag = jnp.exp(lr * step * tau)
        ang = li * step * tau
        return mag * jnp.cos(ang), mag * jnp.sin(ang)

    lb_r, lb_i = (v[0] for v in lam_pow(jnp.ones((1,))))
    nr, ni = lb_r - 1.0, lb_i
    den = lr * lr + li * li
    fr, fi = (nr * lr + ni * li) / den, (ni * lr - nr * li) / den
    br, bi = b_re.astype(f), b_im.astype(f)
    bb_r = fr[..., None] * br - fi[..., None] * bi
    bb_i = fr[..., None] * bi + fi[..., None] * br
    cr, ci = c_re.astype(f), c_im.astype(f)

    pr, pi = lam_pow(jnp.arange(t + 1))
    cl_r = cr[None] * pr[:, :, None, :] - ci[None] * pi[:, :, None, :]
    cl_i = cr[None] * pi[:, :, None, :] + ci[None] * pr[:, :, None, :]
    kern_t = jnp.einsum("tghp,gpk->gkth", cl_r[:t], bb_r, precision="highest") - jnp.einsum(
        "tghp,gpk->gkth", cl_i[:t], bb_i, precision="highest")
    rr, ri = pr[t - 1 - jnp.arange(t)], pi[t - 1 - jnp.arange(t)]
    bs_r = rr[..., None] * bb_r[None] - ri[..., None] * bb_i[None]
    bs_i = rr[..., None] * bb_i[None] + ri[..., None] * bb_r[None]
    bs = jnp.concatenate([bs_r, bs_i], axis=2)
    bs = bs.transpose(1, 0, 3, 2).reshape(S5_GROUPS, t * S5_GROUP_CH, 2 * S5_STATE)
    cs = jnp.concatenate([cl_r[1:], -cl_i[1:]], axis=3)
    cs = cs.transpose(1, 3, 0, 2).reshape(S5_GROUPS, 2 * S5_STATE, t * S5_GROUP_CH)
    ar, ai = pr[t], pi[t]
    a1, a2 = [], []
    k = 1
    while k < n_chunks:
        a1.append(jnp.concatenate([ar, ar], axis=1))
        a2.append(jnp.concatenate([-ai, ai], axis=1))
        ar, ai = ar * ar - ai * ai, 2.0 * ar * ai
        k *= 2
    a1 = jnp.stack(a1, axis=1)
    a2 = jnp.stack(a2, axis=1)
    noct = S5_GROUPS // S5_OCT
    hc, ns2 = S5_GROUP_CH, 2 * S5_STATE
    width = t * hc
    lanes = t * LANES
    r_i, c_i = np.arange(width)[:, None], np.arange(lanes)[None, :]
    rep = jnp.asarray((r_i // hc == c_i // LANES) & (r_i % hc == c_i % hc), dtype=BF16)
    kern_o = kern_t.reshape(noct, LANES, width).astype(BF16)
    toep_o = pl.pallas_call(
        _s5_toeplitz_kernel,
        grid=(noct, t // 8),
        in_specs=[pl.BlockSpec((1, LANES, width), lambda o, s: (o, 0, 0)), _full((width, lanes))],
        out_specs=pl.BlockSpec((1, 8 * LANES, lanes), lambda o, s: (o, s, 0)),
        out_shape=jax.ShapeDtypeStruct((noct, lanes, lanes), BF16),
        scratch_shapes=[pltpu.VMEM((t, LANES, LANES), F32)],
        compiler_params=_cparams("parallel", "arbitrary"),
        name="s5_toeplitz_table",
    )(kern_o, rep)
    cs_o = pl.pallas_call(
        _s5_readout_kernel,
        grid=(noct,),
        in_specs=[pl.BlockSpec((1, S5_OCT * ns2, width), lambda o: (o, 0, 0)), _full((width, lanes))],
        out_specs=pl.BlockSpec((1, S5_OCT * ns2, lanes), lambda o: (o, 0, 0)),
        out_shape=jax.ShapeDtypeStruct((noct, S5_OCT * ns2, lanes), BF16),
        compiler_params=_cparams("parallel"),
        name="s5_readout_table",
    )(cs.reshape(noct, S5_OCT * ns2, width).astype(BF16), rep)
    bs_slabs = bs.reshape(noct, S5_OCT, t, hc, ns2).transpose(0, 2, 1, 3, 4).reshape(noct, t, LANES, ns2).astype(BF16)
    bs_o = pl.pallas_call(
        _s5_state_in_kernel,
        grid=(noct,),
        in_specs=[pl.BlockSpec((1, t, LANES, ns2), lambda o: (o, 0, 0, 0))],
        out_specs=pl.BlockSpec((1, lanes, S5_OCT * ns2), lambda o: (o, 0, 0)),
        out_shape=jax.ShapeDtypeStruct((noct, lanes, S5_OCT * ns2), BF16),
        compiler_params=_cparams("parallel"),
        name="s5_state_in_table",
    )(bs_slabs)

    def oct_rows(a):
        return a.reshape(noct, S5_OCT, -1, ns2).transpose(0, 2, 1, 3).reshape(noct, -1, S5_OCT * ns2)

    return toep_o, bs_o, cs_o, oct_rows(a1), oct_rows(a2)


def _same_group(shape, row_div, col_mod, col_div):
    r = lax.broadcasted_iota(jnp.int32, shape, 0)
    c = lax.broadcasted_iota(jnp.int32, shape, 1)
    return (r // row_div) == ((c % col_mod) // col_div)


def _s5_toeplitz_kernel(k_ref, rep_ref, o_ref, full_s):
    nlag = full_s.shape[0]
    steps_here = o_ref.shape[1] // LANES

    @pl.when(pl.program_id(1) == 0)
    def _():
        full = _dot(k_ref[0], rep_ref[...])
        full = jnp.where(_same_group(full.shape, S5_GROUP_CH, LANES, S5_GROUP_CH), full, 0.0)
        for lag in range(nlag):
            full_s[lag] = full[:, lag * LANES:(lag + 1) * LANES]

    for i in range(steps_here):
        s = pl.program_id(1) * steps_here + i
        for t in range(nlag):
            tile = full_s[jnp.maximum(t - s, 0)]
            o_ref[0, i * LANES:(i + 1) * LANES, t * LANES:(t + 1) * LANES] = (
                jnp.where(t >= s, tile, 0.0).astype(o_ref.dtype))


def _s5_readout_kernel(c_ref, rep_ref, o_ref):
    full = _dot(c_ref[0], rep_ref[...])
    keep = _same_group(full.shape, 2 * S5_STATE, LANES, S5_GROUP_CH)
    o_ref[0] = jnp.where(keep, full, 0.0).astype(o_ref.dtype)


def _s5_state_in_kernel(b_ref, o_ref):
    keep = _same_group((LANES, o_ref.shape[2]), S5_GROUP_CH, o_ref.shape[2], 2 * S5_STATE)
    for s in range(b_ref.shape[1]):
        full = jnp.concatenate([b_ref[0, s]] * S5_OCT, axis=1)
        o_ref[0, s * LANES:(s + 1) * LANES, :] = jnp.where(keep, full, jnp.zeros((), full.dtype))


def _s5_state_kernel(bsz, u_ref, bs_ref, a1_ref, a2_ref, hi_ref, lo_ref):
    sc = _dot(u_ref[...], bs_ref[0])
    n = sc.shape[0] // bsz
    width = sc.shape[1]
    rowi = lax.broadcasted_iota(jnp.int32, (n, width), 0)

    def swap_re_im(x):
        return jnp.concatenate([pltpu.roll(x[:, j * LANES:(j + 1) * LANES], S5_STATE, 1)
                                for j in range(width // LANES)], axis=1)

    h_in = []
    for b in range(bsz):
        x = sc[b * n:(b + 1) * n]
        k, step = 1, 0
        while k < n:
            xs = jnp.where(rowi >= k, pltpu.roll(x, k, 0), 0.0)
            x = x + a1_ref[0, step:step + 1, :] * xs + a2_ref[0, step:step + 1, :] * swap_re_im(xs)
            k *= 2
            step += 1
        h_in.append(jnp.where(rowi >= 1, pltpu.roll(x, 1, 0), 0.0))
    h_in = jnp.concatenate(h_in, axis=0)
    hi = h_in.astype(BF16)
    hi_ref[0] = hi
    lo_ref[0] = (h_in - hi.astype(F32)).astype(BF16)


def _s5_out_kernel(u_ref, toep_ref, hi_ref, lo_ref, cs_ref, o_ref):
    n = pl.program_id(1)
    cols = toep_ref.shape[2]
    off = _dot(hi_ref[0], cs_ref[0]) + _dot(lo_ref[0], cs_ref[0])
    for nn in range(toep_ref.shape[1] // cols):
        @pl.when(n == nn)
        def _(nn=nn):
            k = (nn + 1) * cols
            y = off + _dot(u_ref[:, 0:k], toep_ref[0, 0:k, :])
            for t8 in range(cols // LANES):
                o_ref[:, t8, :] = y[:, t8 * LANES:(t8 + 1) * LANES].astype(o_ref.dtype)


def _s5_glu_kernel(y_ref, u_ref, d_ref, w_ref, b_ref, o_ref):
    y = _gelu_tanh(y_ref[...] + d_ref[...] * u_ref[...])
    gate = _sigmoid(_dot(y.astype(BF16), w_ref[...]) + b_ref[...])
    o_ref[...] = (y * gate).astype(o_ref.dtype)


def _s5_mixer(u5, u_chunks, bsz, seq, a_re, a_im, log_dt, b_re, b_im, c_re, c_im, d_skip, glu_w, glu_b, tm=1024):
    m = u5.shape[0]
    t = S5_CHUNK
    nch = m // t
    noct = S5_GROUPS // S5_OCT
    lanes = t * LANES
    sw = S5_OCT * 2 * S5_STATE
    toep, bs, cs, a1, a2 = _s5_params(a_re, a_im, log_dt, b_re, b_im, c_re, c_im, nch // bsz)
    nsteps = a1.shape[1]
    h_hi, h_lo = pl.pallas_call(
        functools.partial(_s5_state_kernel, bsz),
        grid=(noct,),
        in_specs=[
            pl.BlockSpec((nch, lanes), lambda o: (0, o)),
            pl.BlockSpec((1, lanes, sw), lambda o: (o, 0, 0)),
            pl.BlockSpec((1, nsteps, sw), lambda o: (o, 0, 0)),
            pl.BlockSpec((1, nsteps, sw), lambda o: (o, 0, 0)),
        ],
        out_specs=[pl.BlockSpec((1, nch, sw), lambda o: (o, 0, 0))] * 2,
        out_shape=[jax.ShapeDtypeStruct((noct, nch, sw), BF16)] * 2,
        compiler_params=_cparams("parallel"),
        name="s5_state",
    )(u_chunks, bs, a1, a2)
    tsub = 8
    per_tile = pl.Buffered(1)
    y = pl.pallas_call(
        _s5_out_kernel,
        grid=(noct, t // tsub),
        in_specs=[
            pl.BlockSpec((nch, lanes), lambda o, n: (0, o), pipeline_mode=per_tile),
            pl.BlockSpec((1, lanes, tsub * LANES), lambda o, n: (o, 0, n)),
            pl.BlockSpec((1, nch, sw), lambda o, n: (o, 0, 0), pipeline_mode=per_tile),
            pl.BlockSpec((1, nch, sw), lambda o, n: (o, 0, 0), pipeline_mode=per_tile),
            pl.BlockSpec((1, sw, tsub * LANES), lambda o, n: (o, 0, n)),
        ],
        out_specs=pl.BlockSpec((nch, tsub, LANES), lambda o, n: (0, n, o)),
        out_shape=jax.ShapeDtypeStruct((nch, t, S5_CH), BF16),
        compiler_params=_cparams("parallel", "parallel"),
        name="s5_scan",
    )(u_chunks, toep, h_hi, h_lo, cs).reshape(m, S5_CH)
    return pl.pallas_call(
        _s5_glu_kernel,
        grid=(m // tm,),
        in_specs=[
            pl.BlockSpec((tm, S5_CH), lambda i: (i, 0)),
            pl.BlockSpec((tm, S5_CH), lambda i: (i, 0)),
            _full((1, S5_CH)), _full((S5_CH, S5_CH)), _full((1, S5_CH)),
        ],
        out_specs=pl.BlockSpec((tm, S5_CH), lambda i: (i, 0)),
        out_shape=jax.ShapeDtypeStruct((m, S5_CH), BF16),
        compiler_params=_cparams("parallel"),
        name="s5_glu",
    )(y, u5, d_skip.reshape(1, S5_CH).astype(F32), glu_w.astype(BF16), glu_b.reshape(1, S5_CH).astype(F32))


def _even_mixers(h, bsz, seq, g_mix, w_in, conv_w, conv_b, dt_bias, a_log, d_skip, norm_g, pe, w1, b1, w2, b2):
    d = h.shape[1]
    scale = HEAD_DIM ** -0.5 * math.log2(math.e)
    o = SSD_IN
    w_ssd = jnp.concatenate([w_in[:, :SSD_IN], jnp.zeros((d, LANES - SSD_HEADS), w_in.dtype)], axis=1)
    w_q = w_in[:, o:o + NSA_Q] * scale
    kv = [w_in[:, o + NSA_Q + i * NSA_KVW:o + NSA_Q + (i + 1) * NSA_KVW] for i in range(6)]
    w_gate = w_in[:, o + NSA_Q + 6 * NSA_KVW:].reshape(d, NSA_KV, NSA_RPG, 3).transpose(0, 1, 3, 2)
    w_gate = jnp.concatenate([w_gate.reshape(d, NSA_KV, 3 * NSA_RPG),
                              jnp.zeros((d, NSA_KV, NSA_GATE_ROWS - 3 * NSA_RPG), w_in.dtype)],
                             axis=2).reshape(d, NSA_KV * NSA_GATE_ROWS)

    def per_group_halves(w):
        wg = w.reshape(d, NSA_KV, HEAD_DIM)
        return jnp.concatenate([wg, jnp.zeros_like(wg)], axis=2).reshape(d, NSA_KV * NSA_KVW)
    segs = [
        ("nat", w_ssd, F32),
        ("tt", w_q, BF16),
        ("pc", jnp.concatenate([kv[0], kv[1]], axis=1), F32),
        ("nat+", per_group_halves(kv[2]), BF16, _nsa_block_onehot(NSA_KTILE)),
        ("nat", per_group_halves(kv[4]), BF16),
        ("tt", jnp.concatenate([kv[3], kv[5]], axis=1), BF16),
        ("tt", w_gate, F32),
    ]
    u_ssd, q_tt, kvc, k_slc, k_win, v_tt, g_tt = _norm_proj(h, g_mix, segs)
    ya = _ssd_mixer(u_ssd, bsz, seq, conv_w, conv_b, dt_bias, a_log, d_skip, norm_g)
    yb = _nsa_mixer(q_tt, kvc, k_slc, k_win, v_tt, g_tt, bsz, seq, pe, w1, b1, w2, b2)
    return ya, yb


def _odd_mixers(h, bsz, seq, g_mix, w_in, sinks, a_re, a_im, log_dt, b_re, b_im, c_re, c_im, d_skip, glu_w, glu_b):
    scale = HEAD_DIM ** -0.5 * math.log2(math.e)
    segs = [
        ("tt", w_in[:, :SWA_Q] * scale, BF16),
        ("nat", w_in[:, SWA_Q:SWA_Q + SWA_KVW], BF16),
        ("tt", w_in[:, SWA_Q + SWA_KVW:SWA_Q + 2 * SWA_KVW], BF16),
        ("nat+ch", w_in[:, SWA_Q + 2 * SWA_KVW:], F32),
    ]
    q_tt, k_nat, v_tt, u5, u_chunks = _norm_proj(h, g_mix, segs)
    yc = _swa_mixer(q_tt, k_nat, v_tt, sinks, bsz, seq)
    yd = _s5_mixer(u5, u_chunks, bsz, seq, a_re, a_im, log_dt, b_re, b_im, c_re, c_im, d_skip, glu_w, glu_b)
    return yc, yd


def kernel(x, norm_mix, norm_mlp, norm_final, mlp_w_up, mlp_w_down, ev_w_in, ev_w_out, ssd_conv_w, ssd_conv_b,
           ssd_dt_bias, ssd_a_log, ssd_d, ssd_norm, nsa_pe, nsa_cmp_w1, nsa_cmp_b1, nsa_cmp_w2, nsa_cmp_b2,
           od_w_in, od_w_out, swa_sinks, s5_a_re, s5_a_im, s5_log_dt, s5_b_re, s5_b_im, s5_c_re, s5_c_im,
           s5_d, s5_glu_w, s5_glu_b):
    bsz, seq, d = x.shape
    depth = norm_mix.shape[0]
    assert seq % (NSA_UNROLL * NSA_KTILE) == 0 and seq >= NSA_WIN + NSA_QBLK
    assert seq % (SWA_STEP_BLOCKS * SWA_WIN) == 0 and (bsz * seq) % 1024 == 0 and d % LANES == 0
    h = x.reshape(bsz * seq, d)
    for layer in range(depth):
        i = layer // 2
        if layer % 2 == 0:
            ya, yb = _even_mixers(h, bsz, seq, norm_mix[layer], ev_w_in[i], ssd_conv_w[i], ssd_conv_b[i],
                                  ssd_dt_bias[i], ssd_a_log[i], ssd_d[i], ssd_norm[i], nsa_pe[i],
                                  nsa_cmp_w1[i], nsa_cmp_b1[i], nsa_cmp_w2[i], nsa_cmp_b2[i])
            w_out = ev_w_out[i]
        else:
            ya, yb = _odd_mixers(h, bsz, seq, norm_mix[layer], od_w_in[i], swa_sinks[i], s5_a_re[i], s5_a_im[i],
                                 s5_log_dt[i], s5_b_re[i], s5_b_im[i], s5_c_re[i], s5_c_im[i], s5_d[i],
                                 s5_glu_w[i], s5_glu_b[i])
            w_out = od_w_out[i]
        h = _mix_out_mlp(h, ya, yb, w_out, norm_mlp[layer], mlp_w_up[layer], mlp_w_down[layer], norm_final,
                         final=(layer == depth - 1))
    return h.reshape(bsz, seq, d)
```

```python
import functools
import math

import jax
import jax.numpy as jnp
import numpy as np
from jax import lax
from jax.experimental import pallas as pl
from jax.experimental.pallas import tpu as pltpu

F32 = jnp.float32
BF16 = jnp.bfloat16

EPS = 1e-6
NEG = -1e30
FORCE = 1e9
HEAD_DIM = 64
LANES = 128
BF16_ROWS = 16
VMEM_LIMIT_BYTES = 56 * 1024 * 1024

SSD_HEADS = 8
SSD_INNER = 512
SSD_GROUPS = 2
SSD_STATE = 128
SSD_CONV = 4
SSD_CHUNK = 128
SSD_STEP_CHUNKS = 2
SSD_CONV_DIM = 1024
SSD_IN = SSD_INNER + SSD_CONV_DIM + SSD_HEADS

NSA_KV = 2
NSA_RPG = 4
NSA_CMP_LEN = 32
NSA_CMP_STRIDE = 16
NSA_SLC_LEN = 64
NSA_TOPK = 16
NSA_WIN = 512
NSA_CMP_HIDDEN = 256
NSA_QBLK = 128
NSA_Q = 512
NSA_KVW = 128
NSA_KTILE = 512
NSA_UNROLL = 2
NSA_GATE_ROWS = 16

SWA_RPG = 4
SWA_WIN = 128
SWA_Q = 512
SWA_KVW = 128
SWA_STEP_BLOCKS = 4

S5_CH = 512
S5_GROUP_CH = 16
S5_GROUPS = 32
S5_STATE = 64
S5_CHUNK = 32
S5_OCT = LANES // S5_GROUP_CH


def _cparams(*sem):
    return pltpu.CompilerParams(dimension_semantics=sem, vmem_limit_bytes=VMEM_LIMIT_BYTES)


def _full(shape):
    n = len(shape)
    return pl.BlockSpec(shape, lambda *_: (0,) * n)


def _dot(a, b):
    return jnp.dot(a, b, preferred_element_type=F32)


def _dot_nt(a, b):
    return lax.dot_general(a, b, (((1,), (1,)), ((), ())), preferred_element_type=F32)


def _split3(a):
    hi = a.astype(BF16)
    r1 = a - hi.astype(F32)
    mid = r1.astype(BF16)
    lo = (r1 - mid.astype(F32)).astype(BF16)
    return hi, mid, lo


def _dot_exact_rhs(a, b_exact):
    hi, mid, lo = _split3(a)
    return _dot(hi, b_exact) + _dot(mid, b_exact) + _dot(lo, b_exact)


def _dot_exact_lhs(a_exact, b):
    hi, mid, lo = _split3(b)
    return _dot(a_exact, hi) + _dot(a_exact, mid) + _dot(a_exact, lo)


def _rms(x, g):
    return x * lax.rsqrt(jnp.mean(x * x, axis=-1, keepdims=True) + EPS) * g


def _gelu_tanh(x):
    c = math.sqrt(2.0 / math.pi)
    return 0.5 * x * (1.0 + jnp.tanh(c * (x + 0.044715 * (x * x * x))))


def _sigmoid(x):
    return 1.0 / (1.0 + jnp.exp(-x))


def _proj_kernel(kinds, tm, h_ref, g_ref, *refs):
    n = len(kinds)
    n_add = sum(k == "nat+" for k in kinds)
    n_out = n + sum(k == "nat+ch" for k in kinds)
    w_refs, add_refs = refs[:n], list(refs[n:n + n_add])
    o_refs, scratch = list(refs[n + n_add:n + n_add + n_out]), list(refs[n + n_add + n_out:])
    yb = _rms(h_ref[...], g_ref[...]).astype(BF16)
    for kind, w_ref in zip(kinds, w_refs):
        o_ref = o_refs.pop(0)
        if kind == "nat":
            o_ref[...] = _dot(yb, w_ref[...]).astype(o_ref.dtype)
        elif kind == "nat+ch":
            ch_ref, ch_s = o_refs.pop(0), scratch.pop(0)
            res = _dot(yb, w_ref[...])
            o_ref[...] = res.astype(o_ref.dtype)
            for j in range(ch_s.shape[0]):
                ch_s[j] = res[:, j * LANES:(j + 1) * LANES]
            for t in range(S5_CHUNK):
                for j in range(ch_s.shape[0]):
                    col = (j * S5_CHUNK + t) * LANES
                    ch_ref[:, col:col + LANES] = ch_s[j, pl.ds(t, tm // S5_CHUNK, stride=S5_CHUNK), :].astype(ch_ref.dtype)
        elif kind == "nat+":
            o_ref[...] = _dot(yb, w_ref[...]).astype(o_ref.dtype) + add_refs.pop(0)[...]
        elif kind == "pc":
            pc_s = scratch.pop(0)
            res = _dot(yb, w_ref[...])
            flat = NSA_CMP_STRIDE * HEAD_DIM
            per_tile = LANES // HEAD_DIM
            for j in range(pc_s.shape[0]):
                pc_s[j] = res[:, j * LANES:(j + 1) * LANES]
            for p in range(NSA_CMP_STRIDE):
                for j in range(pc_s.shape[0]):
                    tok = pc_s[j, pl.ds(p, tm // NSA_CMP_STRIDE, stride=NSA_CMP_STRIDE), :]
                    for c in range(per_tile):
                        col = (j * per_tile + c) * flat + p * HEAD_DIM
                        o_ref[:, col:col + HEAD_DIM] = tok[:, c * HEAD_DIM:(c + 1) * HEAD_DIM].astype(o_ref.dtype)
        else:
            ot = _dot_nt(w_ref[...], yb)
            for j in range(tm // LANES):
                o_ref[j] = ot[:, j * LANES:(j + 1) * LANES].astype(o_ref.dtype)


def _norm_proj(h, g, segs, tm=512):
    m, d = h.shape
    kinds = tuple(s[0] for s in segs)
    adds = [s[3] for s in segs if s[0] == "nat+"]
    ws, w_specs, out_shapes, out_specs, scratch = [], [], [], [], []
    for kind, w, dt in (s[:3] for s in segs):
        n_out = w.shape[1]
        if kind == "pc":
            ws.append(w.astype(BF16))
            w_specs.append(_full((d, n_out)))
            out_shapes.append(jax.ShapeDtypeStruct((m // NSA_CMP_STRIDE, NSA_CMP_STRIDE * n_out), dt))
            out_specs.append(pl.BlockSpec((tm // NSA_CMP_STRIDE, NSA_CMP_STRIDE * n_out), lambda i: (i, 0)))
            scratch.append(pltpu.VMEM((n_out // LANES, tm, LANES), F32))
        elif kind in ("nat", "nat+", "nat+ch"):
            ws.append(w.astype(BF16))
            w_specs.append(_full((d, n_out)))
            out_shapes.append(jax.ShapeDtypeStruct((m, n_out), dt))
            out_specs.append(pl.BlockSpec((tm, n_out), lambda i: (i, 0)))
            if kind == "nat+ch":
                out_shapes.append(jax.ShapeDtypeStruct((m // S5_CHUNK, S5_CHUNK * n_out), BF16))
                out_specs.append(pl.BlockSpec((tm // S5_CHUNK, S5_CHUNK * n_out), lambda i: (i, 0)))
                scratch.append(pltpu.VMEM((n_out // LANES, tm, LANES), F32))
        else:
            ws.append(w.T.astype(BF16))
            w_specs.append(_full((n_out, d)))
            out_shapes.append(jax.ShapeDtypeStruct((m // LANES, n_out, LANES), dt))
            out_specs.append(pl.BlockSpec((tm // LANES, n_out, LANES), lambda i: (i, 0, 0)))
    return pl.pallas_call(
        functools.partial(_proj_kernel, kinds, tm),
        grid=(m // tm,),
        in_specs=[pl.BlockSpec((tm, d), lambda i: (i, 0)), _full((1, d))] + w_specs + [
            pl.BlockSpec((tm, a.shape[1]), functools.partial(lambda i, nb: (i % nb, 0), nb=a.shape[0] // tm))
            for a in adds],
        out_specs=out_specs,
        out_shape=out_shapes,
        scratch_shapes=scratch,
        compiler_params=_cparams("parallel"),
        name="norm_proj",
    )(h, g.reshape(1, d), *ws, *adds)


def _mlp_kernel(final, h_ref, ya_ref, yb_ref, woa_ref, wob_ref, gm_ref, wup_ref, wdn_ref, gf_ref,
                o_ref, h2_s, xn_s, acc_s):
    j = pl.program_id(1)

    @pl.when(j == 0)
    def _():
        h2 = h_ref[...] + _dot(ya_ref[...], woa_ref[...]) + _dot(yb_ref[...], wob_ref[...])
        h2_s[...] = h2
        xn_s[...] = _rms(h2, gm_ref[...]).astype(BF16)
        acc_s[...] = jnp.zeros_like(acc_s)

    hid = jnp.square(jnp.maximum(_dot(xn_s[...], wup_ref[...]), 0.0))
    acc_s[...] += _dot(hid.astype(BF16), wdn_ref[...])

    @pl.when(j == pl.num_programs(1) - 1)
    def _():
        out = h2_s[...] + acc_s[...]
        if final:
            out = _rms(out, gf_ref[...])
        o_ref[...] = out


def _mix_out_mlp(h, ya, yb, w_out, g_mlp, w_up, w_down, g_final, final, tm=1024, tf=1024):
    m, d = h.shape
    dff = w_up.shape[1]
    na = ya.shape[1]
    nb = yb.shape[1]
    return pl.pallas_call(
        functools.partial(_mlp_kernel, final),
        grid=(m // tm, dff // tf),
        in_specs=[
            pl.BlockSpec((tm, d), lambda i, j: (i, 0)),
            pl.BlockSpec((tm, na), lambda i, j: (i, 0)),
            pl.BlockSpec((tm, nb), lambda i, j: (i, 0)),
            _full((na, d)), _full((nb, d)), _full((1, d)),
            pl.BlockSpec((d, tf), lambda i, j: (0, j)),
            pl.BlockSpec((tf, d), lambda i, j: (j, 0)),
            _full((1, d)),
        ],
        out_specs=pl.BlockSpec((tm, d), lambda i, j: (i, 0)),
        out_shape=jax.ShapeDtypeStruct((m, d), F32),
        scratch_shapes=[pltpu.VMEM((tm, d), F32), pltpu.VMEM((tm, d), BF16), pltpu.VMEM((tm, d), F32)],
        compiler_params=_cparams("parallel", "arbitrary"),
        name="out_proj_mlp",
    )(h, ya, yb, w_out[:na].astype(BF16), w_out[na:].astype(BF16), g_mlp.reshape(1, d),
      w_up.astype(BF16), w_down.astype(BF16), g_final.reshape(1, d))


def _ssd_kernel(u_ref, cw_ref, cb_ref, dtb_ref, alog_ref, dsk_ref, ng_ref, o_ref, xext_s, st_s):
    @pl.when(pl.program_id(1) == 0)
    def _():
        xext_s[0:8, :] = jnp.zeros((8, SSD_CONV_DIM), F32)
        st_s[...] = jnp.zeros_like(st_s)

    for sub in range(SSD_STEP_CHUNKS):
        rows = slice(sub * SSD_CHUNK, (sub + 1) * SSD_CHUNK)
        _ssd_chunk(u_ref.at[rows], cw_ref, cb_ref, dtb_ref, alog_ref, dsk_ref, ng_ref, o_ref.at[rows], xext_s, st_s)


def _ssd_chunk(u_ref, cw_ref, cb_ref, dtb_ref, alog_ref, dsk_ref, ng_ref, o_ref, xext_s, st_s):
    t = SSD_CHUNK
    z = u_ref[:, 0:SSD_INNER]
    dt_raw = u_ref[:, SSD_INNER + SSD_CONV_DIM:]
    xext_s[8:8 + t, :] = u_ref[:, SSD_INNER:SSD_INNER + SSD_CONV_DIM]
    xfull = xext_s[...]
    conv = cb_ref[...] + cw_ref[SSD_CONV - 1:SSD_CONV, :] * xfull[8:8 + t]
    for back in range(1, SSD_CONV):
        k = SSD_CONV - 1 - back
        conv = conv + cw_ref[k:k + 1, :] * pltpu.roll(xfull, back, 0)[8:8 + t]
    xext_s[0:8, :] = xext_s[t:t + 8, :]
    xc = conv * _sigmoid(conv)
    xs = xc[:, 0:SSD_INNER]
    gn = SSD_GROUPS * SSD_STATE

    dtp = dt_raw + dtb_ref[...]
    dt = jnp.maximum(dtp, 0.0) + jnp.log1p(jnp.exp(-jnp.abs(dtp)))
    a = -jnp.exp(alog_ref[...])
    da = dt * a

    row = lax.broadcasted_iota(jnp.int32, (t, t), 0)
    col = lax.broadcasted_iota(jnp.int32, (t, t), 1)
    causal = col <= row
    tril = jnp.where(causal, 1.0, 0.0).astype(BF16)
    a_cum = _dot_exact_lhs(tril, da)
    er = lax.broadcasted_iota(jnp.int32, (LANES, SSD_INNER), 0)
    ec = lax.broadcasted_iota(jnp.int32, (LANES, SSD_INNER), 1)
    expand = jnp.where((ec >> 6) == er, 1.0, 0.0).astype(BF16)
    a_cum_x = _dot_exact_rhs(a_cum, expand)
    dt_x = _dot_exact_rhs(dt, expand)
    a_cum_t = a_cum.T
    a_last_x = a_cum_x[t - 1:t, :]
    decay_end_x = jnp.exp(a_last_x - a_cum_x)
    decay_in_x = jnp.exp(a_cum_x)
    chunk_decay_x = jnp.exp(a_last_x)

    xd = xs * dt_x
    xd_end = (xd * decay_end_x).astype(BF16)
    xd_b = xd.astype(BF16)
    lane = lax.broadcasted_iota(jnp.int32, (t, LANES), 1)
    first_half = lane < HEAD_DIM

    pieces = []
    for g in range(SSD_GROUPS):
        bm = xc[:, SSD_INNER + g * SSD_STATE:SSD_INNER + (g + 1) * SSD_STATE]
        cm = xc[:, SSD_INNER + gn + g * SSD_STATE:SSD_INNER + gn + (g + 1) * SSD_STATE].astype(BF16)
        bm_t = bm.T.astype(BF16)
        cb = _dot_nt(cm, bm.astype(BF16))
        for pr in range(2):
            i = g * 2 + pr
            sl = slice(i * LANES, (i + 1) * LANES)
            ms = []
            for hh in range(2):
                h = 2 * i + hh
                seg = a_cum[:, h:h + 1] - a_cum_t[h:h + 1, :]
                dec = jnp.exp(jnp.where(causal, seg, NEG))
                ms.append((cb * dec).astype(BF16))
            y_diag = jnp.where(first_half, _dot(ms[0], xd_b[:, sl]), _dot(ms[1], xd_b[:, sl]))
            st = st_s[i]
            y_off = _dot(cm, st.astype(BF16)) * decay_in_x[:, sl]
            st_s[i] = st * chunk_decay_x[:, sl] + _dot(bm_t, xd_end[:, sl])
            pieces.append(y_diag + y_off)
    y = jnp.concatenate(pieces, axis=1) + xs * dsk_ref[...]
    y = y * (z * _sigmoid(z))
    half = SSD_INNER // SSD_GROUPS
    outs = [_rms(y[:, g * half:(g + 1) * half], ng_ref[:, g * half:(g + 1) * half]) for g in range(SSD_GROUPS)]
    o_ref[...] = jnp.concatenate(outs, axis=1).astype(o_ref.dtype)


def _ssd_mixer(u_ssd, bsz, seq, conv_w, conv_b, dt_bias, a_log, d_skip, norm_g):
    m, width = u_ssd.shape
    step_rows = SSD_STEP_CHUNKS * SSD_CHUNK
    nch = seq // step_rows
    pad = LANES - SSD_HEADS

    def padded(v):
        return jnp.concatenate([v.astype(F32), jnp.zeros((pad,), F32)]).reshape(1, LANES)

    return pl.pallas_call(
        _ssd_kernel,
        grid=(bsz, nch),
        in_specs=[
            pl.BlockSpec((step_rows, width), lambda b, c: (b * nch + c, 0)),
            _full((SSD_CONV, SSD_CONV_DIM)), _full((1, SSD_CONV_DIM)),
            _full((1, LANES)), _full((1, LANES)), _full((1, SSD_INNER)), _full((1, SSD_INNER)),
        ],
        out_specs=pl.BlockSpec((step_rows, SSD_INNER), lambda b, c: (b * nch + c, 0)),
        out_shape=jax.ShapeDtypeStruct((m, SSD_INNER), BF16),
        scratch_shapes=[pltpu.VMEM((SSD_CHUNK + 8, SSD_CONV_DIM), F32),
                        pltpu.VMEM((SSD_HEADS // 2, SSD_STATE, LANES), F32)],
        compiler_params=_cparams("arbitrary", "arbitrary"),
        name="ssd_mixer",
    )(u_ssd, conv_w.astype(F32), conv_b.reshape(1, -1).astype(F32), padded(dt_bias), padded(a_log),
      jnp.repeat(d_skip.astype(F32), HEAD_DIM).reshape(1, SSD_INNER), norm_g.reshape(1, SSD_INNER).astype(F32))


def _nsa_compress_kernel(x_ref, pe_ref, w1_ref, b1_ref, w2_ref, b2_ref, o_ref):
    x = x_ref[...]
    npc = x.shape[0]
    half = NSA_CMP_STRIDE * HEAD_DIM
    top = _dot((x + pe_ref[0, :, 0:half]).astype(BF16), w1_ref[0, 0:half, :])
    bot = _dot((x + pe_ref[0, :, half:]).astype(BF16), w1_ref[0, half:, :])
    pre = top + pltpu.roll(bot, npc - 1, 0) + b1_ref[0]
    hid = _gelu_tanh(pre).astype(BF16)
    out = _dot(hid, w2_ref[0]) + b2_ref[0]
    rowi = lax.broadcasted_iota(jnp.int32, out.shape, 0)
    o_ref[0, 0, 0] = jnp.where(rowi < npc - 1, out, 0.0)


def _nsa_compress(kvc_pieces, bsz, seq, pe, w1, b1, w2, b2):
    npc = seq // NSA_CMP_STRIDE
    flat = NSA_CMP_STRIDE * HEAD_DIM
    x = kvc_pieces
    return pl.pallas_call(
        _nsa_compress_kernel,
        grid=(bsz, 2, NSA_KV),
        in_specs=[
            pl.BlockSpec((npc, flat), lambda b, s, g: (b, s * NSA_KV + g)),
            pl.BlockSpec((1, 1, 2 * flat), lambda b, s, g: (s, 0, 0)),
            pl.BlockSpec((1, 2 * flat, NSA_CMP_HIDDEN), lambda b, s, g: (s, 0, 0)),
            pl.BlockSpec((1, 1, NSA_CMP_HIDDEN), lambda b, s, g: (s, 0, 0)),
            pl.BlockSpec((1, NSA_CMP_HIDDEN, HEAD_DIM), lambda b, s, g: (s, 0, 0)),
            pl.BlockSpec((1, 1, HEAD_DIM), lambda b, s, g: (s, 0, 0)),
        ],
        out_specs=pl.BlockSpec((1, 1, 1, npc, HEAD_DIM), lambda b, s, g: (b, s, g, 0, 0)),
        out_shape=jax.ShapeDtypeStruct((bsz, 2, NSA_KV, npc, HEAD_DIM), F32),
        compiler_params=_cparams("parallel", "parallel", "parallel"),
        name="nsa_compress",
    )(x, pe.reshape(2, 1, 2 * flat).astype(F32), w1.astype(BF16), b1.reshape(2, 1, -1).astype(F32),
      w2.astype(BF16), b2.reshape(2, 1, -1).astype(F32))


def _nsa_cmp_split(ns):
    return max(1, min(4, (4 * ns) // LANES))


def _nsa_block_onehot(rows):
    r = lax.broadcasted_iota(jnp.int32, (rows, 2 * NSA_KVW), 0)
    c = lax.broadcasted_iota(jnp.int32, (rows, 2 * NSA_KVW), 1)
    blk = (r % NSA_KTILE) // NSA_SLC_LEN
    return jnp.where((c % NSA_KVW) == HEAD_DIM + blk, 1.0, 0.0).astype(BF16)


def _nsa_kernel(q_ref, kc_ref, vc_ref, ks_ref, kw_ref, vs_ref, vw_ref, g_ref, o_ref, *scratch):
    groups = range(NSA_KV)
    per = len(scratch) // NSA_KV
    bias_s, qaug_s, qaug2_s, sa_s, sb_s, pa_s, pb_s = (
        tuple(scratch[g * per + i] for g in groups) for i in range(per))
    qb = pl.program_id(1)
    nqt = NSA_RPG * NSA_QBLK
    s0 = qb * NSA_QBLK
    ncp = kc_ref.shape[2]
    ns = ncp // 4
    heads = [slice(r * NSA_QBLK, (r + 1) * NSA_QBLK) for r in range(NSA_RPG)]
    klanes = [slice(g * NSA_KVW, (g + 1) * NSA_KVW) for g in groups]
    vrows = [slice(g * HEAD_DIM, (g + 1) * HEAD_DIM) for g in groups]

    qtiles = NSA_QBLK // LANES
    qcat = [jnp.concatenate([q_ref[j, (g * NSA_RPG + r) * HEAD_DIM:(g * NSA_RPG + r + 1) * HEAD_DIM, :]
                             for r in range(NSA_RPG) for j in range(qtiles)], axis=1) for g in groups]
    qpos = s0 + lax.broadcasted_iota(jnp.int32, (1, NSA_QBLK), 1)
    for g in groups:
        qaug_s[g][0:HEAD_DIM, :] = qcat[g]
        qaug_s[g][HEAD_DIM:, :] = jnp.zeros((HEAD_DIM, nqt), BF16)

    split = _nsa_cmp_split(ns)
    chunk = ns // split
    cvalid = jnp.where(qpos >= NSA_CMP_LEN - 1, 1.0, 0.0)
    cur = qpos // NSA_SLC_LEN
    taken = -3.0e38

    def cmp_and_select(nchunks):
        out = []
        for g in groups:
            out.extend(cmp_and_select_group(g, nchunks))
        return tuple(out)

    def cmp_and_select_group(g, nchunks):
        rows, jmax = nchunks * 4 * chunk, nchunks * chunk
        kc = kc_ref[0, g, 0:rows, :]
        rc = lax.broadcasted_iota(jnp.int32, (rows, 1), 0)
        ncmp = 4 * ((rc // (4 * chunk)) * chunk + rc % chunk) + (rc % (4 * chunk)) // chunk
        cbias = jnp.where((ncmp * NSA_CMP_STRIDE + (NSA_CMP_LEN - 1)) <= qpos, 0.0, NEG)
        psum = jnp.zeros((rows, NSA_QBLK), F32)
        p_all = []
        for sl in heads:
            s = _dot(kc, qcat[g][:, sl]) + cbias
            e = jnp.exp2(s - jnp.max(s, axis=0, keepdims=True))
            p = e * (cvalid / jnp.sum(e, axis=0, keepdims=True))
            psum = psum + p
            p_all.append(p.astype(BF16))
        o_cmp = _dot(vc_ref[0, g, :, 0:rows], jnp.concatenate(p_all, axis=1))

        tot, p3 = [], []
        for c in range(nchunks):
            part = [psum[(4 * c + i) * chunk:(4 * c + i + 1) * chunk] for i in range(4)]
            tot.append(part[0] + part[1] + part[2] + part[3])
            p3.append(part[3])
        tot, p3 = jnp.concatenate(tot, axis=0), jnp.concatenate(p3, axis=0)
        rj = lax.broadcasted_iota(jnp.int32, (jmax, NSA_QBLK), 0)
        imp = tot + jnp.where(rj >= 1, pltpu.roll(p3, 1, 0), 0.0)
        forced = (rj == 0) | (rj == cur) | (rj == cur - 1)
        rjf = rj.astype(F32)
        imp = jnp.where(forced, taken, jnp.where(rj <= cur, imp, -FORCE))
        for _ in range(min(NSA_TOPK, ns) - 3):
            mx = jnp.max(imp, axis=0, keepdims=True)
            first = jnp.min(jnp.where(imp == mx, rjf, float(ns)), axis=0, keepdims=True)
            imp = jnp.where(rjf == first, taken, imp)
        bias = jnp.where(imp == taken, 0.0, NEG)
        if jmax < ns:
            bias = jnp.concatenate([bias, jnp.full((ns - jmax, NSA_QBLK), NEG, F32)], axis=0)
        return o_cmp, bias

    last_block = (s0 + NSA_QBLK - 1) // NSA_SLC_LEN
    selected = lax.switch(last_block // chunk, [functools.partial(cmp_and_select, n + 1) for n in range(split)])
    o_cmp = [selected[2 * g] for g in groups]
    for g in groups:
        bias_s[g][...] = selected[2 * g + 1]

    kt_diag = s0 // NSA_KTILE
    blocks_per_tile = NSA_KTILE // NSA_SLC_LEN
    vtiles = NSA_KTILE // LANES

    def qk_tile(g, kt, qaug_ref):
        k0 = pl.multiple_of(kt * NSA_KTILE, NSA_KTILE)
        b8 = bias_s[g][pl.ds(pl.multiple_of(kt * blocks_per_tile, blocks_per_tile), blocks_per_tile), :]
        b16 = jnp.concatenate([b8, jnp.zeros_like(b8)], axis=0).astype(BF16)
        qaug_ref[HEAD_DIM:HEAD_DIM + BF16_ROWS, :] = jnp.concatenate([b16] * NSA_RPG, axis=1)
        return _dot(ks_ref[pl.ds(k0, NSA_KTILE), klanes[g]], qaug_ref[...])

    ones_rows = jnp.ones((BF16_ROWS, NSA_KTILE), BF16)

    def pv_tile(g, kt, p):
        vt = jnp.concatenate([vs_ref[kt * vtiles + i, vrows[g], :] for i in range(vtiles)], axis=1)
        return _dot(jnp.concatenate([vt, ones_rows], axis=0), p)

    def softmax_tile(s, m_old):
        m_new = jnp.maximum(m_old, jnp.max(s, axis=0, keepdims=True))
        p = jnp.exp2((s - m_new).astype(BF16))
        return p, m_new, jnp.exp2(m_old - m_new)

    def visible(kt):
        kpos = kt * NSA_KTILE + lax.broadcasted_iota(jnp.int32, (NSA_KTILE, 1), 0)
        return jnp.concatenate([kpos <= qpos] * NSA_RPG, axis=1)

    def tile_group(i, carry, last):
        carry = list(carry)
        for k in range(NSA_UNROLL):
            t = NSA_UNROLL * i + k
            for g in groups:
                m_run, acc, alpha_prev = carry[g]
                s_cur, p_cur, s_nxt, p_prv, qa = ((sa_s[g], pa_s[g], sb_s[g], pb_s[g], qaug_s[g]) if k % 2 == 0 else
                                                  (sb_s[g], pb_s[g], sa_s[g], pa_s[g], qaug2_s[g]))
                acc = alpha_prev * acc + pv_tile(g, jnp.maximum(t - 1, 0), p_prv[...])
                s = jnp.where(visible(t), s_cur[...], NEG) if last else s_cur[...]
                p, m_run, alpha_prev = softmax_tile(s, m_run)
                p_cur[...] = p
                if last and k == NSA_UNROLL - 1:
                    acc = alpha_prev * acc + pv_tile(g, t, p_cur[...])
                else:
                    s_nxt[...] = qk_tile(g, t + 1, qa)
                carry[g] = (m_run, acc, alpha_prev)
        return tuple(carry)

    for g in groups:
        qaug2_s[g][...] = qaug_s[g][...]
        pb_s[g][...] = jnp.zeros((NSA_KTILE, nqt), BF16)
        sa_s[g][...] = qk_tile(g, 0, qaug2_s[g])

    span = NSA_WIN + NSA_QBLK
    start = pl.multiple_of(jnp.maximum(s0 - NSA_WIN, 0), NSA_QBLK)
    kp = start + lax.broadcasted_iota(jnp.int32, (span, 1), 0)
    wbias = jnp.where((kp <= qpos) & (kp > qpos - NSA_WIN), 0.0, NEG)
    sblk = start // LANES
    o_win = []
    for g in groups:
        kwin = kw_ref[pl.ds(start, span), klanes[g]]
        pw, dens = [], []
        for sl in heads:
            s = _dot(kwin, qaug_s[g][:, sl]) + wbias
            e = jnp.exp2(s - jnp.max(s, axis=0, keepdims=True))
            dens.append(jnp.sum(e, axis=0, keepdims=True))
            pw.append(e.astype(BF16))
        vwt = jnp.concatenate([vw_ref[sblk + i, vrows[g], :] for i in range(span // LANES)], axis=1)
        o_win.append(_dot(vwt, jnp.concatenate(pw, axis=1)) * (1.0 / jnp.concatenate(dens, axis=1)))

    init = tuple((jnp.full((1, nqt), NEG, F32), jnp.zeros((HEAD_DIM + BF16_ROWS, nqt), F32), jnp.ones((1, nqt), F32))
                 for _ in groups)
    group_diag = kt_diag // NSA_UNROLL
    carry = lax.fori_loop(0, group_diag, lambda i, c: tile_group(i, c, False), init)
    carry = tile_group(group_diag, carry, True)

    gates = _sigmoid(jnp.concatenate([g_ref[j] for j in range(qtiles)], axis=1))
    outs = []
    for g in groups:
        acc = carry[g][1]
        o_slc = acc[0:HEAD_DIM] * (1.0 / acc[HEAD_DIM:HEAD_DIM + 1])
        for r, sl in enumerate(heads):
            row = g * NSA_GATE_ROWS + r
            outs.append(gates[row:row + 1, :] * o_cmp[g][:, sl]
                        + gates[row + NSA_RPG:row + NSA_RPG + 1, :] * o_slc[:, sl]
                        + gates[row + 2 * NSA_RPG:row + 2 * NSA_RPG + 1, :] * o_win[g][:, sl])
    o_ref[...] = jnp.concatenate(outs, axis=0).T.astype(o_ref.dtype)


def _nsa_mixer(q_tt, kvc, k_slc, k_win, v_tt, g_tt, bsz, seq, pe, w1, b1, w2, b2):
    nqb = seq // NSA_QBLK
    qtiles = NSA_QBLK // LANES
    ntile = seq // LANES
    npc = seq // NSA_CMP_STRIDE
    ns = seq // NSA_SLC_LEN
    cmp_out = _nsa_compress(kvc, bsz, seq, pe, w1, b1, w2, b2)
    split = _nsa_cmp_split(ns)
    perm = cmp_out.reshape(bsz, 2, NSA_KV, split, ns // split, 4, HEAD_DIM).transpose(0, 1, 2, 3, 5, 4, 6).reshape(
        bsz, 2, NSA_KV, npc, HEAD_DIM)
    kc = perm[:, 0].astype(BF16)
    vc_t = perm[:, 1].transpose(0, 1, 3, 2).astype(BF16)
    nqt = NSA_RPG * NSA_QBLK
    resident = pl.Buffered(1)
    group_scratch = [pltpu.VMEM((ns, NSA_QBLK), F32),
                     pltpu.VMEM((2 * HEAD_DIM, nqt), BF16), pltpu.VMEM((2 * HEAD_DIM, nqt), BF16),
                     pltpu.VMEM((NSA_KTILE, nqt), F32), pltpu.VMEM((NSA_KTILE, nqt), F32),
                     pltpu.VMEM((NSA_KTILE, nqt), BF16), pltpu.VMEM((NSA_KTILE, nqt), BF16)]
    return pl.pallas_call(
        _nsa_kernel,
        grid=(bsz, nqb),
        in_specs=[
            pl.BlockSpec((qtiles, NSA_Q, LANES), lambda b, q: (b * nqb + q, 0, 0)),
            pl.BlockSpec((1, NSA_KV, npc, HEAD_DIM), lambda b, q: (b, 0, 0, 0), pipeline_mode=resident),
            pl.BlockSpec((1, NSA_KV, HEAD_DIM, npc), lambda b, q: (b, 0, 0, 0), pipeline_mode=resident),
            pl.BlockSpec((seq, NSA_KV * NSA_KVW), lambda b, q: (b, 0), pipeline_mode=resident),
            pl.BlockSpec((seq, NSA_KV * NSA_KVW), lambda b, q: (b, 0), pipeline_mode=resident),
            pl.BlockSpec((ntile, NSA_KVW, LANES), lambda b, q: (b, 0, 0), pipeline_mode=resident),
            pl.BlockSpec((ntile, NSA_KVW, LANES), lambda b, q: (b, 1, 0), pipeline_mode=resident),
            pl.BlockSpec((qtiles, NSA_KV * NSA_GATE_ROWS, LANES), lambda b, q: (b * nqb + q, 0, 0)),
        ],
        out_specs=pl.BlockSpec((NSA_QBLK, NSA_Q), lambda b, q: (b * nqb + q, 0)),
        out_shape=jax.ShapeDtypeStruct((bsz * seq, NSA_Q), BF16),
        scratch_shapes=group_scratch * NSA_KV,
        compiler_params=_cparams("arbitrary", "arbitrary"),
        name="nsa_attention",
    )(q_tt, kc, vc_t, k_slc, k_win, v_tt, v_tt, g_tt)


def _swa_kernel(q_ref, kp_ref, kc_ref, vp_ref, vc_ref, sink_ref, o_ref):
    step = pl.program_id(1)
    t = SWA_WIN
    nb = SWA_STEP_BLOCKS
    krel = lax.broadcasted_iota(jnp.int32, (2 * t, 1), 0) - t
    qrel = lax.broadcasted_iota(jnp.int32, (1, t), 1)
    in_band = (krel <= qrel) & (krel > qrel - SWA_WIN)
    lowest = jnp.where(step > 0, -t, 0)
    mbias = [jnp.where(in_band & (krel >= lowest), 0.0, NEG)] + [jnp.where(in_band, 0.0, NEG)] * (nb - 1)
    k_all = jnp.concatenate([kp_ref[...], kc_ref[...]], axis=0)
    v_tiles = [vp_ref[0]] + [vc_ref[i] for i in range(nb)]
    scores = {}
    for blk in range(nb):
        kband = k_all[blk * t:(blk + 2) * t]
        for g in range(2):
            rows = slice(g * SWA_RPG * HEAD_DIM, (g + 1) * SWA_RPG * HEAD_DIM)
            qg = q_ref[blk, rows, :]
            qcat = jnp.concatenate([qg[r * HEAD_DIM:(r + 1) * HEAD_DIM, :] for r in range(SWA_RPG)], axis=1)
            zq = jnp.zeros_like(qcat)
            qext = jnp.concatenate([qcat, zq] if g == 0 else [zq, qcat], axis=0)
            scores[blk, g] = _dot(kband, qext)
    for blk in range(nb):
        outs = []
        for g in range(2):
            s = scores[blk, g]
            ps, dens = [], []
            for r in range(SWA_RPG):
                h = g * SWA_RPG + r
                sink = sink_ref[h:h + 1, :]
                sr = s[:, r * t:(r + 1) * t] + mbias[blk]
                mx = jnp.maximum(jnp.max(sr, axis=0, keepdims=True), sink)
                e = jnp.exp2(sr - mx)
                dens.append(jnp.sum(e, axis=0, keepdims=True) + jnp.exp2(sink - mx))
                ps.append(e.astype(BF16))
            vband = jnp.concatenate([v_tiles[blk][g * HEAD_DIM:(g + 1) * HEAD_DIM, :],
                                     v_tiles[blk + 1][g * HEAD_DIM:(g + 1) * HEAD_DIM, :]], axis=1)
            og = _dot(vband, jnp.concatenate(ps, axis=1)) * (1.0 / jnp.concatenate(dens, axis=1))
            outs.append(jnp.concatenate([og[:, r * t:(r + 1) * t] for r in range(SWA_RPG)], axis=0).T)
        o_ref[blk * t:(blk + 1) * t, :] = jnp.concatenate(outs, axis=1).astype(o_ref.dtype)


def _swa_mixer(q_tt, k_nat, v_tt, sinks, bsz, seq):
    nb = SWA_STEP_BLOCKS
    nblk = seq // SWA_WIN
    nstep = nblk // nb
    sink_rows = jnp.broadcast_to((sinks.astype(F32) * math.log2(math.e))[:, None], (sinks.shape[0], LANES))
    prev = lambda b, q: b * nblk + jnp.maximum(nb * q - 1, 0)
    return pl.pallas_call(
        _swa_kernel,
        grid=(bsz, nstep),
        in_specs=[
            pl.BlockSpec((nb, SWA_Q, LANES), lambda b, q: (b * nstep + q, 0, 0)),
            pl.BlockSpec((SWA_WIN, SWA_KVW), lambda b, q: (prev(b, q), 0)),
            pl.BlockSpec((nb * SWA_WIN, SWA_KVW), lambda b, q: (b * nstep + q, 0)),
            pl.BlockSpec((1, SWA_KVW, LANES), lambda b, q: (prev(b, q), 0, 0)),
            pl.BlockSpec((nb, SWA_KVW, LANES), lambda b, q: (b * nstep + q, 0, 0)),
            _full(sink_rows.shape),
        ],
        out_specs=pl.BlockSpec((nb * SWA_WIN, SWA_Q), lambda b, q: (b * nstep + q, 0)),
        out_shape=jax.ShapeDtypeStruct((bsz * seq, SWA_Q), BF16),
        compiler_params=_cparams("parallel", "parallel"),
        name="swa_attention",
    )(q_tt, k_nat, k_nat, v_tt, v_tt, sink_rows)


def _s5_params(a_re, a_im, log_dt, b_re, b_im, c_re, c_im, n_chunks):
    f = F32
    t = S5_CHUNK
    step = jnp.exp(log_dt.astype(f))[:, None]
    lr, li = a_re.astype(f), a_im.astype(f)

    def lam_pow(tau):
        tau = tau.astype(f)[:, None, None]
        mag = jnp.exp(lr * step * tau)
        ang = li * step * tau
        return mag * jnp.cos(ang), mag * jnp.sin(ang)

    lb_r, lb_i = (v[0] for v in lam_pow(jnp.ones((1,))))
    nr, ni = lb_r - 1.0, lb_i
    den = lr * lr + li * li
    fr, fi = (nr * lr + ni * li) / den, (ni * lr - nr * li) / den
    br, bi = b_re.astype(f), b_im.astype(f)
    bb_r = fr[..., None] * br - fi[..., None] * bi
    bb_i = fr[..., None] * bi + fi[..., None] * br
    cr, ci = c_re.astype(f), c_im.astype(f)

    pr, pi = lam_pow(jnp.arange(t + 1))
    cl_r = cr[None] * pr[:, :, None, :] - ci[None] * pi[:, :, None, :]
    cl_i = cr[None] * pi[:, :, None, :] + ci[None] * pr[:, :, None, :]
    kern_t = jnp.einsum("tghp,gpk->gkth", cl_r[:t], bb_r, precision="highest") - jnp.einsum(
        "tghp,gpk->gkth", cl_i[:t], bb_i, precision="highest")
    rr, ri = pr[t - 1 - jnp.arange(t)], pi[t - 1 - jnp.arange(t)]
    bs_r = rr[..., None] * bb_r[None] - ri[..., None] * bb_i[None]
    bs_i = rr[..., None] * bb_i[None] + ri[..., None] * bb_r[None]
    bs = jnp.concatenate([bs_r, bs_i], axis=2)
    bs = bs.transpose(1, 0, 3, 2).reshape(S5_GROUPS, t * S5_GROUP_CH, 2 * S5_STATE)
    cs = jnp.concatenate([cl_r[1:], -cl_i[1:]], axis=3)
    cs = cs.transpose(1, 3, 0, 2).reshape(S5_GROUPS, 2 * S5_STATE, t * S5_GROUP_CH)
    ar, ai = pr[t], pi[t]
    a1, a2 = [], []
    k = 1
    while k < n_chunks:
        a1.append(jnp.concatenate([ar, ar], axis=1))
        a2.append(jnp.concatenate([-ai, ai], axis=1))
        ar, ai = ar * ar - ai * ai, 2.0 * ar * ai
        k *= 2
    a1 = jnp.stack(a1, axis=1)
    a2 = jnp.stack(a2, axis=1)
    noct = S5_GROUPS // S5_OCT
    hc, ns2 = S5_GROUP_CH, 2 * S5_STATE
    width = t * hc
    lanes = t * LANES
    r_i, c_i = np.arange(width)[:, None], np.arange(lanes)[None, :]
    rep = jnp.asarray((r_i // hc == c_i // LANES) & (r_i % hc == c_i % hc), dtype=BF16)
    kern_o = kern_t.reshape(noct, LANES, width).astype(BF16)
    toep_o = pl.pallas_call(
        _s5_toeplitz_kernel,
        grid=(noct, t // 8),
        in_specs=[pl.BlockSpec((1, LANES, width), lambda o, s: (o, 0, 0)), _full((width, lanes))],
        out_specs=pl.BlockSpec((1, 8 * LANES, lanes), lambda o, s: (o, s, 0)),
        out_shape=jax.ShapeDtypeStruct((noct, lanes, lanes), BF16),
        scratch_shapes=[pltpu.VMEM((t, LANES, LANES), F32)],
        compiler_params=_cparams("parallel", "arbitrary"),
        name="s5_toeplitz_table",
    )(kern_o, rep)
    cs_o = pl.pallas_call(
        _s5_readout_kernel,
        grid=(noct,),
        in_specs=[pl.BlockSpec((1, S5_OCT * ns2, width), lambda o: (o, 0, 0)), _full((width, lanes))],
        out_specs=pl.BlockSpec((1, S5_OCT * ns2, lanes), lambda o: (o, 0, 0)),
        out_shape=jax.ShapeDtypeStruct((noct, S5_OCT * ns2, lanes), BF16),
        compiler_params=_cparams("parallel"),
        name="s5_readout_table",
    )(cs.reshape(noct, S5_OCT * ns2, width).astype(BF16), rep)
    bs_slabs = bs.reshape(noct, S5_OCT, t, hc, ns2).transpose(0, 2, 1, 3, 4).reshape(noct, t, LANES, ns2).astype(BF16)
    bs_o = pl.pallas_call(
        _s5_state_in_kernel,
        grid=(noct,),
        in_specs=[pl.BlockSpec((1, t, LANES, ns2), lambda o: (o, 0, 0, 0))],
        out_specs=pl.BlockSpec((1, lanes, S5_OCT * ns2), lambda o: (o, 0, 0)),
        out_shape=jax.ShapeDtypeStruct((noct, lanes, S5_OCT * ns2), BF16),
        compiler_params=_cparams("parallel"),
        name="s5_state_in_table",
    )(bs_slabs)

    def oct_rows(a):
        return a.reshape(noct, S5_OCT, -1, ns2).transpose(0, 2, 1, 3).reshape(noct, -1, S5_OCT * ns2)

    return toep_o, bs_o, cs_o, oct_rows(a1), oct_rows(a2)


def _same_group(shape, row_div, col_mod, col_div):
    r = lax.broadcasted_iota(jnp.int32, shape, 0)
    c = lax.broadcasted_iota(jnp.int32, shape, 1)
    return (r // row_div) == ((c % col_mod) // col_div)


def _s5_toeplitz_kernel(k_ref, rep_ref, o_ref, full_s):
    nlag = full_s.shape[0]
    steps_here = o_ref.shape[1] // LANES

    @pl.when(pl.program_id(1) == 0)
    def _():
        full = _dot(k_ref[0], rep_ref[...])
        full = jnp.where(_same_group(full.shape, S5_GROUP_CH, LANES, S5_GROUP_CH), full, 0.0)
        for lag in range(nlag):
            full_s[lag] = full[:, lag * LANES:(lag + 1) * LANES]

    for i in range(steps_here):
        s = pl.program_id(1) * steps_here + i
        for t in range(nlag):
            tile = full_s[jnp.maximum(t - s, 0)]
            o_ref[0, i * LANES:(i + 1) * LANES, t * LANES:(t + 1) * LANES] = (
                jnp.where(t >= s, tile, 0.0).astype(o_ref.dtype))


def _s5_readout_kernel(c_ref, rep_ref, o_ref):
    full = _dot(c_ref[0], rep_ref[...])
    keep = _same_group(full.shape, 2 * S5_STATE, LANES, S5_GROUP_CH)
    o_ref[0] = jnp.where(keep, full, 0.0).astype(o_ref.dtype)


def _s5_state_in_kernel(b_ref, o_ref):
    keep = _same_group((LANES, o_ref.shape[2]), S5_GROUP_CH, o_ref.shape[2], 2 * S5_STATE)
    for s in range(b_ref.shape[1]):
        full = jnp.concatenate([b_ref[0, s]] * S5_OCT, axis=1)
        o_ref[0, s * LANES:(s + 1) * LANES, :] = jnp.where(keep, full, jnp.zeros((), full.dtype))


def _s5_state_kernel(bsz, u_ref, bs_ref, a1_ref, a2_ref, hi_ref, lo_ref):
    sc = _dot(u_ref[...], bs_ref[0])
    n = sc.shape[0] // bsz
    width = sc.shape[1]
    rowi = lax.broadcasted_iota(jnp.int32, (n, width), 0)

    def swap_re_im(x):
        return jnp.concatenate([pltpu.roll(x[:, j * LANES:(j + 1) * LANES], S5_STATE, 1)
                                for j in range(width // LANES)], axis=1)

    h_in = []
    for b in range(bsz):
        x = sc[b * n:(b + 1) * n]
        k, step = 1, 0
        while k < n:
            xs = jnp.where(rowi >= k, pltpu.roll(x, k, 0), 0.0)
            x = x + a1_ref[0, step:step + 1, :] * xs + a2_ref[0, step:step + 1, :] * swap_re_im(xs)
            k *= 2
            step += 1
        h_in.append(jnp.where(rowi >= 1, pltpu.roll(x, 1, 0), 0.0))
    h_in = jnp.concatenate(h_in, axis=0)
    hi = h_in.astype(BF16)
    hi_ref[0] = hi
    lo_ref[0] = (h_in - hi.astype(F32)).astype(BF16)


def _s5_out_kernel(u_ref, toep_ref, hi_ref, lo_ref, cs_ref, o_ref):
    n = pl.program_id(1)
    cols = toep_ref.shape[2]
    off = _dot(hi_ref[0], cs_ref[0]) + _dot(lo_ref[0], cs_ref[0])
    for nn in range(toep_ref.shape[1] // cols):
        @pl.when(n == nn)
        def _(nn=nn):
            k = (nn + 1) * cols
            y = off + _dot(u_ref[:, 0:k], toep_ref[0, 0:k, :])
            for t8 in range(cols // LANES):
                o_ref[:, t8, :] = y[:, t8 * LANES:(t8 + 1) * LANES].astype(o_ref.dtype)


def _s5_glu_kernel(y_ref, u_ref, d_ref, w_ref, b_ref, o_ref):
    y = _gelu_tanh(y_ref[...] + d_ref[...] * u_ref[...])
    gate = _sigmoid(_dot(y.astype(BF16), w_ref[...]) + b_ref[...])
    o_ref[...] = (y * gate).astype(o_ref.dtype)


def _s5_mixer(u5, u_chunks, bsz, seq, a_re, a_im, log_dt, b_re, b_im, c_re, c_im, d_skip, glu_w, glu_b, tm=1024):
    m = u5.shape[0]
    t = S5_CHUNK
    nch = m // t
    noct = S5_GROUPS // S5_OCT
    lanes = t * LANES
    sw = S5_OCT * 2 * S5_STATE
    toep, bs, cs, a1, a2 = _s5_params(a_re, a_im, log_dt, b_re, b_im, c_re, c_im, nch // bsz)
    nsteps = a1.shape[1]
    h_hi, h_lo = pl.pallas_call(
        functools.partial(_s5_state_kernel, bsz),
        grid=(noct,),
        in_specs=[
            pl.BlockSpec((nch, lanes), lambda o: (0, o)),
            pl.BlockSpec((1, lanes, sw), lambda o: (o, 0, 0)),
            pl.BlockSpec((1, nsteps, sw), lambda o: (o, 0, 0)),
            pl.BlockSpec((1, nsteps, sw), lambda o: (o, 0, 0)),
        ],
        out_specs=[pl.BlockSpec((1, nch, sw), lambda o: (o, 0, 0))] * 2,
        out_shape=[jax.ShapeDtypeStruct((noct, nch, sw), BF16)] * 2,
        compiler_params=_cparams("parallel"),
        name="s5_state",
    )(u_chunks, bs, a1, a2)
    tsub = 8
    rows = nch // 2
    y = pl.pallas_call(
        _s5_out_kernel,
        grid=(noct, t // tsub, nch // rows),
        in_specs=[
            pl.BlockSpec((rows, lanes), lambda o, n, r: (r, o)),
            pl.BlockSpec((1, lanes, tsub * LANES), lambda o, n, r: (o, 0, n)),
            pl.BlockSpec((1, rows, sw), lambda o, n, r: (o, r, 0)),
            pl.BlockSpec((1, rows, sw), lambda o, n, r: (o, r, 0)),
            pl.BlockSpec((1, sw, tsub * LANES), lambda o, n, r: (o, 0, n)),
        ],
        out_specs=pl.BlockSpec((rows, tsub, LANES), lambda o, n, r: (r, n, o)),
        out_shape=jax.ShapeDtypeStruct((nch, t, S5_CH), BF16),
        compiler_params=_cparams("parallel", "parallel", "parallel"),
        name="s5_scan",
    )(u_chunks, toep, h_hi, h_lo, cs).reshape(m, S5_CH)
    return pl.pallas_call(
        _s5_glu_kernel,
        grid=(m // tm,),
        in_specs=[
            pl.BlockSpec((tm, S5_CH), lambda i: (i, 0)),
            pl.BlockSpec((tm, S5_CH), lambda i: (i, 0)),
            _full((1, S5_CH)), _full((S5_CH, S5_CH)), _full((1, S5_CH)),
        ],
        out_specs=pl.BlockSpec((tm, S5_CH), lambda i: (i, 0)),
        out_shape=jax.ShapeDtypeStruct((m, S5_CH), BF16),
        compiler_params=_cparams("parallel"),
        name="s5_glu",
    )(y, u5, d_skip.reshape(1, S5_CH).astype(F32), glu_w.astype(BF16), glu_b.reshape(1, S5_CH).astype(F32))


def _even_mixers(h, bsz, seq, g_mix, w_in, conv_w, conv_b, dt_bias, a_log, d_skip, norm_g, pe, w1, b1, w2, b2):
    d = h.shape[1]
    scale = HEAD_DIM ** -0.5 * math.log2(math.e)
    o = SSD_IN
    w_ssd = jnp.concatenate([w_in[:, :SSD_IN], jnp.zeros((d, LANES - SSD_HEADS), w_in.dtype)], axis=1)
    w_q = w_in[:, o:o + NSA_Q] * scale
    kv = [w_in[:, o + NSA_Q + i * NSA_KVW:o + NSA_Q + (i + 1) * NSA_KVW] for i in range(6)]
    w_gate = w_in[:, o + NSA_Q + 6 * NSA_KVW:].reshape(d, NSA_KV, NSA_RPG, 3).transpose(0, 1, 3, 2)
    w_gate = jnp.concatenate([w_gate.reshape(d, NSA_KV, 3 * NSA_RPG),
                              jnp.zeros((d, NSA_KV, NSA_GATE_ROWS - 3 * NSA_RPG), w_in.dtype)],
                             axis=2).reshape(d, NSA_KV * NSA_GATE_ROWS)

    def per_group_halves(w):
        wg = w.reshape(d, NSA_KV, HEAD_DIM)
        return jnp.concatenate([wg, jnp.zeros_like(wg)], axis=2).reshape(d, NSA_KV * NSA_KVW)
    segs = [
        ("nat", w_ssd, F32),
        ("tt", w_q, BF16),
        ("pc", jnp.concatenate([kv[0], kv[1]], axis=1), F32),
        ("nat+", per_group_halves(kv[2]), BF16, _nsa_block_onehot(NSA_KTILE)),
        ("nat", per_group_halves(kv[4]), BF16),
        ("tt", jnp.concatenate([kv[3], kv[5]], axis=1), BF16),
        ("tt", w_gate, F32),
    ]
    u_ssd, q_tt, kvc, k_slc, k_win, v_tt, g_tt = _norm_proj(h, g_mix, segs)
    ya = _ssd_mixer(u_ssd, bsz, seq, conv_w, conv_b, dt_bias, a_log, d_skip, norm_g)
    yb = _nsa_mixer(q_tt, kvc, k_slc, k_win, v_tt, g_tt, bsz, seq, pe, w1, b1, w2, b2)
    return ya, yb


def _odd_mixers(h, bsz, seq, g_mix, w_in, sinks, a_re, a_im, log_dt, b_re, b_im, c_re, c_im, d_skip, glu_w, glu_b):
    scale = HEAD_DIM ** -0.5 * math.log2(math.e)
    segs = [
        ("tt", w_in[:, :SWA_Q] * scale, BF16),
        ("nat", w_in[:, SWA_Q:SWA_Q + SWA_KVW], BF16),
        ("tt", w_in[:, SWA_Q + SWA_KVW:SWA_Q + 2 * SWA_KVW], BF16),
        ("nat+ch", w_in[:, SWA_Q + 2 * SWA_KVW:], F32),
    ]
    q_tt, k_nat, v_tt, u5, u_chunks = _norm_proj(h, g_mix, segs)
    yc = _swa_mixer(q_tt, k_nat, v_tt, sinks, bsz, seq)
    yd = _s5_mixer(u5, u_chunks, bsz, seq, a_re, a_im, log_dt, b_re, b_im, c_re, c_im, d_skip, glu_w, glu_b)
    return yc, yd


def kernel(x, norm_mix, norm_mlp, norm_final, mlp_w_up, mlp_w_down, ev_w_in, ev_w_out, ssd_conv_w, ssd_conv_b,
           ssd_dt_bias, ssd_a_log, ssd_d, ssd_norm, nsa_pe, nsa_cmp_w1, nsa_cmp_b1, nsa_cmp_w2, nsa_cmp_b2,
           od_w_in, od_w_out, swa_sinks, s5_a_re, s5_a_im, s5_log_dt, s5_b_re, s5_b_im, s5_c_re, s5_c_im,
           s5_d, s5_glu_w, s5_glu_b):
    bsz, seq, d = x.shape
    depth = norm_mix.shape[0]
    assert seq % (NSA_UNROLL * NSA_KTILE) == 0 and seq >= NSA_WIN + NSA_QBLK
    assert seq % (SWA_STEP_BLOCKS * SWA_WIN) == 0 and (bsz * seq) % 1024 == 0 and d % LANES == 0
    h = x.reshape(bsz * seq, d)
    for layer in range(depth):
        i = layer // 2
        if layer % 2 == 0:
            ya, yb = _even_mixers(h, bsz, seq, norm_mix[layer], ev_w_in[i], ssd_conv_w[i], ssd_conv_b[i],
                                  ssd_dt_bias[i], ssd_a_log[i], ssd_d[i], ssd_norm[i], nsa_pe[i],
                                  nsa_cmp_w1[i], nsa_cmp_b1[i], nsa_cmp_w2[i], nsa_cmp_b2[i])
            w_out = ev_w_out[i]
        else:
            ya, yb = _odd_mixers(h, bsz, seq, norm_mix[layer], od_w_in[i], swa_sinks[i], s5_a_re[i], s5_a_im[i],
                                 s5_log_dt[i], s5_b_re[i], s5_b_im[i], s5_c_re[i], s5_c_im[i], s5_d[i],
                                 s5_glu_w[i], s5_glu_b[i])
            w_out = od_w_out[i]
        h = _mix_out_mlp(h, ya, yb, w_out, norm_mlp[layer], mlp_w_up[layer], mlp_w_down[layer], norm_final,
                         final=(layer == depth - 1))
    return h.reshape(bsz, seq, d)
```

```python
import functools
import math

import jax
import jax.numpy as jnp
import numpy as np
from jax import lax
from jax.experimental import pallas as pl
from jax.experimental.pallas import tpu as pltpu

F32 = jnp.float32
BF16 = jnp.bfloat16

EPS = 1e-6
NEG = -1e30
FORCE = 1e9
HEAD_DIM = 64
LANES = 128
BF16_ROWS = 16
VMEM_LIMIT_BYTES = 56 * 1024 * 1024

SSD_HEADS = 8
SSD_INNER = 512
SSD_GROUPS = 2
SSD_STATE = 128
SSD_CONV = 4
SSD_CHUNK = 128
SSD_STEP_CHUNKS = 4
SSD_CONV_DIM = 1024
SSD_IN = SSD_INNER + SSD_CONV_DIM + SSD_HEADS

NSA_KV = 2
NSA_RPG = 4
NSA_CMP_LEN = 32
NSA_CMP_STRIDE = 16
NSA_SLC_LEN = 64
NSA_TOPK = 16
NSA_WIN = 512
NSA_CMP_HIDDEN = 256
NSA_QBLK = 128
NSA_Q = 512
NSA_KVW = 128
NSA_KTILE = 512
NSA_UNROLL = 2
NSA_GATE_ROWS = 16

SWA_RPG = 4
SWA_WIN = 128
SWA_Q = 512
SWA_KVW = 128
SWA_STEP_BLOCKS = 4

S5_CH = 512
S5_GROUP_CH = 16
S5_GROUPS = 32
S5_STATE = 64
S5_CHUNK = 32
S5_OCT = LANES // S5_GROUP_CH


def _cparams(*sem):
    return pltpu.CompilerParams(dimension_semantics=sem, vmem_limit_bytes=VMEM_LIMIT_BYTES)


def _full(shape):
    n = len(shape)
    return pl.BlockSpec(shape, lambda *_: (0,) * n)


def _dot(a, b):
    return jnp.dot(a, b, preferred_element_type=F32)


def _dot_nt(a, b):
    return lax.dot_general(a, b, (((1,), (1,)), ((), ())), preferred_element_type=F32)


def _split3(a):
    hi = a.astype(BF16)
    r1 = a - hi.astype(F32)
    mid = r1.astype(BF16)
    lo = (r1 - mid.astype(F32)).astype(BF16)
    return hi, mid, lo


def _dot_exact_rhs(a, b_exact):
    hi, mid, lo = _split3(a)
    return _dot(hi, b_exact) + _dot(mid, b_exact) + _dot(lo, b_exact)


def _dot_exact_lhs(a_exact, b):
    hi, mid, lo = _split3(b)
    return _dot(a_exact, hi) + _dot(a_exact, mid) + _dot(a_exact, lo)


def _rms(x, g):
    return x * lax.rsqrt(jnp.mean(x * x, axis=-1, keepdims=True) + EPS) * g


def _gelu_tanh(x):
    c = math.sqrt(2.0 / math.pi)
    return 0.5 * x * (1.0 + jnp.tanh(c * (x + 0.044715 * (x * x * x))))


def _sigmoid(x):
    return 1.0 / (1.0 + jnp.exp(-x))


def _proj_kernel(kinds, tm, h_ref, g_ref, *refs):
    n = len(kinds)
    n_add = sum(k == "nat+" for k in kinds)
    n_out = n + sum(k == "nat+ch" for k in kinds)
    w_refs, add_refs = refs[:n], list(refs[n:n + n_add])
    o_refs, scratch = list(refs[n + n_add:n + n_add + n_out]), list(refs[n + n_add + n_out:])
    yb = _rms(h_ref[...], g_ref[...]).astype(BF16)
    for kind, w_ref in zip(kinds, w_refs):
        o_ref = o_refs.pop(0)
        if kind == "nat":
            o_ref[...] = _dot(yb, w_ref[...]).astype(o_ref.dtype)
        elif kind == "nat+ch":
            ch_ref, ch_s = o_refs.pop(0), scratch.pop(0)
            res = _dot(yb, w_ref[...])
            o_ref[...] = res.astype(o_ref.dtype)
            for j in range(ch_s.shape[0]):
                ch_s[j] = res[:, j * LANES:(j + 1) * LANES]
            for t in range(S5_CHUNK):
                for j in range(ch_s.shape[0]):
                    col = (j * S5_CHUNK + t) * LANES
                    ch_ref[:, col:col + LANES] = ch_s[j, pl.ds(t, tm // S5_CHUNK, stride=S5_CHUNK), :].astype(ch_ref.dtype)
        elif kind == "nat+":
            o_ref[...] = _dot(yb, w_ref[...]).astype(o_ref.dtype) + add_refs.pop(0)[...]
        elif kind == "pc":
            pc_s = scratch.pop(0)
            res = _dot(yb, w_ref[...])
            flat = NSA_CMP_STRIDE * HEAD_DIM
            per_tile = LANES // HEAD_DIM
            for j in range(pc_s.shape[0]):
                pc_s[j] = res[:, j * LANES:(j + 1) * LANES]
            for p in range(NSA_CMP_STRIDE):
                for j in range(pc_s.shape[0]):
                    tok = pc_s[j, pl.ds(p, tm // NSA_CMP_STRIDE, stride=NSA_CMP_STRIDE), :]
                    for c in range(per_tile):
                        col = (j * per_tile + c) * flat + p * HEAD_DIM
                        o_ref[:, col:col + HEAD_DIM] = tok[:, c * HEAD_DIM:(c + 1) * HEAD_DIM].astype(o_ref.dtype)
        else:
            ot = _dot_nt(w_ref[...], yb)
            for j in range(tm // LANES):
                o_ref[j] = ot[:, j * LANES:(j + 1) * LANES].astype(o_ref.dtype)


def _norm_proj(h, g, segs, tm=512):
    m, d = h.shape
    kinds = tuple(s[0] for s in segs)
    adds = [s[3] for s in segs if s[0] == "nat+"]
    ws, w_specs, out_shapes, out_specs, scratch = [], [], [], [], []
    for kind, w, dt in (s[:3] for s in segs):
        n_out = w.shape[1]
        if kind == "pc":
            ws.append(w.astype(BF16))
            w_specs.append(_full((d, n_out)))
            out_shapes.append(jax.ShapeDtypeStruct((m // NSA_CMP_STRIDE, NSA_CMP_STRIDE * n_out), dt))
            out_specs.append(pl.BlockSpec((tm // NSA_CMP_STRIDE, NSA_CMP_STRIDE * n_out), lambda i: (i, 0)))
            scratch.append(pltpu.VMEM((n_out // LANES, tm, LANES), F32))
        elif kind in ("nat", "nat+", "nat+ch"):
            ws.append(w.astype(BF16))
            w_specs.append(_full((d, n_out)))
            out_shapes.append(jax.ShapeDtypeStruct((m, n_out), dt))
            out_specs.append(pl.BlockSpec((tm, n_out), lambda i: (i, 0)))
            if kind == "nat+ch":
                out_shapes.append(jax.ShapeDtypeStruct((m // S5_CHUNK, S5_CHUNK * n_out), BF16))
                out_specs.append(pl.BlockSpec((tm // S5_CHUNK, S5_CHUNK * n_out), lambda i: (i, 0)))
                scratch.append(pltpu.VMEM((n_out // LANES, tm, LANES), F32))
        else:
            ws.append(w.T.astype(BF16))
            w_specs.append(_full((n_out, d)))
            out_shapes.append(jax.ShapeDtypeStruct((m // LANES, n_out, LANES), dt))
            out_specs.append(pl.BlockSpec((tm // LANES, n_out, LANES), lambda i: (i, 0, 0)))
    return pl.pallas_call(
        functools.partial(_proj_kernel, kinds, tm),
        grid=(m // tm,),
        in_specs=[pl.BlockSpec((tm, d), lambda i: (i, 0)), _full((1, d))] + w_specs + [
            pl.BlockSpec((tm, a.shape[1]), functools.partial(lambda i, nb: (i % nb, 0), nb=a.shape[0] // tm))
            for a in adds],
        out_specs=out_specs,
        out_shape=out_shapes,
        scratch_shapes=scratch,
        compiler_params=_cparams("parallel"),
        name="norm_proj",
    )(h, g.reshape(1, d), *ws, *adds)


def _mlp_kernel(final, h_ref, ya_ref, yb_ref, woa_ref, wob_ref, gm_ref, wup_ref, wdn_ref, gf_ref,
                o_ref, h2_s, xn_s, acc_s):
    j = pl.program_id(1)

    @pl.when(j == 0)
    def _():
        h2 = h_ref[...] + _dot(ya_ref[...], woa_ref[...]) + _dot(yb_ref[...], wob_ref[...])
        h2_s[...] = h2
        xn_s[...] = _rms(h2, gm_ref[...]).astype(BF16)
        acc_s[...] = jnp.zeros_like(acc_s)

    hid = jnp.square(jnp.maximum(_dot(xn_s[...], wup_ref[...]), 0.0))
    acc_s[...] += _dot(hid.astype(BF16), wdn_ref[...])

    @pl.when(j == pl.num_programs(1) - 1)
    def _():
        out = h2_s[...] + acc_s[...]
        if final:
            out = _rms(out, gf_ref[...])
        o_ref[...] = out


def _mix_out_mlp(h, ya, yb, w_out, g_mlp, w_up, w_down, g_final, final, tm=1024, tf=1024):
    m, d = h.shape
    dff = w_up.shape[1]
    na = ya.shape[1]
    nb = yb.shape[1]
    return pl.pallas_call(
        functools.partial(_mlp_kernel, final),
        grid=(m // tm, dff // tf),
        in_specs=[
            pl.BlockSpec((tm, d), lambda i, j: (i, 0)),
            pl.BlockSpec((tm, na), lambda i, j: (i, 0)),
            pl.BlockSpec((tm, nb), lambda i, j: (i, 0)),
            _full((na, d)), _full((nb, d)), _full((1, d)),
            pl.BlockSpec((d, tf), lambda i, j: (0, j)),
            pl.BlockSpec((tf, d), lambda i, j: (j, 0)),
            _full((1, d)),
        ],
        out_specs=pl.BlockSpec((tm, d), lambda i, j: (i, 0)),
        out_shape=jax.ShapeDtypeStruct((m, d), F32),
        scratch_shapes=[pltpu.VMEM((tm, d), F32), pltpu.VMEM((tm, d), BF16), pltpu.VMEM((tm, d), F32)],
        compiler_params=_cparams("parallel", "arbitrary"),
        name="out_proj_mlp",
    )(h, ya, yb, w_out[:na].astype(BF16), w_out[na:].astype(BF16), g_mlp.reshape(1, d),
      w_up.astype(BF16), w_down.astype(BF16), g_final.reshape(1, d))


def _ssd_kernel(u_ref, cw_ref, cb_ref, dtb_ref, alog_ref, dsk_ref, ng_ref, o_ref, xext_s, st_s):
    @pl.when(pl.program_id(1) == 0)
    def _():
        xext_s[0:8, :] = jnp.zeros((8, SSD_CONV_DIM), F32)
        st_s[...] = jnp.zeros_like(st_s)

    for sub in range(SSD_STEP_CHUNKS):
        rows = slice(sub * SSD_CHUNK, (sub + 1) * SSD_CHUNK)
        _ssd_chunk(u_ref.at[rows], cw_ref, cb_ref, dtb_ref, alog_ref, dsk_ref, ng_ref, o_ref.at[rows], xext_s, st_s)


def _ssd_chunk(u_ref, cw_ref, cb_ref, dtb_ref, alog_ref, dsk_ref, ng_ref, o_ref, xext_s, st_s):
    t = SSD_CHUNK
    z = u_ref[:, 0:SSD_INNER]
    dt_raw = u_ref[:, SSD_INNER + SSD_CONV_DIM:]
    xext_s[8:8 + t, :] = u_ref[:, SSD_INNER:SSD_INNER + SSD_CONV_DIM]
    xfull = xext_s[...]
    conv = cb_ref[...] + cw_ref[SSD_CONV - 1:SSD_CONV, :] * xfull[8:8 + t]
    for back in range(1, SSD_CONV):
        k = SSD_CONV - 1 - back
        conv = conv + cw_ref[k:k + 1, :] * pltpu.roll(xfull, back, 0)[8:8 + t]
    xext_s[0:8, :] = xext_s[t:t + 8, :]
    xc = conv * _sigmoid(conv)
    xs = xc[:, 0:SSD_INNER]
    gn = SSD_GROUPS * SSD_STATE

    dtp = dt_raw + dtb_ref[...]
    dt = jnp.maximum(dtp, 0.0) + jnp.log1p(jnp.exp(-jnp.abs(dtp)))
    a = -jnp.exp(alog_ref[...])
    da = dt * a

    row = lax.broadcasted_iota(jnp.int32, (t, t), 0)
    col = lax.broadcasted_iota(jnp.int32, (t, t), 1)
    causal = col <= row
    tril = jnp.where(causal, 1.0, 0.0).astype(BF16)
    a_cum = _dot_exact_lhs(tril, da)
    er = lax.broadcasted_iota(jnp.int32, (LANES, SSD_INNER), 0)
    ec = lax.broadcasted_iota(jnp.int32, (LANES, SSD_INNER), 1)
    expand = jnp.where((ec >> 6) == er, 1.0, 0.0).astype(BF16)
    a_cum_x = _dot_exact_rhs(a_cum, expand)
    dt_x = _dot_exact_rhs(dt, expand)
    a_cum_t = a_cum.T
    a_last_x = a_cum_x[t - 1:t, :]
    decay_end_x = jnp.exp(a_last_x - a_cum_x)
    decay_in_x = jnp.exp(a_cum_x)
    chunk_decay_x = jnp.exp(a_last_x)

    xd = xs * dt_x
    xd_end = (xd * decay_end_x).astype(BF16)
    xd_b = xd.astype(BF16)
    lane = lax.broadcasted_iota(jnp.int32, (t, LANES), 1)
    first_half = lane < HEAD_DIM

    pieces = []
    for g in range(SSD_GROUPS):
        bm = xc[:, SSD_INNER + g * SSD_STATE:SSD_INNER + (g + 1) * SSD_STATE]
        cm = xc[:, SSD_INNER + gn + g * SSD_STATE:SSD_INNER + gn + (g + 1) * SSD_STATE].astype(BF16)
        bm_t = bm.T.astype(BF16)
        cb = _dot_nt(cm, bm.astype(BF16))
        for pr in range(2):
            i = g * 2 + pr
            sl = slice(i * LANES, (i + 1) * LANES)
            ms = []
            for hh in range(2):
                h = 2 * i + hh
                seg = a_cum[:, h:h + 1] - a_cum_t[h:h + 1, :]
                dec = jnp.exp(jnp.where(causal, seg, NEG))
                ms.append((cb * dec).astype(BF16))
            y_diag = jnp.where(first_half, _dot(ms[0], xd_b[:, sl]), _dot(ms[1], xd_b[:, sl]))
            st = st_s[i]
            y_off = _dot(cm, st.astype(BF16)) * decay_in_x[:, sl]
            st_s[i] = st * chunk_decay_x[:, sl] + _dot(bm_t, xd_end[:, sl])
            pieces.append(y_diag + y_off)
    y = jnp.concatenate(pieces, axis=1) + xs * dsk_ref[...]
    y = y * (z * _sigmoid(z))
    half = SSD_INNER // SSD_GROUPS
    outs = [_rms(y[:, g * half:(g + 1) * half], ng_ref[:, g * half:(g + 1) * half]) for g in range(SSD_GROUPS)]
    o_ref[...] = jnp.concatenate(outs, axis=1).astype(o_ref.dtype)


def _ssd_mixer(u_ssd, bsz, seq, conv_w, conv_b, dt_bias, a_log, d_skip, norm_g):
    m, width = u_ssd.shape
    step_rows = SSD_STEP_CHUNKS * SSD_CHUNK
    nch = seq // step_rows
    pad = LANES - SSD_HEADS

    def padded(v):
        return jnp.concatenate([v.astype(F32), jnp.zeros((pad,), F32)]).reshape(1, LANES)

    return pl.pallas_call(
        _ssd_kernel,
        grid=(bsz, nch),
        in_specs=[
            pl.BlockSpec((step_rows, width), lambda b, c: (b * nch + c, 0)),
            _full((SSD_CONV, SSD_CONV_DIM)), _full((1, SSD_CONV_DIM)),
            _full((1, LANES)), _full((1, LANES)), _full((1, SSD_INNER)), _full((1, SSD_INNER)),
        ],
        out_specs=pl.BlockSpec((step_rows, SSD_INNER), lambda b, c: (b * nch + c, 0)),
        out_shape=jax.ShapeDtypeStruct((m, SSD_INNER), BF16),
        scratch_shapes=[pltpu.VMEM((SSD_CHUNK + 8, SSD_CONV_DIM), F32),
                        pltpu.VMEM((SSD_HEADS // 2, SSD_STATE, LANES), F32)],
        compiler_params=_cparams("arbitrary", "arbitrary"),
        name="ssd_mixer",
    )(u_ssd, conv_w.astype(F32), conv_b.reshape(1, -1).astype(F32), padded(dt_bias), padded(a_log),
      jnp.repeat(d_skip.astype(F32), HEAD_DIM).reshape(1, SSD_INNER), norm_g.reshape(1, SSD_INNER).astype(F32))


def _nsa_compress_kernel(x_ref, pe_ref, w1_ref, b1_ref, w2_ref, b2_ref, o_ref):
    x = x_ref[...]
    npc = x.shape[0]
    half = NSA_CMP_STRIDE * HEAD_DIM
    top = _dot((x + pe_ref[0, :, 0:half]).astype(BF16), w1_ref[0, 0:half, :])
    bot = _dot((x + pe_ref[0, :, half:]).astype(BF16), w1_ref[0, half:, :])
    pre = top + pltpu.roll(bot, npc - 1, 0) + b1_ref[0]
    hid = _gelu_tanh(pre).astype(BF16)
    out = _dot(hid, w2_ref[0]) + b2_ref[0]
    rowi = lax.broadcasted_iota(jnp.int32, out.shape, 0)
    o_ref[0, 0, 0] = jnp.where(rowi < npc - 1, out, 0.0)


def _nsa_compress(kvc_pieces, bsz, seq, pe, w1, b1, w2, b2):
    npc = seq // NSA_CMP_STRIDE
    flat = NSA_CMP_STRIDE * HEAD_DIM
    x = kvc_pieces
    return pl.pallas_call(
        _nsa_compress_kernel,
        grid=(bsz, 2, NSA_KV),
        in_specs=[
            pl.BlockSpec((npc, flat), lambda b, s, g: (b, s * NSA_KV + g)),
            pl.BlockSpec((1, 1, 2 * flat), lambda b, s, g: (s, 0, 0)),
            pl.BlockSpec((1, 2 * flat, NSA_CMP_HIDDEN), lambda b, s, g: (s, 0, 0)),
            pl.BlockSpec((1, 1, NSA_CMP_HIDDEN), lambda b, s, g: (s, 0, 0)),
            pl.BlockSpec((1, NSA_CMP_HIDDEN, HEAD_DIM), lambda b, s, g: (s, 0, 0)),
            pl.BlockSpec((1, 1, HEAD_DIM), lambda b, s, g: (s, 0, 0)),
        ],
        out_specs=pl.BlockSpec((1, 1, 1, npc, HEAD_DIM), lambda b, s, g: (b, s, g, 0, 0)),
        out_shape=jax.ShapeDtypeStruct((bsz, 2, NSA_KV, npc, HEAD_DIM), F32),
        compiler_params=_cparams("parallel", "parallel", "parallel"),
        name="nsa_compress",
    )(x, pe.reshape(2, 1, 2 * flat).astype(F32), w1.astype(BF16), b1.reshape(2, 1, -1).astype(F32),
      w2.astype(BF16), b2.reshape(2, 1, -1).astype(F32))


def _nsa_cmp_split(ns):
    return max(1, min(4, (4 * ns) // LANES))


def _nsa_block_onehot(rows):
    r = lax.broadcasted_iota(jnp.int32, (rows, 2 * NSA_KVW), 0)
    c = lax.broadcasted_iota(jnp.int32, (rows, 2 * NSA_KVW), 1)
    blk = (r % NSA_KTILE) // NSA_SLC_LEN
    return jnp.where((c % NSA_KVW) == HEAD_DIM + blk, 1.0, 0.0).astype(BF16)


def _nsa_kernel(q_ref, kc_ref, vc_ref, ks_ref, kw_ref, vs_ref, vw_ref, g_ref, o_ref, *scratch):
    groups = range(NSA_KV)
    per = len(scratch) // NSA_KV
    bias_s, qaug_s, qaug2_s, sa_s, sb_s, pa_s, pb_s = (
        tuple(scratch[g * per + i] for g in groups) for i in range(per))
    qb = pl.program_id(1)
    nqt = NSA_RPG * NSA_QBLK
    s0 = qb * NSA_QBLK
    ncp = kc_ref.shape[2]
    ns = ncp // 4
    heads = [slice(r * NSA_QBLK, (r + 1) * NSA_QBLK) for r in range(NSA_RPG)]
    klanes = [slice(g * NSA_KVW, (g + 1) * NSA_KVW) for g in groups]
    vrows = [slice(g * HEAD_DIM, (g + 1) * HEAD_DIM) for g in groups]

    qtiles = NSA_QBLK // LANES
    qcat = [jnp.concatenate([q_ref[j, (g * NSA_RPG + r) * HEAD_DIM:(g * NSA_RPG + r + 1) * HEAD_DIM, :]
                             for r in range(NSA_RPG) for j in range(qtiles)], axis=1) for g in groups]
    qpos = s0 + lax.broadcasted_iota(jnp.int32, (1, NSA_QBLK), 1)
    for g in groups:
        qaug_s[g][0:HEAD_DIM, :] = qcat[g]
        qaug_s[g][HEAD_DIM:, :] = jnp.zeros((HEAD_DIM, nqt), BF16)

    split = _nsa_cmp_split(ns)
    chunk = ns // split
    cvalid = jnp.where(qpos >= NSA_CMP_LEN - 1, 1.0, 0.0)
    cur = qpos // NSA_SLC_LEN
    taken = -3.0e38

    def cmp_and_select(nchunks):
        out = []
        for g in groups:
            out.extend(cmp_and_select_group(g, nchunks))
        return tuple(out)

    def cmp_and_select_group(g, nchunks):
        rows, jmax = nchunks * 4 * chunk, nchunks * chunk
        kc = kc_ref[0, g, 0:rows, :]
        rc = lax.broadcasted_iota(jnp.int32, (rows, 1), 0)
        ncmp = 4 * ((rc // (4 * chunk)) * chunk + rc % chunk) + (rc % (4 * chunk)) // chunk
        cbias = jnp.where((ncmp * NSA_CMP_STRIDE + (NSA_CMP_LEN - 1)) <= qpos, 0.0, NEG)
        psum = jnp.zeros((rows, NSA_QBLK), F32)
        p_all = []
        for sl in heads:
            s = _dot(kc, qcat[g][:, sl]) + cbias
            e = jnp.exp2(s - jnp.max(s, axis=0, keepdims=True))
            p = e * (cvalid / jnp.sum(e, axis=0, keepdims=True))
            psum = psum + p
            p_all.append(p.astype(BF16))
        o_cmp = _dot(vc_ref[0, g, :, 0:rows], jnp.concatenate(p_all, axis=1))

        tot, p3 = [], []
        for c in range(nchunks):
            part = [psum[(4 * c + i) * chunk:(4 * c + i + 1) * chunk] for i in range(4)]
            tot.append(part[0] + part[1] + part[2] + part[3])
            p3.append(part[3])
        tot, p3 = jnp.concatenate(tot, axis=0), jnp.concatenate(p3, axis=0)
        rj = lax.broadcasted_iota(jnp.int32, (jmax, NSA_QBLK), 0)
        imp = tot + jnp.where(rj >= 1, pltpu.roll(p3, 1, 0), 0.0)
        forced = (rj == 0) | (rj == cur) | (rj == cur - 1)
        rjf = rj.astype(F32)
        imp = jnp.where(forced, taken, jnp.where(rj <= cur, imp, -FORCE))
        for _ in range(min(NSA_TOPK, ns) - 3):
            mx = jnp.max(imp, axis=0, keepdims=True)
            first = jnp.min(jnp.where(imp == mx, rjf, float(ns)), axis=0, keepdims=True)
            imp = jnp.where(rjf == first, taken, imp)
        bias = jnp.where(imp == taken, 0.0, NEG)
        if jmax < ns:
            bias = jnp.concatenate([bias, jnp.full((ns - jmax, NSA_QBLK), NEG, F32)], axis=0)
        return o_cmp, bias

    last_block = (s0 + NSA_QBLK - 1) // NSA_SLC_LEN
    selected = lax.switch(last_block // chunk, [functools.partial(cmp_and_select, n + 1) for n in range(split)])
    o_cmp = [selected[2 * g] for g in groups]
    for g in groups:
        bias_s[g][...] = selected[2 * g + 1]

    kt_diag = s0 // NSA_KTILE
    blocks_per_tile = NSA_KTILE // NSA_SLC_LEN
    vtiles = NSA_KTILE // LANES

    def qk_tile(g, kt, qaug_ref):
        k0 = pl.multiple_of(kt * NSA_KTILE, NSA_KTILE)
        b8 = bias_s[g][pl.ds(pl.multiple_of(kt * blocks_per_tile, blocks_per_tile), blocks_per_tile), :]
        b16 = jnp.concatenate([b8, jnp.zeros_like(b8)], axis=0).astype(BF16)
        qaug_ref[HEAD_DIM:HEAD_DIM + BF16_ROWS, :] = jnp.concatenate([b16] * NSA_RPG, axis=1)
        return _dot(ks_ref[pl.ds(k0, NSA_KTILE), klanes[g]], qaug_ref[...])

    ones_rows = jnp.ones((BF16_ROWS, NSA_KTILE), BF16)

    def pv_tile(g, kt, p):
        vt = jnp.concatenate([vs_ref[kt * vtiles + i, vrows[g], :] for i in range(vtiles)], axis=1)
        return _dot(jnp.concatenate([vt, ones_rows], axis=0), p)

    def softmax_tile(s, m_old):
        m_new = jnp.maximum(m_old, jnp.max(s, axis=0, keepdims=True))
        p = jnp.exp2((s - m_new).astype(BF16))
        return p, m_new, jnp.exp2(m_old - m_new)

    def visible(kt):
        kpos = kt * NSA_KTILE + lax.broadcasted_iota(jnp.int32, (NSA_KTILE, 1), 0)
        return jnp.concatenate([kpos <= qpos] * NSA_RPG, axis=1)

    def tile_group(i, carry, last):
        carry = list(carry)
        for k in range(NSA_UNROLL):
            t = NSA_UNROLL * i + k
            for g in groups:
                m_run, acc, alpha_prev = carry[g]
                s_cur, p_cur, s_nxt, p_prv, qa = ((sa_s[g], pa_s[g], sb_s[g], pb_s[g], qaug_s[g]) if k % 2 == 0 else
                                                  (sb_s[g], pb_s[g], sa_s[g], pa_s[g], qaug2_s[g]))
                acc = alpha_prev * acc + pv_tile(g, jnp.maximum(t - 1, 0), p_prv[...])
                s = jnp.where(visible(t), s_cur[...], NEG) if last else s_cur[...]
                p, m_run, alpha_prev = softmax_tile(s, m_run)
                p_cur[...] = p
                if last and k == NSA_UNROLL - 1:
                    acc = alpha_prev * acc + pv_tile(g, t, p_cur[...])
                else:
                    s_nxt[...] = qk_tile(g, t + 1, qa)
                carry[g] = (m_run, acc, alpha_prev)
        return tuple(carry)

    for g in groups:
        qaug2_s[g][...] = qaug_s[g][...]
        pb_s[g][...] = jnp.zeros((NSA_KTILE, nqt), BF16)
        sa_s[g][...] = qk_tile(g, 0, qaug2_s[g])

    span = NSA_WIN + NSA_QBLK
    start = pl.multiple_of(jnp.maximum(s0 - NSA_WIN, 0), NSA_QBLK)
    kp = start + lax.broadcasted_iota(jnp.int32, (span, 1), 0)
    wbias = jnp.where((kp <= qpos) & (kp > qpos - NSA_WIN), 0.0, NEG)
    sblk = start // LANES
    o_win = []
    for g in groups:
        kwin = kw_ref[pl.ds(start, span), klanes[g]]
        pw, dens = [], []
        for sl in heads:
            s = _dot(kwin, qaug_s[g][:, sl]) + wbias
            e = jnp.exp2(s - jnp.max(s, axis=0, keepdims=True))
            dens.append(jnp.sum(e, axis=0, keepdims=True))
            pw.append(e.astype(BF16))
        vwt = jnp.concatenate([vw_ref[sblk + i, vrows[g], :] for i in range(span // LANES)], axis=1)
        o_win.append(_dot(vwt, jnp.concatenate(pw, axis=1)) * (1.0 / jnp.concatenate(dens, axis=1)))

    init = tuple((jnp.full((1, nqt), NEG, F32), jnp.zeros((HEAD_DIM + BF16_ROWS, nqt), F32), jnp.ones((1, nqt), F32))
                 for _ in groups)
    group_diag = kt_diag // NSA_UNROLL
    carry = lax.fori_loop(0, group_diag, lambda i, c: tile_group(i, c, False), init)
    carry = tile_group(group_diag, carry, True)

    gates = _sigmoid(jnp.concatenate([g_ref[j] for j in range(qtiles)], axis=1))
    outs = []
    for g in groups:
        acc = carry[g][1]
        o_slc = acc[0:HEAD_DIM] * (1.0 / acc[HEAD_DIM:HEAD_DIM + 1])
        for r, sl in enumerate(heads):
            row = g * NSA_GATE_ROWS + r
            outs.append(gates[row:row + 1, :] * o_cmp[g][:, sl]
                        + gates[row + NSA_RPG:row + NSA_RPG + 1, :] * o_slc[:, sl]
                        + gates[row + 2 * NSA_RPG:row + 2 * NSA_RPG + 1, :] * o_win[g][:, sl])
    o_ref[...] = jnp.concatenate(outs, axis=0).T.astype(o_ref.dtype)


def _nsa_mixer(q_tt, kvc, k_slc, k_win, v_tt, g_tt, bsz, seq, pe, w1, b1, w2, b2):
    nqb = seq // NSA_QBLK
    qtiles = NSA_QBLK // LANES
    ntile = seq // LANES
    npc = seq // NSA_CMP_STRIDE
    ns = seq // NSA_SLC_LEN
    cmp_out = _nsa_compress(kvc, bsz, seq, pe, w1, b1, w2, b2)
    split = _nsa_cmp_split(ns)
    perm = cmp_out.reshape(bsz, 2, NSA_KV, split, ns // split, 4, HEAD_DIM).transpose(0, 1, 2, 3, 5, 4, 6).reshape(
        bsz, 2, NSA_KV, npc, HEAD_DIM)
    kc = perm[:, 0].astype(BF16)
    vc_t = perm[:, 1].transpose(0, 1, 3, 2).astype(BF16)
    nqt = NSA_RPG * NSA_QBLK
    resident = pl.Buffered(1)
    group_scratch = [pltpu.VMEM((ns, NSA_QBLK), F32),
                     pltpu.VMEM((2 * HEAD_DIM, nqt), BF16), pltpu.VMEM((2 * HEAD_DIM, nqt), BF16),
                     pltpu.VMEM((NSA_KTILE, nqt), F32), pltpu.VMEM((NSA_KTILE, nqt), F32),
                     pltpu.VMEM((NSA_KTILE, nqt), BF16), pltpu.VMEM((NSA_KTILE, nqt), BF16)]
    return pl.pallas_call(
        _nsa_kernel,
        grid=(bsz, nqb),
        in_specs=[
            pl.BlockSpec((qtiles, NSA_Q, LANES), lambda b, q: (b * nqb + q, 0, 0)),
            pl.BlockSpec((1, NSA_KV, npc, HEAD_DIM), lambda b, q: (b, 0, 0, 0), pipeline_mode=resident),
            pl.BlockSpec((1, NSA_KV, HEAD_DIM, npc), lambda b, q: (b, 0, 0, 0), pipeline_mode=resident),
            pl.BlockSpec((seq, NSA_KV * NSA_KVW), lambda b, q: (b, 0), pipeline_mode=resident),
            pl.BlockSpec((seq, NSA_KV * NSA_KVW), lambda b, q: (b, 0), pipeline_mode=resident),
            pl.BlockSpec((ntile, NSA_KVW, LANES), lambda b, q: (b, 0, 0), pipeline_mode=resident),
            pl.BlockSpec((ntile, NSA_KVW, LANES), lambda b, q: (b, 1, 0), pipeline_mode=resident),
            pl.BlockSpec((qtiles, NSA_KV * NSA_GATE_ROWS, LANES), lambda b, q: (b * nqb + q, 0, 0)),
        ],
        out_specs=pl.BlockSpec((NSA_QBLK, NSA_Q), lambda b, q: (b * nqb + q, 0)),
        out_shape=jax.ShapeDtypeStruct((bsz * seq, NSA_Q), BF16),
        scratch_shapes=group_scratch * NSA_KV,
        compiler_params=_cparams("arbitrary", "arbitrary"),
        name="nsa_attention",
    )(q_tt, kc, vc_t, k_slc, k_win, v_tt, v_tt, g_tt)


def _swa_kernel(q_ref, kp_ref, kc_ref, vp_ref, vc_ref, sink_ref, o_ref):
    step = pl.program_id(1)
    t = SWA_WIN
    nb = SWA_STEP_BLOCKS
    krel = lax.broadcasted_iota(jnp.int32, (2 * t, 1), 0) - t
    qrel = lax.broadcasted_iota(jnp.int32, (1, t), 1)
    in_band = (krel <= qrel) & (krel > qrel - SWA_WIN)
    lowest = jnp.where(step > 0, -t, 0)
    mbias = [jnp.where(in_band & (krel >= lowest), 0.0, NEG)] + [jnp.where(in_band, 0.0, NEG)] * (nb - 1)
    k_all = jnp.concatenate([kp_ref[...], kc_ref[...]], axis=0)
    v_tiles = [vp_ref[0]] + [vc_ref[i] for i in range(nb)]
    scores = {}
    for blk in range(nb):
        kband = k_all[blk * t:(blk + 2) * t]
        for g in range(2):
            rows = slice(g * SWA_RPG * HEAD_DIM, (g + 1) * SWA_RPG * HEAD_DIM)
            qg = q_ref[blk, rows, :]
            qcat = jnp.concatenate([qg[r * HEAD_DIM:(r + 1) * HEAD_DIM, :] for r in range(SWA_RPG)], axis=1)
            zq = jnp.zeros_like(qcat)
            qext = jnp.concatenate([qcat, zq] if g == 0 else [zq, qcat], axis=0)
            scores[blk, g] = _dot(kband, qext)
    for blk in range(nb):
        outs = []
        for g in range(2):
            s = scores[blk, g]
            ps, dens = [], []
            for r in range(SWA_RPG):
                h = g * SWA_RPG + r
                sink = sink_ref[h:h + 1, :]
                sr = s[:, r * t:(r + 1) * t] + mbias[blk]
                mx = jnp.maximum(jnp.max(sr, axis=0, keepdims=True), sink)
                e = jnp.exp2(sr - mx)
                dens.append(jnp.sum(e, axis=0, keepdims=True) + jnp.exp2(sink - mx))
                ps.append(e.astype(BF16))
            vband = jnp.concatenate([v_tiles[blk][g * HEAD_DIM:(g + 1) * HEAD_DIM, :],
                                     v_tiles[blk + 1][g * HEAD_DIM:(g + 1) * HEAD_DIM, :]], axis=1)
            og = _dot(vband, jnp.concatenate(ps, axis=1)) * (1.0 / jnp.concatenate(dens, axis=1))
            outs.append(jnp.concatenate([og[:, r * t:(r + 1) * t] for r in range(SWA_RPG)], axis=0).T)
        o_ref[blk * t:(blk + 1) * t, :] = jnp.concatenate(outs, axis=1).astype(o_ref.dtype)


def _swa_mixer(q_tt, k_nat, v_tt, sinks, bsz, seq):
    nb = SWA_STEP_BLOCKS
    nblk = seq // SWA_WIN
    nstep = nblk // nb
    sink_rows = jnp.broadcast_to((sinks.astype(F32) * math.log2(math.e))[:, None], (sinks.shape[0], LANES))
    prev = lambda b, q: b * nblk + jnp.maximum(nb * q - 1, 0)
    return pl.pallas_call(
        _swa_kernel,
        grid=(bsz, nstep),
        in_specs=[
            pl.BlockSpec((nb, SWA_Q, LANES), lambda b, q: (b * nstep + q, 0, 0)),
            pl.BlockSpec((SWA_WIN, SWA_KVW), lambda b, q: (prev(b, q), 0)),
            pl.BlockSpec((nb * SWA_WIN, SWA_KVW), lambda b, q: (b * nstep + q, 0)),
            pl.BlockSpec((1, SWA_KVW, LANES), lambda b, q: (prev(b, q), 0, 0)),
            pl.BlockSpec((nb, SWA_KVW, LANES), lambda b, q: (b * nstep + q, 0, 0)),
            _full(sink_rows.shape),
        ],
        out_specs=pl.BlockSpec((nb * SWA_WIN, SWA_Q), lambda b, q: (b * nstep + q, 0)),
        out_shape=jax.ShapeDtypeStruct((bsz * seq, SWA_Q), BF16),
        compiler_params=_cparams("parallel", "parallel"),
        name="swa_attention",
    )(q_tt, k_nat, k_nat, v_tt, v_tt, sink_rows)


def _s5_params(a_re, a_im, log_dt, b_re, b_im, c_re, c_im, n_chunks):
    f = F32
    t = S5_CHUNK
    step = jnp.exp(log_dt.astype(f))[:, None]
    lr, li = a_re.astype(f), a_im.astype(f)

    def lam_pow(tau):
        tau = tau.astype(f)[:, None, None]
        mag = jnp.exp(lr * step * tau)
        ang = li * step * tau
        return mag * jnp.cos(ang), mag * jnp.sin(ang)

    lb_r, lb_i = (v[0] for v in lam_pow(jnp.ones((1,))))
    nr, ni = lb_r - 1.0, lb_i
    den = lr * lr + li * li
    fr, fi = (nr * lr + ni * li) / den, (ni * lr - nr * li) / den
    br, bi = b_re.astype(f), b_im.astype(f)
    bb_r = fr[..., None] * br - fi[..., None] * bi
    bb_i = fr[..., None] * bi + fi[..., None] * br
    cr, ci = c_re.astype(f), c_im.astype(f)

    pr, pi = lam_pow(jnp.arange(t + 1))
    cl_r = cr[None] * pr[:, :, None, :] - ci[None] * pi[:, :, None, :]
    cl_i = cr[None] * pi[:, :, None, :] + ci[None] * pr[:, :, None, :]
    kern_t = jnp.einsum("tghp,gpk->gkth", cl_r[:t], bb_r, precision="highest") - jnp.einsum(
        "tghp,gpk->gkth", cl_i[:t], bb_i, precision="highest")
    rr, ri = pr[t - 1 - jnp.arange(t)], pi[t - 1 - jnp.arange(t)]
    bs_r = rr[..., None] * bb_r[None] - ri[..., None] * bb_i[None]
    bs_i = rr[..., None] * bb_i[None] + ri[..., None] * bb_r[None]
    bs = jnp.concatenate([bs_r, bs_i], axis=2)
    bs = bs.transpose(1, 0, 3, 2).reshape(S5_GROUPS, t * S5_GROUP_CH, 2 * S5_STATE)
    cs = jnp.concatenate([cl_r[1:], -cl_i[1:]], axis=3)
    cs = cs.transpose(1, 3, 0, 2).reshape(S5_GROUPS, 2 * S5_STATE, t * S5_GROUP_CH)
    ar, ai = pr[t], pi[t]
    a1, a2 = [], []
    k = 1
    while k < n_chunks:
        a1.append(jnp.concatenate([ar, ar], axis=1))
        a2.append(jnp.concatenate([-ai, ai], axis=1))
        ar, ai = ar * ar - ai * ai, 2.0 * ar * ai
        k *= 2
    a1 = jnp.stack(a1, axis=1)
    a2 = jnp.stack(a2, axis=1)
    noct = S5_GROUPS // S5_OCT
    hc, ns2 = S5_GROUP_CH, 2 * S5_STATE
    width = t * hc
    lanes = t * LANES
    r_i, c_i = np.arange(width)[:, None], np.arange(lanes)[None, :]
    rep = jnp.asarray((r_i // hc == c_i // LANES) & (r_i % hc == c_i % hc), dtype=BF16)
    kern_o = kern_t.reshape(noct, LANES, width).astype(BF16)
    toep_o = pl.pallas_call(
        _s5_toeplitz_kernel,
        grid=(noct, t // 8),
        in_specs=[pl.BlockSpec((1, LANES, width), lambda o, s: (o, 0, 0)), _full((width, lanes))],
        out_specs=pl.BlockSpec((1, 8 * LANES, lanes), lambda o, s: (o, s, 0)),
        out_shape=jax.ShapeDtypeStruct((noct, lanes, lanes), BF16),
        scratch_shapes=[pltpu.VMEM((t, LANES, LANES), F32)],
        compiler_params=_cparams("parallel", "arbitrary"),
        name="s5_toeplitz_table",
    )(kern_o, rep)
    cs_o = pl.pallas_call(
        _s5_readout_kernel,
        grid=(noct,),
        in_specs=[pl.BlockSpec((1, S5_OCT * ns2, width), lambda o: (o, 0, 0)), _full((width, lanes))],
        out_specs=pl.BlockSpec((1, S5_OCT * ns2, lanes), lambda o: (o, 0, 0)),
        out_shape=jax.ShapeDtypeStruct((noct, S5_OCT * ns2, lanes), BF16),
        compiler_params=_cparams("parallel"),
        name="s5_readout_table",
    )(cs.reshape(noct, S5_OCT * ns2, width).astype(BF16), rep)
    bs_slabs = bs.reshape(noct, S5_OCT, t, hc, ns2).transpose(0, 2, 1, 3, 4).reshape(noct, t, LANES, ns2).astype(BF16)
    bs_o = pl.pallas_call(
        _s5_state_in_kernel,
        grid=(noct,),
        in_specs=[pl.BlockSpec((1, t, LANES, ns2), lambda o: (o, 0, 0, 0))],
        out_specs=pl.BlockSpec((1, lanes, S5_OCT * ns2), lambda o: (o, 0, 0)),
        out_shape=jax.ShapeDtypeStruct((noct, lanes, S5_OCT * ns2), BF16),
        compiler_params=_cparams("parallel"),
        name="s5_state_in_table",
    )(bs_slabs)

    def oct_rows(a):
        return a.reshape(noct, S5_OCT, -1, ns2).transpose(0, 2, 1, 3).reshape(noct, -1, S5_OCT * ns2)

    return toep_o, bs_o, cs_o, oct_rows(a1), oct_rows(a2)


def _same_group(shape, row_div, col_mod, col_div):
    r = lax.broadcasted_iota(jnp.int32, shape, 0)
    c = lax.broadcasted_iota(jnp.int32, shape, 1)
    return (r // row_div) == ((c % col_mod) // col_div)


def _s5_toeplitz_kernel(k_ref, rep_ref, o_ref, full_s):
    nlag = full_s.shape[0]
    steps_here = o_ref.shape[1] // LANES

    @pl.when(pl.program_id(1) == 0)
    def _():
        full = _dot(k_ref[0], rep_ref[...])
        full = jnp.where(_same_group(full.shape, S5_GROUP_CH, LANES, S5_GROUP_CH), full, 0.0)
        for lag in range(nlag):
            full_s[lag] = full[:, lag * LANES:(lag + 1) * LANES]

    for i in range(steps_here):
        s = pl.program_id(1) * steps_here + i
        for t in range(nlag):
            tile = full_s[jnp.maximum(t - s, 0)]
            o_ref[0, i * LANES:(i + 1) * LANES, t * LANES:(t + 1) * LANES] = (
                jnp.where(t >= s, tile, 0.0).astype(o_ref.dtype))


def _s5_readout_kernel(c_ref, rep_ref, o_ref):
    full = _dot(c_ref[0], rep_ref[...])
    keep = _same_group(full.shape, 2 * S5_STATE, LANES, S5_GROUP_CH)
    o_ref[0] = jnp.where(keep, full, 0.0).astype(o_ref.dtype)


def _s5_state_in_kernel(b_ref, o_ref):
    keep = _same_group((LANES, o_ref.shape[2]), S5_GROUP_CH, o_ref.shape[2], 2 * S5_STATE)
    for s in range(b_ref.shape[1]):
        full = jnp.concatenate([b_ref[0, s]] * S5_OCT, axis=1)
        o_ref[0, s * LANES:(s + 1) * LANES, :] = jnp.where(keep, full, jnp.zeros((), full.dtype))


def _s5_state_kernel(bsz, u_ref, bs_ref, a1_ref, a2_ref, hi_ref, lo_ref):
    sc = _dot(u_ref[...], bs_ref[0])
    n = sc.shape[0] // bsz
    width = sc.shape[1]
    rowi = lax.broadcasted_iota(jnp.int32, (n, width), 0)

    def swap_re_im(x):
        return jnp.concatenate([pltpu.roll(x[:, j * LANES:(j + 1) * LANES], S5_STATE, 1)
                                for j in range(width // LANES)], axis=1)

    h_in = []
    for b in range(bsz):
        x = sc[b * n:(b + 1) * n]
        k, step = 1, 0
        while k < n:
            xs = jnp.where(rowi >= k, pltpu.roll(x, k, 0), 0.0)
            x = x + a1_ref[0, step:step + 1, :] * xs + a2_ref[0, step:step + 1, :] * swap_re_im(xs)
            k *= 2
            step += 1
        h_in.append(jnp.where(rowi >= 1, pltpu.roll(x, 1, 0), 0.0))
    h_in = jnp.concatenate(h_in, axis=0)
    hi = h_in.astype(BF16)
    hi_ref[0] = hi
    lo_ref[0] = (h_in - hi.astype(F32)).astype(BF16)


def _s5_out_kernel(u_ref, toep_ref, hi_ref, lo_ref, cs_ref, o_ref):
    n = pl.program_id(1)
    cols = toep_ref.shape[2]
    off = _dot(hi_ref[0], cs_ref[0]) + _dot(lo_ref[0], cs_ref[0])
    for nn in range(toep_ref.shape[1] // cols):
        @pl.when(n == nn)
        def _(nn=nn):
            k = (nn + 1) * cols
            y = off + _dot(u_ref[:, 0:k], toep_ref[0, 0:k, :])
            for t8 in range(cols // LANES):
                o_ref[:, t8, :] = y[:, t8 * LANES:(t8 + 1) * LANES].astype(o_ref.dtype)


def _s5_glu_kernel(y_ref, u_ref, d_ref, w_ref, b_ref, o_ref):
    y = _gelu_tanh(y_ref[...] + d_ref[...] * u_ref[...])
    gate = _sigmoid(_dot(y.astype(BF16), w_ref[...]) + b_ref[...])
    o_ref[...] = (y * gate).astype(o_ref.dtype)


def _s5_mixer(u5, u_chunks, bsz, seq, a_re, a_im, log_dt, b_re, b_im, c_re, c_im, d_skip, glu_w, glu_b, tm=1024):
    m = u5.shape[0]
    t = S5_CHUNK
    nch = m // t
    noct = S5_GROUPS // S5_OCT
    lanes = t * LANES
    sw = S5_OCT * 2 * S5_STATE
    toep, bs, cs, a1, a2 = _s5_params(a_re, a_im, log_dt, b_re, b_im, c_re, c_im, nch // bsz)
    nsteps = a1.shape[1]
    h_hi, h_lo = pl.pallas_call(
        functools.partial(_s5_state_kernel, bsz),
        grid=(noct,),
        in_specs=[
            pl.BlockSpec((nch, lanes), lambda o: (0, o)),
            pl.BlockSpec((1, lanes, sw), lambda o: (o, 0, 0)),
            pl.BlockSpec((1, nsteps, sw), lambda o: (o, 0, 0)),
            pl.BlockSpec((1, nsteps, sw), lambda o: (o, 0, 0)),
        ],
        out_specs=[pl.BlockSpec((1, nch, sw), lambda o: (o, 0, 0))] * 2,
        out_shape=[jax.ShapeDtypeStruct((noct, nch, sw), BF16)] * 2,
        compiler_params=_cparams("parallel"),
        name="s5_state",
    )(u_chunks, bs, a1, a2)
    tsub = 8
    rows = nch // 2
    y = pl.pallas_call(
        _s5_out_kernel,
        grid=(noct, t // tsub, nch // rows),
        in_specs=[
            pl.BlockSpec((rows, lanes), lambda o, n, r: (r, o)),
            pl.BlockSpec((1, lanes, tsub * LANES), lambda o, n, r: (o, 0, n)),
            pl.BlockSpec((1, rows, sw), lambda o, n, r: (o, r, 0)),
            pl.BlockSpec((1, rows, sw), lambda o, n, r: (o, r, 0)),
            pl.BlockSpec((1, sw, tsub * LANES), lambda o, n, r: (o, 0, n)),
        ],
        out_specs=pl.BlockSpec((rows, tsub, LANES), lambda o, n, r: (r, n, o)),
        out_shape=jax.ShapeDtypeStruct((nch, t, S5_CH), BF16),
        compiler_params=_cparams("parallel", "parallel", "parallel"),
        name="s5_scan",
    )(u_chunks, toep, h_hi, h_lo, cs).reshape(m, S5_CH)
    return pl.pallas_call(
        _s5_glu_kernel,
        grid=(m // tm,),
        in_specs=[
            pl.BlockSpec((tm, S5_CH), lambda i: (i, 0)),
            pl.BlockSpec((tm, S5_CH), lambda i: (i, 0)),
            _full((1, S5_CH)), _full((S5_CH, S5_CH)), _full((1, S5_CH)),
        ],
        out_specs=pl.BlockSpec((tm, S5_CH), lambda i: (i, 0)),
        out_shape=jax.ShapeDtypeStruct((m, S5_CH), BF16),
        compiler_params=_cparams("parallel"),
        name="s5_glu",
    )(y, u5, d_skip.reshape(1, S5_CH).astype(F32), glu_w.astype(BF16), glu_b.reshape(1, S5_CH).astype(F32))


def _even_mixers(h, bsz, seq, g_mix, w_in, conv_w, conv_b, dt_bias, a_log, d_skip, norm_g, pe, w1, b1, w2, b2):
    d = h.shape[1]
    scale = HEAD_DIM ** -0.5 * math.log2(math.e)
    o = SSD_IN
    w_ssd = jnp.concatenate([w_in[:, :SSD_IN], jnp.zeros((d, LANES - SSD_HEADS), w_in.dtype)], axis=1)
    w_q = w_in[:, o:o + NSA_Q] * scale
    kv = [w_in[:, o + NSA_Q + i * NSA_KVW:o + NSA_Q + (i + 1) * NSA_KVW] for i in range(6)]
    w_gate = w_in[:, o + NSA_Q + 6 * NSA_KVW:].reshape(d, NSA_KV, NSA_RPG, 3).transpose(0, 1, 3, 2)
    w_gate = jnp.concatenate([w_gate.reshape(d, NSA_KV, 3 * NSA_RPG),
                              jnp.zeros((d, NSA_KV, NSA_GATE_ROWS - 3 * NSA_RPG), w_in.dtype)],
                             axis=2).reshape(d, NSA_KV * NSA_GATE_ROWS)

    def per_group_halves(w):
        wg = w.reshape(d, NSA_KV, HEAD_DIM)
        return jnp.concatenate([wg, jnp.zeros_like(wg)], axis=2).reshape(d, NSA_KV * NSA_KVW)
    segs = [
        ("nat", w_ssd, F32),
        ("tt", w_q, BF16),
        ("pc", jnp.concatenate([kv[0], kv[1]], axis=1), F32),
        ("nat+", per_group_halves(kv[2]), BF16, _nsa_block_onehot(NSA_KTILE)),
        ("nat", per_group_halves(kv[4]), BF16),
        ("tt", jnp.concatenate([kv[3], kv[5]], axis=1), BF16),
        ("tt", w_gate, F32),
    ]
    u_ssd, q_tt, kvc, k_slc, k_win, v_tt, g_tt = _norm_proj(h, g_mix, segs)
    ya = _ssd_mixer(u_ssd, bsz, seq, conv_w, conv_b, dt_bias, a_log, d_skip, norm_g)
    yb = _nsa_mixer(q_tt, kvc, k_slc, k_win, v_tt, g_tt, bsz, seq, pe, w1, b1, w2, b2)
    return ya, yb


def _odd_mixers(h, bsz, seq, g_mix, w_in, sinks, a_re, a_im, log_dt, b_re, b_im, c_re, c_im, d_skip, glu_w, glu_b):
    scale = HEAD_DIM ** -0.5 * math.log2(math.e)
    segs = [
        ("tt", w_in[:, :SWA_Q] * scale, BF16),
        ("nat", w_in[:, SWA_Q:SWA_Q + SWA_KVW], BF16),
        ("tt", w_in[:, SWA_Q + SWA_KVW:SWA_Q + 2 * SWA_KVW], BF16),
        ("nat+ch", w_in[:, SWA_Q + 2 * SWA_KVW:], F32),
    ]
    q_tt, k_nat, v_tt, u5, u_chunks = _norm_proj(h, g_mix, segs)
    yc = _swa_mixer(q_tt, k_nat, v_tt, sinks, bsz, seq)
    yd = _s5_mixer(u5, u_chunks, bsz, seq, a_re, a_im, log_dt, b_re, b_im, c_re, c_im, d_skip, glu_w, glu_b)
    return yc, yd


def kernel(x, norm_mix, norm_mlp, norm_final, mlp_w_up, mlp_w_down, ev_w_in, ev_w_out, ssd_conv_w, ssd_conv_b,
           ssd_dt_bias, ssd_a_log, ssd_d, ssd_norm, nsa_pe, nsa_cmp_w1, nsa_cmp_b1, nsa_cmp_w2, nsa_cmp_b2,
           od_w_in, od_w_out, swa_sinks, s5_a_re, s5_a_im, s5_log_dt, s5_b_re, s5_b_im, s5_c_re, s5_c_im,
           s5_d, s5_glu_w, s5_glu_b):
    bsz, seq, d = x.shape
    depth = norm_mix.shape[0]
    assert seq % (NSA_UNROLL * NSA_KTILE) == 0 and seq >= NSA_WIN + NSA_QBLK
    assert seq % (SWA_STEP_BLOCKS * SWA_WIN) == 0 and (bsz * seq) % 1024 == 0 and d % LANES == 0
    h = x.reshape(bsz * seq, d)
    for layer in range(depth):
        i = layer // 2
        if layer % 2 == 0:
            ya, yb = _even_mixers(h, bsz, seq, norm_mix[layer], ev_w_in[i], ssd_conv_w[i], ssd_conv_b[i],
                                  ssd_dt_bias[i], ssd_a_log[i], ssd_d[i], ssd_norm[i], nsa_pe[i],
                                  nsa_cmp_w1[i], nsa_cmp_b1[i], nsa_cmp_w2[i], nsa_cmp_b2[i])
            w_out = ev_w_out[i]
        else:
            ya, yb = _odd_mixers(h, bsz, seq, norm_mix[layer], od_w_in[i], swa_sinks[i], s5_a_re[i], s5_a_im[i],
                                 s5_log_dt[i], s5_b_re[i], s5_b_im[i], s5_c_re[i], s5_c_im[i], s5_d[i],
                                 s5_glu_w[i], s5_glu_b[i])
            w_out = od_w_out[i]
        h = _mix_out_mlp(h, ya, yb, w_out, norm_mlp[layer], mlp_w_up[layer], mlp_w_down[layer], norm_final,
                         final=(layer == depth - 1))
    return h.reshape(bsz, seq, d)
```
